```python
import jax, jax.numpy as jnp
from jax import lax
import numpy as np

D_MODEL = 1024
BATCH = 8
SEQ = 2048
DEPTH = 1
DEC_BATCH = 32
DEC_SEQ = 32
PAST_LEN = 2048

CHUNK = 64
PAST_CHUNKS = 8
BAND = PAST_CHUNKS * CHUNK
W_LRU = D_MODEL // 2
LRU_BLOCKS = 8
LRU_BLOCK = W_LRU // LRU_BLOCKS
CONV_W = 4
RG_C = 8.0
HEAD_DIM = 64
W_ATT = D_MODEL // 2
N_HEADS = W_ATT // HEAD_DIM
MAX_REL = 256
PLE_DIM = 256
EPS = 1e-6
NEG = -1e30
IN_WIDTH = 2 * W_LRU + 4 * W_ATT
SPLITS = [W_LRU, 2 * W_LRU, 2 * W_LRU + W_ATT, 2 * W_LRU + 2 * W_ATT, 2 * W_LRU + 3 * W_ATT]

kernel_name = 'hybrid_rglru_chunkband_stream_step'


def rms_norm(x, g):
    xf = x.astype(jnp.float32)
    y = xf * lax.rsqrt(jnp.mean(xf * xf, axis=-1, keepdims=True) + EPS)
    return (y * g.astype(jnp.float32)).astype(x.dtype)


def causal_conv(u, buf, w, b):
    T = u.shape[1]
    full = jnp.concatenate([buf.astype(u.dtype), u], axis=1)
    out = b + sum(full[:, k:k + T] * w[k] for k in range(CONV_W))
    return out, full[:, -(CONV_W - 1):]


def _lin_combine(e1, e2):
    a1, b1 = e1
    a2, b2 = e2
    return a1 * a2, a2 * b1 + b2


def rg_lru(xc, h0, wa, ba, wx, bx, lam):
    B, T, W = xc.shape
    xb = xc.reshape(B, T, LRU_BLOCKS, LRU_BLOCK)
    r = jax.nn.sigmoid((jnp.einsum('btnd,nde->btne', xb, wa).reshape(B, T, W) + ba).astype(jnp.float32))
    i = jax.nn.sigmoid((jnp.einsum('btnd,nde->btne', xb, wx).reshape(B, T, W) + bx).astype(jnp.float32))
    log_a = -RG_C * r * jax.nn.softplus(-lam.astype(jnp.float32))
    a = jnp.exp(log_a)
    u = jnp.sqrt(-jnp.expm1(2.0 * log_a)) * (i * xc.astype(jnp.float32))
    u = u.at[:, 0].add(a[:, 0] * h0.astype(jnp.float32))
    _, h = lax.associative_scan(_lin_combine, (a, u), axis=1)
    return h.astype(xc.dtype), h[:, -1].astype(xc.dtype)


def rel_bias_lookup(table, tq, tk, offset):
    rel = offset + jnp.arange(tq)[:, None] - jnp.arange(tk)[None, :]
    idx = jnp.clip(rel, -MAX_REL, MAX_REL) + MAX_REL
    return jnp.transpose(table[idx], (2, 0, 1)).astype(jnp.float32)


def band_attention_prompt(q, k, v, table):
    B, S, H, Dh = q.shape
    nc = S // CHUNK
    pad = ((0, 0), (BAND, 0), (0, 0), (0, 0))
    kc = jnp.pad(k, pad).reshape(B, nc + PAST_CHUNKS, CHUNK, H, Dh)
    vc = jnp.pad(v, pad).reshape(B, nc + PAST_CHUNKS, CHUNK, H, Dh)
    kb = jnp.concatenate([kc[:, j:j + nc] for j in range(PAST_CHUNKS + 1)], axis=2)
    vb = jnp.concatenate([vc[:, j:j + nc] for j in range(PAST_CHUNKS + 1)], axis=2)
    qc = q.reshape(B, nc, CHUNK, H, Dh)
    nk = (PAST_CHUNKS + 1) * CHUNK
    s = jnp.einsum('bnqhd,bnkhd->bnhqk', qc, kb).astype(jnp.float32) * (HEAD_DIM ** -0.5)
    s = s + rel_bias_lookup(table, CHUNK, nk, BAND)[None, None]
    key_pos = jnp.arange(nc)[:, None] * CHUNK + jnp.arange(nk)[None, :] - BAND
    s = jnp.where((key_pos >= 0)[None, :, None, None, :], s, NEG)
    pr = jax.nn.softmax(s, axis=-1).astype(v.dtype)
    o = jnp.einsum('bnhqk,bnkhd->bnqhd', pr, vb)
    return o.reshape(B, S, H * Dh)


def band_attention_sample(q, k_new, v_new, k_cache, v_cache, table):
    B, T, H, Dh = q.shape
    L = k_cache.shape[1]
    kk = jnp.concatenate([k_cache.astype(k_new.dtype), k_new], axis=1)
    vv = jnp.concatenate([v_cache.astype(v_new.dtype), v_new], axis=1)
    s = jnp.einsum('bqhd,bkhd->bhqk', q, kk).astype(jnp.float32) * (HEAD_DIM ** -0.5)
    s = s + rel_bias_lookup(table, T, L + T, L)[None]
    pr = jax.nn.softmax(s, axis=-1).astype(vv.dtype)
    o = jnp.einsum('bhqk,bkhd->bqhd', pr, vv)
    return o.reshape(B, T, H * Dh)


def layer_front(x, norm_g, w_in, q_g, k_g):
    B, T, _ = x.shape
    z = rms_norm(x, norm_g) @ w_in
    xl, gl, q, k, v, ga = jnp.split(z, SPLITS, axis=-1)
    q = rms_norm(q.reshape(B, T, N_HEADS, HEAD_DIM), q_g)
    k = rms_norm(k.reshape(B, T, N_HEADS, HEAD_DIM), k_g)
    v = v.reshape(B, T, N_HEADS, HEAD_DIM)
    return xl, gl, q, k, v, ga


def layer_back(x, lru, gl, att, ga, w_out, p, ple_g, w_pg, w_pe):
    mix = jnp.concatenate([lru * jax.nn.silu(gl), att * jax.nn.silu(ga)], axis=-1) @ w_out
    h = x + mix
    gate = jax.nn.sigmoid(rms_norm(h, ple_g) @ w_pg)
    return h + (p @ w_pe) * gate


def setup_inputs(seed: int = 0) -> dict:
    key = jax.random.key(seed)
    ks = jax.random.split(key, 32)
    nrm = lambda k, s, sc: jax.random.normal(k, s, jnp.float32) * sc
    keep_s = min(BAND, PAST_LEN)
    u = jax.random.uniform(ks[20], (DEPTH, W_LRU), jnp.float32, 0.9, 0.999) ** (1.0 / RG_C)
    return {
        'x_prompt': nrm(ks[0], (BATCH, SEQ, D_MODEL), 1.0),
        'x_sample': nrm(ks[1], (DEC_BATCH, DEC_SEQ, D_MODEL), 1.0),
        'p_prompt': nrm(ks[2], (DEPTH, BATCH, SEQ, PLE_DIM), 1.0),
        'p_sample': nrm(ks[3], (DEPTH, DEC_BATCH, DEC_SEQ, PLE_DIM), 1.0),
        'cache_k': nrm(ks[4], (DEPTH, DEC_BATCH, keep_s, N_HEADS, HEAD_DIM), 1.0),
        'cache_v': nrm(ks[5], (DEPTH, DEC_BATCH, keep_s, N_HEADS, HEAD_DIM), 1.0),
        'state_conv': nrm(ks[6], (DEPTH, DEC_BATCH, CONV_W - 1, W_LRU), 1.0),
        'state_lru': nrm(ks[7], (DEPTH, DEC_BATCH, W_LRU), 0.5),
        'norm_g': 1.0 + nrm(ks[8], (DEPTH, D_MODEL), 0.01),
        'w_in': nrm(ks[9], (DEPTH, D_MODEL, IN_WIDTH), D_MODEL ** -0.5),
        'conv_w': nrm(ks[10], (DEPTH, CONV_W, W_LRU), CONV_W ** -0.5),
        'conv_b': nrm(ks[11], (DEPTH, W_LRU), 0.01),
        'gate_a_w': nrm(ks[12], (DEPTH, LRU_BLOCKS, LRU_BLOCK, LRU_BLOCK), LRU_BLOCK ** -0.5),
        'gate_a_b': nrm(ks[13], (DEPTH, W_LRU), 0.01),
        'gate_x_w': nrm(ks[14], (DEPTH, LRU_BLOCKS, LRU_BLOCK, LRU_BLOCK), LRU_BLOCK ** -0.5),
        'gate_x_b': nrm(ks[15], (DEPTH, W_LRU), 0.01),
        'lru_lambda': jnp.log(u / (1.0 - u)),
        'q_norm_g': 1.0 + nrm(ks[16], (DEPTH, HEAD_DIM), 0.01),
        'k_norm_g': 1.0 + nrm(ks[17], (DEPTH, HEAD_DIM), 0.01),
        'rel_bias': nrm(ks[18], (DEPTH, 2 * MAX_REL + 1, N_HEADS), 0.1),
        'w_out': nrm(ks[19], (DEPTH, W_LRU + W_ATT, D_MODEL), (W_LRU + W_ATT) ** -0.5),
        'ple_norm_g': 1.0 + nrm(ks[21], (DEPTH, D_MODEL), 0.01),
        'w_ple_gate': nrm(ks[22], (DEPTH, D_MODEL, D_MODEL), D_MODEL ** -0.5),
        'w_ple_proj': nrm(ks[23], (DEPTH, PLE_DIM, D_MODEL), PLE_DIM ** -0.5),
    }


def reference(x_prompt, x_sample, p_prompt, p_sample, cache_k, cache_v, state_conv, state_lru,
              norm_g, w_in, conv_w, conv_b, gate_a_w, gate_a_b, gate_x_w, gate_x_b, lru_lambda,
              q_norm_g, k_norm_g, rel_bias, w_out, ple_norm_g, w_ple_gate, w_ple_proj):
    yp, ys = x_prompt, x_sample
    B, S, _ = x_prompt.shape
    keep_p = min(BAND, S)
    pk, pv, pc, ph, sk, sv, sc, sh = [], [], [], [], [], [], [], []
    for l in range(DEPTH):
        xl, gl, q, k, v, ga = layer_front(yp, norm_g[l], w_in[l], q_norm_g[l], k_norm_g[l])
        xc, cbuf = causal_conv(xl, jnp.zeros((B, CONV_W - 1, W_LRU), xl.dtype), conv_w[l], conv_b[l])
        lru, h_last = rg_lru(xc, jnp.zeros((B, W_LRU), xl.dtype), gate_a_w[l], gate_a_b[l],
                             gate_x_w[l], gate_x_b[l], lru_lambda[l])
        att = band_attention_prompt(q, k, v, rel_bias[l])
        yp_next = layer_back(yp, lru, gl, att, ga, w_out[l], p_prompt[l], ple_norm_g[l],
                             w_ple_gate[l], w_ple_proj[l])
        pk.append(k[:, S - keep_p:]); pv.append(v[:, S - keep_p:]); pc.append(cbuf); ph.append(h_last)
        xl, gl, q, k, v, ga = layer_front(ys, norm_g[l], w_in[l], q_norm_g[l], k_norm_g[l])
        xc, cbuf = causal_conv(xl, state_conv[l], conv_w[l], conv_b[l])
        lru, h_last = rg_lru(xc, state_lru[l], gate_a_w[l], gate_a_b[l],
                             gate_x_w[l], gate_x_b[l], lru_lambda[l])
        att = band_attention_sample(q, k, v, cache_k[l], cache_v[l], rel_bias[l])
        ys_next = layer_back(ys, lru, gl, att, ga, w_out[l], p_sample[l], ple_norm_g[l],
                             w_ple_gate[l], w_ple_proj[l])
        sk.append(k); sv.append(v); sc.append(cbuf); sh.append(h_last)
        yp, ys = yp_next, ys_next
    return (yp, ys, jnp.stack(pk), jnp.stack(pv), jnp.stack(pc), jnp.stack(ph),
            jnp.stack(sk), jnp.stack(sv), jnp.stack(sc), jnp.stack(sh))
```

```python
import functools

import jax
import jax.numpy as jnp
from jax import lax
from jax.experimental import pallas as pl
from jax.experimental.pallas import tpu as pltpu

D_MODEL = 1024
CHUNK = 64
PAST_CHUNKS = 8
BAND = PAST_CHUNKS * CHUNK
W_LRU = D_MODEL // 2
LRU_BLOCKS = 8
LRU_BLOCK = W_LRU // LRU_BLOCKS
CONV_W = 4
RG_C = 8.0
HEAD_DIM = 64
W_ATT = D_MODEL // 2
N_HEADS = W_ATT // HEAD_DIM
MAX_REL = 256
PLE_DIM = 256
EPS = 1e-6
NEG = -1e30

SUBLANES = 8
HEAD_GROUP = 256
HEADS_PER_GROUP = HEAD_GROUP // HEAD_DIM
N_GROUPS = W_ATT // HEAD_GROUP
QBLOCK = 256
KEY_TILES = BAND // QBLOCK + 1
VMEM_LIMIT = 48 * 1024 * 1024

F32 = jnp.float32
BF16 = jnp.bfloat16


def _dot(a, b):
    return jnp.dot(a, b, preferred_element_type=F32)


def _dot_nt(a, b):
    return lax.dot_general(a, b, (((1,), (1,)), ((), ())), preferred_element_type=F32)


def _rms_rows(x, g):
    ms = jnp.mean(x * x, axis=-1, keepdims=True)
    return x * lax.rsqrt(ms + EPS) * g


def _head_norm_rows(x, ones_bd, g_tiled):
    x2 = x * x
    hi = x2.astype(BF16)
    lo = (x2 - hi.astype(F32)).astype(BF16)
    ms = _dot(hi, ones_bd) + _dot(lo, ones_bd)
    return x * lax.rsqrt(ms + EPS) * g_tiled


def _scan_rows(a_ref, u_ref, h_ref, row0, nrows, h0):
    ridx = lax.broadcasted_iota(jnp.int32, (SUBLANES, W_LRU), 0)

    def body(i, hprev):
        r = pl.multiple_of(row0 + i * SUBLANES, SUBLANES)
        a = a_ref[pl.ds(r, SUBLANES), :]
        u = u_ref[pl.ds(r, SUBLANES), :]
        for s in (1, 2, 4):
            a_s = jnp.where(ridx >= s, pltpu.roll(a, s, 0), 1.0)
            u_s = jnp.where(ridx >= s, pltpu.roll(u, s, 0), 0.0)
            u = a * u_s + u
            a = a * a_s
        h = a * hprev + u
        h_ref[pl.ds(r, SUBLANES), :] = h
        return h[SUBLANES - 1:SUBLANES, :]

    return lax.fori_loop(0, nrows // SUBLANES, body, h0)


def _lru_inputs(xc, wg_ref, bg_ref, lam_ref, a_ref, u_ref):
    gates = _dot(xc.astype(BF16), wg_ref[...]) + bg_ref[...]
    r = jax.nn.sigmoid(gates[:, :W_LRU])
    i = jax.nn.sigmoid(gates[:, W_LRU:])
    log_a = -RG_C * r * jax.nn.softplus(-lam_ref[...])
    a = jnp.exp(log_a)
    a_ref[...] = a
    u_ref[...] = jnp.sqrt(jnp.tanh(-log_a) * (1.0 + a * a)) * (i * xc)


def _conv_rows(cb_ref, base, nrows, cw_ref, cb_bias):
    out = cb_bias + cw_ref[CONV_W - 1:CONV_W, :] * cb_ref[pl.ds(base + SUBLANES, nrows), :]
    for k in range(CONV_W - 1):
        shift = CONV_W - 1 - k
        out = out + cw_ref[k:k + 1, :] * cb_ref[pl.ds(base + SUBLANES - shift, nrows), :]
    return out


def _front_prompt_kernel(x_ref, ng_ref, wnat_ref, wt_ref, cw_ref, cbias_ref, wg_ref, bg_ref, lam_ref,
                         qg_ref, kg_ref, ones_ref,
                         lru_ref, qt_ref, kbf_ref, vt_ref, gat_ref, pk_ref, pv_ref, pc_ref, ph_ref,
                         cb_scr, a_scr, u_scr, h_scr, hlast_scr, *, n_tiles, keep_tiles):
    t = pl.program_id(1)
    tt = x_ref.shape[1]

    @pl.when(t == 0)
    def _():
        cb_scr[0:SUBLANES, :] = jnp.zeros((SUBLANES, W_LRU), F32)
        hlast_scr[...] = jnp.zeros((1, W_LRU), F32)

    xn = _rms_rows(x_ref[0], ng_ref[...]).astype(BF16)
    znat = _dot(xn, wnat_ref[...])
    zt = _dot_nt(wt_ref[...], xn)

    cb_scr[SUBLANES:SUBLANES + tt, :] = znat[:, :W_LRU]
    xc = _conv_rows(cb_scr, 0, tt, cw_ref, cbias_ref[...])
    pc_ref[0] = cb_scr[tt + SUBLANES - (CONV_W - 1):tt + SUBLANES, :]
    cb_scr[0:SUBLANES, :] = cb_scr[tt:tt + SUBLANES, :]
    _lru_inputs(xc, wg_ref, bg_ref, lam_ref, a_scr, u_scr)
    h_last = _scan_rows(a_scr, u_scr, h_scr, 0, tt, hlast_scr[...])
    hlast_scr[...] = h_last
    ph_ref[0] = h_last
    lru_ref[0] = (h_scr[...] * jax.nn.silu(znat[:, W_LRU:2 * W_LRU])).astype(BF16)

    k = _head_norm_rows(znat[:, 2 * W_LRU:], ones_ref[...], kg_ref[...])
    kbf_ref[0] = k.astype(BF16)
    q3 = zt[:W_ATT].reshape(N_HEADS, HEAD_DIM, tt)
    ms = jnp.mean(q3 * q3, axis=1, keepdims=True)
    qn = q3 * lax.rsqrt(ms + EPS) * (qg_ref[...] * (HEAD_DIM ** -0.5))
    qt_ref[0] = qn.reshape(W_ATT, tt).astype(BF16)
    vt = zt[W_ATT:2 * W_ATT]
    vt_ref[0] = vt.astype(BF16)
    gat_ref[0] = jax.nn.silu(zt[2 * W_ATT:])

    @pl.when(t >= n_tiles - keep_tiles)
    def _():
        pk_ref[0] = k
        pv_ref[0] = vt.T


def _front_prompt(x, ng, wnat, wt, cw, cbias, wg, bg, lam, qg, kg, ones_bd):
    b, s, _ = x.shape
    tt = QBLOCK
    n_tiles = s // tt
    keep = min(BAND, s)
    keep_tiles = keep // tt
    const = lambda shape: pl.BlockSpec(shape, lambda i, j: (0,) * len(shape))
    keep_map = lambda i, j: (i, jnp.maximum(j - (n_tiles - keep_tiles), 0), 0)
    kern = functools.partial(_front_prompt_kernel, n_tiles=n_tiles, keep_tiles=keep_tiles)
    return pl.pallas_call(
        kern,
        grid=(b, n_tiles),
        in_specs=[
            pl.BlockSpec((1, tt, D_MODEL), lambda i, j: (i, j, 0)),
            const((1, D_MODEL)),
            const((D_MODEL, 3 * W_LRU)),
            const((3 * W_ATT, D_MODEL)),
            const((CONV_W, W_LRU)),
            const((1, W_LRU)),
            const((W_LRU, 2 * W_LRU)),
            const((1, 2 * W_LRU)),
            const((1, W_LRU)),
            const((HEAD_DIM, 1)),
            const((1, W_ATT)),
            const((W_ATT, W_ATT)),
        ],
        out_specs=[
            pl.BlockSpec((1, tt, W_LRU), lambda i, j: (i, j, 0)),
            pl.BlockSpec((1, W_ATT, tt), lambda i, j: (i, 0, j)),
            pl.BlockSpec((1, tt, W_ATT), lambda i, j: (i, j, 0)),
            pl.BlockSpec((1, W_ATT, tt), lambda i, j: (i, 0, j)),
            pl.BlockSpec((1, W_ATT, tt), lambda i, j: (i, 0, j)),
            pl.BlockSpec((1, tt, W_ATT), keep_map),
            pl.BlockSpec((1, tt, W_ATT), keep_map),
            pl.BlockSpec((1, CONV_W - 1, W_LRU), lambda i, j: (i, 0, 0)),
            pl.BlockSpec((1, 1, W_LRU), lambda i, j: (i, 0, 0)),
        ],
        out_shape=[
            jax.ShapeDtypeStruct((b, s, W_LRU), BF16),
            jax.ShapeDtypeStruct((b, W_ATT, s), BF16),
            jax.ShapeDtypeStruct((b, s, W_ATT), BF16),
            jax.ShapeDtypeStruct((b, W_ATT, s), BF16),
            jax.ShapeDtypeStruct((b, W_ATT, s), F32),
            jax.ShapeDtypeStruct((b, keep, W_ATT), F32),
            jax.ShapeDtypeStruct((b, keep, W_ATT), F32),
            jax.ShapeDtypeStruct((b, CONV_W - 1, W_LRU), F32),
            jax.ShapeDtypeStruct((b, 1, W_LRU), F32),
        ],
        scratch_shapes=[
            pltpu.VMEM((tt + SUBLANES, W_LRU), F32),
            pltpu.VMEM((tt, W_LRU), F32),
            pltpu.VMEM((tt, W_LRU), F32),
            pltpu.VMEM((tt, W_LRU), F32),
            pltpu.VMEM((1, W_LRU), F32),
        ],
        compiler_params=pltpu.CompilerParams(
            dimension_semantics=("arbitrary", "arbitrary"), vmem_limit_bytes=VMEM_LIMIT),
        name="front_prompt",
    )(x, ng, wnat, wt, cw, cbias, wg, bg, lam, qg, kg, ones_bd)


def _attn_prompt_kernel(qt_ref, k0_ref, k1_ref, k2_ref, v0_ref, v1_ref, v2_ref, bias_ref, gat_ref,
                        out_ref, s_scr, p_scr, att_scr):
    j = pl.program_id(1)
    k_refs = (k0_ref, k1_ref, k2_ref)
    v_refs = (v0_ref, v1_ref, v2_ref)
    rows = lax.broadcasted_iota(jnp.int32, (HEAD_GROUP, QBLOCK), 0)
    for h in range(N_HEADS):
        g, hl = divmod(h, HEADS_PER_GROUP)
        gsl = slice(g * HEAD_GROUP, (g + 1) * HEAD_GROUP)
        in_head = (rows >= hl * HEAD_DIM) & (rows < (hl + 1) * HEAD_DIM)
        qm = jnp.where(in_head, qt_ref[0, gsl, :], jnp.zeros((), BF16))
        for i in range(KEY_TILES):
            ksl = slice(i * QBLOCK, (i + 1) * QBLOCK)
            s = _dot(k_refs[i][0, :, gsl], qm) + bias_ref[h, ksl, :]
            if i < KEY_TILES - 1:
                s = jnp.where(j >= KEY_TILES - 1 - i, s, NEG)
            s_scr[ksl, :] = s
        s_all = s_scr[...]
        m = jnp.max(s_all, axis=0, keepdims=True)
        p = jnp.exp(s_all - m)
        l = jnp.sum(p, axis=0, keepdims=True)
        p_scr[...] = p.astype(BF16)
        hsl = slice(h * HEAD_DIM, (h + 1) * HEAD_DIM)
        o = _dot(v_refs[0][0, hsl, :], p_scr[0:QBLOCK, :])
        for i in range(1, KEY_TILES):
            o = o + _dot(v_refs[i][0, hsl, :], p_scr[i * QBLOCK:(i + 1) * QBLOCK, :])
        att_scr[hsl, :] = o / l
    out_ref[0] = (att_scr[...] * gat_ref[0]).T.astype(BF16)


def _attn_prompt(qt, kbf, vt, bias_t, gat):
    b, _, s = qt.shape
    n_blocks = s // QBLOCK
    back = KEY_TILES - 1
    k_specs = [pl.BlockSpec((1, QBLOCK, W_ATT),
                            functools.partial(lambda i, j, d: (i, jnp.maximum(j - d, 0), 0), d=back - n))
               for n in range(KEY_TILES)]
    v_specs = [pl.BlockSpec((1, W_ATT, QBLOCK),
                            functools.partial(lambda i, j, d: (i, 0, jnp.maximum(j - d, 0)), d=back - n))
               for n in range(KEY_TILES)]
    return pl.pallas_call(
        _attn_prompt_kernel,
        grid=(b, n_blocks),
        in_specs=[pl.BlockSpec((1, W_ATT, QBLOCK), lambda i, j: (i, 0, j))] + k_specs + v_specs + [
            pl.BlockSpec((N_HEADS, KEY_TILES * QBLOCK, QBLOCK), lambda i, j: (0, 0, 0)),
            pl.BlockSpec((1, W_ATT, QBLOCK), lambda i, j: (i, 0, j)),
        ],
        out_specs=pl.BlockSpec((1, QBLOCK, W_ATT), lambda i, j: (i, j, 0)),
        out_shape=jax.ShapeDtypeStruct((b, s, W_ATT), BF16),
        scratch_shapes=[
            pltpu.VMEM((KEY_TILES * QBLOCK, QBLOCK), F32),
            pltpu.VMEM((KEY_TILES * QBLOCK, QBLOCK), BF16),
            pltpu.VMEM((W_ATT, QBLOCK), F32),
        ],
        compiler_params=pltpu.CompilerParams(
            dimension_semantics=("arbitrary", "arbitrary"), vmem_limit_bytes=VMEM_LIMIT),
        name="attn_prompt",
    )(qt, kbf, kbf, kbf, vt, vt, vt, bias_t, gat)


def _back_kernel(x_ref, lru_ref, att_ref, p_ref, wo_ref, pg_ref, wpg_ref, wpe_ref, y_ref):
    mix = _dot(lru_ref[...], wo_ref[0:W_LRU, :]) + _dot(att_ref[...], wo_ref[W_LRU:, :])
    h = x_ref[...] + mix
    gate = jax.nn.sigmoid(_dot(_rms_rows(h, pg_ref[...]).astype(BF16), wpg_ref[...]))
    y_ref[...] = h + _dot(p_ref[...].astype(BF16), wpe_ref[...]) * gate


def _back(x2, lru2, att2, p2, wo, pg, wpg, wpe, rows):
    n = x2.shape[0]
    const = lambda shape: pl.BlockSpec(shape, lambda i: (0,) * len(shape))
    return pl.pallas_call(
        _back_kernel,
        grid=(n // rows,),
        in_specs=[
            pl.BlockSpec((rows, D_MODEL), lambda i: (i, 0)),
            pl.BlockSpec((rows, W_LRU), lambda i: (i, 0)),
            pl.BlockSpec((rows, W_ATT), lambda i: (i, 0)),
            pl.BlockSpec((rows, PLE_DIM), lambda i: (i, 0)),
            const((W_LRU + W_ATT, D_MODEL)),
            const((1, D_MODEL)),
            const((D_MODEL, D_MODEL)),
            const((PLE_DIM, D_MODEL)),
        ],
        out_specs=pl.BlockSpec((rows, D_MODEL), lambda i: (i, 0)),
        out_shape=jax.ShapeDtypeStruct((n, D_MODEL), F32),
        compiler_params=pltpu.CompilerParams(
            dimension_semantics=("arbitrary",), vmem_limit_bytes=VMEM_LIMIT),
        name="back",
    )(x2, lru2, att2, p2, wo, pg, wpg, wpe)


def _front_sample_kernel(x_ref, ng_ref, win_ref, cw_ref, cbias_ref, wg_ref, bg_ref, lam_ref,
                         qg_ref, kg_ref, ones_ref, sconv_ref, slru_ref,
                         lru_ref, q_ref, k_ref, v_ref, ga_ref, sc_ref, sh_ref,
                         cb_scr, xc_scr, a_scr, u_scr, h_scr, *, nb, tt):
    seg = tt + SUBLANES
    xn = _rms_rows(x_ref[...], ng_ref[...]).astype(BF16)
    z = _dot(xn, win_ref[...])
    xl = z[:, :W_LRU]
    for s in range(nb):
        cb_scr[s * seg:s * seg + SUBLANES, :] = sconv_ref[s]
        cb_scr[s * seg + SUBLANES:(s + 1) * seg, :] = xl[s * tt:(s + 1) * tt, :]
        xc_scr[s * tt:(s + 1) * tt, :] = _conv_rows(cb_scr, s * seg, tt, cw_ref, cbias_ref[...])
        sc_ref[s] = cb_scr[(s + 1) * seg - (CONV_W - 1):(s + 1) * seg, :]
    _lru_inputs(xc_scr[...], wg_ref, bg_ref, lam_ref, a_scr, u_scr)
    for s in range(nb):
        sh_ref[s] = _scan_rows(a_scr, u_scr, h_scr, s * tt, tt, slru_ref[s])
    lru_ref[...] = (h_scr[...] * jax.nn.silu(z[:, W_LRU:2 * W_LRU])).astype(BF16)

    o = 2 * W_LRU
    q = _head_norm_rows(z[:, o:o + W_ATT], ones_ref[...], qg_ref[...])
    q_ref[...] = (q * (HEAD_DIM ** -0.5)).astype(BF16)
    k_ref[...] = _head_norm_rows(z[:, o + W_ATT:o + 2 * W_ATT], ones_ref[...], kg_ref[...])
    v_ref[...] = z[:, o + 2 * W_ATT:o + 3 * W_ATT]
    ga_ref[...] = jax.nn.silu(z[:, o + 3 * W_ATT:])


def _front_sample(x2, ng, win, cw, cbias, wg, bg, lam, qg_t, kg_t, ones_bd, sconv_pad, slru, nb, tt):
    n = x2.shape[0]
    rows = nb * tt
    const = lambda shape: pl.BlockSpec(shape, lambda i: (0,) * len(shape))
    row_spec = lambda w: pl.BlockSpec((rows, w), lambda i: (i, 0))
    nseq = n // tt
    kern = functools.partial(_front_sample_kernel, nb=nb, tt=tt)
    return pl.pallas_call(
        kern,
        grid=(n // rows,),
        in_specs=[
            row_spec(D_MODEL),
            const((1, D_MODEL)),
            const((D_MODEL, 2 * W_LRU + 4 * W_ATT)),
            const((CONV_W, W_LRU)),
            const((1, W_LRU)),
            const((W_LRU, 2 * W_LRU)),
            const((1, 2 * W_LRU)),
            const((1, W_LRU)),
            const((1, W_ATT)),
            const((1, W_ATT)),
            const((W_ATT, W_ATT)),
            pl.BlockSpec((nb, SUBLANES, W_LRU), lambda i: (i, 0, 0)),
            pl.BlockSpec((nb, 1, W_LRU), lambda i: (i, 0, 0)),
        ],
        out_specs=[
            row_spec(W_LRU), row_spec(W_ATT), row_spec(W_ATT), row_spec(W_ATT), row_spec(W_ATT),
            pl.BlockSpec((nb, CONV_W - 1, W_LRU), lambda i: (i, 0, 0)),
            pl.BlockSpec((nb, 1, W_LRU), lambda i: (i, 0, 0)),
        ],
        out_shape=[
            jax.ShapeDtypeStruct((n, W_LRU), BF16),
            jax.ShapeDtypeStruct((n, W_ATT), BF16),
            jax.ShapeDtypeStruct((n, W_ATT), F32),
            jax.ShapeDtypeStruct((n, W_ATT), F32),
            jax.ShapeDtypeStruct((n, W_ATT), F32),
            jax.ShapeDtypeStruct((nseq, CONV_W - 1, W_LRU), F32),
            jax.ShapeDtypeStruct((nseq, 1, W_LRU), F32),
        ],
        scratch_shapes=[
            pltpu.VMEM((nb * (tt + SUBLANES), W_LRU), F32),
            pltpu.VMEM((rows, W_LRU), F32),
            pltpu.VMEM((rows, W_LRU), F32),
            pltpu.VMEM((rows, W_LRU), F32),
            pltpu.VMEM((rows, W_LRU), F32),
        ],
        compiler_params=pltpu.CompilerParams(
            dimension_semantics=("arbitrary",), vmem_limit_bytes=VMEM_LIMIT),
        name="front_sample",
    )(x2, ng, win, cw, cbias, wg, bg, lam, qg_t, kg_t, ones_bd, sconv_pad, slru)


def _attn_sample_kernel(q_ref, kn_ref, vn_ref, kc_ref, vc_ref, bc_ref, bn_ref, ga_ref, out_ref):
    tt = q_ref.shape[1]
    lanes = lax.broadcasted_iota(jnp.int32, (tt, HEAD_GROUP), 1)
    masks = [(lanes >= hl * HEAD_DIM) & (lanes < (hl + 1) * HEAD_DIM) for hl in range(HEADS_PER_GROUP)]
    for g in range(N_GROUPS):
        gsl = slice(g * HEAD_GROUP, (g + 1) * HEAD_GROUP)
        qg = q_ref[0, :, gsl]
        qs = jnp.concatenate([jnp.where(m, qg, jnp.zeros((), BF16)) for m in masks], axis=0)
        sc = _dot_nt(qs, kc_ref[0, :, gsl].astype(BF16)) + bc_ref[g]
        sn = _dot_nt(qs, kn_ref[0, :, gsl].astype(BF16)) + bn_ref[g]
        m = jnp.maximum(jnp.max(sc, axis=-1, keepdims=True), jnp.max(sn, axis=-1, keepdims=True))
        pc = jnp.exp(sc - m)
        pn = jnp.exp(sn - m)
        l = jnp.sum(pc, axis=-1, keepdims=True) + jnp.sum(pn, axis=-1, keepdims=True)
        o = _dot(pc.astype(BF16), vc_ref[0, :, gsl].astype(BF16))
        o = (o + _dot(pn.astype(BF16), vn_ref[0, :, gsl].astype(BF16))) / l
        att = jnp.zeros((tt, HEAD_GROUP), F32)
        for hl in range(HEADS_PER_GROUP):
            att = att + jnp.where(masks[hl], o[hl * tt:(hl + 1) * tt, :], 0.0)
        out_ref[0, :, gsl] = (att * ga_ref[0, :, gsl]).astype(BF16)


def _attn_sample(q3, k3, v3, kc, vc, bias_c, bias_n, ga3):
    b, tt, _ = q3.shape
    l = kc.shape[1]
    seq = lambda r: pl.BlockSpec((1, r, W_ATT), lambda i: (i, 0, 0))
    stacked = HEADS_PER_GROUP * tt
    return pl.pallas_call(
        _attn_sample_kernel,
        grid=(b,),
        in_specs=[
            seq(tt), seq(tt), seq(tt), seq(l), seq(l),
            pl.BlockSpec((N_GROUPS, stacked, l), lambda i: (0, 0, 0)),
            pl.BlockSpec((N_GROUPS, stacked, tt), lambda i: (0, 0, 0)),
            seq(tt),
        ],
        out_specs=seq(tt),
        out_shape=jax.ShapeDtypeStruct((b, tt, W_ATT), BF16),
        compiler_params=pltpu.CompilerParams(
            dimension_semantics=("arbitrary",), vmem_limit_bytes=VMEM_LIMIT),
        name="attn_sample",
    )(q3, k3, v3, kc, vc, bias_c, bias_n, ga3)


def _block_diag(w):
    n, d, e = w.shape
    eye = jnp.eye(n, dtype=w.dtype)
    return (eye[:, None, :, None] * w[:, :, None, :]).reshape(n * d, n * e)


def _rel_bias(table, q_pos, k_pos):
    idx = jnp.clip(q_pos[:, None] - k_pos[None, :], -MAX_REL, MAX_REL) + MAX_REL
    return jnp.transpose(table[idx], (2, 0, 1)).astype(F32)


def _prompt_bias_t(table):
    k_pos = jnp.arange(KEY_TILES * QBLOCK)
    q_pos = jnp.arange(QBLOCK) + BAND
    bias = _rel_bias(table, q_pos, k_pos)
    dc = q_pos[:, None] // CHUNK - k_pos[None, :] // CHUNK
    bias = jnp.where(((dc >= 0) & (dc <= PAST_CHUNKS))[None], bias, NEG)
    return jnp.transpose(bias, (0, 2, 1))


def _sample_bias(table, tt, l):
    bias = _rel_bias(table, jnp.arange(tt) + l, jnp.arange(l + tt))
    bias = bias.reshape(N_GROUPS, HEADS_PER_GROUP * tt, l + tt)
    return bias[:, :, :l], bias[:, :, l:]


def kernel(x_prompt, x_sample, p_prompt, p_sample, cache_k, cache_v, state_conv, state_lru, norm_g, w_in, conv_w, conv_b, gate_a_w, gate_a_b, gate_x_w, gate_x_b, lru_lambda, q_norm_g, k_norm_g, rel_bias, w_out, ple_norm_g, w_ple_gate, w_ple_proj):
    depth = w_in.shape[0]
    b, s, _ = x_prompt.shape
    db, ds, _ = x_sample.shape
    lc = cache_k.shape[2]
    yp, ys = x_prompt, x_sample.reshape(db * ds, D_MODEL)
    ones_bd = _block_diag(jnp.full((N_HEADS, HEAD_DIM, HEAD_DIM), 1.0 / HEAD_DIM, F32)).astype(BF16)
    outs = [[] for _ in range(8)]
    sample_nb = 8
    for l in range(depth):
        win = w_in[l].astype(BF16)
        o = 2 * W_LRU
        wnat = jnp.concatenate([win[:, :o], win[:, o + W_ATT:o + 2 * W_ATT]], axis=1)
        wt = jnp.concatenate([win[:, o:o + W_ATT], win[:, o + 2 * W_ATT:]], axis=1).T
        ng = norm_g[l].reshape(1, D_MODEL)
        cw = conv_w[l]
        cbias = conv_b[l].reshape(1, W_LRU)
        wg = jnp.concatenate([_block_diag(gate_a_w[l]), _block_diag(gate_x_w[l])], axis=1).astype(BF16)
        bg = jnp.concatenate([gate_a_b[l], gate_x_b[l]]).reshape(1, 2 * W_LRU)
        lam = lru_lambda[l].reshape(1, W_LRU)
        qg_col = q_norm_g[l].reshape(HEAD_DIM, 1)
        qg_t = jnp.tile(q_norm_g[l], N_HEADS).reshape(1, W_ATT)
        kg_t = jnp.tile(k_norm_g[l], N_HEADS).reshape(1, W_ATT)
        wo = w_out[l].astype(BF16)
        pg = ple_norm_g[l].reshape(1, D_MODEL)
        wpg = w_ple_gate[l].astype(BF16)
        wpe = w_ple_proj[l].astype(BF16)

        lru_g, qt, kbf, vt, gat, pk, pv, pc, ph = _front_prompt(
            yp, ng, wnat, wt, cw, cbias, wg, bg, lam, qg_col, kg_t, ones_bd)
        att_g = _attn_prompt(qt, kbf, vt, _prompt_bias_t(rel_bias[l]), gat)
        yp = _back(yp.reshape(b * s, D_MODEL), lru_g.reshape(b * s, W_LRU), att_g.reshape(b * s, W_ATT),
                   p_prompt[l].reshape(b * s, PLE_DIM), wo, pg, wpg, wpe, 512).reshape(b, s, D_MODEL)
        keep = pk.shape[1]
        outs[0].append(pk.reshape(b, keep, N_HEADS, HEAD_DIM))
        outs[1].append(pv.reshape(b, keep, N_HEADS, HEAD_DIM))
        outs[2].append(pc)
        outs[3].append(ph.reshape(b, W_LRU))

        sconv_pad = jnp.pad(state_conv[l], ((0, 0), (SUBLANES - (CONV_W - 1), 0), (0, 0)))
        lru_s, q_s, k_s, v_s, ga_s, sc, sh = _front_sample(
            ys, ng, win, cw, cbias, wg, bg, lam, qg_t, kg_t, ones_bd,
            sconv_pad, state_lru[l].reshape(db, 1, W_LRU), sample_nb, ds)
        bias_c, bias_n = _sample_bias(rel_bias[l], ds, lc)
        att_s = _attn_sample(q_s.reshape(db, ds, W_ATT), k_s.reshape(db, ds, W_ATT), v_s.reshape(db, ds, W_ATT),
                             cache_k[l].reshape(db, lc, W_ATT), cache_v[l].reshape(db, lc, W_ATT),
                             bias_c, bias_n, ga_s.reshape(db, ds, W_ATT))
        ys = _back(ys, lru_s, att_s.reshape(db * ds, W_ATT), p_sample[l].reshape(db * ds, PLE_DIM),
                   wo, pg, wpg, wpe, 256)
        outs[4].append(k_s.reshape(db, ds, N_HEADS, HEAD_DIM))
        outs[5].append(v_s.reshape(db, ds, N_HEADS, HEAD_DIM))
        outs[6].append(sc)
        outs[7].append(sh.reshape(db, W_LRU))
    return (yp, ys.reshape(db, ds, D_MODEL)) + tuple(jnp.stack(o) for o in outs)
```

```python
import functools

import jax
import jax.numpy as jnp
from jax import lax
from jax.experimental import pallas as pl
from jax.experimental.pallas import tpu as pltpu

D_MODEL = 1024
CHUNK = 64
PAST_CHUNKS = 8
BAND = PAST_CHUNKS * CHUNK
W_LRU = D_MODEL // 2
LRU_BLOCKS = 8
LRU_BLOCK = W_LRU // LRU_BLOCKS
CONV_W = 4
RG_C = 8.0
HEAD_DIM = 64
W_ATT = D_MODEL // 2
N_HEADS = W_ATT // HEAD_DIM
MAX_REL = 256
PLE_DIM = 256
EPS = 1e-6
NEG = -1e30

SUBLANES = 8
LANES = 128
HEAD_GROUP = 256
HEADS_PER_GROUP = HEAD_GROUP // HEAD_DIM
N_GROUPS = W_ATT // HEAD_GROUP
QBLOCK = 256
KEY_TILES = BAND // QBLOCK + 1
PROMPT_BIAS_PERIOD = (KEY_TILES + 1) * QBLOCK
VMEM_LIMIT = 48 * 1024 * 1024

F32 = jnp.float32
BF16 = jnp.bfloat16


def _dot(a, b):
    return jnp.dot(a, b, preferred_element_type=F32)


def _dot_nt(a, b):
    return lax.dot_general(a, b, (((1,), (1,)), ((), ())), preferred_element_type=F32)


def _rms_rows(x, g):
    ms = jnp.mean(x * x, axis=-1, keepdims=True)
    return x * lax.rsqrt(ms + EPS) * g


def _head_norm_rows(x, ones_bd, g_tiled):
    x2 = x * x
    hi = x2.astype(BF16)
    lo = (x2 - hi.astype(F32)).astype(BF16)
    ms = _dot(hi, ones_bd) + _dot(lo, ones_bd)
    return x * lax.rsqrt(ms + EPS) * g_tiled


def _scan_rows(a_ref, u_ref, h_ref, row0, nrows, h0):
    ridx = lax.broadcasted_iota(jnp.int32, (SUBLANES, W_LRU), 0)

    def body(i, hprev):
        r = pl.multiple_of(row0 + i * SUBLANES, SUBLANES)
        a = a_ref[pl.ds(r, SUBLANES), :]
        u = u_ref[pl.ds(r, SUBLANES), :]
        for s in (1, 2, 4):
            a_s = jnp.where(ridx >= s, pltpu.roll(a, s, 0), 1.0)
            u_s = jnp.where(ridx >= s, pltpu.roll(u, s, 0), 0.0)
            u = a * u_s + u
            a = a * a_s
        h = a * hprev + u
        h_ref[pl.ds(r, SUBLANES), :] = h
        return h[SUBLANES - 1:SUBLANES, :]

    return lax.fori_loop(0, nrows // SUBLANES, body, h0)


def _lru_inputs(xc, wg_ref, bg_ref, lam_ref, a_ref, u_ref):
    gates = _dot(xc.astype(BF16), wg_ref[...]) + bg_ref[...]
    r = jax.nn.sigmoid(gates[:, :W_LRU])
    i = jax.nn.sigmoid(gates[:, W_LRU:])
    log_a = -RG_C * r * jax.nn.softplus(-lam_ref[...])
    a = jnp.exp(log_a)
    a_ref[...] = a
    u_ref[...] = jnp.sqrt(jnp.tanh(-log_a) * (1.0 + a * a)) * (i * xc)


def _conv_rows(cb_ref, base, nrows, cw_ref, cb_bias):
    out = cb_bias + cw_ref[CONV_W - 1:CONV_W, :] * cb_ref[pl.ds(base + SUBLANES, nrows), :]
    for k in range(CONV_W - 1):
        shift = CONV_W - 1 - k
        out = out + cw_ref[k:k + 1, :] * cb_ref[pl.ds(base + SUBLANES - shift, nrows), :]
    return out


def _front_prompt_kernel(x_ref, ng_ref, wnat_ref, wt_ref, cw_ref, cbias_ref, wg_ref, bg_ref, lam_ref,
                         qg_ref, kg_ref, ones_ref,
                         lru_ref, qt_ref, kbf_ref, vt_ref, gat_ref, pk_ref, pv_ref, pc_ref, ph_ref,
                         cb_scr, a_scr, u_scr, h_scr, hlast_scr, *, n_tiles, keep_tiles):
    t = pl.program_id(1)
    tt = x_ref.shape[1]

    @pl.when(t == 0)
    def _():
        cb_scr[0:SUBLANES, :] = jnp.zeros((SUBLANES, W_LRU), F32)
        hlast_scr[...] = jnp.zeros((1, W_LRU), F32)

    xn = _rms_rows(x_ref[0], ng_ref[...]).astype(BF16)
    znat = _dot(xn, wnat_ref[...])
    zt = _dot_nt(wt_ref[...], xn)

    cb_scr[SUBLANES:SUBLANES + tt, :] = znat[:, :W_LRU]
    xc = _conv_rows(cb_scr, 0, tt, cw_ref, cbias_ref[...])
    pc_ref[0] = cb_scr[tt + SUBLANES - (CONV_W - 1):tt + SUBLANES, :]
    cb_scr[0:SUBLANES, :] = cb_scr[tt:tt + SUBLANES, :]
    _lru_inputs(xc, wg_ref, bg_ref, lam_ref, a_scr, u_scr)
    h_last = _scan_rows(a_scr, u_scr, h_scr, 0, tt, hlast_scr[...])
    hlast_scr[...] = h_last
    ph_ref[0] = h_last
    lru_ref[0] = (h_scr[...] * jax.nn.silu(znat[:, W_LRU:2 * W_LRU])).astype(BF16)

    k = _head_norm_rows(znat[:, 2 * W_LRU:], ones_ref[...], kg_ref[...])
    kbf_ref[0] = k.astype(BF16)
    q3 = zt[:W_ATT].reshape(N_HEADS, HEAD_DIM, tt)
    ms = jnp.mean(q3 * q3, axis=1, keepdims=True)
    qn = q3 * lax.rsqrt(ms + EPS) * (qg_ref[...] * (HEAD_DIM ** -0.5))
    qt_ref[0] = qn.reshape(W_ATT, tt).astype(BF16)
    vt = zt[W_ATT:2 * W_ATT]
    vt_ref[0] = vt.astype(BF16)
    gat_ref[0] = jax.nn.silu(zt[2 * W_ATT:])

    @pl.when(t >= n_tiles - keep_tiles)
    def _():
        pk_ref[0] = k
        pv_ref[0] = vt.T


def _front_prompt(x, ng, wnat, wt, cw, cbias, wg, bg, lam, qg, kg, ones_bd):
    b, s, _ = x.shape
    tt = QBLOCK
    n_tiles = s // tt
    keep = min(BAND, s)
    keep_tiles = keep // tt
    const = lambda shape: pl.BlockSpec(shape, lambda i, j: (0,) * len(shape))
    keep_map = lambda i, j: (i, jnp.maximum(j - (n_tiles - keep_tiles), 0), 0)
    kern = functools.partial(_front_prompt_kernel, n_tiles=n_tiles, keep_tiles=keep_tiles)
    return pl.pallas_call(
        kern,
        grid=(b, n_tiles),
        in_specs=[
            pl.BlockSpec((1, tt, D_MODEL), lambda i, j: (i, j, 0)),
            const((1, D_MODEL)),
            const((D_MODEL, 3 * W_LRU)),
            const((3 * W_ATT, D_MODEL)),
            const((CONV_W, W_LRU)),
            const((1, W_LRU)),
            const((W_LRU, 2 * W_LRU)),
            const((1, 2 * W_LRU)),
            const((1, W_LRU)),
            const((HEAD_DIM, 1)),
            const((1, W_ATT)),
            const((W_ATT, W_ATT)),
        ],
        out_specs=[
            pl.BlockSpec((1, tt, W_LRU), lambda i, j: (i, j, 0)),
            pl.BlockSpec((1, W_ATT, tt), lambda i, j: (i, 0, j)),
            pl.BlockSpec((1, tt, W_ATT), lambda i, j: (i, j, 0)),
            pl.BlockSpec((1, W_ATT, tt), lambda i, j: (i, 0, j)),
            pl.BlockSpec((1, W_ATT, tt), lambda i, j: (i, 0, j)),
            pl.BlockSpec((1, tt, W_ATT), keep_map),
            pl.BlockSpec((1, tt, W_ATT), keep_map),
            pl.BlockSpec((1, CONV_W - 1, W_LRU), lambda i, j: (i, 0, 0)),
            pl.BlockSpec((1, 1, W_LRU), lambda i, j: (i, 0, 0)),
        ],
        out_shape=[
            jax.ShapeDtypeStruct((b, s, W_LRU), BF16),
            jax.ShapeDtypeStruct((b, W_ATT, s), BF16),
            jax.ShapeDtypeStruct((b, s, W_ATT), BF16),
            jax.ShapeDtypeStruct((b, W_ATT, s), BF16),
            jax.ShapeDtypeStruct((b, W_ATT, s), F32),
            jax.ShapeDtypeStruct((b, keep, W_ATT), F32),
            jax.ShapeDtypeStruct((b, keep, W_ATT), F32),
            jax.ShapeDtypeStruct((b, CONV_W - 1, W_LRU), F32),
            jax.ShapeDtypeStruct((b, 1, W_LRU), F32),
        ],
        scratch_shapes=[
            pltpu.VMEM((tt + SUBLANES, W_LRU), F32),
            pltpu.VMEM((tt, W_LRU), F32),
            pltpu.VMEM((tt, W_LRU), F32),
            pltpu.VMEM((tt, W_LRU), F32),
            pltpu.VMEM((1, W_LRU), F32),
        ],
        compiler_params=pltpu.CompilerParams(
            dimension_semantics=("arbitrary", "arbitrary"), vmem_limit_bytes=VMEM_LIMIT),
        name="front_prompt",
    )(x, ng, wnat, wt, cw, cbias, wg, bg, lam, qg, kg, ones_bd)


def _toeplitz_rows(w_row, nrows, row0):
    x = jnp.broadcast_to(w_row, (nrows, w_row.shape[1]))
    return pltpu.roll(x, row0, 1, stride=1, stride_axis=0)


def _fill_prompt_bias(w_ref, bias_ref):
    q_chunk = (lax.broadcasted_iota(jnp.int32, (CHUNK, QBLOCK), 1) + BAND) // CHUNK
    for h in range(N_HEADS):
        def body(n, carry, h=h):
            r0 = pl.multiple_of(n * CHUNK, CHUNK)
            t = _toeplitz_rows(w_ref[h], CHUNK, r0)[:, :QBLOCK]
            dc = q_chunk - n
            bias_ref[h, pl.ds(r0, CHUNK), :] = jnp.where((dc >= 0) & (dc <= PAST_CHUNKS), t, NEG)
            return carry
        lax.fori_loop(0, KEY_TILES * QBLOCK // CHUNK, body, 0)


def _attn_prompt_kernel(qt_ref, k0_ref, k1_ref, k2_ref, v0_ref, v1_ref, v2_ref, w_ref, gat_ref,
                        out_ref, bias_ref, s_scr, p_scr, att_scr):
    j = pl.program_id(1)

    @pl.when((pl.program_id(0) == 0) & (j == 0))
    def _():
        _fill_prompt_bias(w_ref, bias_ref)

    k_refs = (k0_ref, k1_ref, k2_ref)
    v_refs = (v0_ref, v1_ref, v2_ref)
    rows = lax.broadcasted_iota(jnp.int32, (HEAD_GROUP, QBLOCK), 0)
    for h in range(N_HEADS):
        g, hl = divmod(h, HEADS_PER_GROUP)
        gsl = slice(g * HEAD_GROUP, (g + 1) * HEAD_GROUP)
        in_head = (rows >= hl * HEAD_DIM) & (rows < (hl + 1) * HEAD_DIM)
        qm = jnp.where(in_head, qt_ref[0, gsl, :], jnp.zeros((), BF16))
        for i in range(KEY_TILES):
            ksl = slice(i * QBLOCK, (i + 1) * QBLOCK)
            s = _dot(k_refs[i][0, :, gsl], qm) + bias_ref[h, ksl, :]
            if i < KEY_TILES - 1:
                s = jnp.where(j >= KEY_TILES - 1 - i, s, NEG)
            s_scr[ksl, :] = s
        s_all = s_scr[...]
        m = jnp.max(s_all, axis=0, keepdims=True)
        p = jnp.exp(s_all - m)
        l = jnp.sum(p, axis=0, keepdims=True)
        p_scr[...] = p.astype(BF16)
        hsl = slice(h * HEAD_DIM, (h + 1) * HEAD_DIM)
        o = _dot(v_refs[0][0, hsl, :], p_scr[0:QBLOCK, :])
        for i in range(1, KEY_TILES):
            o = o + _dot(v_refs[i][0, hsl, :], p_scr[i * QBLOCK:(i + 1) * QBLOCK, :])
        att_scr[hsl, :] = o / l
    out_ref[0] = (att_scr[...] * gat_ref[0]).T.astype(BF16)


def _attn_prompt(qt, kbf, vt, bias_t, gat):
    b, _, s = qt.shape
    n_blocks = s // QBLOCK
    back = KEY_TILES - 1
    k_specs = [pl.BlockSpec((1, QBLOCK, W_ATT),
                            functools.partial(lambda i, j, d: (i, jnp.maximum(j - d, 0), 0), d=back - n))
               for n in range(KEY_TILES)]
    v_specs = [pl.BlockSpec((1, W_ATT, QBLOCK),
                            functools.partial(lambda i, j, d: (i, 0, jnp.maximum(j - d, 0)), d=back - n))
               for n in range(KEY_TILES)]
    return pl.pallas_call(
        _attn_prompt_kernel,
        grid=(b, n_blocks),
        in_specs=[pl.BlockSpec((1, W_ATT, QBLOCK), lambda i, j: (i, 0, j))] + k_specs + v_specs + [
            pl.BlockSpec((N_HEADS, 1, PROMPT_BIAS_PERIOD), lambda i, j: (0, 0, 0)),
            pl.BlockSpec((1, W_ATT, QBLOCK), lambda i, j: (i, 0, j)),
        ],
        out_specs=pl.BlockSpec((1, QBLOCK, W_ATT), lambda i, j: (i, j, 0)),
        out_shape=jax.ShapeDtypeStruct((b, s, W_ATT), BF16),
        scratch_shapes=[
            pltpu.VMEM((N_HEADS, KEY_TILES * QBLOCK, QBLOCK), F32),
            pltpu.VMEM((KEY_TILES * QBLOCK, QBLOCK), F32),
            pltpu.VMEM((KEY_TILES * QBLOCK, QBLOCK), BF16),
            pltpu.VMEM((W_ATT, QBLOCK), F32),
        ],
        compiler_params=pltpu.CompilerParams(
            dimension_semantics=("arbitrary", "arbitrary"), vmem_limit_bytes=VMEM_LIMIT),
        name="attn_prompt",
    )(qt, kbf, kbf, kbf, vt, vt, vt, bias_t, gat)


def _back_kernel(x_ref, lru_ref, att_ref, p_ref, wo_ref, pg_ref, wpg_ref, wpe_ref, y_ref):
    mix = _dot(lru_ref[...], wo_ref[0:W_LRU, :]) + _dot(att_ref[...], wo_ref[W_LRU:, :])
    h = x_ref[...] + mix
    gate = jax.nn.sigmoid(_dot(_rms_rows(h, pg_ref[...]).astype(BF16), wpg_ref[...]))
    y_ref[...] = h + _dot(p_ref[...].astype(BF16), wpe_ref[...]) * gate


def _back(x2, lru2, att2, p2, wo, pg, wpg, wpe, rows):
    n = x2.shape[0]
    const = lambda shape: pl.BlockSpec(shape, lambda i: (0,) * len(shape))
    return pl.pallas_call(
        _back_kernel,
        grid=(n // rows,),
        in_specs=[
            pl.BlockSpec((rows, D_MODEL), lambda i: (i, 0)),
            pl.BlockSpec((rows, W_LRU), lambda i: (i, 0)),
            pl.BlockSpec((rows, W_ATT), lambda i: (i, 0)),
            pl.BlockSpec((rows, PLE_DIM), lambda i: (i, 0)),
            const((W_LRU + W_ATT, D_MODEL)),
            const((1, D_MODEL)),
            const((D_MODEL, D_MODEL)),
            const((PLE_DIM, D_MODEL)),
        ],
        out_specs=pl.BlockSpec((rows, D_MODEL), lambda i: (i, 0)),
        out_shape=jax.ShapeDtypeStruct((n, D_MODEL), F32),
        compiler_params=pltpu.CompilerParams(
            dimension_semantics=("arbitrary",), vmem_limit_bytes=VMEM_LIMIT),
        name="back",
    )(x2, lru2, att2, p2, wo, pg, wpg, wpe)


def _front_sample_kernel(x_ref, ng_ref, win_ref, cw_ref, cbias_ref, wg_ref, bg_ref, lam_ref,
                         qg_ref, kg_ref, ones_ref, sconv_ref, slru_ref,
                         lru_ref, q_ref, k_ref, v_ref, ga_ref, sc_ref, sh_ref,
                         cb_scr, xc_scr, a_scr, u_scr, h_scr, *, nb, tt):
    seg = tt + SUBLANES
    xn = _rms_rows(x_ref[...], ng_ref[...]).astype(BF16)
    z = _dot(xn, win_ref[...])
    xl = z[:, :W_LRU]
    for s in range(nb):
        cb_scr[s * seg:s * seg + SUBLANES, :] = sconv_ref[s]
        cb_scr[s * seg + SUBLANES:(s + 1) * seg, :] = xl[s * tt:(s + 1) * tt, :]
        xc_scr[s * tt:(s + 1) * tt, :] = _conv_rows(cb_scr, s * seg, tt, cw_ref, cbias_ref[...])
        sc_ref[s] = cb_scr[(s + 1) * seg - (CONV_W - 1):(s + 1) * seg, :]
    _lru_inputs(xc_scr[...], wg_ref, bg_ref, lam_ref, a_scr, u_scr)
    for s in range(nb):
        sh_ref[s] = _scan_rows(a_scr, u_scr, h_scr, s * tt, tt, slru_ref[s])
    lru_ref[...] = (h_scr[...] * jax.nn.silu(z[:, W_LRU:2 * W_LRU])).astype(BF16)

    o = 2 * W_LRU
    q = _head_norm_rows(z[:, o:o + W_ATT], ones_ref[...], qg_ref[...])
    q_ref[...] = (q * (HEAD_DIM ** -0.5)).astype(BF16)
    k_ref[...] = _head_norm_rows(z[:, o + W_ATT:o + 2 * W_ATT], ones_ref[...], kg_ref[...])
    v_ref[...] = z[:, o + 2 * W_ATT:o + 3 * W_ATT]
    ga_ref[...] = jax.nn.silu(z[:, o + 3 * W_ATT:])


def _front_sample(x2, ng, win, cw, cbias, wg, bg, lam, qg_t, kg_t, ones_bd, sconv_pad, slru, nb, tt):
    n = x2.shape[0]
    rows = nb * tt
    const = lambda shape: pl.BlockSpec(shape, lambda i: (0,) * len(shape))
    row_spec = lambda w: pl.BlockSpec((rows, w), lambda i: (i, 0))
    nseq = n // tt
    kern = functools.partial(_front_sample_kernel, nb=nb, tt=tt)
    return pl.pallas_call(
        kern,
        grid=(n // rows,),
        in_specs=[
            row_spec(D_MODEL),
            const((1, D_MODEL)),
            const((D_MODEL, 2 * W_LRU + 4 * W_ATT)),
            const((CONV_W, W_LRU)),
            const((1, W_LRU)),
            const((W_LRU, 2 * W_LRU)),
            const((1, 2 * W_LRU)),
            const((1, W_LRU)),
            const((1, W_ATT)),
            const((1, W_ATT)),
            const((W_ATT, W_ATT)),
            pl.BlockSpec((nb, SUBLANES, W_LRU), lambda i: (i, 0, 0)),
            pl.BlockSpec((nb, 1, W_LRU), lambda i: (i, 0, 0)),
        ],
        out_specs=[
            row_spec(W_LRU), row_spec(W_ATT), row_spec(W_ATT), row_spec(W_ATT), row_spec(W_ATT),
            pl.BlockSpec((nb, CONV_W - 1, W_LRU), lambda i: (i, 0, 0)),
            pl.BlockSpec((nb, 1, W_LRU), lambda i: (i, 0, 0)),
        ],
        out_shape=[
            jax.ShapeDtypeStruct((n, W_LRU), BF16),
            jax.ShapeDtypeStruct((n, W_ATT), BF16),
            jax.ShapeDtypeStruct((n, W_ATT), F32),
            jax.ShapeDtypeStruct((n, W_ATT), F32),
            jax.ShapeDtypeStruct((n, W_ATT), F32),
            jax.ShapeDtypeStruct((nseq, CONV_W - 1, W_LRU), F32),
            jax.ShapeDtypeStruct((nseq, 1, W_LRU), F32),
        ],
        scratch_shapes=[
            pltpu.VMEM((nb * (tt + SUBLANES), W_LRU), F32),
            pltpu.VMEM((rows, W_LRU), F32),
            pltpu.VMEM((rows, W_LRU), F32),
            pltpu.VMEM((rows, W_LRU), F32),
            pltpu.VMEM((rows, W_LRU), F32),
        ],
        compiler_params=pltpu.CompilerParams(
            dimension_semantics=("arbitrary",), vmem_limit_bytes=VMEM_LIMIT),
        name="front_sample",
    )(x2, ng, win, cw, cbias, wg, bg, lam, qg_t, kg_t, ones_bd, sconv_pad, slru)


def _attn_sample_kernel(q_ref, kn_ref, vn_ref, kc_ref, vc_ref, w_ref, ga_ref, out_ref, bias_ref):
    tt = q_ref.shape[1]
    lc = kc_ref.shape[1]

    @pl.when(pl.program_id(0) == 0)
    def _():
        for h in range(N_HEADS):
            g, hl = divmod(h, HEADS_PER_GROUP)
            bias_ref[g, hl * tt:(hl + 1) * tt, :] = _toeplitz_rows(w_ref[h], tt, 0)

    lanes = lax.broadcasted_iota(jnp.int32, (tt, HEAD_GROUP), 1)
    masks = [(lanes >= hl * HEAD_DIM) & (lanes < (hl + 1) * HEAD_DIM) for hl in range(HEADS_PER_GROUP)]
    for g in range(N_GROUPS):
        gsl = slice(g * HEAD_GROUP, (g + 1) * HEAD_GROUP)
        qg = q_ref[0, :, gsl]
        qs = jnp.concatenate([jnp.where(m, qg, jnp.zeros((), BF16)) for m in masks], axis=0)
        sc = _dot_nt(qs, kc_ref[0, :, gsl].astype(BF16)) + bias_ref[g, :, 0:lc]
        sn = _dot_nt(qs, kn_ref[0, :, gsl].astype(BF16)) + bias_ref[g, :, lc:lc + tt]
        m = jnp.maximum(jnp.max(sc, axis=-1, keepdims=True), jnp.max(sn, axis=-1, keepdims=True))
        pc = jnp.exp(sc - m)
        pn = jnp.exp(sn - m)
        l = jnp.sum(pc, axis=-1, keepdims=True) + jnp.sum(pn, axis=-1, keepdims=True)
        o = _dot(pc.astype(BF16), vc_ref[0, :, gsl].astype(BF16))
        o = (o + _dot(pn.astype(BF16), vn_ref[0, :, gsl].astype(BF16))) / l
        att = jnp.zeros((tt, HEAD_GROUP), F32)
        for hl in range(HEADS_PER_GROUP):
            att = att + jnp.where(masks[hl], o[hl * tt:(hl + 1) * tt, :], 0.0)
        out_ref[0, :, gsl] = (att * ga_ref[0, :, gsl]).astype(BF16)


def _attn_sample(q3, k3, v3, kc, vc, w_bias, ga3):
    b, tt, _ = q3.shape
    l = kc.shape[1]
    period = w_bias.shape[-1]
    seq = lambda r: pl.BlockSpec((1, r, W_ATT), lambda i: (i, 0, 0))
    return pl.pallas_call(
        _attn_sample_kernel,
        grid=(b,),
        in_specs=[
            seq(tt), seq(tt), seq(tt), seq(l), seq(l),
            pl.BlockSpec((N_HEADS, 1, period), lambda i: (0, 0, 0)),
            seq(tt),
        ],
        out_specs=seq(tt),
        out_shape=jax.ShapeDtypeStruct((b, tt, W_ATT), BF16),
        scratch_shapes=[pltpu.VMEM((N_GROUPS, HEADS_PER_GROUP * tt, period), F32)],
        compiler_params=pltpu.CompilerParams(
            dimension_semantics=("arbitrary",), vmem_limit_bytes=VMEM_LIMIT),
        name="attn_sample",
    )(q3, k3, v3, kc, vc, w_bias, ga3)


def _block_diag(w):
    n, d, e = w.shape
    eye = jnp.eye(n, dtype=w.dtype)
    return (eye[:, None, :, None] * w[:, :, None, :]).reshape(n * d, n * e)


def _prompt_bias_period(table):
    assert BAND - MAX_REL == MAX_REL and PROMPT_BIAS_PERIOD == KEY_TILES * QBLOCK + QBLOCK
    last = table[2 * MAX_REL:]
    neg_d = jnp.concatenate([table, jnp.broadcast_to(last, (MAX_REL - 1, N_HEADS))])
    w = jnp.concatenate([jnp.broadcast_to(last, (QBLOCK, N_HEADS)), neg_d])
    return w.T.reshape(N_HEADS, 1, PROMPT_BIAS_PERIOD).astype(F32)


def _sample_bias_period(table, tt, l):
    assert l >= MAX_REL
    period = -(-(l + 2 * tt - 1) // LANES) * LANES
    last = table[2 * MAX_REL:]
    n_var = tt + MAX_REL - 1
    var = table[2 * MAX_REL - 1:2 * MAX_REL - 1 - n_var:-1]
    w = jnp.concatenate([jnp.broadcast_to(last, (l - MAX_REL + 1, N_HEADS)), var,
                         jnp.broadcast_to(last, (period - (l + tt), N_HEADS))])
    return w.T.reshape(N_HEADS, 1, period).astype(F32)


def kernel(x_prompt, x_sample, p_prompt, p_sample, cache_k, cache_v, state_conv, state_lru, norm_g, w_in, conv_w, conv_b, gate_a_w, gate_a_b, gate_x_w, gate_x_b, lru_lambda, q_norm_g, k_norm_g, rel_bias, w_out, ple_norm_g, w_ple_gate, w_ple_proj):
    depth = w_in.shape[0]
    b, s, _ = x_prompt.shape
    db, ds, _ = x_sample.shape
    lc = cache_k.shape[2]
    yp, ys = x_prompt, x_sample.reshape(db * ds, D_MODEL)
    ones_bd = _block_diag(jnp.full((N_HEADS, HEAD_DIM, HEAD_DIM), 1.0 / HEAD_DIM, F32)).astype(BF16)
    outs = [[] for _ in range(8)]
    sample_nb = 8
    for l in range(depth):
        win = w_in[l].astype(BF16)
        o = 2 * W_LRU
        wnat = jnp.concatenate([win[:, :o], win[:, o + W_ATT:o + 2 * W_ATT]], axis=1)
        wt = jnp.concatenate([win[:, o:o + W_ATT], win[:, o + 2 * W_ATT:]], axis=1).T
        ng = norm_g[l].reshape(1, D_MODEL)
        cw = conv_w[l]
        cbias = conv_b[l].reshape(1, W_LRU)
        wg = jnp.concatenate([_block_diag(gate_a_w[l]), _block_diag(gate_x_w[l])], axis=1).astype(BF16)
        bg = jnp.concatenate([gate_a_b[l], gate_x_b[l]]).reshape(1, 2 * W_LRU)
        lam = lru_lambda[l].reshape(1, W_LRU)
        qg_col = q_norm_g[l].reshape(HEAD_DIM, 1)
        qg_t = jnp.tile(q_norm_g[l], N_HEADS).reshape(1, W_ATT)
        kg_t = jnp.tile(k_norm_g[l], N_HEADS).reshape(1, W_ATT)
        wo = w_out[l].astype(BF16)
        pg = ple_norm_g[l].reshape(1, D_MODEL)
        wpg = w_ple_gate[l].astype(BF16)
        wpe = w_ple_proj[l].astype(BF16)

        lru_g, qt, kbf, vt, gat, pk, pv, pc, ph = _front_prompt(
            yp, ng, wnat, wt, cw, cbias, wg, bg, lam, qg_col, kg_t, ones_bd)
        att_g = _attn_prompt(qt, kbf, vt, _prompt_bias_period(rel_bias[l]), gat)
        yp = _back(yp.reshape(b * s, D_MODEL), lru_g.reshape(b * s, W_LRU), att_g.reshape(b * s, W_ATT),
                   p_prompt[l].reshape(b * s, PLE_DIM), wo, pg, wpg, wpe, 512).reshape(b, s, D_MODEL)
        keep = pk.shape[1]
        outs[0].append(pk.reshape(b, keep, N_HEADS, HEAD_DIM))
        outs[1].append(pv.reshape(b, keep, N_HEADS, HEAD_DIM))
        outs[2].append(pc)
        outs[3].append(ph.reshape(b, W_LRU))

        sconv_pad = jnp.pad(state_conv[l], ((0, 0), (SUBLANES - (CONV_W - 1), 0), (0, 0)))
        lru_s, q_s, k_s, v_s, ga_s, sc, sh = _front_sample(
            ys, ng, win, cw, cbias, wg, bg, lam, qg_t, kg_t, ones_bd,
            sconv_pad, state_lru[l].reshape(db, 1, W_LRU), sample_nb, ds)
        att_s = _attn_sample(q_s.reshape(db, ds, W_ATT), k_s.reshape(db, ds, W_ATT), v_s.reshape(db, ds, W_ATT),
                             cache_k[l].reshape(db, lc, W_ATT), cache_v[l].reshape(db, lc, W_ATT),
                             _sample_bias_period(rel_bias[l], ds, lc), ga_s.reshape(db, ds, W_ATT))
        ys = _back(ys, lru_s, att_s.reshape(db * ds, W_ATT), p_sample[l].reshape(db * ds, PLE_DIM),
                   wo, pg, wpg, wpe, 256)
        outs[4].append(k_s.reshape(db, ds, N_HEADS, HEAD_DIM))
        outs[5].append(v_s.reshape(db, ds, N_HEADS, HEAD_DIM))
        outs[6].append(sc)
        outs[7].append(sh.reshape(db, W_LRU))
    return (yp, ys.reshape(db, ds, D_MODEL)) + tuple(jnp.stack(o) for o in outs)
```

```python
import functools

import jax
import jax.numpy as jnp
from jax import lax
from jax.experimental import pallas as pl
from jax.experimental.pallas import tpu as pltpu

D_MODEL = 1024
CHUNK = 64
PAST_CHUNKS = 8
BAND = PAST_CHUNKS * CHUNK
W_LRU = D_MODEL // 2
LRU_BLOCKS = 8
LRU_BLOCK = W_LRU // LRU_BLOCKS
CONV_W = 4
RG_C = 8.0
HEAD_DIM = 64
W_ATT = D_MODEL // 2
N_HEADS = W_ATT // HEAD_DIM
MAX_REL = 256
PLE_DIM = 256
EPS = 1e-6
NEG = -1e30
LOG2E = 1.4426950408889634

SUBLANES = 8
LANES = 128
HEAD_GROUP = 256
HEADS_PER_GROUP = HEAD_GROUP // HEAD_DIM
N_GROUPS = W_ATT // HEAD_GROUP
QBLOCK = 256
KEY_TILES = BAND // QBLOCK + 1
PROMPT_BIAS_PERIOD = (KEY_TILES + 1) * QBLOCK
VMEM_LIMIT = 48 * 1024 * 1024

F32 = jnp.float32
BF16 = jnp.bfloat16


def _dot(a, b):
    return jnp.dot(a, b, preferred_element_type=F32)


def _dot_nt(a, b):
    return lax.dot_general(a, b, (((1,), (1,)), ((), ())), preferred_element_type=F32)


def _rms_rows(x, g):
    ms = jnp.mean(x * x, axis=-1, keepdims=True)
    return x * lax.rsqrt(ms + EPS) * g


def _head_norm_rows(x, ones_bd, g_tiled):
    x2 = x * x
    hi = x2.astype(BF16)
    lo = (x2 - hi.astype(F32)).astype(BF16)
    ms = _dot(hi, ones_bd) + _dot(lo, ones_bd)
    return x * lax.rsqrt(ms + EPS) * g_tiled


def _scan_rows(a_ref, u_ref, h_ref, row0, nrows, h0):
    ridx = lax.broadcasted_iota(jnp.int32, (SUBLANES, W_LRU), 0)

    def body(i, hprev):
        r = pl.multiple_of(row0 + i * SUBLANES, SUBLANES)
        a = a_ref[pl.ds(r, SUBLANES), :]
        u = u_ref[pl.ds(r, SUBLANES), :]
        for s in (1, 2, 4):
            a_s = jnp.where(ridx >= s, pltpu.roll(a, s, 0), 1.0)
            u_s = jnp.where(ridx >= s, pltpu.roll(u, s, 0), 0.0)
            u = a * u_s + u
            a = a * a_s
        h = a * hprev + u
        h_ref[pl.ds(r, SUBLANES), :] = h
        return h[SUBLANES - 1:SUBLANES, :]

    return lax.fori_loop(0, nrows // SUBLANES, body, h0)


def _lru_inputs(xc, wg_ref, bg_ref, lam_ref, a_ref, u_ref):
    gates = _dot(xc.astype(BF16), wg_ref[...]) + bg_ref[...]
    r = jax.nn.sigmoid(gates[:, :W_LRU])
    i = jax.nn.sigmoid(gates[:, W_LRU:])
    log_a = -RG_C * r * jax.nn.softplus(-lam_ref[...])
    a = jnp.exp(log_a)
    a_ref[...] = a
    u_ref[...] = jnp.sqrt(jnp.tanh(-log_a) * (1.0 + a * a)) * (i * xc)


def _conv_rows(cb_ref, base, nrows, cw_ref, cb_bias):
    out = cb_bias + cw_ref[CONV_W - 1:CONV_W, :] * cb_ref[pl.ds(base + SUBLANES, nrows), :]
    for k in range(CONV_W - 1):
        shift = CONV_W - 1 - k
        out = out + cw_ref[k:k + 1, :] * cb_ref[pl.ds(base + SUBLANES - shift, nrows), :]
    return out


def _front_prompt_kernel(x_ref, ng_ref, wnat_ref, wt_ref, cw_ref, cbias_ref, wg_ref, bg_ref, lam_ref,
                         qg_ref, kg_ref, ones_ref,
                         lru_ref, qt_ref, kbf_ref, vt_ref, gat_ref, pk_ref, pv_ref, pc_ref, ph_ref,
                         cb_scr, a_scr, u_scr, h_scr, hlast_scr, *, n_tiles, keep_tiles):
    t = pl.program_id(1)
    tt = x_ref.shape[1]

    @pl.when(t == 0)
    def _():
        cb_scr[0:SUBLANES, :] = jnp.zeros((SUBLANES, W_LRU), F32)
        hlast_scr[...] = jnp.zeros((1, W_LRU), F32)

    xn = _rms_rows(x_ref[0], ng_ref[...]).astype(BF16)
    znat = _dot(xn, wnat_ref[...])
    zt = _dot_nt(wt_ref[...], xn)

    cb_scr[SUBLANES:SUBLANES + tt, :] = znat[:, :W_LRU]
    xc = _conv_rows(cb_scr, 0, tt, cw_ref, cbias_ref[...])
    pc_ref[0] = cb_scr[tt + SUBLANES - (CONV_W - 1):tt + SUBLANES, :]
    cb_scr[0:SUBLANES, :] = cb_scr[tt:tt + SUBLANES, :]
    _lru_inputs(xc, wg_ref, bg_ref, lam_ref, a_scr, u_scr)
    h_last = _scan_rows(a_scr, u_scr, h_scr, 0, tt, hlast_scr[...])
    hlast_scr[...] = h_last
    ph_ref[0] = h_last
    lru_ref[0] = (h_scr[...] * jax.nn.silu(znat[:, W_LRU:2 * W_LRU])).astype(BF16)

    k = _head_norm_rows(znat[:, 2 * W_LRU:], ones_ref[...], kg_ref[...])
    kbf_ref[0] = k.astype(BF16)
    q3 = zt[:W_ATT].reshape(N_HEADS, HEAD_DIM, tt)
    ms = jnp.mean(q3 * q3, axis=1, keepdims=True)
    qn = q3 * lax.rsqrt(ms + EPS) * (qg_ref[...] * (HEAD_DIM ** -0.5 * LOG2E))
    qt_ref[0] = qn.reshape(W_ATT, tt).astype(BF16)
    vt = zt[W_ATT:2 * W_ATT]
    vt_ref[0] = vt.astype(BF16)
    gat_ref[0] = jax.nn.silu(zt[2 * W_ATT:])

    @pl.when(t >= n_tiles - keep_tiles)
    def _():
        pk_ref[0] = k
        pv_ref[0] = vt.T


def _front_prompt(x, ng, wnat, wt, cw, cbias, wg, bg, lam, qg, kg, ones_bd):
    b, s, _ = x.shape
    tt = QBLOCK
    n_tiles = s // tt
    keep = min(BAND, s)
    keep_tiles = keep // tt
    const = lambda shape: pl.BlockSpec(shape, lambda i, j: (0,) * len(shape))
    keep_map = lambda i, j: (i, jnp.maximum(j - (n_tiles - keep_tiles), 0), 0)
    kern = functools.partial(_front_prompt_kernel, n_tiles=n_tiles, keep_tiles=keep_tiles)
    return pl.pallas_call(
        kern,
        grid=(b, n_tiles),
        in_specs=[
            pl.BlockSpec((1, tt, D_MODEL), lambda i, j: (i, j, 0)),
            const((1, D_MODEL)),
            const((D_MODEL, 3 * W_LRU)),
            const((3 * W_ATT, D_MODEL)),
            const((CONV_W, W_LRU)),
            const((1, W_LRU)),
            const((W_LRU, 2 * W_LRU)),
            const((1, 2 * W_LRU)),
            const((1, W_LRU)),
            const((HEAD_DIM, 1)),
            const((1, W_ATT)),
            const((W_ATT, W_ATT)),
        ],
        out_specs=[
            pl.BlockSpec((1, tt, W_LRU), lambda i, j: (i, j, 0)),
            pl.BlockSpec((1, W_ATT, tt), lambda i, j: (i, 0, j)),
            pl.BlockSpec((1, tt, W_ATT), lambda i, j: (i, j, 0)),
            pl.BlockSpec((1, W_ATT, tt), lambda i, j: (i, 0, j)),
            pl.BlockSpec((1, W_ATT, tt), lambda i, j: (i, 0, j)),
            pl.BlockSpec((1, tt, W_ATT), keep_map),
            pl.BlockSpec((1, tt, W_ATT), keep_map),
            pl.BlockSpec((1, CONV_W - 1, W_LRU), lambda i, j: (i, 0, 0)),
            pl.BlockSpec((1, 1, W_LRU), lambda i, j: (i, 0, 0)),
        ],
        out_shape=[
            jax.ShapeDtypeStruct((b, s, W_LRU), BF16),
            jax.ShapeDtypeStruct((b, W_ATT, s), BF16),
            jax.ShapeDtypeStruct((b, s, W_ATT), BF16),
            jax.ShapeDtypeStruct((b, W_ATT, s), BF16),
            jax.ShapeDtypeStruct((b, W_ATT, s), F32),
            jax.ShapeDtypeStruct((b, keep, W_ATT), F32),
            jax.ShapeDtypeStruct((b, keep, W_ATT), F32),
            jax.ShapeDtypeStruct((b, CONV_W - 1, W_LRU), F32),
            jax.ShapeDtypeStruct((b, 1, W_LRU), F32),
        ],
        scratch_shapes=[
            pltpu.VMEM((tt + SUBLANES, W_LRU), F32),
            pltpu.VMEM((tt, W_LRU), F32),
            pltpu.VMEM((tt, W_LRU), F32),
            pltpu.VMEM((tt, W_LRU), F32),
            pltpu.VMEM((1, W_LRU), F32),
        ],
        compiler_params=pltpu.CompilerParams(
            dimension_semantics=("arbitrary", "arbitrary"), vmem_limit_bytes=VMEM_LIMIT),
        name="front_prompt",
    )(x, ng, wnat, wt, cw, cbias, wg, bg, lam, qg, kg, ones_bd)


def _toeplitz_rows(w_row, nrows, row0):
    x = jnp.broadcast_to(w_row, (nrows, w_row.shape[1]))
    return pltpu.roll(x, row0, 1, stride=1, stride_axis=0)


def _fill_prompt_bias(w_ref, bias_ref):
    q_chunk = (lax.broadcasted_iota(jnp.int32, (CHUNK, QBLOCK), 1) + BAND) // CHUNK
    for h in range(N_HEADS):
        def body(n, carry, h=h):
            r0 = pl.multiple_of(n * CHUNK, CHUNK)
            t = _toeplitz_rows(w_ref[h], CHUNK, r0)[:, :QBLOCK]
            dc = q_chunk - n
            bias_ref[h, pl.ds(r0, CHUNK), :] = jnp.where((dc >= 0) & (dc <= PAST_CHUNKS), t * LOG2E, NEG)
            return carry
        lax.fori_loop(0, KEY_TILES * QBLOCK // CHUNK, body, 0)


CHUNKS_PER_TILE = QBLOCK // CHUNK
FRAMES_PER_VREG = LANES // CHUNK


def _lane_cols(kc):
    cols = []
    for c in range(QBLOCK // LANES):
        q_lo = PAST_CHUNKS + c * FRAMES_PER_VREG
        q_hi = q_lo + FRAMES_PER_VREG - 1
        if q_lo - PAST_CHUNKS <= kc <= q_hi:
            cols.append(c)
    return cols


def _fold_rows(x):
    return x.reshape(x.shape[0] // SUBLANES, SUBLANES, x.shape[1])


def _attn_prompt_heads(tiles, qt_ref, k_refs, v_refs, bias_ref, s_scr, p_scr, att_scr):
    n_cols = QBLOCK // LANES
    rows = lax.broadcasted_iota(jnp.int32, (HEAD_GROUP, QBLOCK), 0)
    pieces = []
    for i in tiles:
        for cc in range(CHUNKS_PER_TILE):
            kc = i * CHUNKS_PER_TILE + cc
            for c in _lane_cols(kc):
                pieces.append((i, cc, slice(kc * CHUNK, (kc + 1) * CHUNK), slice(c * LANES, (c + 1) * LANES), c))

    def scores(h):
        g, hl = divmod(h, HEADS_PER_GROUP)
        gsl = slice(g * HEAD_GROUP, (g + 1) * HEAD_GROUP)
        in_head = (rows >= hl * HEAD_DIM) & (rows < (hl + 1) * HEAD_DIM)
        qm = jnp.where(in_head, qt_ref[0, gsl, :], jnp.zeros((), BF16))
        s = {i: _dot(k_refs[i][0, :, gsl], qm) for i in tiles}
        m_acc = [jnp.full((SUBLANES, LANES), NEG, F32) for _ in range(n_cols)]
        for i, cc, rsl, lsl, c in pieces:
            sp = s[i][cc * CHUNK:(cc + 1) * CHUNK, lsl] + bias_ref[h, rsl, lsl]
            s_scr[h % 2, rsl, lsl] = sp
            m_acc[c] = jnp.maximum(m_acc[c], jnp.max(_fold_rows(sp), axis=0))
        return [jnp.max(a, axis=0, keepdims=True) for a in m_acc]

    def weights(h, m):
        l_acc = [jnp.zeros((SUBLANES, LANES), F32) for _ in range(n_cols)]
        for _, _, rsl, lsl, c in pieces:
            p = jnp.exp2(s_scr[h % 2, rsl, lsl] - m[c])
            l_acc[c] = l_acc[c] + jnp.sum(_fold_rows(p), axis=0)
            p_scr[h % 2, rsl, lsl] = p.astype(BF16)
        return [1.0 / jnp.sum(a, axis=0, keepdims=True) for a in l_acc]

    def values(h, l_inv):
        hsl = slice(h * HEAD_DIM, (h + 1) * HEAD_DIM)
        o = None
        for i in tiles:
            oi = _dot(v_refs[i][0, hsl, :], p_scr[h % 2, i * QBLOCK:(i + 1) * QBLOCK, :])
            o = oi if o is None else o + oi
        for c in range(n_cols):
            lsl = slice(c * LANES, (c + 1) * LANES)
            att_scr[hsl, lsl] = o[:, lsl] * l_inv[c]

    m_next = scores(0)
    for h in range(N_HEADS):
        m = m_next
        if h + 1 < N_HEADS:
            m_next = scores(h + 1)
        values(h, weights(h, m))


def _attn_prompt_kernel(qt_ref, k0_ref, k1_ref, k2_ref, v0_ref, v1_ref, v2_ref, w_ref, gat_ref,
                        out_ref, bias_ref, s_scr, p_scr, att_scr):
    j = pl.program_id(1)

    @pl.when((pl.program_id(0) == 0) & (j == 0))
    def _():
        _fill_prompt_bias(w_ref, bias_ref)
        p_scr[...] = jnp.zeros(p_scr.shape, BF16)

    k_refs = (k0_ref, k1_ref, k2_ref)
    v_refs = (v0_ref, v1_ref, v2_ref)
    for first in range(KEY_TILES):
        tiles = tuple(range(first, KEY_TILES))
        cond = (j == KEY_TILES - 1 - first) if first > 0 else (j >= KEY_TILES - 1)

        @pl.when(cond)
        def _(tiles=tiles):
            _attn_prompt_heads(tiles, qt_ref, k_refs, v_refs, bias_ref, s_scr, p_scr, att_scr)

    out_ref[0] = (att_scr[...] * gat_ref[0]).T.astype(BF16)


def _attn_prompt(qt, kbf, vt, bias_t, gat):
    b, _, s = qt.shape
    n_blocks = s // QBLOCK
    back = KEY_TILES - 1
    k_specs = [pl.BlockSpec((1, QBLOCK, W_ATT),
                            functools.partial(lambda i, j, d: (i, jnp.maximum(j - d, 0), 0), d=back - n))
               for n in range(KEY_TILES)]
    v_specs = [pl.BlockSpec((1, W_ATT, QBLOCK),
                            functools.partial(lambda i, j, d: (i, 0, jnp.maximum(j - d, 0)), d=back - n))
               for n in range(KEY_TILES)]
    return pl.pallas_call(
        _attn_prompt_kernel,
        grid=(b, n_blocks),
        in_specs=[pl.BlockSpec((1, W_ATT, QBLOCK), lambda i, j: (i, 0, j))] + k_specs + v_specs + [
            pl.BlockSpec((N_HEADS, 1, PROMPT_BIAS_PERIOD), lambda i, j: (0, 0, 0)),
            pl.BlockSpec((1, W_ATT, QBLOCK), lambda i, j: (i, 0, j)),
        ],
        out_specs=pl.BlockSpec((1, QBLOCK, W_ATT), lambda i, j: (i, j, 0)),
        out_shape=jax.ShapeDtypeStruct((b, s, W_ATT), BF16),
        scratch_shapes=[
            pltpu.VMEM((N_HEADS, KEY_TILES * QBLOCK, QBLOCK), F32),
            pltpu.VMEM((2, KEY_TILES * QBLOCK, QBLOCK), F32),
            pltpu.VMEM((2, KEY_TILES * QBLOCK, QBLOCK), BF16),
            pltpu.VMEM((W_ATT, QBLOCK), F32),
        ],
        compiler_params=pltpu.CompilerParams(
            dimension_semantics=("arbitrary", "arbitrary"), vmem_limit_bytes=VMEM_LIMIT),
        name="attn_prompt",
    )(qt, kbf, kbf, kbf, vt, vt, vt, bias_t, gat)


def _back_kernel(x_ref, lru_ref, att_ref, p_ref, wo_ref, pg_ref, wpg_ref, wpe_ref, y_ref):
    mix = _dot(lru_ref[...], wo_ref[0:W_LRU, :]) + _dot(att_ref[...], wo_ref[W_LRU:, :])
    h = x_ref[...] + mix
    gate = jax.nn.sigmoid(_dot(_rms_rows(h, pg_ref[...]).astype(BF16), wpg_ref[...]))
    y_ref[...] = h + _dot(p_ref[...].astype(BF16), wpe_ref[...]) * gate


def _back(x2, lru2, att2, p2, wo, pg, wpg, wpe, rows):
    n = x2.shape[0]
    const = lambda shape: pl.BlockSpec(shape, lambda i: (0,) * len(shape))
    return pl.pallas_call(
        _back_kernel,
        grid=(n // rows,),
        in_specs=[
            pl.BlockSpec((rows, D_MODEL), lambda i: (i, 0)),
            pl.BlockSpec((rows, W_LRU), lambda i: (i, 0)),
            pl.BlockSpec((rows, W_ATT), lambda i: (i, 0)),
            pl.BlockSpec((rows, PLE_DIM), lambda i: (i, 0)),
            const((W_LRU + W_ATT, D_MODEL)),
            const((1, D_MODEL)),
            const((D_MODEL, D_MODEL)),
            const((PLE_DIM, D_MODEL)),
        ],
        out_specs=pl.BlockSpec((rows, D_MODEL), lambda i: (i, 0)),
        out_shape=jax.ShapeDtypeStruct((n, D_MODEL), F32),
        compiler_params=pltpu.CompilerParams(
            dimension_semantics=("arbitrary",), vmem_limit_bytes=VMEM_LIMIT),
        name="back",
    )(x2, lru2, att2, p2, wo, pg, wpg, wpe)


def _front_sample_kernel(x_ref, ng_ref, win_ref, cw_ref, cbias_ref, wg_ref, bg_ref, lam_ref,
                         qg_ref, kg_ref, ones_ref, sconv_ref, slru_ref,
                         lru_ref, q_ref, k_ref, v_ref, ga_ref, sc_ref, sh_ref,
                         cb_scr, xc_scr, a_scr, u_scr, h_scr, *, nb, tt):
    seg = tt + SUBLANES
    xn = _rms_rows(x_ref[...], ng_ref[...]).astype(BF16)
    z = _dot(xn, win_ref[...])
    xl = z[:, :W_LRU]
    for s in range(nb):
        cb_scr[s * seg:s * seg + SUBLANES, :] = sconv_ref[s]
        cb_scr[s * seg + SUBLANES:(s + 1) * seg, :] = xl[s * tt:(s + 1) * tt, :]
        xc_scr[s * tt:(s + 1) * tt, :] = _conv_rows(cb_scr, s * seg, tt, cw_ref, cbias_ref[...])
        sc_ref[s] = cb_scr[(s + 1) * seg - (CONV_W - 1):(s + 1) * seg, :]
    _lru_inputs(xc_scr[...], wg_ref, bg_ref, lam_ref, a_scr, u_scr)
    for s in range(nb):
        sh_ref[s] = _scan_rows(a_scr, u_scr, h_scr, s * tt, tt, slru_ref[s])
    lru_ref[...] = (h_scr[...] * jax.nn.silu(z[:, W_LRU:2 * W_LRU])).astype(BF16)

    o = 2 * W_LRU
    q = _head_norm_rows(z[:, o:o + W_ATT], ones_ref[...], qg_ref[...])
    q_ref[...] = (q * (HEAD_DIM ** -0.5)).astype(BF16)
    k_ref[...] = _head_norm_rows(z[:, o + W_ATT:o + 2 * W_ATT], ones_ref[...], kg_ref[...])
    v_ref[...] = z[:, o + 2 * W_ATT:o + 3 * W_ATT]
    ga_ref[...] = jax.nn.silu(z[:, o + 3 * W_ATT:])


def _front_sample(x2, ng, win, cw, cbias, wg, bg, lam, qg_t, kg_t, ones_bd, sconv_pad, slru, nb, tt):
    n = x2.shape[0]
    rows = nb * tt
    const = lambda shape: pl.BlockSpec(shape, lambda i: (0,) * len(shape))
    row_spec = lambda w: pl.BlockSpec((rows, w), lambda i: (i, 0))
    nseq = n // tt
    kern = functools.partial(_front_sample_kernel, nb=nb, tt=tt)
    return pl.pallas_call(
        kern,
        grid=(n // rows,),
        in_specs=[
            row_spec(D_MODEL),
            const((1, D_MODEL)),
            const((D_MODEL, 2 * W_LRU + 4 * W_ATT)),
            const((CONV_W, W_LRU)),
            const((1, W_LRU)),
            const((W_LRU, 2 * W_LRU)),
            const((1, 2 * W_LRU)),
            const((1, W_LRU)),
            const((1, W_ATT)),
            const((1, W_ATT)),
            const((W_ATT, W_ATT)),
            pl.BlockSpec((nb, SUBLANES, W_LRU), lambda i: (i, 0, 0)),
            pl.BlockSpec((nb, 1, W_LRU), lambda i: (i, 0, 0)),
        ],
        out_specs=[
            row_spec(W_LRU), row_spec(W_ATT), row_spec(W_ATT), row_spec(W_ATT), row_spec(W_ATT),
            pl.BlockSpec((nb, CONV_W - 1, W_LRU), lambda i: (i, 0, 0)),
            pl.BlockSpec((nb, 1, W_LRU), lambda i: (i, 0, 0)),
        ],
        out_shape=[
            jax.ShapeDtypeStruct((n, W_LRU), BF16),
            jax.ShapeDtypeStruct((n, W_ATT), BF16),
            jax.ShapeDtypeStruct((n, W_ATT), F32),
            jax.ShapeDtypeStruct((n, W_ATT), F32),
            jax.ShapeDtypeStruct((n, W_ATT), F32),
            jax.ShapeDtypeStruct((nseq, CONV_W - 1, W_LRU), F32),
            jax.ShapeDtypeStruct((nseq, 1, W_LRU), F32),
        ],
        scratch_shapes=[
            pltpu.VMEM((nb * (tt + SUBLANES), W_LRU), F32),
            pltpu.VMEM((rows, W_LRU), F32),
            pltpu.VMEM((rows, W_LRU), F32),
            pltpu.VMEM((rows, W_LRU), F32),
            pltpu.VMEM((rows, W_LRU), F32),
        ],
        compiler_params=pltpu.CompilerParams(
            dimension_semantics=("arbitrary",), vmem_limit_bytes=VMEM_LIMIT),
        name="front_sample",
    )(x2, ng, win, cw, cbias, wg, bg, lam, qg_t, kg_t, ones_bd, sconv_pad, slru)


def _attn_sample_kernel(q_ref, kn_ref, vn_ref, kc_ref, vc_ref, w_ref, ga_ref, out_ref, bias_ref):
    tt = q_ref.shape[1]
    lc = kc_ref.shape[1]

    @pl.when(pl.program_id(0) == 0)
    def _():
        for h in range(N_HEADS):
            g, hl = divmod(h, HEADS_PER_GROUP)
            bias_ref[g, hl * tt:(hl + 1) * tt, :] = _toeplitz_rows(w_ref[h], tt, 0)

    lanes = lax.broadcasted_iota(jnp.int32, (tt, HEAD_GROUP), 1)
    masks = [(lanes >= hl * HEAD_DIM) & (lanes < (hl + 1) * HEAD_DIM) for hl in range(HEADS_PER_GROUP)]
    for g in range(N_GROUPS):
        gsl = slice(g * HEAD_GROUP, (g + 1) * HEAD_GROUP)
        qg = q_ref[0, :, gsl]
        qs = jnp.concatenate([jnp.where(m, qg, jnp.zeros((), BF16)) for m in masks], axis=0)
        sc = _dot_nt(qs, kc_ref[0, :, gsl].astype(BF16)) + bias_ref[g, :, 0:lc]
        sn = _dot_nt(qs, kn_ref[0, :, gsl].astype(BF16)) + bias_ref[g, :, lc:lc + tt]
        m = jnp.maximum(jnp.max(sc, axis=-1, keepdims=True), jnp.max(sn, axis=-1, keepdims=True))
        pc = jnp.exp(sc - m)
        pn = jnp.exp(sn - m)
        l = jnp.sum(pc, axis=-1, keepdims=True) + jnp.sum(pn, axis=-1, keepdims=True)
        o = _dot(pc.astype(BF16), vc_ref[0, :, gsl].astype(BF16))
        o = (o + _dot(pn.astype(BF16), vn_ref[0, :, gsl].astype(BF16))) / l
        att = jnp.zeros((tt, HEAD_GROUP), F32)
        for hl in range(HEADS_PER_GROUP):
            att = att + jnp.where(masks[hl], o[hl * tt:(hl + 1) * tt, :], 0.0)
        out_ref[0, :, gsl] = (att * ga_ref[0, :, gsl]).astype(BF16)


def _attn_sample(q3, k3, v3, kc, vc, w_bias, ga3):
    b, tt, _ = q3.shape
    l = kc.shape[1]
    period = w_bias.shape[-1]
    seq = lambda r: pl.BlockSpec((1, r, W_ATT), lambda i: (i, 0, 0))
    return pl.pallas_call(
        _attn_sample_kernel,
        grid=(b,),
        in_specs=[
            seq(tt), seq(tt), seq(tt), seq(l), seq(l),
            pl.BlockSpec((N_HEADS, 1, period), lambda i: (0, 0, 0)),
            seq(tt),
        ],
        out_specs=seq(tt),
        out_shape=jax.ShapeDtypeStruct((b, tt, W_ATT), BF16),
        scratch_shapes=[pltpu.VMEM((N_GROUPS, HEADS_PER_GROUP * tt, period), F32)],
        compiler_params=pltpu.CompilerParams(
            dimension_semantics=("arbitrary",), vmem_limit_bytes=VMEM_LIMIT),
        name="attn_sample",
    )(q3, k3, v3, kc, vc, w_bias, ga3)


def _block_diag(w):
    n, d, e = w.shape
    eye = jnp.eye(n, dtype=w.dtype)
    return (eye[:, None, :, None] * w[:, :, None, :]).reshape(n * d, n * e)


def _prompt_bias_period(table):
    assert BAND - MAX_REL == MAX_REL and PROMPT_BIAS_PERIOD == KEY_TILES * QBLOCK + QBLOCK
    last = table[2 * MAX_REL:]
    neg_d = jnp.concatenate([table, jnp.broadcast_to(last, (MAX_REL - 1, N_HEADS))])
    w = jnp.concatenate([jnp.broadcast_to(last, (QBLOCK, N_HEADS)), neg_d])
    return w.T.reshape(N_HEADS, 1, PROMPT_BIAS_PERIOD).astype(F32)


def _sample_bias_period(table, tt, l):
    assert l >= MAX_REL
    period = -(-(l + 2 * tt - 1) // LANES) * LANES
    last = table[2 * MAX_REL:]
    n_var = tt + MAX_REL - 1
    var = table[2 * MAX_REL - 1:2 * MAX_REL - 1 - n_var:-1]
    w = jnp.concatenate([jnp.broadcast_to(last, (l - MAX_REL + 1, N_HEADS)), var,
                         jnp.broadcast_to(last, (period - (l + tt), N_HEADS))])
    return w.T.reshape(N_HEADS, 1, period).astype(F32)


def kernel(x_prompt, x_sample, p_prompt, p_sample, cache_k, cache_v, state_conv, state_lru, norm_g, w_in, conv_w, conv_b, gate_a_w, gate_a_b, gate_x_w, gate_x_b, lru_lambda, q_norm_g, k_norm_g, rel_bias, w_out, ple_norm_g, w_ple_gate, w_ple_proj):
    depth = w_in.shape[0]
    b, s, _ = x_prompt.shape
    db, ds, _ = x_sample.shape
    lc = cache_k.shape[2]
    yp, ys = x_prompt, x_sample.reshape(db * ds, D_MODEL)
    ones_bd = _block_diag(jnp.full((N_HEADS, HEAD_DIM, HEAD_DIM), 1.0 / HEAD_DIM, F32)).astype(BF16)
    outs = [[] for _ in range(8)]
    sample_nb = 8
    for l in range(depth):
        win = w_in[l].astype(BF16)
        o = 2 * W_LRU
        wnat = jnp.concatenate([win[:, :o], win[:, o + W_ATT:o + 2 * W_ATT]], axis=1)
        wt = jnp.concatenate([win[:, o:o + W_ATT], win[:, o + 2 * W_ATT:]], axis=1).T
        ng = norm_g[l].reshape(1, D_MODEL)
        cw = conv_w[l]
        cbias = conv_b[l].reshape(1, W_LRU)
        wg = jnp.concatenate([_block_diag(gate_a_w[l]), _block_diag(gate_x_w[l])], axis=1).astype(BF16)
        bg = jnp.concatenate([gate_a_b[l], gate_x_b[l]]).reshape(1, 2 * W_LRU)
        lam = lru_lambda[l].reshape(1, W_LRU)
        qg_col = q_norm_g[l].reshape(HEAD_DIM, 1)
        qg_t = jnp.tile(q_norm_g[l], N_HEADS).reshape(1, W_ATT)
        kg_t = jnp.tile(k_norm_g[l], N_HEADS).reshape(1, W_ATT)
        wo = w_out[l].astype(BF16)
        pg = ple_norm_g[l].reshape(1, D_MODEL)
        wpg = w_ple_gate[l].astype(BF16)
        wpe = w_ple_proj[l].astype(BF16)

        lru_g, qt, kbf, vt, gat, pk, pv, pc, ph = _front_prompt(
            yp, ng, wnat, wt, cw, cbias, wg, bg, lam, qg_col, kg_t, ones_bd)
        att_g = _attn_prompt(qt, kbf, vt, _prompt_bias_period(rel_bias[l]), gat)
        yp = _back(yp.reshape(b * s, D_MODEL), lru_g.reshape(b * s, W_LRU), att_g.reshape(b * s, W_ATT),
                   p_prompt[l].reshape(b * s, PLE_DIM), wo, pg, wpg, wpe, 512).reshape(b, s, D_MODEL)
        keep = pk.shape[1]
        outs[0].append(pk.reshape(b, keep, N_HEADS, HEAD_DIM))
        outs[1].append(pv.reshape(b, keep, N_HEADS, HEAD_DIM))
        outs[2].append(pc)
        outs[3].append(ph.reshape(b, W_LRU))

        sconv_pad = jnp.pad(state_conv[l], ((0, 0), (SUBLANES - (CONV_W - 1), 0), (0, 0)))
        lru_s, q_s, k_s, v_s, ga_s, sc, sh = _front_sample(
            ys, ng, win, cw, cbias, wg, bg, lam, qg_t, kg_t, ones_bd,
            sconv_pad, state_lru[l].reshape(db, 1, W_LRU), sample_nb, ds)
        att_s = _attn_sample(q_s.reshape(db, ds, W_ATT), k_s.reshape(db, ds, W_ATT), v_s.reshape(db, ds, W_ATT),
                             cache_k[l].reshape(db, lc, W_ATT), cache_v[l].reshape(db, lc, W_ATT),
                             _sample_bias_period(rel_bias[l], ds, lc), ga_s.reshape(db, ds, W_ATT))
        ys = _back(ys, lru_s, att_s.reshape(db * ds, W_ATT), p_sample[l].reshape(db * ds, PLE_DIM),
                   wo, pg, wpg, wpe, 256)
        outs[4].append(k_s.reshape(db, ds, N_HEADS, HEAD_DIM))
        outs[5].append(v_s.reshape(db, ds, N_HEADS, HEAD_DIM))
        outs[6].append(sc)
        outs[7].append(sh.reshape(db, W_LRU))
    return (yp, ys.reshape(db, ds, D_MODEL)) + tuple(jnp.stack(o) for o in outs)
```

```python
import functools

import jax
import jax.numpy as jnp
from jax import lax
from jax.experimental import pallas as pl
from jax.experimental.pallas import tpu as pltpu

D_MODEL = 1024
CHUNK = 64
PAST_CHUNKS = 8
BAND = PAST_CHUNKS * CHUNK
W_LRU = D_MODEL // 2
LRU_BLOCKS = 8
LRU_BLOCK = W_LRU // LRU_BLOCKS
CONV_W = 4
RG_C = 8.0
HEAD_DIM = 64
W_ATT = D_MODEL // 2
N_HEADS = W_ATT // HEAD_DIM
MAX_REL = 256
PLE_DIM = 256
EPS = 1e-6
NEG = -1e30
LOG2E = 1.4426950408889634

SUBLANES = 8
LANES = 128
HEAD_GROUP = 256
HEADS_PER_GROUP = HEAD_GROUP // HEAD_DIM
N_GROUPS = W_ATT // HEAD_GROUP
QBLOCK = 256
KEY_TILES = BAND // QBLOCK + 1
PROMPT_BIAS_PERIOD = (KEY_TILES + 1) * QBLOCK
FRONT_BLOCK = 1024
VMEM_LIMIT = 56 * 1024 * 1024

F32 = jnp.float32
BF16 = jnp.bfloat16


def _dot(a, b):
    return jnp.dot(a, b, preferred_element_type=F32)


def _dot_nt(a, b):
    return lax.dot_general(a, b, (((1,), (1,)), ((), ())), preferred_element_type=F32)


def _rms_rows(x, g):
    ms = jnp.mean(x * x, axis=-1, keepdims=True)
    return x * lax.rsqrt(ms + EPS) * g


def _head_norm_rows(x, ones_bd, g_tiled):
    x2 = x * x
    hi = x2.astype(BF16)
    lo = (x2 - hi.astype(F32)).astype(BF16)
    ms = _dot(hi, ones_bd) + _dot(lo, ones_bd)
    return x * lax.rsqrt(ms + EPS) * g_tiled


def _scan_rows(a_ref, u_ref, h_ref, row0, nrows, h0, unroll=False):
    ridx = lax.broadcasted_iota(jnp.int32, (SUBLANES, W_LRU), 0)

    def body(i, hprev):
        r = pl.multiple_of(row0 + i * SUBLANES, SUBLANES)
        a = a_ref[pl.ds(r, SUBLANES), :]
        u = u_ref[pl.ds(r, SUBLANES), :]
        for s in (1, 2, 4):
            a_s = jnp.where(ridx >= s, pltpu.roll(a, s, 0), 1.0)
            u_s = jnp.where(ridx >= s, pltpu.roll(u, s, 0), 0.0)
            u = a * u_s + u
            a = a * a_s
        h = a * hprev + u
        h_ref[pl.ds(r, SUBLANES), :] = h
        return h[SUBLANES - 1:SUBLANES, :]

    return lax.fori_loop(0, nrows // SUBLANES, body, h0, unroll=unroll)


def _lru_inputs(xc, wg_ref, bg_ref, lam_ref, a_ref, u_ref):
    gates = _dot(xc.astype(BF16), wg_ref[...]) + bg_ref[...]
    r = jax.nn.sigmoid(gates[:, :W_LRU])
    i = jax.nn.sigmoid(gates[:, W_LRU:])
    log_a = -RG_C * r * jax.nn.softplus(-lam_ref[...])
    a = jnp.exp(log_a)
    a_ref[...] = a
    u_ref[...] = jnp.sqrt(jnp.tanh(-log_a) * (1.0 + a * a)) * (i * xc)


def _conv_rows(cb_ref, base, nrows, cw_ref, cb_bias):
    out = cb_bias + cw_ref[CONV_W - 1:CONV_W, :] * cb_ref[pl.ds(base + SUBLANES, nrows), :]
    for k in range(CONV_W - 1):
        shift = CONV_W - 1 - k
        out = out + cw_ref[k:k + 1, :] * cb_ref[pl.ds(base + SUBLANES - shift, nrows), :]
    return out


def _front_prompt_kernel(x_ref, ng_ref, wnat_ref, wt_ref, cw_ref, cbias_ref, wg_ref, bg_ref, lam_ref,
                         qg_ref, kg_ref, ones_ref,
                         lru_ref, qt_ref, kbf_ref, vt_ref, gat_ref, pk_ref, pv_ref, pc_ref, ph_ref,
                         zn_scr, zt_scr, cb_scr, a_scr, u_scr, h_scr, hlast_scr, *, keep_subs):
    sub = QBLOCK
    n_sub = x_ref.shape[1] // sub

    @pl.when(pl.program_id(1) == 0)
    def _():
        cb_scr[0:SUBLANES, :] = jnp.zeros((SUBLANES, W_LRU), F32)
        hlast_scr[...] = jnp.zeros((1, W_LRU), F32)

    def project(i):
        xn = _rms_rows(x_ref[0, i * sub:(i + 1) * sub, :], ng_ref[...]).astype(BF16)
        zn_scr[i % 2] = _dot(xn, wnat_ref[...])
        zt_scr[i % 2] = _dot_nt(wt_ref[...], xn)

    def finish(i):
        rows = slice(i * sub, (i + 1) * sub)
        zn = zn_scr.at[i % 2]
        zt = zt_scr.at[i % 2]
        cb_scr[SUBLANES:SUBLANES + sub, :] = zn[:, :W_LRU]
        xc = _conv_rows(cb_scr, 0, sub, cw_ref, cbias_ref[...])
        pc_ref[0] = cb_scr[sub + SUBLANES - (CONV_W - 1):sub + SUBLANES, :]
        cb_scr[0:SUBLANES, :] = cb_scr[sub:sub + SUBLANES, :]
        _lru_inputs(xc, wg_ref, bg_ref, lam_ref, a_scr, u_scr)
        h_last = _scan_rows(a_scr, u_scr, h_scr, 0, sub, hlast_scr[...], unroll=True)
        hlast_scr[...] = h_last
        ph_ref[0] = h_last
        lru_ref[0, rows, :] = (h_scr[...] * jax.nn.silu(zn[:, W_LRU:2 * W_LRU])).astype(BF16)
        k = _head_norm_rows(zn[:, 2 * W_LRU:], ones_ref[...], kg_ref[...])
        kbf_ref[0, rows, :] = k.astype(BF16)
        q3 = zt[0:W_ATT, :].reshape(N_HEADS, HEAD_DIM, sub)
        ms = jnp.mean(q3 * q3, axis=1, keepdims=True)
        qn = q3 * lax.rsqrt(ms + EPS) * (qg_ref[...] * (HEAD_DIM ** -0.5 * LOG2E))
        qt_ref[0, :, rows] = qn.reshape(W_ATT, sub).astype(BF16)
        vt = zt[W_ATT:2 * W_ATT, :]
        vt_ref[0, :, rows] = vt.astype(BF16)
        gat_ref[0, :, rows] = jax.nn.silu(zt[2 * W_ATT:, :])
        if i >= n_sub - keep_subs:
            first = (i - (n_sub - keep_subs)) * sub
            pk_ref[0, first:first + sub, :] = k
            pv_ref[0, first:first + sub, :] = vt.T

    project(0)
    for i in range(n_sub):
        if i + 1 < n_sub:
            project(i + 1)
        finish(i)


def _front_prompt(x, ng, wnat, wt, cw, cbias, wg, bg, lam, qg, kg, ones_bd):
    b, s, _ = x.shape
    tb = min(FRONT_BLOCK, s)
    keep = min(BAND, s)
    assert s % tb == 0 and tb % QBLOCK == 0 and keep % QBLOCK == 0 and keep <= tb
    const = lambda shape: pl.BlockSpec(shape, lambda i, j: (0,) * len(shape))
    rows_spec = lambda w: pl.BlockSpec((1, tb, w), lambda i, j: (i, j, 0))
    cols_spec = pl.BlockSpec((1, W_ATT, tb), lambda i, j: (i, 0, j))
    per_seq = lambda r, w: pl.BlockSpec((1, r, w), lambda i, j: (i, 0, 0))
    kern = functools.partial(_front_prompt_kernel, keep_subs=keep // QBLOCK)
    return pl.pallas_call(
        kern,
        grid=(b, s // tb),
        in_specs=[
            rows_spec(D_MODEL),
            const((1, D_MODEL)),
            const((D_MODEL, 3 * W_LRU)),
            const((3 * W_ATT, D_MODEL)),
            const((CONV_W, W_LRU)),
            const((1, W_LRU)),
            const((W_LRU, 2 * W_LRU)),
            const((1, 2 * W_LRU)),
            const((1, W_LRU)),
            const((HEAD_DIM, 1)),
            const((1, W_ATT)),
            const((W_ATT, W_ATT)),
        ],
        out_specs=[
            rows_spec(W_LRU),
            cols_spec,
            rows_spec(W_ATT),
            cols_spec,
            cols_spec,
            per_seq(keep, W_ATT),
            per_seq(keep, W_ATT),
            per_seq(CONV_W - 1, W_LRU),
            per_seq(1, W_LRU),
        ],
        out_shape=[
            jax.ShapeDtypeStruct((b, s, W_LRU), BF16),
            jax.ShapeDtypeStruct((b, W_ATT, s), BF16),
            jax.ShapeDtypeStruct((b, s, W_ATT), BF16),
            jax.ShapeDtypeStruct((b, W_ATT, s), BF16),
            jax.ShapeDtypeStruct((b, W_ATT, s), F32),
            jax.ShapeDtypeStruct((b, keep, W_ATT), F32),
            jax.ShapeDtypeStruct((b, keep, W_ATT), F32),
            jax.ShapeDtypeStruct((b, CONV_W - 1, W_LRU), F32),
            jax.ShapeDtypeStruct((b, 1, W_LRU), F32),
        ],
        scratch_shapes=[
            pltpu.VMEM((2, QBLOCK, 3 * W_LRU), F32),
            pltpu.VMEM((2, 3 * W_ATT, QBLOCK), F32),
            pltpu.VMEM((QBLOCK + SUBLANES, W_LRU), F32),
            pltpu.VMEM((QBLOCK, W_LRU), F32),
            pltpu.VMEM((QBLOCK, W_LRU), F32),
            pltpu.VMEM((QBLOCK, W_LRU), F32),
            pltpu.VMEM((1, W_LRU), F32),
        ],
        compiler_params=pltpu.CompilerParams(
            dimension_semantics=("arbitrary", "arbitrary"), vmem_limit_bytes=VMEM_LIMIT),
        name="front_prompt",
    )(x, ng, wnat, wt, cw, cbias, wg, bg, lam, qg, kg, ones_bd)


def _toeplitz_rows(w_row, nrows, row0):
    x = jnp.broadcast_to(w_row, (nrows, w_row.shape[1]))
    return pltpu.roll(x, row0, 1, stride=1, stride_axis=0)


def _fill_prompt_bias(w_ref, bias_ref):
    q_chunk = (lax.broadcasted_iota(jnp.int32, (CHUNK, QBLOCK), 1) + BAND) // CHUNK
    for h in range(N_HEADS):
        def body(n, carry, h=h):
            r0 = pl.multiple_of(n * CHUNK, CHUNK)
            t = _toeplitz_rows(w_ref[h], CHUNK, r0)[:, :QBLOCK]
            dc = q_chunk - n
            bias_ref[h, pl.ds(r0, CHUNK), :] = jnp.where((dc >= 0) & (dc <= PAST_CHUNKS), t * LOG2E, NEG)
            return carry
        lax.fori_loop(0, KEY_TILES * QBLOCK // CHUNK, body, 0)


CHUNKS_PER_TILE = QBLOCK // CHUNK
FRAMES_PER_VREG = LANES // CHUNK


def _lane_cols(kc):
    cols = []
    for c in range(QBLOCK // LANES):
        q_lo = PAST_CHUNKS + c * FRAMES_PER_VREG
        q_hi = q_lo + FRAMES_PER_VREG - 1
        if q_lo - PAST_CHUNKS <= kc <= q_hi:
            cols.append(c)
    return cols


def _fold_rows(x):
    return x.reshape(x.shape[0] // SUBLANES, SUBLANES, x.shape[1])


def _attn_prompt_heads(tiles, qt_ref, k_refs, v_refs, bias_ref, s_scr, p_scr, att_scr):
    n_cols = QBLOCK // LANES
    rows = lax.broadcasted_iota(jnp.int32, (HEAD_GROUP, QBLOCK), 0)
    pieces = []
    for i in tiles:
        for cc in range(CHUNKS_PER_TILE):
            kc = i * CHUNKS_PER_TILE + cc
            for c in _lane_cols(kc):
                pieces.append((i, cc, slice(kc * CHUNK, (kc + 1) * CHUNK), slice(c * LANES, (c + 1) * LANES), c))

    def scores(h):
        g, hl = divmod(h, HEADS_PER_GROUP)
        gsl = slice(g * HEAD_GROUP, (g + 1) * HEAD_GROUP)
        in_head = (rows >= hl * HEAD_DIM) & (rows < (hl + 1) * HEAD_DIM)
        qm = jnp.where(in_head, qt_ref[0, gsl, :], jnp.zeros((), BF16))
        s = {i: _dot(k_refs[i][0, :, gsl], qm) for i in tiles}
        m_acc = [jnp.full((SUBLANES, LANES), NEG, F32) for _ in range(n_cols)]
        for i, cc, rsl, lsl, c in pieces:
            sp = s[i][cc * CHUNK:(cc + 1) * CHUNK, lsl] + bias_ref[h, rsl, lsl]
            s_scr[h % 2, rsl, lsl] = sp
            m_acc[c] = jnp.maximum(m_acc[c], jnp.max(_fold_rows(sp), axis=0))
        return [jnp.max(a, axis=0, keepdims=True) for a in m_acc]

    def weights(h, m):
        l_acc = [jnp.zeros((SUBLANES, LANES), F32) for _ in range(n_cols)]
        for _, _, rsl, lsl, c in pieces:
            p = jnp.exp2(s_scr[h % 2, rsl, lsl] - m[c])
            l_acc[c] = l_acc[c] + jnp.sum(_fold_rows(p), axis=0)
            p_scr[h % 2, rsl, lsl] = p.astype(BF16)
        return [1.0 / jnp.sum(a, axis=0, keepdims=True) for a in l_acc]

    def values(h, l_inv):
        hsl = slice(h * HEAD_DIM, (h + 1) * HEAD_DIM)
        o = None
        for i in tiles:
            oi = _dot(v_refs[i][0, hsl, :], p_scr[h % 2, i * QBLOCK:(i + 1) * QBLOCK, :])
            o = oi if o is None else o + oi
        for c in range(n_cols):
            lsl = slice(c * LANES, (c + 1) * LANES)
            att_scr[hsl, lsl] = o[:, lsl] * l_inv[c]

    m_next = scores(0)
    for h in range(N_HEADS):
        m = m_next
        if h + 1 < N_HEADS:
            m_next = scores(h + 1)
        values(h, weights(h, m))


def _attn_prompt_kernel(qt_ref, k0_ref, k1_ref, k2_ref, v0_ref, v1_ref, v2_ref, w_ref, gat_ref,
                        out_ref, bias_ref, s_scr, p_scr, att_scr):
    j = pl.program_id(1)

    @pl.when((pl.program_id(0) == 0) & (j == 0))
    def _():
        _fill_prompt_bias(w_ref, bias_ref)
        p_scr[...] = jnp.zeros(p_scr.shape, BF16)

    k_refs = (k0_ref, k1_ref, k2_ref)
    v_refs = (v0_ref, v1_ref, v2_ref)
    for first in range(KEY_TILES):
        tiles = tuple(range(first, KEY_TILES))
        cond = (j == KEY_TILES - 1 - first) if first > 0 else (j >= KEY_TILES - 1)

        @pl.when(cond)
        def _(tiles=tiles):
            _attn_prompt_heads(tiles, qt_ref, k_refs, v_refs, bias_ref, s_scr, p_scr, att_scr)

    out_ref[0] = (att_scr[...] * gat_ref[0]).T.astype(BF16)


def _attn_prompt(qt, kbf, vt, bias_t, gat):
    b, _, s = qt.shape
    n_blocks = s // QBLOCK
    back = KEY_TILES - 1
    k_specs = [pl.BlockSpec((1, QBLOCK, W_ATT),
                            functools.partial(lambda i, j, d: (i, jnp.maximum(j - d, 0), 0), d=back - n))
               for n in range(KEY_TILES)]
    v_specs = [pl.BlockSpec((1, W_ATT, QBLOCK),
                            functools.partial(lambda i, j, d: (i, 0, jnp.maximum(j - d, 0)), d=back - n))
               for n in range(KEY_TILES)]
    return pl.pallas_call(
        _attn_prompt_kernel,
        grid=(b, n_blocks),
        in_specs=[pl.BlockSpec((1, W_ATT, QBLOCK), lambda i, j: (i, 0, j))] + k_specs + v_specs + [
            pl.BlockSpec((N_HEADS, 1, PROMPT_BIAS_PERIOD), lambda i, j: (0, 0, 0)),
            pl.BlockSpec((1, W_ATT, QBLOCK), lambda i, j: (i, 0, j)),
        ],
        out_specs=pl.BlockSpec((1, QBLOCK, W_ATT), lambda i, j: (i, j, 0)),
        out_shape=jax.ShapeDtypeStruct((b, s, W_ATT), BF16),
        scratch_shapes=[
            pltpu.VMEM((N_HEADS, KEY_TILES * QBLOCK, QBLOCK), F32),
            pltpu.VMEM((2, KEY_TILES * QBLOCK, QBLOCK), F32),
            pltpu.VMEM((2, KEY_TILES * QBLOCK, QBLOCK), BF16),
            pltpu.VMEM((W_ATT, QBLOCK), F32),
        ],
        compiler_params=pltpu.CompilerParams(
            dimension_semantics=("arbitrary", "arbitrary"), vmem_limit_bytes=VMEM_LIMIT),
        name="attn_prompt",
    )(qt, kbf, kbf, kbf, vt, vt, vt, bias_t, gat)


def _back_kernel(x_ref, lru_ref, att_ref, p_ref, wo_ref, pg_ref, wpg_ref, wpe_ref, y_ref):
    mix = _dot(lru_ref[...], wo_ref[0:W_LRU, :]) + _dot(att_ref[...], wo_ref[W_LRU:, :])
    h = x_ref[...] + mix
    gate = jax.nn.sigmoid(_dot(_rms_rows(h, pg_ref[...]).astype(BF16), wpg_ref[...]))
    y_ref[...] = h + _dot(p_ref[...].astype(BF16), wpe_ref[...]) * gate


def _back(x2, lru2, att2, p2, wo, pg, wpg, wpe, rows):
    n = x2.shape[0]
    const = lambda shape: pl.BlockSpec(shape, lambda i: (0,) * len(shape))
    return pl.pallas_call(
        _back_kernel,
        grid=(n // rows,),
        in_specs=[
            pl.BlockSpec((rows, D_MODEL), lambda i: (i, 0)),
            pl.BlockSpec((rows, W_LRU), lambda i: (i, 0)),
            pl.BlockSpec((rows, W_ATT), lambda i: (i, 0)),
            pl.BlockSpec((rows, PLE_DIM), lambda i: (i, 0)),
            const((W_LRU + W_ATT, D_MODEL)),
            const((1, D_MODEL)),
            const((D_MODEL, D_MODEL)),
            const((PLE_DIM, D_MODEL)),
        ],
        out_specs=pl.BlockSpec((rows, D_MODEL), lambda i: (i, 0)),
        out_shape=jax.ShapeDtypeStruct((n, D_MODEL), F32),
        compiler_params=pltpu.CompilerParams(
            dimension_semantics=("arbitrary",), vmem_limit_bytes=VMEM_LIMIT),
        name="back",
    )(x2, lru2, att2, p2, wo, pg, wpg, wpe)


def _front_sample_kernel(x_ref, ng_ref, win_ref, cw_ref, cbias_ref, wg_ref, bg_ref, lam_ref,
                         qg_ref, kg_ref, ones_ref, sconv_ref, slru_ref,
                         lru_ref, q_ref, k_ref, v_ref, ga_ref, sc_ref, sh_ref,
                         cb_scr, xc_scr, a_scr, u_scr, h_scr, *, nb, tt):
    seg = tt + SUBLANES
    xn = _rms_rows(x_ref[...], ng_ref[...]).astype(BF16)
    z = _dot(xn, win_ref[...])
    xl = z[:, :W_LRU]
    for s in range(nb):
        cb_scr[s * seg:s * seg + SUBLANES, :] = sconv_ref[s]
        cb_scr[s * seg + SUBLANES:(s + 1) * seg, :] = xl[s * tt:(s + 1) * tt, :]
        xc_scr[s * tt:(s + 1) * tt, :] = _conv_rows(cb_scr, s * seg, tt, cw_ref, cbias_ref[...])
        sc_ref[s] = cb_scr[(s + 1) * seg - (CONV_W - 1):(s + 1) * seg, :]
    _lru_inputs(xc_scr[...], wg_ref, bg_ref, lam_ref, a_scr, u_scr)
    for s in range(nb):
        sh_ref[s] = _scan_rows(a_scr, u_scr, h_scr, s * tt, tt, slru_ref[s])
    lru_ref[...] = (h_scr[...] * jax.nn.silu(z[:, W_LRU:2 * W_LRU])).astype(BF16)

    o = 2 * W_LRU
    q = _head_norm_rows(z[:, o:o + W_ATT], ones_ref[...], qg_ref[...])
    q_ref[...] = (q * (HEAD_DIM ** -0.5)).astype(BF16)
    k_ref[...] = _head_norm_rows(z[:, o + W_ATT:o + 2 * W_ATT], ones_ref[...], kg_ref[...])
    v_ref[...] = z[:, o + 2 * W_ATT:o + 3 * W_ATT]
    ga_ref[...] = jax.nn.silu(z[:, o + 3 * W_ATT:])


def _front_sample(x2, ng, win, cw, cbias, wg, bg, lam, qg_t, kg_t, ones_bd, sconv_pad, slru, nb, tt):
    n = x2.shape[0]
    rows = nb * tt
    const = lambda shape: pl.BlockSpec(shape, lambda i: (0,) * len(shape))
    row_spec = lambda w: pl.BlockSpec((rows, w), lambda i: (i, 0))
    nseq = n // tt
    kern = functools.partial(_front_sample_kernel, nb=nb, tt=tt)
    return pl.pallas_call(
        kern,
        grid=(n // rows,),
        in_specs=[
            row_spec(D_MODEL),
            const((1, D_MODEL)),
            const((D_MODEL, 2 * W_LRU + 4 * W_ATT)),
            const((CONV_W, W_LRU)),
            const((1, W_LRU)),
            const((W_LRU, 2 * W_LRU)),
            const((1, 2 * W_LRU)),
            const((1, W_LRU)),
            const((1, W_ATT)),
            const((1, W_ATT)),
            const((W_ATT, W_ATT)),
            pl.BlockSpec((nb, SUBLANES, W_LRU), lambda i: (i, 0, 0)),
            pl.BlockSpec((nb, 1, W_LRU), lambda i: (i, 0, 0)),
        ],
        out_specs=[
            row_spec(W_LRU), row_spec(W_ATT), row_spec(W_ATT), row_spec(W_ATT), row_spec(W_ATT),
            pl.BlockSpec((nb, CONV_W - 1, W_LRU), lambda i: (i, 0, 0)),
            pl.BlockSpec((nb, 1, W_LRU), lambda i: (i, 0, 0)),
        ],
        out_shape=[
            jax.ShapeDtypeStruct((n, W_LRU), BF16),
            jax.ShapeDtypeStruct((n, W_ATT), BF16),
            jax.ShapeDtypeStruct((n, W_ATT), F32),
            jax.ShapeDtypeStruct((n, W_ATT), F32),
            jax.ShapeDtypeStruct((n, W_ATT), F32),
            jax.ShapeDtypeStruct((nseq, CONV_W - 1, W_LRU), F32),
            jax.ShapeDtypeStruct((nseq, 1, W_LRU), F32),
        ],
        scratch_shapes=[
            pltpu.VMEM((nb * (tt + SUBLANES), W_LRU), F32),
            pltpu.VMEM((rows, W_LRU), F32),
            pltpu.VMEM((rows, W_LRU), F32),
            pltpu.VMEM((rows, W_LRU), F32),
            pltpu.VMEM((rows, W_LRU), F32),
        ],
        compiler_params=pltpu.CompilerParams(
            dimension_semantics=("arbitrary",), vmem_limit_bytes=VMEM_LIMIT),
        name="front_sample",
    )(x2, ng, win, cw, cbias, wg, bg, lam, qg_t, kg_t, ones_bd, sconv_pad, slru)


def _attn_sample_kernel(q_ref, kn_ref, vn_ref, kc_ref, vc_ref, w_ref, ga_ref, out_ref, bias_ref):
    tt = q_ref.shape[1]
    lc = kc_ref.shape[1]

    @pl.when(pl.program_id(0) == 0)
    def _():
        for h in range(N_HEADS):
            g, hl = divmod(h, HEADS_PER_GROUP)
            bias_ref[g, hl * tt:(hl + 1) * tt, :] = _toeplitz_rows(w_ref[h], tt, 0)

    lanes = lax.broadcasted_iota(jnp.int32, (tt, HEAD_GROUP), 1)
    masks = [(lanes >= hl * HEAD_DIM) & (lanes < (hl + 1) * HEAD_DIM) for hl in range(HEADS_PER_GROUP)]
    for g in range(N_GROUPS):
        gsl = slice(g * HEAD_GROUP, (g + 1) * HEAD_GROUP)
        qg = q_ref[0, :, gsl]
        qs = jnp.concatenate([jnp.where(m, qg, jnp.zeros((), BF16)) for m in masks], axis=0)
        sc = _dot_nt(qs, kc_ref[0, :, gsl].astype(BF16)) + bias_ref[g, :, 0:lc]
        sn = _dot_nt(qs, kn_ref[0, :, gsl].astype(BF16)) + bias_ref[g, :, lc:lc + tt]
        m = jnp.maximum(jnp.max(sc, axis=-1, keepdims=True), jnp.max(sn, axis=-1, keepdims=True))
        pc = jnp.exp(sc - m)
        pn = jnp.exp(sn - m)
        l = jnp.sum(pc, axis=-1, keepdims=True) + jnp.sum(pn, axis=-1, keepdims=True)
        o = _dot(pc.astype(BF16), vc_ref[0, :, gsl].astype(BF16))
        o = (o + _dot(pn.astype(BF16), vn_ref[0, :, gsl].astype(BF16))) / l
        att = jnp.zeros((tt, HEAD_GROUP), F32)
        for hl in range(HEADS_PER_GROUP):
            att = att + jnp.where(masks[hl], o[hl * tt:(hl + 1) * tt, :], 0.0)
        out_ref[0, :, gsl] = (att * ga_ref[0, :, gsl]).astype(BF16)


def _attn_sample(q3, k3, v3, kc, vc, w_bias, ga3):
    b, tt, _ = q3.shape
    l = kc.shape[1]
    period = w_bias.shape[-1]
    seq = lambda r: pl.BlockSpec((1, r, W_ATT), lambda i: (i, 0, 0))
    return pl.pallas_call(
        _attn_sample_kernel,
        grid=(b,),
        in_specs=[
            seq(tt), seq(tt), seq(tt), seq(l), seq(l),
            pl.BlockSpec((N_HEADS, 1, period), lambda i: (0, 0, 0)),
            seq(tt),
        ],
        out_specs=seq(tt),
        out_shape=jax.ShapeDtypeStruct((b, tt, W_ATT), BF16),
        scratch_shapes=[pltpu.VMEM((N_GROUPS, HEADS_PER_GROUP * tt, period), F32)],
        compiler_params=pltpu.CompilerParams(
            dimension_semantics=("arbitrary",), vmem_limit_bytes=VMEM_LIMIT),
        name="attn_sample",
    )(q3, k3, v3, kc, vc, w_bias, ga3)


def _block_diag(w):
    n, d, e = w.shape
    eye = jnp.eye(n, dtype=w.dtype)
    return (eye[:, None, :, None] * w[:, :, None, :]).reshape(n * d, n * e)


def _prompt_bias_period(table):
    assert BAND - MAX_REL == MAX_REL and PROMPT_BIAS_PERIOD == KEY_TILES * QBLOCK + QBLOCK
    last = table[2 * MAX_REL:]
    neg_d = jnp.concatenate([table, jnp.broadcast_to(last, (MAX_REL - 1, N_HEADS))])
    w = jnp.concatenate([jnp.broadcast_to(last, (QBLOCK, N_HEADS)), neg_d])
    return w.T.reshape(N_HEADS, 1, PROMPT_BIAS_PERIOD).astype(F32)


def _sample_bias_period(table, tt, l):
    assert l >= MAX_REL
    period = -(-(l + 2 * tt - 1) // LANES) * LANES
    last = table[2 * MAX_REL:]
    n_var = tt + MAX_REL - 1
    var = table[2 * MAX_REL - 1:2 * MAX_REL - 1 - n_var:-1]
    w = jnp.concatenate([jnp.broadcast_to(last, (l - MAX_REL + 1, N_HEADS)), var,
                         jnp.broadcast_to(last, (period - (l + tt), N_HEADS))])
    return w.T.reshape(N_HEADS, 1, period).astype(F32)


def kernel(x_prompt, x_sample, p_prompt, p_sample, cache_k, cache_v, state_conv, state_lru, norm_g, w_in, conv_w, conv_b, gate_a_w, gate_a_b, gate_x_w, gate_x_b, lru_lambda, q_norm_g, k_norm_g, rel_bias, w_out, ple_norm_g, w_ple_gate, w_ple_proj):
    depth = w_in.shape[0]
    b, s, _ = x_prompt.shape
    db, ds, _ = x_sample.shape
    lc = cache_k.shape[2]
    yp, ys = x_prompt, x_sample.reshape(db * ds, D_MODEL)
    ones_bd = _block_diag(jnp.full((N_HEADS, HEAD_DIM, HEAD_DIM), 1.0 / HEAD_DIM, F32)).astype(BF16)
    outs = [[] for _ in range(8)]
    sample_nb = 8
    for l in range(depth):
        win = w_in[l].astype(BF16)
        o = 2 * W_LRU
        wnat = jnp.concatenate([win[:, :o], win[:, o + W_ATT:o + 2 * W_ATT]], axis=1)
        wt = jnp.concatenate([win[:, o:o + W_ATT], win[:, o + 2 * W_ATT:]], axis=1).T
        ng = norm_g[l].reshape(1, D_MODEL)
        cw = conv_w[l]
        cbias = conv_b[l].reshape(1, W_LRU)
        wg = jnp.concatenate([_block_diag(gate_a_w[l]), _block_diag(gate_x_w[l])], axis=1).astype(BF16)
        bg = jnp.concatenate([gate_a_b[l], gate_x_b[l]]).reshape(1, 2 * W_LRU)
        lam = lru_lambda[l].reshape(1, W_LRU)
        qg_col = q_norm_g[l].reshape(HEAD_DIM, 1)
        qg_t = jnp.tile(q_norm_g[l], N_HEADS).reshape(1, W_ATT)
        kg_t = jnp.tile(k_norm_g[l], N_HEADS).reshape(1, W_ATT)
        wo = w_out[l].astype(BF16)
        pg = ple_norm_g[l].reshape(1, D_MODEL)
        wpg = w_ple_gate[l].astype(BF16)
        wpe = w_ple_proj[l].astype(BF16)

        lru_g, qt, kbf, vt, gat, pk, pv, pc, ph = _front_prompt(
            yp, ng, wnat, wt, cw, cbias, wg, bg, lam, qg_col, kg_t, ones_bd)
        att_g = _attn_prompt(qt, kbf, vt, _prompt_bias_period(rel_bias[l]), gat)
        yp = _back(yp.reshape(b * s, D_MODEL), lru_g.reshape(b * s, W_LRU), att_g.reshape(b * s, W_ATT),
                   p_prompt[l].reshape(b * s, PLE_DIM), wo, pg, wpg, wpe, 512).reshape(b, s, D_MODEL)
        keep = pk.shape[1]
        outs[0].append(pk.reshape(b, keep, N_HEADS, HEAD_DIM))
        outs[1].append(pv.reshape(b, keep, N_HEADS, HEAD_DIM))
        outs[2].append(pc)
        outs[3].append(ph.reshape(b, W_LRU))

        sconv_pad = jnp.pad(state_conv[l], ((0, 0), (SUBLANES - (CONV_W - 1), 0), (0, 0)))
        lru_s, q_s, k_s, v_s, ga_s, sc, sh = _front_sample(
            ys, ng, win, cw, cbias, wg, bg, lam, qg_t, kg_t, ones_bd,
            sconv_pad, state_lru[l].reshape(db, 1, W_LRU), sample_nb, ds)
        att_s = _attn_sample(q_s.reshape(db, ds, W_ATT), k_s.reshape(db, ds, W_ATT), v_s.reshape(db, ds, W_ATT),
                             cache_k[l].reshape(db, lc, W_ATT), cache_v[l].reshape(db, lc, W_ATT),
                             _sample_bias_period(rel_bias[l], ds, lc), ga_s.reshape(db, ds, W_ATT))
        ys = _back(ys, lru_s, att_s.reshape(db * ds, W_ATT), p_sample[l].reshape(db * ds, PLE_DIM),
                   wo, pg, wpg, wpe, 256)
        outs[4].append(k_s.reshape(db, ds, N_HEADS, HEAD_DIM))
        outs[5].append(v_s.reshape(db, ds, N_HEADS, HEAD_DIM))
        outs[6].append(sc)
        outs[7].append(sh.reshape(db, W_LRU))
    return (yp, ys.reshape(db, ds, D_MODEL)) + tuple(jnp.stack(o) for o in outs)
```

```python
import functools

import jax
import jax.numpy as jnp
from jax import lax
from jax.experimental import pallas as pl
from jax.experimental.pallas import tpu as pltpu

D_MODEL = 1024
CHUNK = 64
PAST_CHUNKS = 8
BAND = PAST_CHUNKS * CHUNK
W_LRU = D_MODEL // 2
LRU_BLOCKS = 8
LRU_BLOCK = W_LRU // LRU_BLOCKS
CONV_W = 4
RG_C = 8.0
HEAD_DIM = 64
W_ATT = D_MODEL // 2
N_HEADS = W_ATT // HEAD_DIM
MAX_REL = 256
PLE_DIM = 256
EPS = 1e-6
NEG = -1e30
LOG2E = 1.4426950408889634

SUBLANES = 8
LANES = 128
HEAD_GROUP = 256
HEADS_PER_GROUP = HEAD_GROUP // HEAD_DIM
N_GROUPS = W_ATT // HEAD_GROUP
QBLOCK = 256
KEY_TILES = BAND // QBLOCK + 1
PROMPT_BIAS_PERIOD = (KEY_TILES + 1) * QBLOCK
FRONT_BLOCK = 1024
SAMPLE_SEQS_PER_STEP = 4
VMEM_LIMIT = 56 * 1024 * 1024

F32 = jnp.float32
BF16 = jnp.bfloat16


def _dot(a, b):
    return jnp.dot(a, b, preferred_element_type=F32)


def _dot_nt(a, b):
    return lax.dot_general(a, b, (((1,), (1,)), ((), ())), preferred_element_type=F32)


def _rms_rows(x, g):
    ms = jnp.mean(x * x, axis=-1, keepdims=True)
    return x * lax.rsqrt(ms + EPS) * g


def _head_norm_rows(x, ones_bd, g_tiled):
    x2 = x * x
    hi = x2.astype(BF16)
    lo = (x2 - hi.astype(F32)).astype(BF16)
    ms = jnp.concatenate(
        [_dot(hi[:, g * HEAD_GROUP:(g + 1) * HEAD_GROUP], ones_bd)
         + _dot(lo[:, g * HEAD_GROUP:(g + 1) * HEAD_GROUP], ones_bd) for g in range(N_GROUPS)], axis=1)
    return x * lax.rsqrt(ms + EPS) * g_tiled


def _scan_rows(a_ref, u_ref, h_ref, row0, nrows, h0, unroll=False):
    ridx = lax.broadcasted_iota(jnp.int32, (SUBLANES, W_LRU), 0)

    def body(i, hprev):
        r = pl.multiple_of(row0 + i * SUBLANES, SUBLANES)
        a = a_ref[pl.ds(r, SUBLANES), :]
        u = u_ref[pl.ds(r, SUBLANES), :]
        for s in (1, 2, 4):
            a_s = jnp.where(ridx >= s, pltpu.roll(a, s, 0), 1.0)
            u_s = jnp.where(ridx >= s, pltpu.roll(u, s, 0), 0.0)
            u = a * u_s + u
            a = a * a_s
        h = a * hprev + u
        h_ref[pl.ds(r, SUBLANES), :] = h
        return h[SUBLANES - 1:SUBLANES, :]

    return lax.fori_loop(0, nrows // SUBLANES, body, h0, unroll=unroll)


def _lru_inputs(xc, wg_ref, bg_ref, lam_ref, a_ref, u_ref):
    xcb = xc.astype(BF16)
    half = wg_ref.shape[1]
    for j in range(wg_ref.shape[0]):
        sl = slice(j * half, (j + 1) * half)
        gates = _dot(xcb[:, sl], wg_ref[j]) + bg_ref[j]
        r = jax.nn.sigmoid(gates[:, :half])
        i = jax.nn.sigmoid(gates[:, half:])
        log_a = -RG_C * r * jax.nn.softplus(-lam_ref[:, sl])
        a = jnp.exp(log_a)
        a_ref[:, sl] = a
        u_ref[:, sl] = jnp.sqrt(jnp.tanh(-log_a) * (1.0 + a * a)) * (i * xc[:, sl])


def _conv_rows(cb_ref, base, nrows, cw_ref, cb_bias):
    out = cb_bias + cw_ref[CONV_W - 1:CONV_W, :] * cb_ref[pl.ds(base + SUBLANES, nrows), :]
    for k in range(CONV_W - 1):
        shift = CONV_W - 1 - k
        out = out + cw_ref[k:k + 1, :] * cb_ref[pl.ds(base + SUBLANES - shift, nrows), :]
    return out


def _front_prompt_kernel(x_ref, ng_ref, wnat_ref, wt_ref, cw_ref, cbias_ref, wg_ref, bg_ref, lam_ref,
                         qg_ref, kg_ref, ones_ref,
                         lru_ref, qt_ref, kbf_ref, vt_ref, gat_ref, pk_ref, pv_ref, pc_ref, ph_ref,
                         zn_scr, zt_scr, cb_scr, a_scr, u_scr, h_scr, hlast_scr, *, keep_subs):
    sub = QBLOCK
    n_sub = x_ref.shape[1] // sub

    @pl.when(pl.program_id(1) == 0)
    def _():
        cb_scr[0:SUBLANES, :] = jnp.zeros((SUBLANES, W_LRU), F32)
        hlast_scr[...] = jnp.zeros((1, W_LRU), F32)

    def project(i):
        xn = _rms_rows(x_ref[0, i * sub:(i + 1) * sub, :], ng_ref[...]).astype(BF16)
        zn_scr[i % 2] = _dot(xn, wnat_ref[...])
        zt_scr[i % 2] = _dot_nt(wt_ref[...], xn)

    def finish(i):
        rows = slice(i * sub, (i + 1) * sub)
        zn = zn_scr.at[i % 2]
        zt = zt_scr.at[i % 2]
        cb_scr[SUBLANES:SUBLANES + sub, :] = zn[:, :W_LRU]
        xc = _conv_rows(cb_scr, 0, sub, cw_ref, cbias_ref[...])
        pc_ref[0] = cb_scr[sub + SUBLANES - (CONV_W - 1):sub + SUBLANES, :]
        cb_scr[0:SUBLANES, :] = cb_scr[sub:sub + SUBLANES, :]
        _lru_inputs(xc, wg_ref, bg_ref, lam_ref, a_scr, u_scr)
        h_last = _scan_rows(a_scr, u_scr, h_scr, 0, sub, hlast_scr[...], unroll=True)
        hlast_scr[...] = h_last
        ph_ref[0] = h_last
        lru_ref[0, rows, :] = (h_scr[...] * jax.nn.silu(zn[:, W_LRU:2 * W_LRU])).astype(BF16)
        k = _head_norm_rows(zn[:, 2 * W_LRU:], ones_ref[...], kg_ref[...])
        kbf_ref[0, rows, :] = k.astype(BF16)
        q3 = zt[0:W_ATT, :].reshape(N_HEADS, HEAD_DIM, sub)
        ms = jnp.mean(q3 * q3, axis=1, keepdims=True)
        qn = q3 * lax.rsqrt(ms + EPS) * (qg_ref[...] * (HEAD_DIM ** -0.5 * LOG2E))
        qt_ref[0, :, rows] = qn.reshape(W_ATT, sub).astype(BF16)
        vt = zt[W_ATT:2 * W_ATT, :]
        vt_ref[0, :, rows] = vt.astype(BF16)
        gat_ref[0, :, rows] = jax.nn.silu(zt[2 * W_ATT:, :])
        if i >= n_sub - keep_subs:
            first = (i - (n_sub - keep_subs)) * sub
            pk_ref[0, first:first + sub, :] = k
            pv_ref[0, first:first + sub, :] = vt.T

    project(0)
    for i in range(n_sub):
        if i + 1 < n_sub:
            project(i + 1)
        finish(i)


def _front_prompt(x, ng, wnat, wt, cw, cbias, wg, bg, lam, qg, kg, ones_bd):
    b, s, _ = x.shape
    tb = min(FRONT_BLOCK, s)
    keep = min(BAND, s)
    assert s % tb == 0 and tb % QBLOCK == 0 and keep % QBLOCK == 0 and keep <= tb
    const = lambda shape: pl.BlockSpec(shape, lambda i, j: (0,) * len(shape))
    rows_spec = lambda w: pl.BlockSpec((1, tb, w), lambda i, j: (i, j, 0))
    cols_spec = pl.BlockSpec((1, W_ATT, tb), lambda i, j: (i, 0, j))
    per_seq = lambda r, w: pl.BlockSpec((1, r, w), lambda i, j: (i, 0, 0))
    kern = functools.partial(_front_prompt_kernel, keep_subs=keep // QBLOCK)
    return pl.pallas_call(
        kern,
        grid=(b, s // tb),
        in_specs=[
            rows_spec(D_MODEL),
            const((1, D_MODEL)),
            const((D_MODEL, 3 * W_LRU)),
            const((3 * W_ATT, D_MODEL)),
            const((CONV_W, W_LRU)),
            const((1, W_LRU)),
            const((W_LRU // HEAD_GROUP, HEAD_GROUP, 2 * HEAD_GROUP)),
            const((W_LRU // HEAD_GROUP, 1, 2 * HEAD_GROUP)),
            const((1, W_LRU)),
            const((HEAD_DIM, 1)),
            const((1, W_ATT)),
            const((HEAD_GROUP, HEAD_GROUP)),
        ],
        out_specs=[
            rows_spec(W_LRU),
            cols_spec,
            rows_spec(W_ATT),
            cols_spec,
            cols_spec,
            per_seq(keep, W_ATT),
            per_seq(keep, W_ATT),
            per_seq(CONV_W - 1, W_LRU),
            per_seq(1, W_LRU),
        ],
        out_shape=[
            jax.ShapeDtypeStruct((b, s, W_LRU), BF16),
            jax.ShapeDtypeStruct((b, W_ATT, s), BF16),
            jax.ShapeDtypeStruct((b, s, W_ATT), BF16),
            jax.ShapeDtypeStruct((b, W_ATT, s), BF16),
            jax.ShapeDtypeStruct((b, W_ATT, s), F32),
            jax.ShapeDtypeStruct((b, keep, W_ATT), F32),
            jax.ShapeDtypeStruct((b, keep, W_ATT), F32),
            jax.ShapeDtypeStruct((b, CONV_W - 1, W_LRU), F32),
            jax.ShapeDtypeStruct((b, 1, W_LRU), F32),
        ],
        scratch_shapes=[
            pltpu.VMEM((2, QBLOCK, 3 * W_LRU), F32),
            pltpu.VMEM((2, 3 * W_ATT, QBLOCK), F32),
            pltpu.VMEM((QBLOCK + SUBLANES, W_LRU), F32),
            pltpu.VMEM((QBLOCK, W_LRU), F32),
            pltpu.VMEM((QBLOCK, W_LRU), F32),
            pltpu.VMEM((QBLOCK, W_LRU), F32),
            pltpu.VMEM((1, W_LRU), F32),
        ],
        compiler_params=pltpu.CompilerParams(
            dimension_semantics=("arbitrary", "arbitrary"), vmem_limit_bytes=VMEM_LIMIT),
        name="front_prompt",
    )(x, ng, wnat, wt, cw, cbias, wg, bg, lam, qg, kg, ones_bd)


def _toeplitz_rows(w_row, nrows, row0):
    x = jnp.broadcast_to(w_row, (nrows, w_row.shape[1]))
    return pltpu.roll(x, row0, 1, stride=1, stride_axis=0)


def _fill_prompt_bias(w_ref, bias_ref):
    q_chunk = (lax.broadcasted_iota(jnp.int32, (CHUNK, QBLOCK), 1) + BAND) // CHUNK
    for h in range(N_HEADS):
        def body(n, carry, h=h):
            r0 = pl.multiple_of(n * CHUNK, CHUNK)
            t = _toeplitz_rows(w_ref[h], CHUNK, r0)[:, :QBLOCK]
            dc = q_chunk - n
            bias_ref[h, pl.ds(r0, CHUNK), :] = jnp.where((dc >= 0) & (dc <= PAST_CHUNKS), t * LOG2E, NEG)
            return carry
        lax.fori_loop(0, KEY_TILES * QBLOCK // CHUNK, body, 0)


CHUNKS_PER_TILE = QBLOCK // CHUNK
FRAMES_PER_VREG = LANES // CHUNK


def _lane_cols(kc):
    cols = []
    for c in range(QBLOCK // LANES):
        q_lo = PAST_CHUNKS + c * FRAMES_PER_VREG
        q_hi = q_lo + FRAMES_PER_VREG - 1
        if q_lo - PAST_CHUNKS <= kc <= q_hi:
            cols.append(c)
    return cols


def _fold_rows(x):
    return x.reshape(x.shape[0] // SUBLANES, SUBLANES, x.shape[1])


def _attn_prompt_heads(tiles, qt_ref, k_refs, v_refs, bias_ref, s_scr, p_scr, att_scr):
    n_cols = QBLOCK // LANES
    rows = lax.broadcasted_iota(jnp.int32, (HEAD_GROUP, QBLOCK), 0)
    pieces = []
    for i in tiles:
        for cc in range(CHUNKS_PER_TILE):
            kc = i * CHUNKS_PER_TILE + cc
            for c in _lane_cols(kc):
                pieces.append((i, cc, slice(kc * CHUNK, (kc + 1) * CHUNK), slice(c * LANES, (c + 1) * LANES), c))

    def scores(h):
        g, hl = divmod(h, HEADS_PER_GROUP)
        gsl = slice(g * HEAD_GROUP, (g + 1) * HEAD_GROUP)
        in_head = (rows >= hl * HEAD_DIM) & (rows < (hl + 1) * HEAD_DIM)
        qm = jnp.where(in_head, qt_ref[0, gsl, :], jnp.zeros((), BF16))
        s = {i: _dot(k_refs[i][0, :, gsl], qm) for i in tiles}
        m_acc = [jnp.full((SUBLANES, LANES), NEG, F32) for _ in range(n_cols)]
        for i, cc, rsl, lsl, c in pieces:
            sp = s[i][cc * CHUNK:(cc + 1) * CHUNK, lsl] + bias_ref[h, rsl, lsl]
            s_scr[h % 2, rsl, lsl] = sp
            m_acc[c] = jnp.maximum(m_acc[c], jnp.max(_fold_rows(sp), axis=0))
        return [jnp.max(a, axis=0, keepdims=True) for a in m_acc]

    def weights(h, m):
        l_acc = [jnp.zeros((SUBLANES, LANES), F32) for _ in range(n_cols)]
        for _, _, rsl, lsl, c in pieces:
            p = jnp.exp2(s_scr[h % 2, rsl, lsl] - m[c])
            l_acc[c] = l_acc[c] + jnp.sum(_fold_rows(p), axis=0)
            p_scr[h % 2, rsl, lsl] = p.astype(BF16)
        return [1.0 / jnp.sum(a, axis=0, keepdims=True) for a in l_acc]

    def values(h, l_inv):
        hsl = slice(h * HEAD_DIM, (h + 1) * HEAD_DIM)
        o = None
        for i in tiles:
            oi = _dot(v_refs[i][0, hsl, :], p_scr[h % 2, i * QBLOCK:(i + 1) * QBLOCK, :])
            o = oi if o is None else o + oi
        for c in range(n_cols):
            lsl = slice(c * LANES, (c + 1) * LANES)
            att_scr[hsl, lsl] = o[:, lsl] * l_inv[c]

    m_next = scores(0)
    for h in range(N_HEADS):
        m = m_next
        if h + 1 < N_HEADS:
            m_next = scores(h + 1)
        values(h, weights(h, m))


def _attn_prompt_kernel(qt_ref, k0_ref, k1_ref, k2_ref, v0_ref, v1_ref, v2_ref, w_ref, gat_ref,
                        out_ref, bias_ref, s_scr, p_scr, att_scr):
    j = pl.program_id(1)

    @pl.when((pl.program_id(0) == 0) & (j == 0))
    def _():
        _fill_prompt_bias(w_ref, bias_ref)
        p_scr[...] = jnp.zeros(p_scr.shape, BF16)

    k_refs = (k0_ref, k1_ref, k2_ref)
    v_refs = (v0_ref, v1_ref, v2_ref)
    for first in range(KEY_TILES):
        tiles = tuple(range(first, KEY_TILES))
        cond = (j == KEY_TILES - 1 - first) if first > 0 else (j >= KEY_TILES - 1)

        @pl.when(cond)
        def _(tiles=tiles):
            _attn_prompt_heads(tiles, qt_ref, k_refs, v_refs, bias_ref, s_scr, p_scr, att_scr)

    out_ref[0] = (att_scr[...] * gat_ref[0]).T.astype(BF16)


def _attn_prompt(qt, kbf, vt, bias_t, gat):
    b, _, s = qt.shape
    n_blocks = s // QBLOCK
    back = KEY_TILES - 1
    k_specs = [pl.BlockSpec((1, QBLOCK, W_ATT),
                            functools.partial(lambda i, j, d: (i, jnp.maximum(j - d, 0), 0), d=back - n))
               for n in range(KEY_TILES)]
    v_specs = [pl.BlockSpec((1, W_ATT, QBLOCK),
                            functools.partial(lambda i, j, d: (i, 0, jnp.maximum(j - d, 0)), d=back - n))
               for n in range(KEY_TILES)]
    return pl.pallas_call(
        _attn_prompt_kernel,
        grid=(b, n_blocks),
        in_specs=[pl.BlockSpec((1, W_ATT, QBLOCK), lambda i, j: (i, 0, j))] + k_specs + v_specs + [
            pl.BlockSpec((N_HEADS, 1, PROMPT_BIAS_PERIOD), lambda i, j: (0, 0, 0)),
            pl.BlockSpec((1, W_ATT, QBLOCK), lambda i, j: (i, 0, j)),
        ],
        out_specs=pl.BlockSpec((1, QBLOCK, W_ATT), lambda i, j: (i, j, 0)),
        out_shape=jax.ShapeDtypeStruct((b, s, W_ATT), BF16),
        scratch_shapes=[
            pltpu.VMEM((N_HEADS, KEY_TILES * QBLOCK, QBLOCK), F32),
            pltpu.VMEM((2, KEY_TILES * QBLOCK, QBLOCK), F32),
            pltpu.VMEM((2, KEY_TILES * QBLOCK, QBLOCK), BF16),
            pltpu.VMEM((W_ATT, QBLOCK), F32),
        ],
        compiler_params=pltpu.CompilerParams(
            dimension_semantics=("arbitrary", "arbitrary"), vmem_limit_bytes=VMEM_LIMIT),
        name="attn_prompt",
    )(qt, kbf, kbf, kbf, vt, vt, vt, bias_t, gat)


def _back_kernel(x_ref, lru_ref, att_ref, p_ref, wo_ref, pg_ref, wpg_ref, wpe_ref, y_ref):
    mix = _dot(lru_ref[...], wo_ref[0:W_LRU, :]) + _dot(att_ref[...], wo_ref[W_LRU:, :])
    h = x_ref[...] + mix
    gate = jax.nn.sigmoid(_dot(_rms_rows(h, pg_ref[...]).astype(BF16), wpg_ref[...]))
    y_ref[...] = h + _dot(p_ref[...].astype(BF16), wpe_ref[...]) * gate


def _back(x2, lru2, att2, p2, wo, pg, wpg, wpe, rows):
    n = x2.shape[0]
    const = lambda shape: pl.BlockSpec(shape, lambda i: (0,) * len(shape))
    return pl.pallas_call(
        _back_kernel,
        grid=(n // rows,),
        in_specs=[
            pl.BlockSpec((rows, D_MODEL), lambda i: (i, 0)),
            pl.BlockSpec((rows, W_LRU), lambda i: (i, 0)),
            pl.BlockSpec((rows, W_ATT), lambda i: (i, 0)),
            pl.BlockSpec((rows, PLE_DIM), lambda i: (i, 0)),
            const((W_LRU + W_ATT, D_MODEL)),
            const((1, D_MODEL)),
            const((D_MODEL, D_MODEL)),
            const((PLE_DIM, D_MODEL)),
        ],
        out_specs=pl.BlockSpec((rows, D_MODEL), lambda i: (i, 0)),
        out_shape=jax.ShapeDtypeStruct((n, D_MODEL), F32),
        compiler_params=pltpu.CompilerParams(
            dimension_semantics=("arbitrary",), vmem_limit_bytes=VMEM_LIMIT),
        name="back",
    )(x2, lru2, att2, p2, wo, pg, wpg, wpe)


def _front_sample_kernel(x_ref, ng_ref, win_ref, cw_ref, cbias_ref, wg_ref, bg_ref, lam_ref,
                         qg_ref, kg_ref, ones_ref, sconv_ref, slru_ref,
                         lru_ref, q_ref, k_ref, v_ref, ga_ref, sc_ref, sh_ref,
                         cb_scr, xc_scr, a_scr, u_scr, h_scr, *, nb, tt):
    seg = tt + SUBLANES
    xn = _rms_rows(x_ref[...], ng_ref[...]).astype(BF16)
    z = _dot(xn, win_ref[...])
    xl = z[:, :W_LRU]
    for s in range(nb):
        cb_scr[s * seg:s * seg + SUBLANES, :] = sconv_ref[s]
        cb_scr[s * seg + SUBLANES:(s + 1) * seg, :] = xl[s * tt:(s + 1) * tt, :]
        xc_scr[s * tt:(s + 1) * tt, :] = _conv_rows(cb_scr, s * seg, tt, cw_ref, cbias_ref[...])
        sc_ref[s] = cb_scr[(s + 1) * seg - (CONV_W - 1):(s + 1) * seg, :]
    _lru_inputs(xc_scr[...], wg_ref, bg_ref, lam_ref, a_scr, u_scr)
    for s in range(nb):
        sh_ref[s] = _scan_rows(a_scr, u_scr, h_scr, s * tt, tt, slru_ref[s])
    lru_ref[...] = (h_scr[...] * jax.nn.silu(z[:, W_LRU:2 * W_LRU])).astype(BF16)

    o = 2 * W_LRU
    q = _head_norm_rows(z[:, o:o + W_ATT], ones_ref[...], qg_ref[...])
    q_ref[...] = (q * (HEAD_DIM ** -0.5)).astype(BF16)
    k_ref[...] = _head_norm_rows(z[:, o + W_ATT:o + 2 * W_ATT], ones_ref[...], kg_ref[...])
    v_ref[...] = z[:, o + 2 * W_ATT:o + 3 * W_ATT]
    ga_ref[...] = jax.nn.silu(z[:, o + 3 * W_ATT:])


def _front_sample(x2, ng, win, cw, cbias, wg, bg, lam, qg_t, kg_t, ones_bd, sconv_pad, slru, nb, tt):
    n = x2.shape[0]
    rows = nb * tt
    const = lambda shape: pl.BlockSpec(shape, lambda i: (0,) * len(shape))
    row_spec = lambda w: pl.BlockSpec((rows, w), lambda i: (i, 0))
    nseq = n // tt
    kern = functools.partial(_front_sample_kernel, nb=nb, tt=tt)
    return pl.pallas_call(
        kern,
        grid=(n // rows,),
        in_specs=[
            row_spec(D_MODEL),
            const((1, D_MODEL)),
            const((D_MODEL, 2 * W_LRU + 4 * W_ATT)),
            const((CONV_W, W_LRU)),
            const((1, W_LRU)),
            const((W_LRU // HEAD_GROUP, HEAD_GROUP, 2 * HEAD_GROUP)),
            const((W_LRU // HEAD_GROUP, 1, 2 * HEAD_GROUP)),
            const((1, W_LRU)),
            const((1, W_ATT)),
            const((1, W_ATT)),
            const((HEAD_GROUP, HEAD_GROUP)),
            pl.BlockSpec((nb, SUBLANES, W_LRU), lambda i: (i, 0, 0)),
            pl.BlockSpec((nb, 1, W_LRU), lambda i: (i, 0, 0)),
        ],
        out_specs=[
            row_spec(W_LRU), row_spec(W_ATT), row_spec(W_ATT), row_spec(W_ATT), row_spec(W_ATT),
            pl.BlockSpec((nb, CONV_W - 1, W_LRU), lambda i: (i, 0, 0)),
            pl.BlockSpec((nb, 1, W_LRU), lambda i: (i, 0, 0)),
        ],
        out_shape=[
            jax.ShapeDtypeStruct((n, W_LRU), BF16),
            jax.ShapeDtypeStruct((n, W_ATT), BF16),
            jax.ShapeDtypeStruct((n, W_ATT), F32),
            jax.ShapeDtypeStruct((n, W_ATT), F32),
            jax.ShapeDtypeStruct((n, W_ATT), F32),
            jax.ShapeDtypeStruct((nseq, CONV_W - 1, W_LRU), F32),
            jax.ShapeDtypeStruct((nseq, 1, W_LRU), F32),
        ],
        scratch_shapes=[
            pltpu.VMEM((nb * (tt + SUBLANES), W_LRU), F32),
            pltpu.VMEM((rows, W_LRU), F32),
            pltpu.VMEM((rows, W_LRU), F32),
            pltpu.VMEM((rows, W_LRU), F32),
            pltpu.VMEM((rows, W_LRU), F32),
        ],
        compiler_params=pltpu.CompilerParams(
            dimension_semantics=("arbitrary",), vmem_limit_bytes=VMEM_LIMIT),
        name="front_sample",
    )(x2, ng, win, cw, cbias, wg, bg, lam, qg_t, kg_t, ones_bd, sconv_pad, slru)


def _attn_sample_kernel(q_ref, kn_ref, vn_ref, kc_ref, vc_ref, w_ref, ga_ref, out_ref, bias_ref):
    tt = q_ref.shape[1]
    lc = kc_ref.shape[1]

    @pl.when(pl.program_id(0) == 0)
    def _():
        for h in range(N_HEADS):
            g, hl = divmod(h, HEADS_PER_GROUP)
            bias_ref[g, hl * tt:(hl + 1) * tt, :] = _toeplitz_rows(w_ref[h], tt, 0)

    lanes = lax.broadcasted_iota(jnp.int32, (tt, HEAD_GROUP), 1)
    masks = [(lanes >= hl * HEAD_DIM) & (lanes < (hl + 1) * HEAD_DIM) for hl in range(HEADS_PER_GROUP)]
    units = [(s, g) for s in range(q_ref.shape[0]) for g in range(N_GROUPS)]

    def scores(s, g):
        gsl = slice(g * HEAD_GROUP, (g + 1) * HEAD_GROUP)
        qg = q_ref[s, :, gsl]
        qs = jnp.concatenate([jnp.where(m, qg, jnp.zeros((), BF16)) for m in masks], axis=0)
        sc = _dot_nt(qs, kc_ref[s, :, gsl]) + bias_ref[g, :, 0:lc]
        sn = _dot_nt(qs, kn_ref[s, :, gsl].astype(BF16)) + bias_ref[g, :, lc:lc + tt]
        return sc, sn

    def finish(s, g, sc, sn):
        gsl = slice(g * HEAD_GROUP, (g + 1) * HEAD_GROUP)
        m = jnp.maximum(jnp.max(sc, axis=-1, keepdims=True), jnp.max(sn, axis=-1, keepdims=True))
        pc = jnp.exp(sc - m)
        pn = jnp.exp(sn - m)
        l = jnp.sum(pc, axis=-1, keepdims=True) + jnp.sum(pn, axis=-1, keepdims=True)
        o = _dot(pc.astype(BF16), vc_ref[s, :, gsl])
        o = (o + _dot(pn.astype(BF16), vn_ref[s, :, gsl].astype(BF16))) * (1.0 / l)
        att = jnp.zeros((tt, HEAD_GROUP), F32)
        for hl in range(HEADS_PER_GROUP):
            att = att + jnp.where(masks[hl], o[hl * tt:(hl + 1) * tt, :], 0.0)
        out_ref[s, :, gsl] = (att * ga_ref[s, :, gsl]).astype(BF16)

    nxt = scores(*units[0])
    for n, (s, g) in enumerate(units):
        cur = nxt
        if n + 1 < len(units):
            nxt = scores(*units[n + 1])
        finish(s, g, *cur)


def _attn_sample(q3, k3, v3, kc, vc, w_bias, ga3):
    b, tt, _ = q3.shape
    l = kc.shape[1]
    period = w_bias.shape[-1]
    ns = SAMPLE_SEQS_PER_STEP
    assert b % ns == 0
    seq = lambda r: pl.BlockSpec((ns, r, W_ATT), lambda i: (i, 0, 0))
    return pl.pallas_call(
        _attn_sample_kernel,
        grid=(b // ns,),
        in_specs=[
            seq(tt), seq(tt), seq(tt), seq(l), seq(l),
            pl.BlockSpec((N_HEADS, 1, period), lambda i: (0, 0, 0)),
            seq(tt),
        ],
        out_specs=seq(tt),
        out_shape=jax.ShapeDtypeStruct((b, tt, W_ATT), BF16),
        scratch_shapes=[pltpu.VMEM((N_GROUPS, HEADS_PER_GROUP * tt, period), F32)],
        compiler_params=pltpu.CompilerParams(
            dimension_semantics=("arbitrary",), vmem_limit_bytes=VMEM_LIMIT),
        name="attn_sample",
    )(q3, k3, v3, kc, vc, w_bias, ga3)


def _block_diag(w):
    n, d, e = w.shape
    eye = jnp.eye(n, dtype=w.dtype)
    return (eye[:, None, :, None] * w[:, :, None, :]).reshape(n * d, n * e)


def _prompt_bias_period(table):
    assert BAND - MAX_REL == MAX_REL and PROMPT_BIAS_PERIOD == KEY_TILES * QBLOCK + QBLOCK
    last = table[2 * MAX_REL:]
    neg_d = jnp.concatenate([table, jnp.broadcast_to(last, (MAX_REL - 1, N_HEADS))])
    w = jnp.concatenate([jnp.broadcast_to(last, (QBLOCK, N_HEADS)), neg_d])
    return w.T.reshape(N_HEADS, 1, PROMPT_BIAS_PERIOD).astype(F32)


def _sample_bias_period(table, tt, l):
    assert l >= MAX_REL
    period = -(-(l + 2 * tt - 1) // LANES) * LANES
    last = table[2 * MAX_REL:]
    n_var = tt + MAX_REL - 1
    var = table[2 * MAX_REL - 1:2 * MAX_REL - 1 - n_var:-1]
    w = jnp.concatenate([jnp.broadcast_to(last, (l - MAX_REL + 1, N_HEADS)), var,
                         jnp.broadcast_to(last, (period - (l + tt), N_HEADS))])
    return w.T.reshape(N_HEADS, 1, period).astype(F32)


def kernel(x_prompt, x_sample, p_prompt, p_sample, cache_k, cache_v, state_conv, state_lru, norm_g, w_in, conv_w, conv_b, gate_a_w, gate_a_b, gate_x_w, gate_x_b, lru_lambda, q_norm_g, k_norm_g, rel_bias, w_out, ple_norm_g, w_ple_gate, w_ple_proj):
    depth = w_in.shape[0]
    b, s, _ = x_prompt.shape
    db, ds, _ = x_sample.shape
    lc = cache_k.shape[2]
    yp, ys = x_prompt, x_sample.reshape(db * ds, D_MODEL)
    ones_bd = _block_diag(jnp.full((HEADS_PER_GROUP, HEAD_DIM, HEAD_DIM), 1.0 / HEAD_DIM, F32)).astype(BF16)
    gate_halves = W_LRU // HEAD_GROUP
    blocks_per_half = LRU_BLOCKS // gate_halves
    outs = [[] for _ in range(8)]
    sample_nb = 8
    for l in range(depth):
        win = w_in[l].astype(BF16)
        o = 2 * W_LRU
        wnat = jnp.concatenate([win[:, :o], win[:, o + W_ATT:o + 2 * W_ATT]], axis=1)
        wt = jnp.concatenate([win[:, o:o + W_ATT], win[:, o + 2 * W_ATT:]], axis=1).T
        ng = norm_g[l].reshape(1, D_MODEL)
        cw = conv_w[l]
        cbias = conv_b[l].reshape(1, W_LRU)
        wg = jnp.stack([
            jnp.concatenate([_block_diag(w[j * blocks_per_half:(j + 1) * blocks_per_half])
                             for w in (gate_a_w[l], gate_x_w[l])], axis=1)
            for j in range(gate_halves)]).astype(BF16)
        bg = jnp.concatenate([gate_a_b[l].reshape(gate_halves, 1, HEAD_GROUP),
                              gate_x_b[l].reshape(gate_halves, 1, HEAD_GROUP)], axis=2)
        lam = lru_lambda[l].reshape(1, W_LRU)
        qg_col = q_norm_g[l].reshape(HEAD_DIM, 1)
        qg_t = jnp.tile(q_norm_g[l], N_HEADS).reshape(1, W_ATT)
        kg_t = jnp.tile(k_norm_g[l], N_HEADS).reshape(1, W_ATT)
        wo = w_out[l].astype(BF16)
        pg = ple_norm_g[l].reshape(1, D_MODEL)
        wpg = w_ple_gate[l].astype(BF16)
        wpe = w_ple_proj[l].astype(BF16)

        lru_g, qt, kbf, vt, gat, pk, pv, pc, ph = _front_prompt(
            yp, ng, wnat, wt, cw, cbias, wg, bg, lam, qg_col, kg_t, ones_bd)
        att_g = _attn_prompt(qt, kbf, vt, _prompt_bias_period(rel_bias[l]), gat)
        yp = _back(yp.reshape(b * s, D_MODEL), lru_g.reshape(b * s, W_LRU), att_g.reshape(b * s, W_ATT),
                   p_prompt[l].reshape(b * s, PLE_DIM), wo, pg, wpg, wpe, 512).reshape(b, s, D_MODEL)
        keep = pk.shape[1]
        outs[0].append(pk.reshape(b, keep, N_HEADS, HEAD_DIM))
        outs[1].append(pv.reshape(b, keep, N_HEADS, HEAD_DIM))
        outs[2].append(pc)
        outs[3].append(ph.reshape(b, W_LRU))

        sconv_pad = jnp.pad(state_conv[l], ((0, 0), (SUBLANES - (CONV_W - 1), 0), (0, 0)))
        lru_s, q_s, k_s, v_s, ga_s, sc, sh = _front_sample(
            ys, ng, win, cw, cbias, wg, bg, lam, qg_t, kg_t, ones_bd,
            sconv_pad, state_lru[l].reshape(db, 1, W_LRU), sample_nb, ds)
        att_s = _attn_sample(q_s.reshape(db, ds, W_ATT), k_s.reshape(db, ds, W_ATT), v_s.reshape(db, ds, W_ATT),
                             cache_k[l].reshape(db, lc, W_ATT).astype(BF16),
                             cache_v[l].reshape(db, lc, W_ATT).astype(BF16),
                             _sample_bias_period(rel_bias[l], ds, lc), ga_s.reshape(db, ds, W_ATT))
        ys = _back(ys, lru_s, att_s.reshape(db * ds, W_ATT), p_sample[l].reshape(db * ds, PLE_DIM),
                   wo, pg, wpg, wpe, 256)
        outs[4].append(k_s.reshape(db, ds, N_HEADS, HEAD_DIM))
        outs[5].append(v_s.reshape(db, ds, N_HEADS, HEAD_DIM))
        outs[6].append(sc)
        outs[7].append(sh.reshape(db, W_LRU))
    return (yp, ys.reshape(db, ds, D_MODEL)) + tuple(jnp.stack(o) for o in outs)
```

```python
import functools

import jax
import jax.numpy as jnp
from jax import lax
from jax.experimental import pallas as pl
from jax.experimental.pallas import tpu as pltpu

D_MODEL = 1024
CHUNK = 64
PAST_CHUNKS = 8
BAND = PAST_CHUNKS * CHUNK
W_LRU = D_MODEL // 2
LRU_BLOCKS = 8
LRU_BLOCK = W_LRU // LRU_BLOCKS
CONV_W = 4
RG_C = 8.0
HEAD_DIM = 64
W_ATT = D_MODEL // 2
N_HEADS = W_ATT // HEAD_DIM
MAX_REL = 256
PLE_DIM = 256
EPS = 1e-6
NEG = -1e30
LOG2E = 1.4426950408889634

SUBLANES = 8
LANES = 128
HEAD_GROUP = 256
HEADS_PER_GROUP = HEAD_GROUP // HEAD_DIM
N_GROUPS = W_ATT // HEAD_GROUP
QBLOCK = 256
KEY_TILES = BAND // QBLOCK + 1
PROMPT_BIAS_PERIOD = (KEY_TILES + 1) * QBLOCK
FRONT_BLOCK = 1024
SAMPLE_SEQS_PER_STEP = 4
VMEM_LIMIT = 56 * 1024 * 1024

F32 = jnp.float32
BF16 = jnp.bfloat16


def _dot(a, b):
    return jnp.dot(a, b, preferred_element_type=F32)


def _dot_nt(a, b):
    return lax.dot_general(a, b, (((1,), (1,)), ((), ())), preferred_element_type=F32)


def _rms_rows(x, g):
    ms = jnp.mean(x * x, axis=-1, keepdims=True)
    return x * lax.rsqrt(ms + EPS) * g


def _head_norm_rows(x, ones_bd, g_tiled):
    x2 = x * x
    hi = x2.astype(BF16)
    lo = (x2 - hi.astype(F32)).astype(BF16)
    ms = jnp.concatenate(
        [_dot(hi[:, g * HEAD_GROUP:(g + 1) * HEAD_GROUP], ones_bd)
         + _dot(lo[:, g * HEAD_GROUP:(g + 1) * HEAD_GROUP], ones_bd) for g in range(N_GROUPS)], axis=1)
    return x * lax.rsqrt(ms + EPS) * g_tiled


def _scan_rows(a_ref, u_ref, h_ref, row0, nrows, h0, unroll=False):
    ridx = lax.broadcasted_iota(jnp.int32, (SUBLANES, W_LRU), 0)

    def body(i, hprev):
        r = pl.multiple_of(row0 + i * SUBLANES, SUBLANES)
        a = a_ref[pl.ds(r, SUBLANES), :]
        u = u_ref[pl.ds(r, SUBLANES), :]
        for s in (1, 2, 4):
            a_s = jnp.where(ridx >= s, pltpu.roll(a, s, 0), 1.0)
            u_s = jnp.where(ridx >= s, pltpu.roll(u, s, 0), 0.0)
            u = a * u_s + u
            a = a * a_s
        h = a * hprev + u
        h_ref[pl.ds(r, SUBLANES), :] = h
        return h[SUBLANES - 1:SUBLANES, :]

    return lax.fori_loop(0, nrows // SUBLANES, body, h0, unroll=unroll)


def _lru_inputs(xc, wg_ref, bg_ref, lam_ref, a_ref, u_ref):
    xcb = xc.astype(BF16)
    half = wg_ref.shape[1]
    for j in range(wg_ref.shape[0]):
        sl = slice(j * half, (j + 1) * half)
        gates = _dot(xcb[:, sl], wg_ref[j]) + bg_ref[j]
        r = jax.nn.sigmoid(gates[:, :half])
        i = jax.nn.sigmoid(gates[:, half:])
        log_a = -RG_C * r * jax.nn.softplus(-lam_ref[:, sl])
        a = jnp.exp(log_a)
        a_ref[:, sl] = a
        u_ref[:, sl] = jnp.sqrt(jnp.tanh(-log_a) * (1.0 + a * a)) * (i * xc[:, sl])


def _conv_rows(cb_ref, base, nrows, cw_ref, cb_bias):
    out = cb_bias + cw_ref[CONV_W - 1:CONV_W, :] * cb_ref[pl.ds(base + SUBLANES, nrows), :]
    for k in range(CONV_W - 1):
        shift = CONV_W - 1 - k
        out = out + cw_ref[k:k + 1, :] * cb_ref[pl.ds(base + SUBLANES - shift, nrows), :]
    return out


def _front_prompt_kernel(x_ref, ng_ref, wnat_ref, wt_ref, cw_ref, cbias_ref, wg_ref, bg_ref, lam_ref,
                         qg_ref, kg_ref, ones_ref,
                         lru_ref, qt_ref, kbf_ref, vt_ref, gat_ref, pk_ref, pv_ref, pc_ref, ph_ref,
                         zn_scr, zt_scr, cb_scr, a_scr, u_scr, h_scr, hlast_scr, *, keep_subs):
    sub = QBLOCK
    n_sub = x_ref.shape[1] // sub

    @pl.when(pl.program_id(1) == 0)
    def _():
        cb_scr[0:SUBLANES, :] = jnp.zeros((SUBLANES, W_LRU), F32)
        hlast_scr[...] = jnp.zeros((1, W_LRU), F32)

    def project(i):
        xn = _rms_rows(x_ref[0, i * sub:(i + 1) * sub, :], ng_ref[...]).astype(BF16)
        zn_scr[i % 2] = _dot(xn, wnat_ref[...])
        zt_scr[i % 2] = _dot_nt(wt_ref[...], xn)

    def finish(i):
        rows = slice(i * sub, (i + 1) * sub)
        zn = zn_scr.at[i % 2]
        zt = zt_scr.at[i % 2]
        cb_scr[SUBLANES:SUBLANES + sub, :] = zn[:, :W_LRU]
        xc = _conv_rows(cb_scr, 0, sub, cw_ref, cbias_ref[...])
        pc_ref[0] = cb_scr[sub + SUBLANES - (CONV_W - 1):sub + SUBLANES, :]
        cb_scr[0:SUBLANES, :] = cb_scr[sub:sub + SUBLANES, :]
        _lru_inputs(xc, wg_ref, bg_ref, lam_ref, a_scr, u_scr)
        h_last = _scan_rows(a_scr, u_scr, h_scr, 0, sub, hlast_scr[...], unroll=True)
        hlast_scr[...] = h_last
        ph_ref[0] = h_last
        lru_ref[0, rows, :] = (h_scr[...] * jax.nn.silu(zn[:, W_LRU:2 * W_LRU])).astype(BF16)
        k = _head_norm_rows(zn[:, 2 * W_LRU:], ones_ref[...], kg_ref[...])
        kbf_ref[0, rows, :] = k.astype(BF16)
        q3 = zt[0:W_ATT, :].reshape(N_HEADS, HEAD_DIM, sub)
        ms = jnp.mean(q3 * q3, axis=1, keepdims=True)
        qn = q3 * lax.rsqrt(ms + EPS) * (qg_ref[...] * (HEAD_DIM ** -0.5 * LOG2E))
        qt_ref[0, :, rows] = qn.reshape(W_ATT, sub).astype(BF16)
        vt = zt[W_ATT:2 * W_ATT, :]
        vt_ref[0, :, rows] = vt.astype(BF16)
        gat_ref[0, :, rows] = jax.nn.silu(zt[2 * W_ATT:, :])
        if i >= n_sub - keep_subs:
            first = (i - (n_sub - keep_subs)) * sub
            pk_ref[0, :, first:first + sub] = k.T
            pv_ref[0, :, first:first + sub] = vt

    project(0)
    for i in range(n_sub):
        if i + 1 < n_sub:
            project(i + 1)
        finish(i)


def _front_prompt(x, ng, wnat, wt, cw, cbias, wg, bg, lam, qg, kg, ones_bd):
    b, s, _ = x.shape
    tb = min(FRONT_BLOCK, s)
    keep = min(BAND, s)
    assert s % tb == 0 and tb % QBLOCK == 0 and keep % QBLOCK == 0 and keep <= tb
    const = lambda shape: pl.BlockSpec(shape, lambda i, j: (0,) * len(shape))
    rows_spec = lambda w: pl.BlockSpec((1, tb, w), lambda i, j: (i, j, 0))
    cols_spec = pl.BlockSpec((1, W_ATT, tb), lambda i, j: (i, 0, j))
    per_seq = lambda r, w: pl.BlockSpec((1, r, w), lambda i, j: (i, 0, 0))
    kern = functools.partial(_front_prompt_kernel, keep_subs=keep // QBLOCK)
    return pl.pallas_call(
        kern,
        grid=(b, s // tb),
        in_specs=[
            rows_spec(D_MODEL),
            const((1, D_MODEL)),
            const((D_MODEL, 3 * W_LRU)),
            const((3 * W_ATT, D_MODEL)),
            const((CONV_W, W_LRU)),
            const((1, W_LRU)),
            const((W_LRU // HEAD_GROUP, HEAD_GROUP, 2 * HEAD_GROUP)),
            const((W_LRU // HEAD_GROUP, 1, 2 * HEAD_GROUP)),
            const((1, W_LRU)),
            const((HEAD_DIM, 1)),
            const((1, W_ATT)),
            const((HEAD_GROUP, HEAD_GROUP)),
        ],
        out_specs=[
            rows_spec(W_LRU),
            cols_spec,
            rows_spec(W_ATT),
            cols_spec,
            cols_spec,
            per_seq(W_ATT, keep),
            per_seq(W_ATT, keep),
            per_seq(CONV_W - 1, W_LRU),
            per_seq(1, W_LRU),
        ],
        out_shape=[
            jax.ShapeDtypeStruct((b, s, W_LRU), BF16),
            jax.ShapeDtypeStruct((b, W_ATT, s), BF16),
            jax.ShapeDtypeStruct((b, s, W_ATT), BF16),
            jax.ShapeDtypeStruct((b, W_ATT, s), BF16),
            jax.ShapeDtypeStruct((b, W_ATT, s), F32),
            jax.ShapeDtypeStruct((b, W_ATT, keep), F32),
            jax.ShapeDtypeStruct((b, W_ATT, keep), F32),
            jax.ShapeDtypeStruct((b, CONV_W - 1, W_LRU), F32),
            jax.ShapeDtypeStruct((b, 1, W_LRU), F32),
        ],
        scratch_shapes=[
            pltpu.VMEM((2, QBLOCK, 3 * W_LRU), F32),
            pltpu.VMEM((2, 3 * W_ATT, QBLOCK), F32),
            pltpu.VMEM((QBLOCK + SUBLANES, W_LRU), F32),
            pltpu.VMEM((QBLOCK, W_LRU), F32),
            pltpu.VMEM((QBLOCK, W_LRU), F32),
            pltpu.VMEM((QBLOCK, W_LRU), F32),
            pltpu.VMEM((1, W_LRU), F32),
        ],
        compiler_params=pltpu.CompilerParams(
            dimension_semantics=("arbitrary", "arbitrary"), vmem_limit_bytes=VMEM_LIMIT),
        name="front_prompt",
    )(x, ng, wnat, wt, cw, cbias, wg, bg, lam, qg, kg, ones_bd)


def _toeplitz_rows(w_row, nrows, row0):
    x = jnp.broadcast_to(w_row, (nrows, w_row.shape[1]))
    return pltpu.roll(x, row0, 1, stride=1, stride_axis=0)


def _fill_prompt_bias(w_ref, bias_ref):
    q_chunk = (lax.broadcasted_iota(jnp.int32, (CHUNK, QBLOCK), 1) + BAND) // CHUNK
    for h in range(N_HEADS):
        def body(n, carry, h=h):
            r0 = pl.multiple_of(n * CHUNK, CHUNK)
            t = _toeplitz_rows(w_ref[h], CHUNK, r0)[:, :QBLOCK]
            dc = q_chunk - n
            bias_ref[h, pl.ds(r0, CHUNK), :] = jnp.where((dc >= 0) & (dc <= PAST_CHUNKS), t * LOG2E, NEG)
            return carry
        lax.fori_loop(0, KEY_TILES * QBLOCK // CHUNK, body, 0)


CHUNKS_PER_TILE = QBLOCK // CHUNK
FRAMES_PER_VREG = LANES // CHUNK


def _lane_cols(kc):
    cols = []
    for c in range(QBLOCK // LANES):
        q_lo = PAST_CHUNKS + c * FRAMES_PER_VREG
        q_hi = q_lo + FRAMES_PER_VREG - 1
        if q_lo - PAST_CHUNKS <= kc <= q_hi:
            cols.append(c)
    return cols


def _fold_rows(x):
    return x.reshape(x.shape[0] // SUBLANES, SUBLANES, x.shape[1])


def _attn_prompt_heads(tiles, qt_ref, k_refs, v_refs, bias_ref, s_scr, p_scr, att_scr):
    n_cols = QBLOCK // LANES
    rows = lax.broadcasted_iota(jnp.int32, (HEAD_GROUP, QBLOCK), 0)
    pieces = []
    for i in tiles:
        for cc in range(CHUNKS_PER_TILE):
            kc = i * CHUNKS_PER_TILE + cc
            for c in _lane_cols(kc):
                pieces.append((i, cc, slice(kc * CHUNK, (kc + 1) * CHUNK), slice(c * LANES, (c + 1) * LANES), c))

    def scores(h):
        g, hl = divmod(h, HEADS_PER_GROUP)
        gsl = slice(g * HEAD_GROUP, (g + 1) * HEAD_GROUP)
        in_head = (rows >= hl * HEAD_DIM) & (rows < (hl + 1) * HEAD_DIM)
        qm = jnp.where(in_head, qt_ref[0, gsl, :], jnp.zeros((), BF16))
        s = {i: _dot(k_refs[i][0, :, gsl], qm) for i in tiles}
        m_acc = [jnp.full((SUBLANES, LANES), NEG, F32) for _ in range(n_cols)]
        for i, cc, rsl, lsl, c in pieces:
            sp = s[i][cc * CHUNK:(cc + 1) * CHUNK, lsl] + bias_ref[h, rsl, lsl]
            s_scr[h % 2, rsl, lsl] = sp
            m_acc[c] = jnp.maximum(m_acc[c], jnp.max(_fold_rows(sp), axis=0))
        return [jnp.max(a, axis=0, keepdims=True) for a in m_acc]

    def weights(h, m):
        l_acc = [jnp.zeros((SUBLANES, LANES), F32) for _ in range(n_cols)]
        for _, _, rsl, lsl, c in pieces:
            p = jnp.exp2(s_scr[h % 2, rsl, lsl] - m[c])
            l_acc[c] = l_acc[c] + jnp.sum(_fold_rows(p), axis=0)
            p_scr[h % 2, rsl, lsl] = p.astype(BF16)
        return [1.0 / jnp.sum(a, axis=0, keepdims=True) for a in l_acc]

    def values(h, l_inv):
        hsl = slice(h * HEAD_DIM, (h + 1) * HEAD_DIM)
        o = None
        for i in tiles:
            oi = _dot(v_refs[i][0, hsl, :], p_scr[h % 2, i * QBLOCK:(i + 1) * QBLOCK, :])
            o = oi if o is None else o + oi
        for c in range(n_cols):
            lsl = slice(c * LANES, (c + 1) * LANES)
            att_scr[hsl, lsl] = o[:, lsl] * l_inv[c]

    m_next = scores(0)
    for h in range(N_HEADS):
        m = m_next
        if h + 1 < N_HEADS:
            m_next = scores(h + 1)
        values(h, weights(h, m))


def _attn_prompt_kernel(qt_ref, k0_ref, k1_ref, k2_ref, v0_ref, v1_ref, v2_ref, w_ref, gat_ref,
                        out_ref, bias_ref, s_scr, p_scr, att_scr):
    j = pl.program_id(1)

    @pl.when((pl.program_id(0) == 0) & (j == 0))
    def _():
        _fill_prompt_bias(w_ref, bias_ref)
        p_scr[...] = jnp.zeros(p_scr.shape, BF16)

    k_refs = (k0_ref, k1_ref, k2_ref)
    v_refs = (v0_ref, v1_ref, v2_ref)
    for first in range(KEY_TILES):
        tiles = tuple(range(first, KEY_TILES))
        cond = (j == KEY_TILES - 1 - first) if first > 0 else (j >= KEY_TILES - 1)

        @pl.when(cond)
        def _(tiles=tiles):
            _attn_prompt_heads(tiles, qt_ref, k_refs, v_refs, bias_ref, s_scr, p_scr, att_scr)

    out_ref[0] = (att_scr[...] * gat_ref[0]).T.astype(BF16)


def _attn_prompt(qt, kbf, vt, bias_t, gat):
    b, _, s = qt.shape
    n_blocks = s // QBLOCK
    back = KEY_TILES - 1
    k_specs = [pl.BlockSpec((1, QBLOCK, W_ATT),
                            functools.partial(lambda i, j, d: (i, jnp.maximum(j - d, 0), 0), d=back - n))
               for n in range(KEY_TILES)]
    v_specs = [pl.BlockSpec((1, W_ATT, QBLOCK),
                            functools.partial(lambda i, j, d: (i, 0, jnp.maximum(j - d, 0)), d=back - n))
               for n in range(KEY_TILES)]
    return pl.pallas_call(
        _attn_prompt_kernel,
        grid=(b, n_blocks),
        in_specs=[pl.BlockSpec((1, W_ATT, QBLOCK), lambda i, j: (i, 0, j))] + k_specs + v_specs + [
            pl.BlockSpec((N_HEADS, 1, PROMPT_BIAS_PERIOD), lambda i, j: (0, 0, 0)),
            pl.BlockSpec((1, W_ATT, QBLOCK), lambda i, j: (i, 0, j)),
        ],
        out_specs=pl.BlockSpec((1, QBLOCK, W_ATT), lambda i, j: (i, j, 0)),
        out_shape=jax.ShapeDtypeStruct((b, s, W_ATT), BF16),
        scratch_shapes=[
            pltpu.VMEM((N_HEADS, KEY_TILES * QBLOCK, QBLOCK), F32),
            pltpu.VMEM((2, KEY_TILES * QBLOCK, QBLOCK), F32),
            pltpu.VMEM((2, KEY_TILES * QBLOCK, QBLOCK), BF16),
            pltpu.VMEM((W_ATT, QBLOCK), F32),
        ],
        compiler_params=pltpu.CompilerParams(
            dimension_semantics=("arbitrary", "arbitrary"), vmem_limit_bytes=VMEM_LIMIT),
        name="attn_prompt",
    )(qt, kbf, kbf, kbf, vt, vt, vt, bias_t, gat)


def _back_kernel(x_ref, lru_ref, att_ref, p_ref, wo_ref, pg_ref, wpg_ref, wpe_ref, y_ref):
    mix = _dot(lru_ref[...], wo_ref[0:W_LRU, :]) + _dot(att_ref[...], wo_ref[W_LRU:, :])
    h = x_ref[...] + mix
    gate = jax.nn.sigmoid(_dot(_rms_rows(h, pg_ref[...]).astype(BF16), wpg_ref[...]))
    y_ref[...] = h + _dot(p_ref[...].astype(BF16), wpe_ref[...]) * gate


def _back(x2, lru2, att2, p2, wo, pg, wpg, wpe, rows):
    n = x2.shape[0]
    const = lambda shape: pl.BlockSpec(shape, lambda i: (0,) * len(shape))
    return pl.pallas_call(
        _back_kernel,
        grid=(n // rows,),
        in_specs=[
            pl.BlockSpec((rows, D_MODEL), lambda i: (i, 0)),
            pl.BlockSpec((rows, W_LRU), lambda i: (i, 0)),
            pl.BlockSpec((rows, W_ATT), lambda i: (i, 0)),
            pl.BlockSpec((rows, PLE_DIM), lambda i: (i, 0)),
            const((W_LRU + W_ATT, D_MODEL)),
            const((1, D_MODEL)),
            const((D_MODEL, D_MODEL)),
            const((PLE_DIM, D_MODEL)),
        ],
        out_specs=pl.BlockSpec((rows, D_MODEL), lambda i: (i, 0)),
        out_shape=jax.ShapeDtypeStruct((n, D_MODEL), F32),
        compiler_params=pltpu.CompilerParams(
            dimension_semantics=("arbitrary",), vmem_limit_bytes=VMEM_LIMIT),
        name="back",
    )(x2, lru2, att2, p2, wo, pg, wpg, wpe)


def _front_sample_kernel(x_ref, ng_ref, win_ref, cw_ref, cbias_ref, wg_ref, bg_ref, lam_ref,
                         qg_ref, kg_ref, ones_ref, sconv_ref, slru_ref,
                         lru_ref, q_ref, k_ref, v_ref, ga_ref, sc_ref, sh_ref,
                         cb_scr, xc_scr, a_scr, u_scr, h_scr, *, nb, tt):
    seg = tt + SUBLANES
    xn = _rms_rows(x_ref[...], ng_ref[...]).astype(BF16)
    z = _dot(xn, win_ref[...])
    xl = z[:, :W_LRU]
    for s in range(nb):
        cb_scr[s * seg:s * seg + SUBLANES, :] = sconv_ref[s]
        cb_scr[s * seg + SUBLANES:(s + 1) * seg, :] = xl[s * tt:(s + 1) * tt, :]
        xc_scr[s * tt:(s + 1) * tt, :] = _conv_rows(cb_scr, s * seg, tt, cw_ref, cbias_ref[...])
        sc_ref[s] = cb_scr[(s + 1) * seg - (CONV_W - 1):(s + 1) * seg, :]
    _lru_inputs(xc_scr[...], wg_ref, bg_ref, lam_ref, a_scr, u_scr)
    for s in range(nb):
        sh_ref[s] = _scan_rows(a_scr, u_scr, h_scr, s * tt, tt, slru_ref[s])
    lru_ref[...] = (h_scr[...] * jax.nn.silu(z[:, W_LRU:2 * W_LRU])).astype(BF16)

    o = 2 * W_LRU
    q = _head_norm_rows(z[:, o:o + W_ATT], ones_ref[...], qg_ref[...])
    q_ref[...] = (q * (HEAD_DIM ** -0.5)).astype(BF16)
    k_ref[...] = _head_norm_rows(z[:, o + W_ATT:o + 2 * W_ATT], ones_ref[...], kg_ref[...])
    v_ref[...] = z[:, o + 2 * W_ATT:o + 3 * W_ATT]
    ga_ref[...] = jax.nn.silu(z[:, o + 3 * W_ATT:])


def _front_sample(x2, ng, win, cw, cbias, wg, bg, lam, qg_t, kg_t, ones_bd, sconv_pad, slru, nb, tt):
    n = x2.shape[0]
    rows = nb * tt
    const = lambda shape: pl.BlockSpec(shape, lambda i: (0,) * len(shape))
    row_spec = lambda w: pl.BlockSpec((rows, w), lambda i: (i, 0))
    nseq = n // tt
    kern = functools.partial(_front_sample_kernel, nb=nb, tt=tt)
    return pl.pallas_call(
        kern,
        grid=(n // rows,),
        in_specs=[
            row_spec(D_MODEL),
            const((1, D_MODEL)),
            const((D_MODEL, 2 * W_LRU + 4 * W_ATT)),
            const((CONV_W, W_LRU)),
            const((1, W_LRU)),
            const((W_LRU // HEAD_GROUP, HEAD_GROUP, 2 * HEAD_GROUP)),
            const((W_LRU // HEAD_GROUP, 1, 2 * HEAD_GROUP)),
            const((1, W_LRU)),
            const((1, W_ATT)),
            const((1, W_ATT)),
            const((HEAD_GROUP, HEAD_GROUP)),
            pl.BlockSpec((nb, SUBLANES, W_LRU), lambda i: (i, 0, 0)),
            pl.BlockSpec((nb, 1, W_LRU), lambda i: (i, 0, 0)),
        ],
        out_specs=[
            row_spec(W_LRU), row_spec(W_ATT), row_spec(W_ATT), row_spec(W_ATT), row_spec(W_ATT),
            pl.BlockSpec((nb, CONV_W - 1, W_LRU), lambda i: (i, 0, 0)),
            pl.BlockSpec((nb, 1, W_LRU), lambda i: (i, 0, 0)),
        ],
        out_shape=[
            jax.ShapeDtypeStruct((n, W_LRU), BF16),
            jax.ShapeDtypeStruct((n, W_ATT), BF16),
            jax.ShapeDtypeStruct((n, W_ATT), F32),
            jax.ShapeDtypeStruct((n, W_ATT), F32),
            jax.ShapeDtypeStruct((n, W_ATT), F32),
            jax.ShapeDtypeStruct((nseq, CONV_W - 1, W_LRU), F32),
            jax.ShapeDtypeStruct((nseq, 1, W_LRU), F32),
        ],
        scratch_shapes=[
            pltpu.VMEM((nb * (tt + SUBLANES), W_LRU), F32),
            pltpu.VMEM((rows, W_LRU), F32),
            pltpu.VMEM((rows, W_LRU), F32),
            pltpu.VMEM((rows, W_LRU), F32),
            pltpu.VMEM((rows, W_LRU), F32),
        ],
        compiler_params=pltpu.CompilerParams(
            dimension_semantics=("arbitrary",), vmem_limit_bytes=VMEM_LIMIT),
        name="front_sample",
    )(x2, ng, win, cw, cbias, wg, bg, lam, qg_t, kg_t, ones_bd, sconv_pad, slru)


def _attn_sample_kernel(q_ref, kn_ref, vn_ref, kc_ref, vc_ref, w_ref, ga_ref, out_ref, bias_ref):
    tt = q_ref.shape[1]
    lc = kc_ref.shape[-1]

    @pl.when(pl.program_id(0) == 0)
    def _():
        for h in range(N_HEADS):
            g, hl = divmod(h, HEADS_PER_GROUP)
            bias_ref[g, hl * tt:(hl + 1) * tt, :] = _toeplitz_rows(w_ref[h], tt, 0)

    lanes = lax.broadcasted_iota(jnp.int32, (tt, HEAD_GROUP), 1)
    masks = [(lanes >= hl * HEAD_DIM) & (lanes < (hl + 1) * HEAD_DIM) for hl in range(HEADS_PER_GROUP)]
    units = [(s, g) for s in range(q_ref.shape[0]) for g in range(N_GROUPS)]

    def cached(ref, s, g):
        heads = ref[s, g * HEADS_PER_GROUP:(g + 1) * HEADS_PER_GROUP]
        return heads.reshape(HEAD_GROUP, lc).astype(BF16)

    def scores(s, g):
        gsl = slice(g * HEAD_GROUP, (g + 1) * HEAD_GROUP)
        qg = q_ref[s, :, gsl]
        qs = jnp.concatenate([jnp.where(m, qg, jnp.zeros((), BF16)) for m in masks], axis=0)
        sc = _dot(qs, cached(kc_ref, s, g)) + bias_ref[g, :, 0:lc]
        sn = _dot_nt(qs, kn_ref[s, :, gsl].astype(BF16)) + bias_ref[g, :, lc:lc + tt]
        return sc, sn

    def finish(s, g, sc, sn):
        gsl = slice(g * HEAD_GROUP, (g + 1) * HEAD_GROUP)
        m = jnp.maximum(jnp.max(sc, axis=-1, keepdims=True), jnp.max(sn, axis=-1, keepdims=True))
        pc = jnp.exp(sc - m)
        pn = jnp.exp(sn - m)
        l = jnp.sum(pc, axis=-1, keepdims=True) + jnp.sum(pn, axis=-1, keepdims=True)
        o = _dot_nt(pc.astype(BF16), cached(vc_ref, s, g))
        o = (o + _dot(pn.astype(BF16), vn_ref[s, :, gsl].astype(BF16))) * (1.0 / l)
        att = jnp.zeros((tt, HEAD_GROUP), F32)
        for hl in range(HEADS_PER_GROUP):
            att = att + jnp.where(masks[hl], o[hl * tt:(hl + 1) * tt, :], 0.0)
        out_ref[s, :, gsl] = (att * ga_ref[s, :, gsl]).astype(BF16)

    nxt = scores(*units[0])
    for n, (s, g) in enumerate(units):
        cur = nxt
        if n + 1 < len(units):
            nxt = scores(*units[n + 1])
        finish(s, g, *cur)


def _attn_sample(q3, k3, v3, kc_t, vc_t, w_bias, ga3):
    b, tt, _ = q3.shape
    l = kc_t.shape[-1]
    period = w_bias.shape[-1]
    ns = SAMPLE_SEQS_PER_STEP
    assert b % ns == 0
    seq = lambda r: pl.BlockSpec((ns, r, W_ATT), lambda i: (i, 0, 0))
    cache = pl.BlockSpec((ns, N_HEADS, HEAD_DIM, l), lambda i: (i, 0, 0, 0))
    return pl.pallas_call(
        _attn_sample_kernel,
        grid=(b // ns,),
        in_specs=[
            seq(tt), seq(tt), seq(tt), cache, cache,
            pl.BlockSpec((N_HEADS, 1, period), lambda i: (0, 0, 0)),
            seq(tt),
        ],
        out_specs=seq(tt),
        out_shape=jax.ShapeDtypeStruct((b, tt, W_ATT), BF16),
        scratch_shapes=[pltpu.VMEM((N_GROUPS, HEADS_PER_GROUP * tt, period), F32)],
        compiler_params=pltpu.CompilerParams(
            dimension_semantics=("arbitrary",), vmem_limit_bytes=VMEM_LIMIT),
        name="attn_sample",
    )(q3, k3, v3, kc_t, vc_t, w_bias, ga3)


def _block_diag(w):
    n, d, e = w.shape
    eye = jnp.eye(n, dtype=w.dtype)
    return (eye[:, None, :, None] * w[:, :, None, :]).reshape(n * d, n * e)


def _prompt_bias_period(table):
    assert BAND - MAX_REL == MAX_REL and PROMPT_BIAS_PERIOD == KEY_TILES * QBLOCK + QBLOCK
    last = table[2 * MAX_REL:]
    neg_d = jnp.concatenate([table, jnp.broadcast_to(last, (MAX_REL - 1, N_HEADS))])
    w = jnp.concatenate([jnp.broadcast_to(last, (QBLOCK, N_HEADS)), neg_d])
    return w.T.reshape(N_HEADS, 1, PROMPT_BIAS_PERIOD).astype(F32)


def _sample_bias_period(table, tt, l):
    assert l >= MAX_REL
    period = -(-(l + 2 * tt - 1) // LANES) * LANES
    last = table[2 * MAX_REL:]
    n_var = tt + MAX_REL - 1
    var = table[2 * MAX_REL - 1:2 * MAX_REL - 1 - n_var:-1]
    w = jnp.concatenate([jnp.broadcast_to(last, (l - MAX_REL + 1, N_HEADS)), var,
                         jnp.broadcast_to(last, (period - (l + tt), N_HEADS))])
    return w.T.reshape(N_HEADS, 1, period).astype(F32)


def kernel(x_prompt, x_sample, p_prompt, p_sample, cache_k, cache_v, state_conv, state_lru, norm_g, w_in, conv_w, conv_b, gate_a_w, gate_a_b, gate_x_w, gate_x_b, lru_lambda, q_norm_g, k_norm_g, rel_bias, w_out, ple_norm_g, w_ple_gate, w_ple_proj):
    depth = w_in.shape[0]
    b, s, _ = x_prompt.shape
    db, ds, _ = x_sample.shape
    lc = cache_k.shape[2]
    yp, ys = x_prompt, x_sample.reshape(db * ds, D_MODEL)
    ones_bd = _block_diag(jnp.full((HEADS_PER_GROUP, HEAD_DIM, HEAD_DIM), 1.0 / HEAD_DIM, F32)).astype(BF16)
    gate_halves = W_LRU // HEAD_GROUP
    blocks_per_half = LRU_BLOCKS // gate_halves
    outs = [[] for _ in range(8)]
    sample_nb = 8
    for l in range(depth):
        win = w_in[l].astype(BF16)
        o = 2 * W_LRU
        wnat = jnp.concatenate([win[:, :o], win[:, o + W_ATT:o + 2 * W_ATT]], axis=1)
        wt = jnp.concatenate([win[:, o:o + W_ATT], win[:, o + 2 * W_ATT:]], axis=1).T
        ng = norm_g[l].reshape(1, D_MODEL)
        cw = conv_w[l]
        cbias = conv_b[l].reshape(1, W_LRU)
        wg = jnp.stack([
            jnp.concatenate([_block_diag(w[j * blocks_per_half:(j + 1) * blocks_per_half])
                             for w in (gate_a_w[l], gate_x_w[l])], axis=1)
            for j in range(gate_halves)]).astype(BF16)
        bg = jnp.concatenate([gate_a_b[l].reshape(gate_halves, 1, HEAD_GROUP),
                              gate_x_b[l].reshape(gate_halves, 1, HEAD_GROUP)], axis=2)
        lam = lru_lambda[l].reshape(1, W_LRU)
        qg_col = q_norm_g[l].reshape(HEAD_DIM, 1)
        qg_t = jnp.tile(q_norm_g[l], N_HEADS).reshape(1, W_ATT)
        kg_t = jnp.tile(k_norm_g[l], N_HEADS).reshape(1, W_ATT)
        wo = w_out[l].astype(BF16)
        pg = ple_norm_g[l].reshape(1, D_MODEL)
        wpg = w_ple_gate[l].astype(BF16)
        wpe = w_ple_proj[l].astype(BF16)

        lru_g, qt, kbf, vt, gat, pk, pv, pc, ph = _front_prompt(
            yp, ng, wnat, wt, cw, cbias, wg, bg, lam, qg_col, kg_t, ones_bd)
        att_g = _attn_prompt(qt, kbf, vt, _prompt_bias_period(rel_bias[l]), gat)
        yp = _back(yp.reshape(b * s, D_MODEL), lru_g.reshape(b * s, W_LRU), att_g.reshape(b * s, W_ATT),
                   p_prompt[l].reshape(b * s, PLE_DIM), wo, pg, wpg, wpe, 512).reshape(b, s, D_MODEL)
        to_frames = lambda a: jnp.transpose(a.reshape(b, N_HEADS, HEAD_DIM, a.shape[-1]), (0, 3, 1, 2))
        outs[0].append(to_frames(pk))
        outs[1].append(to_frames(pv))
        outs[2].append(pc)
        outs[3].append(ph.reshape(b, W_LRU))

        sconv_pad = jnp.pad(state_conv[l], ((0, 0), (SUBLANES - (CONV_W - 1), 0), (0, 0)))
        lru_s, q_s, k_s, v_s, ga_s, sc, sh = _front_sample(
            ys, ng, win, cw, cbias, wg, bg, lam, qg_t, kg_t, ones_bd,
            sconv_pad, state_lru[l].reshape(db, 1, W_LRU), sample_nb, ds)
        att_s = _attn_sample(q_s.reshape(db, ds, W_ATT), k_s.reshape(db, ds, W_ATT), v_s.reshape(db, ds, W_ATT),
                             jnp.transpose(cache_k[l], (0, 2, 3, 1)), jnp.transpose(cache_v[l], (0, 2, 3, 1)),
                             _sample_bias_period(rel_bias[l], ds, lc), ga_s.reshape(db, ds, W_ATT))
        ys = _back(ys, lru_s, att_s.reshape(db * ds, W_ATT), p_sample[l].reshape(db * ds, PLE_DIM),
                   wo, pg, wpg, wpe, 256)
        outs[4].append(k_s.reshape(db, ds, N_HEADS, HEAD_DIM))
        outs[5].append(v_s.reshape(db, ds, N_HEADS, HEAD_DIM))
        outs[6].append(sc)
        outs[7].append(sh.reshape(db, W_LRU))
    return (yp, ys.reshape(db, ds, D_MODEL)) + tuple(jnp.stack(o) for o in outs)
```

```python
import functools

import jax
import jax.numpy as jnp
from jax import lax
from jax.experimental import pallas as pl
from jax.experimental.pallas import tpu as pltpu

D_MODEL = 1024
CHUNK = 64
PAST_CHUNKS = 8
BAND = PAST_CHUNKS * CHUNK
W_LRU = D_MODEL // 2
LRU_BLOCKS = 8
LRU_BLOCK = W_LRU // LRU_BLOCKS
CONV_W = 4
RG_C = 8.0
HEAD_DIM = 64
W_ATT = D_MODEL // 2
N_HEADS = W_ATT // HEAD_DIM
MAX_REL = 256
PLE_DIM = 256
EPS = 1e-6
NEG = -1e30
LOG2E = 1.4426950408889634

SUBLANES = 8
LANES = 128
HEAD_GROUP = 256
HEADS_PER_GROUP = HEAD_GROUP // HEAD_DIM
N_GROUPS = W_ATT // HEAD_GROUP
QBLOCK = 256
KEY_TILES = BAND // QBLOCK + 1
PROMPT_BIAS_PERIOD = (KEY_TILES + 1) * QBLOCK
FRONT_BLOCK = 1024
SAMPLE_SEQS_PER_STEP = 4
VMEM_LIMIT = 56 * 1024 * 1024

F32 = jnp.float32
BF16 = jnp.bfloat16


def _dot(a, b):
    return jnp.dot(a, b, preferred_element_type=F32)


def _dot_nt(a, b):
    return lax.dot_general(a, b, (((1,), (1,)), ((), ())), preferred_element_type=F32)


def _rms_rows(x, g):
    ms = jnp.mean(x * x, axis=-1, keepdims=True)
    return x * lax.rsqrt(ms + EPS) * g


def _head_norm_rows(x, ones_bd, g_tiled):
    x2 = x * x
    hi = x2.astype(BF16)
    lo = (x2 - hi.astype(F32)).astype(BF16)
    ms = jnp.concatenate(
        [_dot(hi[:, g * HEAD_GROUP:(g + 1) * HEAD_GROUP], ones_bd)
         + _dot(lo[:, g * HEAD_GROUP:(g + 1) * HEAD_GROUP], ones_bd) for g in range(N_GROUPS)], axis=1)
    return x * lax.rsqrt(ms + EPS) * g_tiled


SEG = 32
N_SEG = SUBLANES
SEG_PITCH = 40
SLABS = W_LRU // LANES


def _seg_rows(s):
    return slice(s * SEG_PITCH, s * SEG_PITCH + SEG)


def _segments_pass(a_s, u_s, slab, h, with_product=False, store=False):
    prod = jnp.ones((N_SEG, LANES), F32) if with_product else None
    for p in range(SEG):
        at = pl.ds(p, N_SEG, stride=SEG_PITCH)
        a = a_s[slab, at, :]
        h = a * h + u_s[slab, at, :]
        if with_product:
            prod = a * prod
        if store:
            u_s[slab, at, :] = h
    return h, prod


def _scan_chained(a_s, u_s, h_prev):
    ridx = lax.broadcasted_iota(jnp.int32, (N_SEG, LANES), 0)
    last = []
    for slab in range(SLABS):
        hp = h_prev[:, slab * LANES:(slab + 1) * LANES]
        u, a = _segments_pass(a_s, u_s, slab, jnp.zeros((N_SEG, LANES), F32), with_product=True)
        for s in (1, 2, 4):
            a_sh = jnp.where(ridx >= s, pltpu.roll(a, s, 0), 1.0)
            u_sh = jnp.where(ridx >= s, pltpu.roll(u, s, 0), 0.0)
            u = a * u_sh + u
            a = a * a_sh
        ends = a * hp + u
        starts = jnp.where(ridx == 0, hp, pltpu.roll(ends, 1, 0))
        _segments_pass(a_s, u_s, slab, starts, store=True)
        last.append(ends[N_SEG - 1:N_SEG, :])
    return jnp.concatenate(last, axis=1)


def _scan_independent(a_s, u_s, h0):
    return jnp.concatenate(
        [_segments_pass(a_s, u_s, slab, h0[:, slab * LANES:(slab + 1) * LANES], store=True)[0]
         for slab in range(SLABS)], axis=1)


def _lru_gate_dots(xcb_scr, g_scr, wg_ref):
    halves, half = wg_ref.shape[0], wg_ref.shape[1]
    for j in range(halves):
        g_scr[:, 2 * j * half:2 * (j + 1) * half] = _dot(xcb_scr[:, j * half:(j + 1) * half], wg_ref[j])


def _lru_inputs(xc_scr, g_scr, bg_ref, lam_ref, a_s, u_s):
    halves, half = bg_ref.shape[0], bg_ref.shape[2] // 2
    for j in range(halves):
        sl = slice(j * half, (j + 1) * half)
        decay = -RG_C * jax.nn.softplus(-lam_ref[:, sl])
        for s in range(N_SEG):
            rs = slice(s * SEG, (s + 1) * SEG)
            gates = g_scr[rs, 2 * j * half:2 * (j + 1) * half] + bg_ref[j]
            r = jax.nn.sigmoid(gates[:, :half])
            i = jax.nn.sigmoid(gates[:, half:])
            log_a = r * decay
            a = jnp.exp(log_a)
            u = jnp.sqrt(jnp.tanh(-log_a) * (1.0 + a * a)) * (i * xc_scr[rs, sl])
            for k in range(half // LANES):
                slab = j * (half // LANES) + k
                a_s[slab, _seg_rows(s), :] = a[:, k * LANES:(k + 1) * LANES]
                u_s[slab, _seg_rows(s), :] = u[:, k * LANES:(k + 1) * LANES]


def _gated_lru(u_s, gate, gate_row0, col0, out_ref, row0):
    for s in range(N_SEG):
        for slab in range(SLABS):
            g = gate[gate_row0 + s * SEG:gate_row0 + (s + 1) * SEG,
                     col0 + slab * LANES:col0 + (slab + 1) * LANES]
            out_ref[row0 + s * SEG:row0 + (s + 1) * SEG, slab * LANES:(slab + 1) * LANES] = (
                u_s[slab, _seg_rows(s), :] * jax.nn.silu(g)).astype(BF16)


def _conv_rows(cb_ref, base, nrows, cw_ref, cb_bias):
    def rows(shift):
        return cb_ref[base + SUBLANES - shift:base + SUBLANES - shift + nrows, 0:W_LRU]

    out = cb_bias + cw_ref[CONV_W - 1:CONV_W, :] * rows(0)
    for k in range(CONV_W - 1):
        out = out + cw_ref[k:k + 1, :] * rows(CONV_W - 1 - k)
    return out


def _head_mean_squares(src, row0, col0, hi_scr, lo_scr, ms_scr, ones_ref):
    for s in range(N_SEG):
        x = src[row0 + s * SEG:row0 + (s + 1) * SEG, col0:col0 + W_ATT]
        x2 = x * x
        hi = x2.astype(BF16)
        hi_scr[s * SEG:(s + 1) * SEG, :] = hi
        lo_scr[s * SEG:(s + 1) * SEG, :] = (x2 - hi.astype(F32)).astype(BF16)
    for g in range(N_GROUPS):
        gsl = slice(g * HEAD_GROUP, (g + 1) * HEAD_GROUP)
        ms_scr[:, gsl] = _dot(hi_scr[:, gsl], ones_ref[...]) + _dot(lo_scr[:, gsl], ones_ref[...])


def _head_norm_segments(src, row0, col0, ms_scr, g_tiled, emit):
    for s in range(N_SEG):
        x = src[row0 + s * SEG:row0 + (s + 1) * SEG, col0:col0 + W_ATT]
        emit(s, x * lax.rsqrt(ms_scr[s * SEG:(s + 1) * SEG, :] + EPS) * g_tiled)


def _front_prompt_kernel(x_ref, xnext_ref, ng_ref, wnat_ref, wt_ref, cw_ref, cbias_ref, wg_ref, bg_ref, lam_ref,
                         qg_ref, kg_ref, ones_ref,
                         lru_ref, qt_ref, kbf_ref, vt_ref, gat_ref, pk_ref, pv_ref, pc_ref, ph_ref,
                         zn0_scr, zn1_scr, zt0_scr, zt1_scr, xn_scr, xc_scr, xcb_scr, g_scr, hi_scr, lo_scr,
                         ms_scr, kn_scr, a_scr, u_scr, hlast_scr, *, keep_subs):
    zn_scr = (zn0_scr, zn1_scr)
    zt_scr = (zt0_scr, zt1_scr)
    sub = QBLOCK
    assert sub == N_SEG * SEG
    n_sub = x_ref.shape[1] // sub
    assert n_sub % 2 == 0

    @pl.when(pl.program_id(1) == 0)
    def _():
        zn_scr[0][0:SUBLANES, 0:W_LRU] = jnp.zeros((SUBLANES, W_LRU), F32)
        hlast_scr[...] = jnp.zeros((1, W_LRU), F32)

    def project_rows(i):
        src, base = (x_ref, i * sub) if i < n_sub else (xnext_ref, 0)
        for s in range(N_SEG):
            rs = slice(s * SEG, (s + 1) * SEG)
            xn_scr[rs, :] = _rms_rows(src[0, base + s * SEG:base + (s + 1) * SEG, :], ng_ref[...]).astype(BF16)
        zn_scr[i % 2][SUBLANES:SUBLANES + sub, :] = _dot(xn_scr[...], wnat_ref[...])

    def project_cols(i):
        zt_scr[i % 2][...] = _dot_nt(wt_ref[...], xn_scr[...])

    def small_dots(i):
        zn = zn_scr[i % 2]
        for s in range(N_SEG):
            xc = _conv_rows(zn, s * SEG, SEG, cw_ref, cbias_ref[...])
            xc_scr[s * SEG:(s + 1) * SEG, :] = xc
            xcb_scr[s * SEG:(s + 1) * SEG, :] = xc.astype(BF16)
        pc_ref[0] = zn[sub + SUBLANES - (CONV_W - 1):sub + SUBLANES, 0:W_LRU]
        zn_scr[1 - i % 2][0:SUBLANES, 0:W_LRU] = zn[sub:sub + SUBLANES, 0:W_LRU]
        _lru_gate_dots(xcb_scr, g_scr, wg_ref)
        _head_mean_squares(zn, SUBLANES, 2 * W_LRU, hi_scr, lo_scr, ms_scr, ones_ref)

    def finish_lru(i):
        _lru_inputs(xc_scr, g_scr, bg_ref, lam_ref, a_scr, u_scr)
        h_last = _scan_chained(a_scr, u_scr, hlast_scr[...])
        hlast_scr[...] = h_last
        ph_ref[0] = h_last
        _gated_lru(u_scr, zn_scr[i % 2], SUBLANES, W_LRU, lru_ref.at[0], i * sub)

    def finish_att(i):
        slot = i % 2
        r0 = i * sub
        cols = slice(r0, r0 + sub)
        zn = zn_scr[slot]
        zt = zt_scr[slot]
        keep_first = (i - (n_sub - keep_subs)) * sub

        def emit_k(s, kn):
            kbf_ref[0, r0 + s * SEG:r0 + (s + 1) * SEG, :] = kn.astype(BF16)
            if keep_first >= 0:
                kn_scr[s * SEG:(s + 1) * SEG, :] = kn

        _head_norm_segments(zn, SUBLANES, 2 * W_LRU, ms_scr, kg_ref[...], emit_k)
        q_scale = qg_ref[...] * (HEAD_DIM ** -0.5 * LOG2E)
        for h in range(N_HEADS):
            hs = slice(h * HEAD_DIM, (h + 1) * HEAD_DIM)
            q = zt[hs, :]
            ms = jnp.mean(q * q, axis=0, keepdims=True)
            qt_ref[0, hs, cols] = (q * lax.rsqrt(ms + EPS) * q_scale).astype(BF16)
            vt = zt[W_ATT + h * HEAD_DIM:W_ATT + (h + 1) * HEAD_DIM, :]
            vt_ref[0, hs, cols] = vt.astype(BF16)
            if keep_first >= 0:
                pv_ref[0, hs, keep_first:keep_first + sub] = vt
            gat_ref[0, hs, cols] = jax.nn.silu(zt[2 * W_ATT + h * HEAD_DIM:2 * W_ATT + (h + 1) * HEAD_DIM, :])
        if keep_first >= 0:
            pk_ref[0, :, keep_first:keep_first + sub] = kn_scr[...].T

    @pl.when((pl.program_id(0) == 0) & (pl.program_id(1) == 0))
    def _():
        project_rows(0)
        project_cols(0)

    for i in range(n_sub):
        small_dots(i)
        project_rows(i + 1)
        finish_lru(i)
        project_cols(i + 1)
        finish_att(i)


def _front_prompt(x, ng, wnat, wt, cw, cbias, wg, bg, lam, qg, kg, ones_bd):
    b, s, _ = x.shape
    tb = min(FRONT_BLOCK, s)
    keep = min(BAND, s)
    assert s % tb == 0 and tb % QBLOCK == 0 and keep % QBLOCK == 0 and keep <= tb
    const = lambda shape: pl.BlockSpec(shape, lambda i, j: (0,) * len(shape))
    rows_spec = lambda w: pl.BlockSpec((1, tb, w), lambda i, j: (i, j, 0))
    cols_spec = pl.BlockSpec((1, W_ATT, tb), lambda i, j: (i, 0, j))
    per_seq = lambda r, w: pl.BlockSpec((1, r, w), lambda i, j: (i, 0, 0))
    kern = functools.partial(_front_prompt_kernel, keep_subs=keep // QBLOCK)
    steps, subs_per_seq = s // tb, s // QBLOCK

    def next_first_subtile(i, j):
        n = jnp.minimum((i * steps + j + 1) * (tb // QBLOCK), b * subs_per_seq - 1)
        return (n // subs_per_seq, n % subs_per_seq, 0)

    return pl.pallas_call(
        kern,
        grid=(b, steps),
        in_specs=[
            rows_spec(D_MODEL),
            pl.BlockSpec((1, QBLOCK, D_MODEL), next_first_subtile),
            const((1, D_MODEL)),
            const((D_MODEL, 3 * W_LRU)),
            const((3 * W_ATT, D_MODEL)),
            const((CONV_W, W_LRU)),
            const((1, W_LRU)),
            const((W_LRU // HEAD_GROUP, HEAD_GROUP, 2 * HEAD_GROUP)),
            const((W_LRU // HEAD_GROUP, 1, 2 * HEAD_GROUP)),
            const((1, W_LRU)),
            const((HEAD_DIM, 1)),
            const((1, W_ATT)),
            const((HEAD_GROUP, HEAD_GROUP)),
        ],
        out_specs=[
            rows_spec(W_LRU),
            cols_spec,
            rows_spec(W_ATT),
            cols_spec,
            cols_spec,
            per_seq(W_ATT, keep),
            per_seq(W_ATT, keep),
            per_seq(CONV_W - 1, W_LRU),
            per_seq(1, W_LRU),
        ],
        out_shape=[
            jax.ShapeDtypeStruct((b, s, W_LRU), BF16),
            jax.ShapeDtypeStruct((b, W_ATT, s), BF16),
            jax.ShapeDtypeStruct((b, s, W_ATT), BF16),
            jax.ShapeDtypeStruct((b, W_ATT, s), BF16),
            jax.ShapeDtypeStruct((b, W_ATT, s), F32),
            jax.ShapeDtypeStruct((b, W_ATT, keep), F32),
            jax.ShapeDtypeStruct((b, W_ATT, keep), F32),
            jax.ShapeDtypeStruct((b, CONV_W - 1, W_LRU), F32),
            jax.ShapeDtypeStruct((b, 1, W_LRU), F32),
        ],
        scratch_shapes=[
            pltpu.VMEM((QBLOCK + SUBLANES, 3 * W_LRU), F32),
            pltpu.VMEM((QBLOCK + SUBLANES, 3 * W_LRU), F32),
            pltpu.VMEM((3 * W_ATT, QBLOCK), F32),
            pltpu.VMEM((3 * W_ATT, QBLOCK), F32),
            pltpu.VMEM((QBLOCK, D_MODEL), BF16),
            pltpu.VMEM((QBLOCK, W_LRU), F32),
            pltpu.VMEM((QBLOCK, W_LRU), BF16),
            pltpu.VMEM((QBLOCK, 2 * W_LRU), F32),
            pltpu.VMEM((QBLOCK, W_ATT), BF16),
            pltpu.VMEM((QBLOCK, W_ATT), BF16),
            pltpu.VMEM((QBLOCK, W_ATT), F32),
            pltpu.VMEM((QBLOCK, W_ATT), F32),
            pltpu.VMEM((SLABS, N_SEG * SEG_PITCH, LANES), F32),
            pltpu.VMEM((SLABS, N_SEG * SEG_PITCH, LANES), F32),
            pltpu.VMEM((1, W_LRU), F32),
        ],
        compiler_params=pltpu.CompilerParams(
            dimension_semantics=("arbitrary", "arbitrary"), vmem_limit_bytes=VMEM_LIMIT),
        name="front_prompt",
    )(x, x, ng, wnat, wt, cw, cbias, wg, bg, lam, qg, kg, ones_bd)


def _toeplitz_rows(w_row, nrows, row0):
    x = jnp.broadcast_to(w_row, (nrows, w_row.shape[1]))
    return pltpu.roll(x, row0, 1, stride=1, stride_axis=0)


def _fill_prompt_bias(w_ref, bias_ref):
    q_chunk = (lax.broadcasted_iota(jnp.int32, (CHUNK, QBLOCK), 1) + BAND) // CHUNK
    for h in range(N_HEADS):
        def body(n, carry, h=h):
            r0 = pl.multiple_of(n * CHUNK, CHUNK)
            t = _toeplitz_rows(w_ref[h], CHUNK, r0)[:, :QBLOCK]
            dc = q_chunk - n
            bias_ref[h, pl.ds(r0, CHUNK), :] = jnp.where((dc >= 0) & (dc <= PAST_CHUNKS), t * LOG2E, NEG)
            return carry
        lax.fori_loop(0, KEY_TILES * QBLOCK // CHUNK, body, 0)


CHUNKS_PER_TILE = QBLOCK // CHUNK
FRAMES_PER_VREG = LANES // CHUNK


def _lane_cols(kc):
    cols = []
    for c in range(QBLOCK // LANES):
        q_lo = PAST_CHUNKS + c * FRAMES_PER_VREG
        q_hi = q_lo + FRAMES_PER_VREG - 1
        if q_lo - PAST_CHUNKS <= kc <= q_hi:
            cols.append(c)
    return cols


def _fold_rows(x):
    return x.reshape(x.shape[0] // SUBLANES, SUBLANES, x.shape[1])


def _attn_prompt_heads(tiles, qt_ref, k_refs, v_refs, bias_ref, s_scr, p_scr, att_scr):
    n_cols = QBLOCK // LANES
    rows = lax.broadcasted_iota(jnp.int32, (HEAD_GROUP, QBLOCK), 0)
    pieces = []
    for i in tiles:
        for cc in range(CHUNKS_PER_TILE):
            kc = i * CHUNKS_PER_TILE + cc
            for c in _lane_cols(kc):
                pieces.append((i, cc, slice(kc * CHUNK, (kc + 1) * CHUNK), slice(c * LANES, (c + 1) * LANES), c))

    def scores(h):
        g, hl = divmod(h, HEADS_PER_GROUP)
        gsl = slice(g * HEAD_GROUP, (g + 1) * HEAD_GROUP)
        in_head = (rows >= hl * HEAD_DIM) & (rows < (hl + 1) * HEAD_DIM)
        qm = jnp.where(in_head, qt_ref[0, gsl, :], jnp.zeros((), BF16))
        s = {i: _dot(k_refs[i][0, :, gsl], qm) for i in tiles}
        m_acc = [jnp.full((SUBLANES, LANES), NEG, F32) for _ in range(n_cols)]
        for i, cc, rsl, lsl, c in pieces:
            sp = s[i][cc * CHUNK:(cc + 1) * CHUNK, lsl] + bias_ref[h, rsl, lsl]
            s_scr[h % 2, rsl, lsl] = sp
            m_acc[c] = jnp.maximum(m_acc[c], jnp.max(_fold_rows(sp), axis=0))
        return [jnp.max(a, axis=0, keepdims=True) for a in m_acc]

    def weights(h, m):
        l_acc = [jnp.zeros((SUBLANES, LANES), F32) for _ in range(n_cols)]
        for _, _, rsl, lsl, c in pieces:
            p = jnp.exp2(s_scr[h % 2, rsl, lsl] - m[c])
            l_acc[c] = l_acc[c] + jnp.sum(_fold_rows(p), axis=0)
            p_scr[h % 2, rsl, lsl] = p.astype(BF16)
        return [1.0 / jnp.sum(a, axis=0, keepdims=True) for a in l_acc]

    def values(h, l_inv):
        hsl = slice(h * HEAD_DIM, (h + 1) * HEAD_DIM)
        o = None
        for i in tiles:
            oi = _dot(v_refs[i][0, hsl, :], p_scr[h % 2, i * QBLOCK:(i + 1) * QBLOCK, :])
            o = oi if o is None else o + oi
        for c in range(n_cols):
            lsl = slice(c * LANES, (c + 1) * LANES)
            att_scr[hsl, lsl] = o[:, lsl] * l_inv[c]

    m_next = scores(0)
    for h in range(N_HEADS):
        m = m_next
        if h + 1 < N_HEADS:
            m_next = scores(h + 1)
        values(h, weights(h, m))


def _attn_prompt_kernel(qt_ref, k0_ref, k1_ref, k2_ref, v0_ref, v1_ref, v2_ref, w_ref, gat_ref,
                        out_ref, bias_ref, s_scr, p_scr, att_scr):
    j = pl.program_id(1)

    @pl.when((pl.program_id(0) == 0) & (j == 0))
    def _():
        _fill_prompt_bias(w_ref, bias_ref)
        p_scr[...] = jnp.zeros(p_scr.shape, BF16)

    k_refs = (k0_ref, k1_ref, k2_ref)
    v_refs = (v0_ref, v1_ref, v2_ref)
    for first in range(KEY_TILES):
        tiles = tuple(range(first, KEY_TILES))
        cond = (j == KEY_TILES - 1 - first) if first > 0 else (j >= KEY_TILES - 1)

        @pl.when(cond)
        def _(tiles=tiles):
            _attn_prompt_heads(tiles, qt_ref, k_refs, v_refs, bias_ref, s_scr, p_scr, att_scr)

    out_ref[0] = (att_scr[...] * gat_ref[0]).T.astype(BF16)


def _attn_prompt(qt, kbf, vt, bias_t, gat):
    b, _, s = qt.shape
    n_blocks = s // QBLOCK
    back = KEY_TILES - 1
    k_specs = [pl.BlockSpec((1, QBLOCK, W_ATT),
                            functools.partial(lambda i, j, d: (i, jnp.maximum(j - d, 0), 0), d=back - n))
               for n in range(KEY_TILES)]
    v_specs = [pl.BlockSpec((1, W_ATT, QBLOCK),
                            functools.partial(lambda i, j, d: (i, 0, jnp.maximum(j - d, 0)), d=back - n))
               for n in range(KEY_TILES)]
    return pl.pallas_call(
        _attn_prompt_kernel,
        grid=(b, n_blocks),
        in_specs=[pl.BlockSpec((1, W_ATT, QBLOCK), lambda i, j: (i, 0, j))] + k_specs + v_specs + [
            pl.BlockSpec((N_HEADS, 1, PROMPT_BIAS_PERIOD), lambda i, j: (0, 0, 0)),
            pl.BlockSpec((1, W_ATT, QBLOCK), lambda i, j: (i, 0, j)),
        ],
        out_specs=pl.BlockSpec((1, QBLOCK, W_ATT), lambda i, j: (i, j, 0)),
        out_shape=jax.ShapeDtypeStruct((b, s, W_ATT), BF16),
        scratch_shapes=[
            pltpu.VMEM((N_HEADS, KEY_TILES * QBLOCK, QBLOCK), F32),
            pltpu.VMEM((2, KEY_TILES * QBLOCK, QBLOCK), F32),
            pltpu.VMEM((2, KEY_TILES * QBLOCK, QBLOCK), BF16),
            pltpu.VMEM((W_ATT, QBLOCK), F32),
        ],
        compiler_params=pltpu.CompilerParams(
            dimension_semantics=("arbitrary", "arbitrary"), vmem_limit_bytes=VMEM_LIMIT),
        name="attn_prompt",
    )(qt, kbf, kbf, kbf, vt, vt, vt, bias_t, gat)


def _back_kernel(x_ref, lru_ref, att_ref, p_ref, wo_ref, pg_ref, wpg_ref, wpe_ref, y_ref):
    mix = _dot(lru_ref[...], wo_ref[0:W_LRU, :]) + _dot(att_ref[...], wo_ref[W_LRU:, :])
    h = x_ref[...] + mix
    gate = jax.nn.sigmoid(_dot(_rms_rows(h, pg_ref[...]).astype(BF16), wpg_ref[...]))
    y_ref[...] = h + _dot(p_ref[...].astype(BF16), wpe_ref[...]) * gate


def _back(x2, lru2, att2, p2, wo, pg, wpg, wpe, rows):
    n = x2.shape[0]
    const = lambda shape: pl.BlockSpec(shape, lambda i: (0,) * len(shape))
    return pl.pallas_call(
        _back_kernel,
        grid=(n // rows,),
        in_specs=[
            pl.BlockSpec((rows, D_MODEL), lambda i: (i, 0)),
            pl.BlockSpec((rows, W_LRU), lambda i: (i, 0)),
            pl.BlockSpec((rows, W_ATT), lambda i: (i, 0)),
            pl.BlockSpec((rows, PLE_DIM), lambda i: (i, 0)),
            const((W_LRU + W_ATT, D_MODEL)),
            const((1, D_MODEL)),
            const((D_MODEL, D_MODEL)),
            const((PLE_DIM, D_MODEL)),
        ],
        out_specs=pl.BlockSpec((rows, D_MODEL), lambda i: (i, 0)),
        out_shape=jax.ShapeDtypeStruct((n, D_MODEL), F32),
        compiler_params=pltpu.CompilerParams(
            dimension_semantics=("arbitrary",), vmem_limit_bytes=VMEM_LIMIT),
        name="back",
    )(x2, lru2, att2, p2, wo, pg, wpg, wpe)


def _front_sample_kernel(x_ref, ng_ref, win_ref, cw_ref, cbias_ref, wg_ref, bg_ref, lam_ref,
                         qg_ref, kg_ref, ones_ref, sconv_ref, slru_ref,
                         lru_ref, q_ref, k_ref, v_ref, ga_ref, sc_ref, sh_ref,
                         cb_scr, xc_scr, xcb_scr, g_scr, a_scr, u_scr, *, nb, tt):
    assert nb == N_SEG and tt == SEG
    seg = tt + SUBLANES
    xn = _rms_rows(x_ref[...], ng_ref[...]).astype(BF16)
    z = _dot(xn, win_ref[...])
    xl = z[:, :W_LRU]
    for s in range(nb):
        cb_scr[s * seg:s * seg + SUBLANES, :] = sconv_ref[s]
        cb_scr[s * seg + SUBLANES:(s + 1) * seg, :] = xl[s * tt:(s + 1) * tt, :]
        xc = _conv_rows(cb_scr, s * seg, tt, cw_ref, cbias_ref[...])
        xc_scr[s * tt:(s + 1) * tt, :] = xc
        xcb_scr[s * tt:(s + 1) * tt, :] = xc.astype(BF16)
        sc_ref[s] = cb_scr[(s + 1) * seg - (CONV_W - 1):(s + 1) * seg, :]
    _lru_gate_dots(xcb_scr, g_scr, wg_ref)
    _lru_inputs(xc_scr, g_scr, bg_ref, lam_ref, a_scr, u_scr)
    sh_ref[...] = _scan_independent(a_scr, u_scr, slru_ref[...])
    _gated_lru(u_scr, z, 0, W_LRU, lru_ref, 0)

    o = 2 * W_LRU
    q = _head_norm_rows(z[:, o:o + W_ATT], ones_ref[...], qg_ref[...])
    q_ref[...] = (q * (HEAD_DIM ** -0.5)).astype(BF16)
    k_ref[...] = _head_norm_rows(z[:, o + W_ATT:o + 2 * W_ATT], ones_ref[...], kg_ref[...])
    v_ref[...] = z[:, o + 2 * W_ATT:o + 3 * W_ATT]
    ga_ref[...] = jax.nn.silu(z[:, o + 3 * W_ATT:])


def _front_sample(x2, ng, win, cw, cbias, wg, bg, lam, qg_t, kg_t, ones_bd, sconv_pad, slru, nb, tt):
    n = x2.shape[0]
    rows = nb * tt
    const = lambda shape: pl.BlockSpec(shape, lambda i: (0,) * len(shape))
    row_spec = lambda w: pl.BlockSpec((rows, w), lambda i: (i, 0))
    nseq = n // tt
    kern = functools.partial(_front_sample_kernel, nb=nb, tt=tt)
    return pl.pallas_call(
        kern,
        grid=(n // rows,),
        in_specs=[
            row_spec(D_MODEL),
            const((1, D_MODEL)),
            const((D_MODEL, 2 * W_LRU + 4 * W_ATT)),
            const((CONV_W, W_LRU)),
            const((1, W_LRU)),
            const((W_LRU // HEAD_GROUP, HEAD_GROUP, 2 * HEAD_GROUP)),
            const((W_LRU // HEAD_GROUP, 1, 2 * HEAD_GROUP)),
            const((1, W_LRU)),
            const((1, W_ATT)),
            const((1, W_ATT)),
            const((HEAD_GROUP, HEAD_GROUP)),
            pl.BlockSpec((nb, SUBLANES, W_LRU), lambda i: (i, 0, 0)),
            pl.BlockSpec((nb, W_LRU), lambda i: (i, 0)),
        ],
        out_specs=[
            row_spec(W_LRU), row_spec(W_ATT), row_spec(W_ATT), row_spec(W_ATT), row_spec(W_ATT),
            pl.BlockSpec((nb, CONV_W - 1, W_LRU), lambda i: (i, 0, 0)),
            pl.BlockSpec((nb, W_LRU), lambda i: (i, 0)),
        ],
        out_shape=[
            jax.ShapeDtypeStruct((n, W_LRU), BF16),
            jax.ShapeDtypeStruct((n, W_ATT), BF16),
            jax.ShapeDtypeStruct((n, W_ATT), F32),
            jax.ShapeDtypeStruct((n, W_ATT), F32),
            jax.ShapeDtypeStruct((n, W_ATT), F32),
            jax.ShapeDtypeStruct((nseq, CONV_W - 1, W_LRU), F32),
            jax.ShapeDtypeStruct((nseq, W_LRU), F32),
        ],
        scratch_shapes=[
            pltpu.VMEM((nb * (tt + SUBLANES), W_LRU), F32),
            pltpu.VMEM((rows, W_LRU), F32),
            pltpu.VMEM((rows, W_LRU), BF16),
            pltpu.VMEM((rows, 2 * W_LRU), F32),
            pltpu.VMEM((SLABS, N_SEG * SEG_PITCH, LANES), F32),
            pltpu.VMEM((SLABS, N_SEG * SEG_PITCH, LANES), F32),
        ],
        compiler_params=pltpu.CompilerParams(
            dimension_semantics=("arbitrary",), vmem_limit_bytes=VMEM_LIMIT),
        name="front_sample",
    )(x2, ng, win, cw, cbias, wg, bg, lam, qg_t, kg_t, ones_bd, sconv_pad, slru)


def _attn_sample_kernel(q_ref, kn_ref, vn_ref, kc_ref, vc_ref, w_ref, ga_ref, out_ref, bias_ref):
    tt = q_ref.shape[1]
    lc = kc_ref.shape[-1]

    @pl.when(pl.program_id(0) == 0)
    def _():
        for h in range(N_HEADS):
            g, hl = divmod(h, HEADS_PER_GROUP)
            bias_ref[g, hl * tt:(hl + 1) * tt, :] = _toeplitz_rows(w_ref[h], tt, 0)

    lanes = lax.broadcasted_iota(jnp.int32, (tt, HEAD_GROUP), 1)
    masks = [(lanes >= hl * HEAD_DIM) & (lanes < (hl + 1) * HEAD_DIM) for hl in range(HEADS_PER_GROUP)]
    units = [(s, g) for s in range(q_ref.shape[0]) for g in range(N_GROUPS)]

    def cached(ref, s, g):
        heads = ref[s, g * HEADS_PER_GROUP:(g + 1) * HEADS_PER_GROUP]
        return heads.reshape(HEAD_GROUP, lc).astype(BF16)

    def scores(s, g):
        gsl = slice(g * HEAD_GROUP, (g + 1) * HEAD_GROUP)
        qg = q_ref[s, :, gsl]
        qs = jnp.concatenate([jnp.where(m, qg, jnp.zeros((), BF16)) for m in masks], axis=0)
        sc = _dot(qs, cached(kc_ref, s, g)) + bias_ref[g, :, 0:lc]
        sn = _dot_nt(qs, kn_ref[s, :, gsl].astype(BF16)) + bias_ref[g, :, lc:lc + tt]
        return sc, sn

    def finish(s, g, sc, sn):
        gsl = slice(g * HEAD_GROUP, (g + 1) * HEAD_GROUP)
        m = jnp.maximum(jnp.max(sc, axis=-1, keepdims=True), jnp.max(sn, axis=-1, keepdims=True))
        pc = jnp.exp(sc - m)
        pn = jnp.exp(sn - m)
        l = jnp.sum(pc, axis=-1, keepdims=True) + jnp.sum(pn, axis=-1, keepdims=True)
        o = _dot_nt(pc.astype(BF16), cached(vc_ref, s, g))
        o = (o + _dot(pn.astype(BF16), vn_ref[s, :, gsl].astype(BF16))) * (1.0 / l)
        att = jnp.zeros((tt, HEAD_GROUP), F32)
        for hl in range(HEADS_PER_GROUP):
            att = att + jnp.where(masks[hl], o[hl * tt:(hl + 1) * tt, :], 0.0)
        out_ref[s, :, gsl] = (att * ga_ref[s, :, gsl]).astype(BF16)

    nxt = scores(*units[0])
    for n, (s, g) in enumerate(units):
        cur = nxt
        if n + 1 < len(units):
            nxt = scores(*units[n + 1])
        finish(s, g, *cur)


def _attn_sample(q3, k3, v3, kc_t, vc_t, w_bias, ga3):
    b, tt, _ = q3.shape
    l = kc_t.shape[-1]
    period = w_bias.shape[-1]
    ns = SAMPLE_SEQS_PER_STEP
    assert b % ns == 0
    seq = lambda r: pl.BlockSpec((ns, r, W_ATT), lambda i: (i, 0, 0))
    cache = pl.BlockSpec((ns, N_HEADS, HEAD_DIM, l), lambda i: (i, 0, 0, 0))
    return pl.pallas_call(
        _attn_sample_kernel,
        grid=(b // ns,),
        in_specs=[
            seq(tt), seq(tt), seq(tt), cache, cache,
            pl.BlockSpec((N_HEADS, 1, period), lambda i: (0, 0, 0)),
            seq(tt),
        ],
        out_specs=seq(tt),
        out_shape=jax.ShapeDtypeStruct((b, tt, W_ATT), BF16),
        scratch_shapes=[pltpu.VMEM((N_GROUPS, HEADS_PER_GROUP * tt, period), F32)],
        compiler_params=pltpu.CompilerParams(
            dimension_semantics=("arbitrary",), vmem_limit_bytes=VMEM_LIMIT),
        name="attn_sample",
    )(q3, k3, v3, kc_t, vc_t, w_bias, ga3)


def _block_diag(w):
    n, d, e = w.shape
    eye = jnp.eye(n, dtype=w.dtype)
    return (eye[:, None, :, None] * w[:, :, None, :]).reshape(n * d, n * e)


def _prompt_bias_period(table):
    assert BAND - MAX_REL == MAX_REL and PROMPT_BIAS_PERIOD == KEY_TILES * QBLOCK + QBLOCK
    last = table[2 * MAX_REL:]
    neg_d = jnp.concatenate([table, jnp.broadcast_to(last, (MAX_REL - 1, N_HEADS))])
    w = jnp.concatenate([jnp.broadcast_to(last, (QBLOCK, N_HEADS)), neg_d])
    return w.T.reshape(N_HEADS, 1, PROMPT_BIAS_PERIOD).astype(F32)


def _sample_bias_period(table, tt, l):
    assert l >= MAX_REL
    period = -(-(l + 2 * tt - 1) // LANES) * LANES
    last = table[2 * MAX_REL:]
    n_var = tt + MAX_REL - 1
    var = table[2 * MAX_REL - 1:2 * MAX_REL - 1 - n_var:-1]
    w = jnp.concatenate([jnp.broadcast_to(last, (l - MAX_REL + 1, N_HEADS)), var,
                         jnp.broadcast_to(last, (period - (l + tt), N_HEADS))])
    return w.T.reshape(N_HEADS, 1, period).astype(F32)


def kernel(x_prompt, x_sample, p_prompt, p_sample, cache_k, cache_v, state_conv, state_lru, norm_g, w_in, conv_w, conv_b, gate_a_w, gate_a_b, gate_x_w, gate_x_b, lru_lambda, q_norm_g, k_norm_g, rel_bias, w_out, ple_norm_g, w_ple_gate, w_ple_proj):
    depth = w_in.shape[0]
    b, s, _ = x_prompt.shape
    db, ds, _ = x_sample.shape
    lc = cache_k.shape[2]
    yp, ys = x_prompt, x_sample.reshape(db * ds, D_MODEL)
    ones_bd = _block_diag(jnp.full((HEADS_PER_GROUP, HEAD_DIM, HEAD_DIM), 1.0 / HEAD_DIM, F32)).astype(BF16)
    gate_halves = W_LRU // HEAD_GROUP
    blocks_per_half = LRU_BLOCKS // gate_halves
    outs = [[] for _ in range(8)]
    sample_nb = 8
    for l in range(depth):
        win = w_in[l].astype(BF16)
        o = 2 * W_LRU
        wnat = jnp.concatenate([win[:, :o], win[:, o + W_ATT:o + 2 * W_ATT]], axis=1)
        wt = jnp.concatenate([win[:, o:o + W_ATT], win[:, o + 2 * W_ATT:]], axis=1).T
        ng = norm_g[l].reshape(1, D_MODEL)
        cw = conv_w[l]
        cbias = conv_b[l].reshape(1, W_LRU)
        wg = jnp.stack([
            jnp.concatenate([_block_diag(w[j * blocks_per_half:(j + 1) * blocks_per_half])
                             for w in (gate_a_w[l], gate_x_w[l])], axis=1)
            for j in range(gate_halves)]).astype(BF16)
        bg = jnp.concatenate([gate_a_b[l].reshape(gate_halves, 1, HEAD_GROUP),
                              gate_x_b[l].reshape(gate_halves, 1, HEAD_GROUP)], axis=2)
        lam = lru_lambda[l].reshape(1, W_LRU)
        qg_col = q_norm_g[l].reshape(HEAD_DIM, 1)
        qg_t = jnp.tile(q_norm_g[l], N_HEADS).reshape(1, W_ATT)
        kg_t = jnp.tile(k_norm_g[l], N_HEADS).reshape(1, W_ATT)
        wo = w_out[l].astype(BF16)
        pg = ple_norm_g[l].reshape(1, D_MODEL)
        wpg = w_ple_gate[l].astype(BF16)
        wpe = w_ple_proj[l].astype(BF16)

        lru_g, qt, kbf, vt, gat, pk, pv, pc, ph = _front_prompt(
            yp, ng, wnat, wt, cw, cbias, wg, bg, lam, qg_col, kg_t, ones_bd)
        att_g = _attn_prompt(qt, kbf, vt, _prompt_bias_period(rel_bias[l]), gat)
        yp = _back(yp.reshape(b * s, D_MODEL), lru_g.reshape(b * s, W_LRU), att_g.reshape(b * s, W_ATT),
                   p_prompt[l].reshape(b * s, PLE_DIM), wo, pg, wpg, wpe, 512).reshape(b, s, D_MODEL)
        to_frames = lambda a: jnp.transpose(a.reshape(b, N_HEADS, HEAD_DIM, a.shape[-1]), (0, 3, 1, 2))
        outs[0].append(to_frames(pk))
        outs[1].append(to_frames(pv))
        outs[2].append(pc)
        outs[3].append(ph.reshape(b, W_LRU))

        sconv_pad = jnp.pad(state_conv[l], ((0, 0), (SUBLANES - (CONV_W - 1), 0), (0, 0)))
        lru_s, q_s, k_s, v_s, ga_s, sc, sh = _front_sample(
            ys, ng, win, cw, cbias, wg, bg, lam, qg_t, kg_t, ones_bd,
            sconv_pad, state_lru[l], sample_nb, ds)
        att_s = _attn_sample(q_s.reshape(db, ds, W_ATT), k_s.reshape(db, ds, W_ATT), v_s.reshape(db, ds, W_ATT),
                             jnp.transpose(cache_k[l], (0, 2, 3, 1)), jnp.transpose(cache_v[l], (0, 2, 3, 1)),
                             _sample_bias_period(rel_bias[l], ds, lc), ga_s.reshape(db, ds, W_ATT))
        ys = _back(ys, lru_s, att_s.reshape(db * ds, W_ATT), p_sample[l].reshape(db * ds, PLE_DIM),
                   wo, pg, wpg, wpe, 256)
        outs[4].append(k_s.reshape(db, ds, N_HEADS, HEAD_DIM))
        outs[5].append(v_s.reshape(db, ds, N_HEADS, HEAD_DIM))
        outs[6].append(sc)
        outs[7].append(sh)
    return (yp, ys.reshape(db, ds, D_MODEL)) + tuple(jnp.stack(o) for o in outs)
```

```python
import functools

import jax
import jax.numpy as jnp
from jax import lax
from jax.experimental import pallas as pl
from jax.experimental.pallas import tpu as pltpu

D_MODEL = 1024
CHUNK = 64
PAST_CHUNKS = 8
BAND = PAST_CHUNKS * CHUNK
W_LRU = D_MODEL // 2
LRU_BLOCKS = 8
LRU_BLOCK = W_LRU // LRU_BLOCKS
CONV_W = 4
RG_C = 8.0
HEAD_DIM = 64
W_ATT = D_MODEL // 2
N_HEADS = W_ATT // HEAD_DIM
MAX_REL = 256
PLE_DIM = 256
EPS = 1e-6
NEG = -1e30
LOG2E = 1.4426950408889634

SUBLANES = 8
LANES = 128
HEAD_GROUP = 256
HEADS_PER_GROUP = HEAD_GROUP // HEAD_DIM
N_GROUPS = W_ATT // HEAD_GROUP
QBLOCK = 256
KEY_TILES = BAND // QBLOCK + 1
PROMPT_BIAS_PERIOD = (KEY_TILES + 1) * QBLOCK
FRONT_BLOCK = 1024
SAMPLE_SEQS_PER_STEP = 4
VMEM_LIMIT = 56 * 1024 * 1024

F32 = jnp.float32
BF16 = jnp.bfloat16


def _dot(a, b):
    return jnp.dot(a, b, preferred_element_type=F32)


def _dot_nt(a, b):
    return lax.dot_general(a, b, (((1,), (1,)), ((), ())), preferred_element_type=F32)


def _rms_rows(x, g):
    ms = jnp.mean(x * x, axis=-1, keepdims=True)
    return x * lax.rsqrt(ms + EPS) * g


def _head_norm_rows(x, ones_bd, g_tiled):
    x2 = x * x
    hi = x2.astype(BF16)
    lo = (x2 - hi.astype(F32)).astype(BF16)
    ms = jnp.concatenate(
        [_dot(hi[:, g * HEAD_GROUP:(g + 1) * HEAD_GROUP], ones_bd)
         + _dot(lo[:, g * HEAD_GROUP:(g + 1) * HEAD_GROUP], ones_bd) for g in range(N_GROUPS)], axis=1)
    return x * lax.rsqrt(ms + EPS) * g_tiled


def _scan_rows(a_ref, u_ref, h_ref, row0, nrows, h0, unroll=False):
    ridx = lax.broadcasted_iota(jnp.int32, (SUBLANES, W_LRU), 0)

    def body(i, hprev):
        r = pl.multiple_of(row0 + i * SUBLANES, SUBLANES)
        a = a_ref[pl.ds(r, SUBLANES), :]
        u = u_ref[pl.ds(r, SUBLANES), :]
        for s in (1, 2, 4):
            a_s = jnp.where(ridx >= s, pltpu.roll(a, s, 0), 1.0)
            u_s = jnp.where(ridx >= s, pltpu.roll(u, s, 0), 0.0)
            u = a * u_s + u
            a = a * a_s
        h = a * hprev + u
        h_ref[pl.ds(r, SUBLANES), :] = h
        return h[SUBLANES - 1:SUBLANES, :]

    return lax.fori_loop(0, nrows // SUBLANES, body, h0, unroll=unroll)


def _lru_inputs(xc, wg_ref, bg_ref, lam_ref, a_ref, u_ref):
    xcb = xc.astype(BF16)
    half = wg_ref.shape[1]
    for j in range(wg_ref.shape[0]):
        sl = slice(j * half, (j + 1) * half)
        gates = _dot(xcb[:, sl], wg_ref[j]) + bg_ref[j]
        r = jax.nn.sigmoid(gates[:, :half])
        i = jax.nn.sigmoid(gates[:, half:])
        log_a = -RG_C * r * jax.nn.softplus(-lam_ref[:, sl])
        a = jnp.exp(log_a)
        a_ref[:, sl] = a
        u_ref[:, sl] = jnp.sqrt(jnp.tanh(-log_a) * (1.0 + a * a)) * (i * xc[:, sl])


def _conv_rows(cb_ref, base, nrows, cw_ref, cb_bias):
    out = cb_bias + cw_ref[CONV_W - 1:CONV_W, :] * cb_ref[pl.ds(base + SUBLANES, nrows), :]
    for k in range(CONV_W - 1):
        shift = CONV_W - 1 - k
        out = out + cw_ref[k:k + 1, :] * cb_ref[pl.ds(base + SUBLANES - shift, nrows), :]
    return out


def _front_prompt_kernel(x_ref, ng_ref, wnat_ref, wt_ref, cw_ref, cbias_ref, wg_ref, bg_ref, lam_ref,
                         qg_ref, kg_ref, ones_ref,
                         lru_ref, qt_ref, kbf_ref, vt_ref, gat_ref, pk_ref, pv_ref, pc_ref, ph_ref,
                         zn_scr, zt_scr, cb_scr, a_scr, u_scr, h_scr, hlast_scr, *, keep_subs):
    sub = QBLOCK
    n_sub = x_ref.shape[1] // sub

    @pl.when(pl.program_id(1) == 0)
    def _():
        cb_scr[0:SUBLANES, :] = jnp.zeros((SUBLANES, W_LRU), F32)
        hlast_scr[...] = jnp.zeros((1, W_LRU), F32)

    def project(i):
        xn = _rms_rows(x_ref[0, i * sub:(i + 1) * sub, :], ng_ref[...]).astype(BF16)
        zn_scr[i % 2] = _dot(xn, wnat_ref[...])
        zt_scr[i % 2] = _dot_nt(wt_ref[...], xn)

    def finish(i):
        rows = slice(i * sub, (i + 1) * sub)
        zn = zn_scr.at[i % 2]
        zt = zt_scr.at[i % 2]
        cb_scr[SUBLANES:SUBLANES + sub, :] = zn[:, :W_LRU]
        xc = _conv_rows(cb_scr, 0, sub, cw_ref, cbias_ref[...])
        pc_ref[0] = cb_scr[sub + SUBLANES - (CONV_W - 1):sub + SUBLANES, :]
        cb_scr[0:SUBLANES, :] = cb_scr[sub:sub + SUBLANES, :]
        _lru_inputs(xc, wg_ref, bg_ref, lam_ref, a_scr, u_scr)
        h_last = _scan_rows(a_scr, u_scr, h_scr, 0, sub, hlast_scr[...], unroll=True)
        hlast_scr[...] = h_last
        ph_ref[0] = h_last
        lru_ref[0, rows, :] = (h_scr[...] * jax.nn.silu(zn[:, W_LRU:2 * W_LRU])).astype(BF16)
        k = _head_norm_rows(zn[:, 2 * W_LRU:], ones_ref[...], kg_ref[...])
        kbf_ref[0, rows, :] = k.astype(BF16)
        q3 = zt[0:W_ATT, :].reshape(N_HEADS, HEAD_DIM, sub)
        ms = jnp.mean(q3 * q3, axis=1, keepdims=True)
        qn = q3 * lax.rsqrt(ms + EPS) * (qg_ref[...] * (HEAD_DIM ** -0.5 * LOG2E))
        qt_ref[0, :, rows] = qn.reshape(W_ATT, sub).astype(BF16)
        vt = zt[W_ATT:2 * W_ATT, :]
        vt_ref[0, :, rows] = vt.astype(BF16)
        gat_ref[0, :, rows] = jax.nn.silu(zt[2 * W_ATT:, :])
        if i >= n_sub - keep_subs:
            first = (i - (n_sub - keep_subs)) * sub
            pk_ref[0, :, first:first + sub] = k.T
            pv_ref[0, :, first:first + sub] = vt

    project(0)
    for i in range(n_sub):
        if i + 1 < n_sub:
            project(i + 1)
        finish(i)


def _front_prompt(x, ng, wnat, wt, cw, cbias, wg, bg, lam, qg, kg, ones_bd):
    b, s, _ = x.shape
    tb = min(FRONT_BLOCK, s)
    keep = min(BAND, s)
    assert s % tb == 0 and tb % QBLOCK == 0 and keep % QBLOCK == 0 and keep <= tb
    const = lambda shape: pl.BlockSpec(shape, lambda i, j: (0,) * len(shape))
    rows_spec = lambda w: pl.BlockSpec((1, tb, w), lambda i, j: (i, j, 0))
    cols_spec = pl.BlockSpec((1, W_ATT, tb), lambda i, j: (i, 0, j))
    per_seq = lambda r, w: pl.BlockSpec((1, r, w), lambda i, j: (i, 0, 0))
    kern = functools.partial(_front_prompt_kernel, keep_subs=keep // QBLOCK)
    return pl.pallas_call(
        kern,
        grid=(b, s // tb),
        in_specs=[
            rows_spec(D_MODEL),
            const((1, D_MODEL)),
            const((D_MODEL, 3 * W_LRU)),
            const((3 * W_ATT, D_MODEL)),
            const((CONV_W, W_LRU)),
            const((1, W_LRU)),
            const((W_LRU // HEAD_GROUP, HEAD_GROUP, 2 * HEAD_GROUP)),
            const((W_LRU // HEAD_GROUP, 1, 2 * HEAD_GROUP)),
            const((1, W_LRU)),
            const((HEAD_DIM, 1)),
            const((1, W_ATT)),
            const((HEAD_GROUP, HEAD_GROUP)),
        ],
        out_specs=[
            rows_spec(W_LRU),
            cols_spec,
            rows_spec(W_ATT),
            cols_spec,
            cols_spec,
            per_seq(W_ATT, keep),
            per_seq(W_ATT, keep),
            per_seq(CONV_W - 1, W_LRU),
            per_seq(1, W_LRU),
        ],
        out_shape=[
            jax.ShapeDtypeStruct((b, s, W_LRU), BF16),
            jax.ShapeDtypeStruct((b, W_ATT, s), BF16),
            jax.ShapeDtypeStruct((b, s, W_ATT), BF16),
            jax.ShapeDtypeStruct((b, W_ATT, s), BF16),
            jax.ShapeDtypeStruct((b, W_ATT, s), F32),
            jax.ShapeDtypeStruct((b, W_ATT, keep), F32),
            jax.ShapeDtypeStruct((b, W_ATT, keep), F32),
            jax.ShapeDtypeStruct((b, CONV_W - 1, W_LRU), F32),
            jax.ShapeDtypeStruct((b, 1, W_LRU), F32),
        ],
        scratch_shapes=[
            pltpu.VMEM((2, QBLOCK, 3 * W_LRU), F32),
            pltpu.VMEM((2, 3 * W_ATT, QBLOCK), F32),
            pltpu.VMEM((QBLOCK + SUBLANES, W_LRU), F32),
            pltpu.VMEM((QBLOCK, W_LRU), F32),
            pltpu.VMEM((QBLOCK, W_LRU), F32),
            pltpu.VMEM((QBLOCK, W_LRU), F32),
            pltpu.VMEM((1, W_LRU), F32),
        ],
        compiler_params=pltpu.CompilerParams(
            dimension_semantics=("arbitrary", "arbitrary"), vmem_limit_bytes=VMEM_LIMIT),
        name="front_prompt",
    )(x, ng, wnat, wt, cw, cbias, wg, bg, lam, qg, kg, ones_bd)


def _toeplitz_rows(w_row, nrows, row0):
    x = jnp.broadcast_to(w_row, (nrows, w_row.shape[1]))
    return pltpu.roll(x, row0, 1, stride=1, stride_axis=0)


def _fill_prompt_bias(w_ref, bias_ref):
    q_chunk = (lax.broadcasted_iota(jnp.int32, (CHUNK, QBLOCK), 1) + BAND) // CHUNK
    for h in range(N_HEADS):
        def body(n, carry, h=h):
            r0 = pl.multiple_of(n * CHUNK, CHUNK)
            t = _toeplitz_rows(w_ref[h], CHUNK, r0)[:, :QBLOCK]
            dc = q_chunk - n
            bias_ref[h, pl.ds(r0, CHUNK), :] = jnp.where((dc >= 0) & (dc <= PAST_CHUNKS), t * LOG2E, NEG)
            return carry
        lax.fori_loop(0, KEY_TILES * QBLOCK // CHUNK, body, 0)


CHUNKS_PER_TILE = QBLOCK // CHUNK
FRAMES_PER_VREG = LANES // CHUNK


def _lane_cols(kc):
    cols = []
    for c in range(QBLOCK // LANES):
        q_lo = PAST_CHUNKS + c * FRAMES_PER_VREG
        q_hi = q_lo + FRAMES_PER_VREG - 1
        if q_lo - PAST_CHUNKS <= kc <= q_hi:
            cols.append(c)
    return cols


def _fold_rows(x):
    return x.reshape(x.shape[0] // SUBLANES, SUBLANES, x.shape[1])


def _attn_prompt_heads(tiles, qt_ref, k_refs, v_refs, bias_ref, s_scr, p_scr, att_scr):
    n_cols = QBLOCK // LANES
    rows = lax.broadcasted_iota(jnp.int32, (HEAD_GROUP, QBLOCK), 0)
    pieces = []
    for i in tiles:
        for cc in range(CHUNKS_PER_TILE):
            kc = i * CHUNKS_PER_TILE + cc
            for c in _lane_cols(kc):
                pieces.append((i, cc, slice(kc * CHUNK, (kc + 1) * CHUNK), slice(c * LANES, (c + 1) * LANES), c))

    def scores(h):
        g, hl = divmod(h, HEADS_PER_GROUP)
        gsl = slice(g * HEAD_GROUP, (g + 1) * HEAD_GROUP)
        in_head = (rows >= hl * HEAD_DIM) & (rows < (hl + 1) * HEAD_DIM)
        qm = jnp.where(in_head, qt_ref[0, gsl, :], jnp.zeros((), BF16))
        s = {i: _dot(k_refs[i][0, :, gsl], qm) for i in tiles}
        m_acc = [jnp.full((SUBLANES, LANES), NEG, F32) for _ in range(n_cols)]
        for i, cc, rsl, lsl, c in pieces:
            sp = s[i][cc * CHUNK:(cc + 1) * CHUNK, lsl] + bias_ref[h, rsl, lsl]
            s_scr[h % 3, rsl, lsl] = sp
            m_acc[c] = jnp.maximum(m_acc[c], jnp.max(_fold_rows(sp), axis=0))
        return [jnp.max(a, axis=0, keepdims=True) for a in m_acc]

    def weights(h, m):
        l_acc = [jnp.zeros((SUBLANES, LANES), F32) for _ in range(n_cols)]
        for _, _, rsl, lsl, c in pieces:
            p = jnp.exp2(s_scr[h % 3, rsl, lsl] - m[c])
            l_acc[c] = l_acc[c] + jnp.sum(_fold_rows(p), axis=0)
            p_scr[h % 2, rsl, lsl] = p.astype(BF16)
        return [1.0 / jnp.sum(a, axis=0, keepdims=True) for a in l_acc]

    def values(h, l_inv):
        hsl = slice(h * HEAD_DIM, (h + 1) * HEAD_DIM)
        o = None
        for i in tiles:
            oi = _dot(v_refs[i][0, hsl, :], p_scr[h % 2, i * QBLOCK:(i + 1) * QBLOCK, :])
            o = oi if o is None else o + oi
        for c in range(n_cols):
            lsl = slice(c * LANES, (c + 1) * LANES)
            att_scr[hsl, lsl] = o[:, lsl] * l_inv[c]

    m = {h: scores(h) for h in range(min(2, N_HEADS))}
    l_inv = {}
    for h in range(N_HEADS):
        if h + 2 < N_HEADS:
            m[h + 2] = scores(h + 2)
        l_inv[h] = weights(h, m.pop(h))
        if h >= 1:
            values(h - 1, l_inv.pop(h - 1))
    values(N_HEADS - 1, l_inv.pop(N_HEADS - 1))


def _attn_prompt_kernel(qt_ref, k0_ref, k1_ref, k2_ref, v0_ref, v1_ref, v2_ref, w_ref, gat_ref,
                        out_ref, bias_ref, s_scr, p_scr, att_scr):
    j = pl.program_id(1)

    @pl.when((pl.program_id(0) == 0) & (j == 0))
    def _():
        _fill_prompt_bias(w_ref, bias_ref)
        p_scr[...] = jnp.zeros(p_scr.shape, BF16)

    k_refs = (k0_ref, k1_ref, k2_ref)
    v_refs = (v0_ref, v1_ref, v2_ref)
    for first in range(KEY_TILES):
        tiles = tuple(range(first, KEY_TILES))
        cond = (j == KEY_TILES - 1 - first) if first > 0 else (j >= KEY_TILES - 1)

        @pl.when(cond)
        def _(tiles=tiles):
            _attn_prompt_heads(tiles, qt_ref, k_refs, v_refs, bias_ref, s_scr, p_scr, att_scr)

    out_ref[0] = (att_scr[...] * gat_ref[0]).T.astype(BF16)


def _attn_prompt(qt, kbf, vt, bias_t, gat):
    b, _, s = qt.shape
    n_blocks = s // QBLOCK
    back = KEY_TILES - 1
    k_specs = [pl.BlockSpec((1, QBLOCK, W_ATT),
                            functools.partial(lambda i, j, d: (i, jnp.maximum(j - d, 0), 0), d=back - n))
               for n in range(KEY_TILES)]
    v_specs = [pl.BlockSpec((1, W_ATT, QBLOCK),
                            functools.partial(lambda i, j, d: (i, 0, jnp.maximum(j - d, 0)), d=back - n))
               for n in range(KEY_TILES)]
    return pl.pallas_call(
        _attn_prompt_kernel,
        grid=(b, n_blocks),
        in_specs=[pl.BlockSpec((1, W_ATT, QBLOCK), lambda i, j: (i, 0, j))] + k_specs + v_specs + [
            pl.BlockSpec((N_HEADS, 1, PROMPT_BIAS_PERIOD), lambda i, j: (0, 0, 0)),
            pl.BlockSpec((1, W_ATT, QBLOCK), lambda i, j: (i, 0, j)),
        ],
        out_specs=pl.BlockSpec((1, QBLOCK, W_ATT), lambda i, j: (i, j, 0)),
        out_shape=jax.ShapeDtypeStruct((b, s, W_ATT), BF16),
        scratch_shapes=[
            pltpu.VMEM((N_HEADS, KEY_TILES * QBLOCK, QBLOCK), F32),
            pltpu.VMEM((3, KEY_TILES * QBLOCK, QBLOCK), F32),
            pltpu.VMEM((2, KEY_TILES * QBLOCK, QBLOCK), BF16),
            pltpu.VMEM((W_ATT, QBLOCK), F32),
        ],
        compiler_params=pltpu.CompilerParams(
            dimension_semantics=("arbitrary", "arbitrary"), vmem_limit_bytes=VMEM_LIMIT),
        name="attn_prompt",
    )(qt, kbf, kbf, kbf, vt, vt, vt, bias_t, gat)


def _back_kernel(x_ref, lru_ref, att_ref, p_ref, wo_ref, pg_ref, wpg_ref, wpe_ref, y_ref):
    mix = _dot(lru_ref[...], wo_ref[0:W_LRU, :]) + _dot(att_ref[...], wo_ref[W_LRU:, :])
    h = x_ref[...] + mix
    gate = jax.nn.sigmoid(_dot(_rms_rows(h, pg_ref[...]).astype(BF16), wpg_ref[...]))
    y_ref[...] = h + _dot(p_ref[...].astype(BF16), wpe_ref[...]) * gate


def _back(x2, lru2, att2, p2, wo, pg, wpg, wpe, rows):
    n = x2.shape[0]
    const = lambda shape: pl.BlockSpec(shape, lambda i: (0,) * len(shape))
    return pl.pallas_call(
        _back_kernel,
        grid=(n // rows,),
        in_specs=[
            pl.BlockSpec((rows, D_MODEL), lambda i: (i, 0)),
            pl.BlockSpec((rows, W_LRU), lambda i: (i, 0)),
            pl.BlockSpec((rows, W_ATT), lambda i: (i, 0)),
            pl.BlockSpec((rows, PLE_DIM), lambda i: (i, 0)),
            const((W_LRU + W_ATT, D_MODEL)),
            const((1, D_MODEL)),
            const((D_MODEL, D_MODEL)),
            const((PLE_DIM, D_MODEL)),
        ],
        out_specs=pl.BlockSpec((rows, D_MODEL), lambda i: (i, 0)),
        out_shape=jax.ShapeDtypeStruct((n, D_MODEL), F32),
        compiler_params=pltpu.CompilerParams(
            dimension_semantics=("arbitrary",), vmem_limit_bytes=VMEM_LIMIT),
        name="back",
    )(x2, lru2, att2, p2, wo, pg, wpg, wpe)


def _front_sample_kernel(x_ref, ng_ref, win_ref, cw_ref, cbias_ref, wg_ref, bg_ref, lam_ref,
                         qg_ref, kg_ref, ones_ref, sconv_ref, slru_ref,
                         lru_ref, q_ref, k_ref, v_ref, ga_ref, sc_ref, sh_ref,
                         cb_scr, xc_scr, a_scr, u_scr, h_scr, *, nb, tt):
    seg = tt + SUBLANES
    xn = _rms_rows(x_ref[...], ng_ref[...]).astype(BF16)
    z = _dot(xn, win_ref[...])
    xl = z[:, :W_LRU]
    for s in range(nb):
        cb_scr[s * seg:s * seg + SUBLANES, :] = sconv_ref[s]
        cb_scr[s * seg + SUBLANES:(s + 1) * seg, :] = xl[s * tt:(s + 1) * tt, :]
        xc_scr[s * tt:(s + 1) * tt, :] = _conv_rows(cb_scr, s * seg, tt, cw_ref, cbias_ref[...])
        sc_ref[s] = cb_scr[(s + 1) * seg - (CONV_W - 1):(s + 1) * seg, :]
    _lru_inputs(xc_scr[...], wg_ref, bg_ref, lam_ref, a_scr, u_scr)
    for s in range(nb):
        sh_ref[s] = _scan_rows(a_scr, u_scr, h_scr, s * tt, tt, slru_ref[s])
    lru_ref[...] = (h_scr[...] * jax.nn.silu(z[:, W_LRU:2 * W_LRU])).astype(BF16)

    o = 2 * W_LRU
    q = _head_norm_rows(z[:, o:o + W_ATT], ones_ref[...], qg_ref[...])
    q_ref[...] = (q * (HEAD_DIM ** -0.5)).astype(BF16)
    k_ref[...] = _head_norm_rows(z[:, o + W_ATT:o + 2 * W_ATT], ones_ref[...], kg_ref[...])
    v_ref[...] = z[:, o + 2 * W_ATT:o + 3 * W_ATT]
    ga_ref[...] = jax.nn.silu(z[:, o + 3 * W_ATT:])


def _front_sample(x2, ng, win, cw, cbias, wg, bg, lam, qg_t, kg_t, ones_bd, sconv_pad, slru, nb, tt):
    n = x2.shape[0]
    rows = nb * tt
    const = lambda shape: pl.BlockSpec(shape, lambda i: (0,) * len(shape))
    row_spec = lambda w: pl.BlockSpec((rows, w), lambda i: (i, 0))
    nseq = n // tt
    kern = functools.partial(_front_sample_kernel, nb=nb, tt=tt)
    return pl.pallas_call(
        kern,
        grid=(n // rows,),
        in_specs=[
            row_spec(D_MODEL),
            const((1, D_MODEL)),
            const((D_MODEL, 2 * W_LRU + 4 * W_ATT)),
            const((CONV_W, W_LRU)),
            const((1, W_LRU)),
            const((W_LRU // HEAD_GROUP, HEAD_GROUP, 2 * HEAD_GROUP)),
            const((W_LRU // HEAD_GROUP, 1, 2 * HEAD_GROUP)),
            const((1, W_LRU)),
            const((1, W_ATT)),
            const((1, W_ATT)),
            const((HEAD_GROUP, HEAD_GROUP)),
            pl.BlockSpec((nb, SUBLANES, W_LRU), lambda i: (i, 0, 0)),
            pl.BlockSpec((nb, 1, W_LRU), lambda i: (i, 0, 0)),
        ],
        out_specs=[
            row_spec(W_LRU), row_spec(W_ATT), row_spec(W_ATT), row_spec(W_ATT), row_spec(W_ATT),
            pl.BlockSpec((nb, CONV_W - 1, W_LRU), lambda i: (i, 0, 0)),
            pl.BlockSpec((nb, 1, W_LRU), lambda i: (i, 0, 0)),
        ],
        out_shape=[
            jax.ShapeDtypeStruct((n, W_LRU), BF16),
            jax.ShapeDtypeStruct((n, W_ATT), BF16),
            jax.ShapeDtypeStruct((n, W_ATT), F32),
            jax.ShapeDtypeStruct((n, W_ATT), F32),
            jax.ShapeDtypeStruct((n, W_ATT), F32),
            jax.ShapeDtypeStruct((nseq, CONV_W - 1, W_LRU), F32),
            jax.ShapeDtypeStruct((nseq, 1, W_LRU), F32),
        ],
        scratch_shapes=[
            pltpu.VMEM((nb * (tt + SUBLANES), W_LRU), F32),
            pltpu.VMEM((rows, W_LRU), F32),
            pltpu.VMEM((rows, W_LRU), F32),
            pltpu.VMEM((rows, W_LRU), F32),
            pltpu.VMEM((rows, W_LRU), F32),
        ],
        compiler_params=pltpu.CompilerParams(
            dimension_semantics=("arbitrary",), vmem_limit_bytes=VMEM_LIMIT),
        name="front_sample",
    )(x2, ng, win, cw, cbias, wg, bg, lam, qg_t, kg_t, ones_bd, sconv_pad, slru)


def _attn_sample_kernel(q_ref, kn_ref, vn_ref, kc_ref, vc_ref, w_ref, ga_ref, out_ref, bias_ref):
    tt = q_ref.shape[1]
    lc = kc_ref.shape[-1]

    @pl.when(pl.program_id(0) == 0)
    def _():
        for h in range(N_HEADS):
            g, hl = divmod(h, HEADS_PER_GROUP)
            bias_ref[g, hl * tt:(hl + 1) * tt, :] = _toeplitz_rows(w_ref[h], tt, 0)

    lanes = lax.broadcasted_iota(jnp.int32, (tt, HEAD_GROUP), 1)
    masks = [(lanes >= hl * HEAD_DIM) & (lanes < (hl + 1) * HEAD_DIM) for hl in range(HEADS_PER_GROUP)]
    units = [(s, g) for s in range(q_ref.shape[0]) for g in range(N_GROUPS)]

    def cached(ref, s, g):
        heads = ref[s, g * HEADS_PER_GROUP:(g + 1) * HEADS_PER_GROUP]
        return heads.reshape(HEAD_GROUP, lc).astype(BF16)

    def scores(s, g):
        gsl = slice(g * HEAD_GROUP, (g + 1) * HEAD_GROUP)
        qg = q_ref[s, :, gsl]
        qs = jnp.concatenate([jnp.where(m, qg, jnp.zeros((), BF16)) for m in masks], axis=0)
        sc = _dot(qs, cached(kc_ref, s, g)) + bias_ref[g, :, 0:lc]
        sn = _dot_nt(qs, kn_ref[s, :, gsl].astype(BF16)) + bias_ref[g, :, lc:lc + tt]
        return sc, sn

    def finish(s, g, sc, sn):
        gsl = slice(g * HEAD_GROUP, (g + 1) * HEAD_GROUP)
        m = jnp.maximum(jnp.max(sc, axis=-1, keepdims=True), jnp.max(sn, axis=-1, keepdims=True))
        pc = jnp.exp(sc - m)
        pn = jnp.exp(sn - m)
        l = jnp.sum(pc, axis=-1, keepdims=True) + jnp.sum(pn, axis=-1, keepdims=True)
        o = _dot_nt(pc.astype(BF16), cached(vc_ref, s, g))
        o = (o + _dot(pn.astype(BF16), vn_ref[s, :, gsl].astype(BF16))) * (1.0 / l)
        att = jnp.zeros((tt, HEAD_GROUP), F32)
        for hl in range(HEADS_PER_GROUP):
            att = att + jnp.where(masks[hl], o[hl * tt:(hl + 1) * tt, :], 0.0)
        out_ref[s, :, gsl] = (att * ga_ref[s, :, gsl]).astype(BF16)

    nxt = scores(*units[0])
    for n, (s, g) in enumerate(units):
        cur = nxt
        if n + 1 < len(units):
            nxt = scores(*units[n + 1])
        finish(s, g, *cur)


def _attn_sample(q3, k3, v3, kc_t, vc_t, w_bias, ga3):
    b, tt, _ = q3.shape
    l = kc_t.shape[-1]
    period = w_bias.shape[-1]
    ns = SAMPLE_SEQS_PER_STEP
    assert b % ns == 0
    seq = lambda r: pl.BlockSpec((ns, r, W_ATT), lambda i: (i, 0, 0))
    cache = pl.BlockSpec((ns, N_HEADS, HEAD_DIM, l), lambda i: (i, 0, 0, 0))
    return pl.pallas_call(
        _attn_sample_kernel,
        grid=(b // ns,),
        in_specs=[
            seq(tt), seq(tt), seq(tt), cache, cache,
            pl.BlockSpec((N_HEADS, 1, period), lambda i: (0, 0, 0)),
            seq(tt),
        ],
        out_specs=seq(tt),
        out_shape=jax.ShapeDtypeStruct((b, tt, W_ATT), BF16),
        scratch_shapes=[pltpu.VMEM((N_GROUPS, HEADS_PER_GROUP * tt, period), F32)],
        compiler_params=pltpu.CompilerParams(
            dimension_semantics=("arbitrary",), vmem_limit_bytes=VMEM_LIMIT),
        name="attn_sample",
    )(q3, k3, v3, kc_t, vc_t, w_bias, ga3)


def _block_diag(w):
    n, d, e = w.shape
    eye = jnp.eye(n, dtype=w.dtype)
    return (eye[:, None, :, None] * w[:, :, None, :]).reshape(n * d, n * e)


def _prompt_bias_period(table):
    assert BAND - MAX_REL == MAX_REL and PROMPT_BIAS_PERIOD == KEY_TILES * QBLOCK + QBLOCK
    last = table[2 * MAX_REL:]
    neg_d = jnp.concatenate([table, jnp.broadcast_to(last, (MAX_REL - 1, N_HEADS))])
    w = jnp.concatenate([jnp.broadcast_to(last, (QBLOCK, N_HEADS)), neg_d])
    return w.T.reshape(N_HEADS, 1, PROMPT_BIAS_PERIOD).astype(F32)


def _sample_bias_period(table, tt, l):
    assert l >= MAX_REL
    period = -(-(l + 2 * tt - 1) // LANES) * LANES
    last = table[2 * MAX_REL:]
    n_var = tt + MAX_REL - 1
    var = table[2 * MAX_REL - 1:2 * MAX_REL - 1 - n_var:-1]
    w = jnp.concatenate([jnp.broadcast_to(last, (l - MAX_REL + 1, N_HEADS)), var,
                         jnp.broadcast_to(last, (period - (l + tt), N_HEADS))])
    return w.T.reshape(N_HEADS, 1, period).astype(F32)


def kernel(x_prompt, x_sample, p_prompt, p_sample, cache_k, cache_v, state_conv, state_lru, norm_g, w_in, conv_w, conv_b, gate_a_w, gate_a_b, gate_x_w, gate_x_b, lru_lambda, q_norm_g, k_norm_g, rel_bias, w_out, ple_norm_g, w_ple_gate, w_ple_proj):
    depth = w_in.shape[0]
    b, s, _ = x_prompt.shape
    db, ds, _ = x_sample.shape
    lc = cache_k.shape[2]
    yp, ys = x_prompt, x_sample.reshape(db * ds, D_MODEL)
    ones_bd = _block_diag(jnp.full((HEADS_PER_GROUP, HEAD_DIM, HEAD_DIM), 1.0 / HEAD_DIM, F32)).astype(BF16)
    gate_halves = W_LRU // HEAD_GROUP
    blocks_per_half = LRU_BLOCKS // gate_halves
    outs = [[] for _ in range(8)]
    sample_nb = 8
    for l in range(depth):
        win = w_in[l].astype(BF16)
        o = 2 * W_LRU
        wnat = jnp.concatenate([win[:, :o], win[:, o + W_ATT:o + 2 * W_ATT]], axis=1)
        wt = jnp.concatenate([win[:, o:o + W_ATT], win[:, o + 2 * W_ATT:]], axis=1).T
        ng = norm_g[l].reshape(1, D_MODEL)
        cw = conv_w[l]
        cbias = conv_b[l].reshape(1, W_LRU)
        wg = jnp.stack([
            jnp.concatenate([_block_diag(w[j * blocks_per_half:(j + 1) * blocks_per_half])
                             for w in (gate_a_w[l], gate_x_w[l])], axis=1)
            for j in range(gate_halves)]).astype(BF16)
        bg = jnp.concatenate([gate_a_b[l].reshape(gate_halves, 1, HEAD_GROUP),
                              gate_x_b[l].reshape(gate_halves, 1, HEAD_GROUP)], axis=2)
        lam = lru_lambda[l].reshape(1, W_LRU)
        qg_col = q_norm_g[l].reshape(HEAD_DIM, 1)
        qg_t = jnp.tile(q_norm_g[l], N_HEADS).reshape(1, W_ATT)
        kg_t = jnp.tile(k_norm_g[l], N_HEADS).reshape(1, W_ATT)
        wo = w_out[l].astype(BF16)
        pg = ple_norm_g[l].reshape(1, D_MODEL)
        wpg = w_ple_gate[l].astype(BF16)
        wpe = w_ple_proj[l].astype(BF16)

        lru_g, qt, kbf, vt, gat, pk, pv, pc, ph = _front_prompt(
            yp, ng, wnat, wt, cw, cbias, wg, bg, lam, qg_col, kg_t, ones_bd)
        att_g = _attn_prompt(qt, kbf, vt, _prompt_bias_period(rel_bias[l]), gat)
        yp = _back(yp.reshape(b * s, D_MODEL), lru_g.reshape(b * s, W_LRU), att_g.reshape(b * s, W_ATT),
                   p_prompt[l].reshape(b * s, PLE_DIM), wo, pg, wpg, wpe, 512).reshape(b, s, D_MODEL)
        to_frames = lambda a: jnp.transpose(a.reshape(b, N_HEADS, HEAD_DIM, a.shape[-1]), (0, 3, 1, 2))
        outs[0].append(to_frames(pk))
        outs[1].append(to_frames(pv))
        outs[2].append(pc)
        outs[3].append(ph.reshape(b, W_LRU))

        sconv_pad = jnp.pad(state_conv[l], ((0, 0), (SUBLANES - (CONV_W - 1), 0), (0, 0)))
        lru_s, q_s, k_s, v_s, ga_s, sc, sh = _front_sample(
            ys, ng, win, cw, cbias, wg, bg, lam, qg_t, kg_t, ones_bd,
            sconv_pad, state_lru[l].reshape(db, 1, W_LRU), sample_nb, ds)
        att_s = _attn_sample(q_s.reshape(db, ds, W_ATT), k_s.reshape(db, ds, W_ATT), v_s.reshape(db, ds, W_ATT),
                             jnp.transpose(cache_k[l], (0, 2, 3, 1)), jnp.transpose(cache_v[l], (0, 2, 3, 1)),
                             _sample_bias_period(rel_bias[l], ds, lc), ga_s.reshape(db, ds, W_ATT))
        ys = _back(ys, lru_s, att_s.reshape(db * ds, W_ATT), p_sample[l].reshape(db * ds, PLE_DIM),
                   wo, pg, wpg, wpe, 256)
        outs[4].append(k_s.reshape(db, ds, N_HEADS, HEAD_DIM))
        outs[5].append(v_s.reshape(db, ds, N_HEADS, HEAD_DIM))
        outs[6].append(sc)
        outs[7].append(sh.reshape(db, W_LRU))
    return (yp, ys.reshape(db, ds, D_MODEL)) + tuple(jnp.stack(o) for o in outs)
```

```python
import functools

import jax
import jax.numpy as jnp
from jax import lax
from jax.experimental import pallas as pl
from jax.experimental.pallas import tpu as pltpu

D_MODEL = 1024
CHUNK = 64
PAST_CHUNKS = 8
BAND = PAST_CHUNKS * CHUNK
W_LRU = D_MODEL // 2
LRU_BLOCKS = 8
LRU_BLOCK = W_LRU // LRU_BLOCKS
CONV_W = 4
RG_C = 8.0
HEAD_DIM = 64
W_ATT = D_MODEL // 2
N_HEADS = W_ATT // HEAD_DIM
MAX_REL = 256
PLE_DIM = 256
EPS = 1e-6
NEG = -1e30
LOG2E = 1.4426950408889634

SUBLANES = 8
LANES = 128
HEAD_GROUP = 256
HEADS_PER_GROUP = HEAD_GROUP // HEAD_DIM
N_GROUPS = W_ATT // HEAD_GROUP
QBLOCK = 256
KEY_TILES = BAND // QBLOCK + 1
PROMPT_BIAS_PERIOD = (KEY_TILES + 1) * QBLOCK
FRONT_BLOCK = 1024
SAMPLE_SEQS_PER_STEP = 4
VMEM_LIMIT = 56 * 1024 * 1024

F32 = jnp.float32
BF16 = jnp.bfloat16


def _dot(a, b):
    return jnp.dot(a, b, preferred_element_type=F32)


def _dot_nt(a, b):
    return lax.dot_general(a, b, (((1,), (1,)), ((), ())), preferred_element_type=F32)


def _rms_rows(x, g):
    ms = jnp.mean(x * x, axis=-1, keepdims=True)
    return x * lax.rsqrt(ms + EPS) * g


def _head_norm_rows(x, ones_bd, g_tiled):
    x2 = x * x
    hi = x2.astype(BF16)
    lo = (x2 - hi.astype(F32)).astype(BF16)
    ms = jnp.concatenate(
        [_dot(hi[:, g * HEAD_GROUP:(g + 1) * HEAD_GROUP], ones_bd)
         + _dot(lo[:, g * HEAD_GROUP:(g + 1) * HEAD_GROUP], ones_bd) for g in range(N_GROUPS)], axis=1)
    return x * lax.rsqrt(ms + EPS) * g_tiled


def _scan_rows(a_ref, u_ref, h_ref, row0, nrows, h0, unroll=False):
    ridx = lax.broadcasted_iota(jnp.int32, (SUBLANES, W_LRU), 0)

    def body(i, hprev):
        r = pl.multiple_of(row0 + i * SUBLANES, SUBLANES)
        a = a_ref[pl.ds(r, SUBLANES), :]
        u = u_ref[pl.ds(r, SUBLANES), :]
        for s in (1, 2, 4):
            a_s = jnp.where(ridx >= s, pltpu.roll(a, s, 0), 1.0)
            u_s = jnp.where(ridx >= s, pltpu.roll(u, s, 0), 0.0)
            u = a * u_s + u
            a = a * a_s
        h = a * hprev + u
        h_ref[pl.ds(r, SUBLANES), :] = h
        return h[SUBLANES - 1:SUBLANES, :]

    return lax.fori_loop(0, nrows // SUBLANES, body, h0, unroll=unroll)


def _lru_inputs(xc, wg_ref, bg_ref, lam_ref, a_ref, u_ref):
    xcb = xc.astype(BF16)
    half = wg_ref.shape[1]
    for j in range(wg_ref.shape[0]):
        sl = slice(j * half, (j + 1) * half)
        gates = _dot(xcb[:, sl], wg_ref[j]) + bg_ref[j]
        r = jax.nn.sigmoid(gates[:, :half])
        i = jax.nn.sigmoid(gates[:, half:])
        log_a = -RG_C * r * jax.nn.softplus(-lam_ref[:, sl])
        a = jnp.exp(log_a)
        a_ref[:, sl] = a
        u_ref[:, sl] = jnp.sqrt(jnp.tanh(-log_a) * (1.0 + a * a)) * (i * xc[:, sl])


def _conv_rows(cb_ref, base, nrows, cw_ref, cb_bias):
    out = cb_bias + cw_ref[CONV_W - 1:CONV_W, :] * cb_ref[pl.ds(base + SUBLANES, nrows), :]
    for k in range(CONV_W - 1):
        shift = CONV_W - 1 - k
        out = out + cw_ref[k:k + 1, :] * cb_ref[pl.ds(base + SUBLANES - shift, nrows), :]
    return out


def _front_prompt_kernel(x_ref, ng_ref, wnat_ref, wt_ref, cw_ref, cbias_ref, wg_ref, bg_ref, lam_ref,
                         qg_ref, kg_ref, ones_ref,
                         lru_ref, qt_ref, kbf_ref, vt_ref, gat_ref, pk_ref, pv_ref, pc_ref, ph_ref,
                         zn_scr, zt_scr, cb_scr, a_scr, u_scr, h_scr, hlast_scr, *, keep_subs):
    sub = QBLOCK
    n_sub = x_ref.shape[1] // sub

    @pl.when(pl.program_id(1) == 0)
    def _():
        cb_scr[0:SUBLANES, :] = jnp.zeros((SUBLANES, W_LRU), F32)
        hlast_scr[...] = jnp.zeros((1, W_LRU), F32)

    def project(i):
        xn = _rms_rows(x_ref[0, i * sub:(i + 1) * sub, :], ng_ref[...]).astype(BF16)
        zn_scr[i % 2] = _dot(xn, wnat_ref[...])
        zt_scr[i % 2] = _dot_nt(wt_ref[...], xn)

    def finish(i):
        rows = slice(i * sub, (i + 1) * sub)
        zn = zn_scr.at[i % 2]
        zt = zt_scr.at[i % 2]
        cb_scr[SUBLANES:SUBLANES + sub, :] = zn[:, :W_LRU]
        xc = _conv_rows(cb_scr, 0, sub, cw_ref, cbias_ref[...])
        pc_ref[0] = cb_scr[sub + SUBLANES - (CONV_W - 1):sub + SUBLANES, :]
        cb_scr[0:SUBLANES, :] = cb_scr[sub:sub + SUBLANES, :]
        _lru_inputs(xc, wg_ref, bg_ref, lam_ref, a_scr, u_scr)
        h_last = _scan_rows(a_scr, u_scr, h_scr, 0, sub, hlast_scr[...], unroll=True)
        hlast_scr[...] = h_last
        ph_ref[0] = h_last
        lru_ref[0, rows, :] = (h_scr[...] * jax.nn.silu(zn[:, W_LRU:2 * W_LRU])).astype(BF16)
        k = _head_norm_rows(zn[:, 2 * W_LRU:], ones_ref[...], kg_ref[...])
        kbf_ref[0, rows, :] = k.astype(BF16)
        q3 = zt[0:W_ATT, :].reshape(N_HEADS, HEAD_DIM, sub)
        ms = jnp.mean(q3 * q3, axis=1, keepdims=True)
        qn = q3 * lax.rsqrt(ms + EPS) * (qg_ref[...] * (HEAD_DIM ** -0.5 * LOG2E))
        qt_ref[0, :, rows] = qn.reshape(W_ATT, sub).astype(BF16)
        vt = zt[W_ATT:2 * W_ATT, :]
        vt_ref[0, :, rows] = vt.astype(BF16)
        gat_ref[0, :, rows] = jax.nn.silu(zt[2 * W_ATT:, :])
        if i >= n_sub - keep_subs:
            first = (i - (n_sub - keep_subs)) * sub
            pk_ref[0, :, first:first + sub] = k.T
            pv_ref[0, :, first:first + sub] = vt

    project(0)
    for i in range(n_sub):
        if i + 1 < n_sub:
            project(i + 1)
        finish(i)


def _front_prompt(x, ng, wnat, wt, cw, cbias, wg, bg, lam, qg, kg, ones_bd):
    b, s, _ = x.shape
    tb = min(FRONT_BLOCK, s)
    keep = min(BAND, s)
    assert s % tb == 0 and tb % QBLOCK == 0 and keep % QBLOCK == 0 and keep <= tb
    const = lambda shape: pl.BlockSpec(shape, lambda i, j: (0,) * len(shape))
    rows_spec = lambda w: pl.BlockSpec((1, tb, w), lambda i, j: (i, j, 0))
    cols_spec = pl.BlockSpec((1, W_ATT, tb), lambda i, j: (i, 0, j))
    per_seq = lambda r, w: pl.BlockSpec((1, r, w), lambda i, j: (i, 0, 0))
    kern = functools.partial(_front_prompt_kernel, keep_subs=keep // QBLOCK)
    return pl.pallas_call(
        kern,
        grid=(b, s // tb),
        in_specs=[
            rows_spec(D_MODEL),
            const((1, D_MODEL)),
            const((D_MODEL, 3 * W_LRU)),
            const((3 * W_ATT, D_MODEL)),
            const((CONV_W, W_LRU)),
            const((1, W_LRU)),
            const((W_LRU // HEAD_GROUP, HEAD_GROUP, 2 * HEAD_GROUP)),
            const((W_LRU // HEAD_GROUP, 1, 2 * HEAD_GROUP)),
            const((1, W_LRU)),
            const((HEAD_DIM, 1)),
            const((1, W_ATT)),
            const((HEAD_GROUP, HEAD_GROUP)),
        ],
        out_specs=[
            rows_spec(W_LRU),
            cols_spec,
            rows_spec(W_ATT),
            cols_spec,
            cols_spec,
            per_seq(W_ATT, keep),
            per_seq(W_ATT, keep),
            per_seq(CONV_W - 1, W_LRU),
            per_seq(1, W_LRU),
        ],
        out_shape=[
            jax.ShapeDtypeStruct((b, s, W_LRU), BF16),
            jax.ShapeDtypeStruct((b, W_ATT, s), BF16),
            jax.ShapeDtypeStruct((b, s, W_ATT), BF16),
            jax.ShapeDtypeStruct((b, W_ATT, s), BF16),
            jax.ShapeDtypeStruct((b, W_ATT, s), F32),
            jax.ShapeDtypeStruct((b, W_ATT, keep), F32),
            jax.ShapeDtypeStruct((b, W_ATT, keep), F32),
            jax.ShapeDtypeStruct((b, CONV_W - 1, W_LRU), F32),
            jax.ShapeDtypeStruct((b, 1, W_LRU), F32),
        ],
        scratch_shapes=[
            pltpu.VMEM((2, QBLOCK, 3 * W_LRU), F32),
            pltpu.VMEM((2, 3 * W_ATT, QBLOCK), F32),
            pltpu.VMEM((QBLOCK + SUBLANES, W_LRU), F32),
            pltpu.VMEM((QBLOCK, W_LRU), F32),
            pltpu.VMEM((QBLOCK, W_LRU), F32),
            pltpu.VMEM((QBLOCK, W_LRU), F32),
            pltpu.VMEM((1, W_LRU), F32),
        ],
        compiler_params=pltpu.CompilerParams(
            dimension_semantics=("arbitrary", "arbitrary"), vmem_limit_bytes=VMEM_LIMIT),
        name="front_prompt",
    )(x, ng, wnat, wt, cw, cbias, wg, bg, lam, qg, kg, ones_bd)


def _toeplitz_rows(w_row, nrows, row0):
    x = jnp.broadcast_to(w_row, (nrows, w_row.shape[1]))
    return pltpu.roll(x, row0, 1, stride=1, stride_axis=0)


def _fill_prompt_bias(w_ref, bias_ref):
    q_chunk = (lax.broadcasted_iota(jnp.int32, (CHUNK, QBLOCK), 1) + BAND) // CHUNK
    for h in range(N_HEADS):
        def body(n, carry, h=h):
            r0 = pl.multiple_of(n * CHUNK, CHUNK)
            t = _toeplitz_rows(w_ref[h], CHUNK, r0)[:, :QBLOCK]
            dc = q_chunk - n
            bias_ref[h, pl.ds(r0, CHUNK), :] = jnp.where((dc >= 0) & (dc <= PAST_CHUNKS), t * LOG2E, NEG)
            return carry
        lax.fori_loop(0, KEY_TILES * QBLOCK // CHUNK, body, 0)


SCORE_LEAD = 3
CHUNKS_PER_TILE = QBLOCK // CHUNK
FRAMES_PER_VREG = LANES // CHUNK


def _lane_cols(kc):
    cols = []
    for c in range(QBLOCK // LANES):
        q_lo = PAST_CHUNKS + c * FRAMES_PER_VREG
        q_hi = q_lo + FRAMES_PER_VREG - 1
        if q_lo - PAST_CHUNKS <= kc <= q_hi:
            cols.append(c)
    return cols


def _fold_rows(x):
    return x.reshape(x.shape[0] // SUBLANES, SUBLANES, x.shape[1])


def _attn_prompt_heads(tiles, key0, qt_ref, k_ref, vt_ref, bias_ref, s_scr, att_scr):
    n_cols = QBLOCK // LANES
    rows = lax.broadcasted_iota(jnp.int32, (HEAD_GROUP, QBLOCK), 0)
    pieces = []
    for i in tiles:
        for cc in range(CHUNKS_PER_TILE):
            kc = i * CHUNKS_PER_TILE + cc
            for c in _lane_cols(kc):
                pieces.append((i, cc, slice(kc * CHUNK, (kc + 1) * CHUNK), slice(c * LANES, (c + 1) * LANES), c))

    def scores(h):
        g, hl = divmod(h, HEADS_PER_GROUP)
        gsl = slice(g * HEAD_GROUP, (g + 1) * HEAD_GROUP)
        in_head = (rows >= hl * HEAD_DIM) & (rows < (hl + 1) * HEAD_DIM)
        qm = jnp.where(in_head, qt_ref[0, gsl, :], jnp.zeros((), BF16))
        s = {i: _dot(k_ref[0, pl.ds(key0 + (i - tiles[0]) * QBLOCK, QBLOCK), gsl], qm) for i in tiles}
        m_acc = [jnp.full((SUBLANES, LANES), NEG, F32) for _ in range(n_cols)]
        for i, cc, rsl, lsl, c in pieces:
            sp = s[i][cc * CHUNK:(cc + 1) * CHUNK, lsl] + bias_ref[h, rsl, lsl]
            s_scr[h % (SCORE_LEAD + 1), rsl, lsl] = sp
            m_acc[c] = jnp.maximum(m_acc[c], jnp.max(_fold_rows(sp), axis=0))
        return [jnp.max(a, axis=0, keepdims=True) for a in m_acc]

    def weights(h, m):
        tiles_p = {}
        for i in tiles:
            blocks = []
            for cc in range(CHUNKS_PER_TILE):
                kc = i * CHUNKS_PER_TILE + cc
                rsl = slice(kc * CHUNK, (kc + 1) * CHUNK)
                cols = []
                for c in range(n_cols):
                    if c in _lane_cols(kc):
                        lsl = slice(c * LANES, (c + 1) * LANES)
                        cols.append(jnp.exp2(s_scr[h % (SCORE_LEAD + 1), rsl, lsl] - m[c]).astype(BF16))
                    else:
                        cols.append(jnp.zeros((CHUNK, LANES), BF16))
                blocks.append(jnp.concatenate(cols, axis=1))
            tiles_p[i] = jnp.concatenate(blocks, axis=0)
        return tiles_p

    ones_rows = jnp.ones((2 * SUBLANES, QBLOCK), BF16)

    def values(h, tiles_p):
        hsl = slice(h * HEAD_DIM, (h + 1) * HEAD_DIM)
        o = None
        for i in tiles:
            vt = vt_ref[0, hsl, pl.ds(key0 + (i - tiles[0]) * QBLOCK, QBLOCK)]
            oi = _dot(jnp.concatenate([vt, ones_rows], axis=0), tiles_p[i])
            o = oi if o is None else o + oi
        att_scr[hsl, :] = o[0:HEAD_DIM, :] * (1.0 / o[HEAD_DIM:HEAD_DIM + 1, :])

    m = {h: scores(h) for h in range(min(SCORE_LEAD, N_HEADS))}
    l_inv = {}
    for h in range(N_HEADS):
        if h + SCORE_LEAD < N_HEADS:
            m[h + SCORE_LEAD] = scores(h + SCORE_LEAD)
        l_inv[h] = weights(h, m.pop(h))
        if h >= 1:
            values(h - 1, l_inv.pop(h - 1))
    values(N_HEADS - 1, l_inv.pop(N_HEADS - 1))


def _attn_prompt_kernel(qt_ref, k_ref, vt_ref, w_ref, gat_ref, out_ref, bias_ref, s_scr, att_scr):
    j = pl.program_id(1)

    @pl.when((pl.program_id(0) == 0) & (j == 0))
    def _():
        _fill_prompt_bias(w_ref, bias_ref)

    for first in range(KEY_TILES):
        tiles = tuple(range(first, KEY_TILES))
        if first > 0:
            cond, key0 = (j == KEY_TILES - 1 - first), 0
        else:
            cond, key0 = (j >= KEY_TILES - 1), pl.multiple_of((j - (KEY_TILES - 1)) * QBLOCK, QBLOCK)

        @pl.when(cond)
        def _(tiles=tiles, key0=key0):
            _attn_prompt_heads(tiles, key0, qt_ref, k_ref, vt_ref, bias_ref, s_scr, att_scr)

    out_ref[0] = (att_scr[...] * gat_ref[0]).T.astype(BF16)


def _attn_prompt(qt, kbf, vt, bias_t, gat):
    b, _, s = qt.shape
    n_blocks = s // QBLOCK
    return pl.pallas_call(
        _attn_prompt_kernel,
        grid=(b, n_blocks),
        in_specs=[
            pl.BlockSpec((1, W_ATT, QBLOCK), lambda i, j: (i, 0, j)),
            pl.BlockSpec((1, s, W_ATT), lambda i, j: (i, 0, 0)),
            pl.BlockSpec((1, W_ATT, s), lambda i, j: (i, 0, 0)),
            pl.BlockSpec((N_HEADS, 1, PROMPT_BIAS_PERIOD), lambda i, j: (0, 0, 0)),
            pl.BlockSpec((1, W_ATT, QBLOCK), lambda i, j: (i, 0, j)),
        ],
        out_specs=pl.BlockSpec((1, QBLOCK, W_ATT), lambda i, j: (i, j, 0)),
        out_shape=jax.ShapeDtypeStruct((b, s, W_ATT), BF16),
        scratch_shapes=[
            pltpu.VMEM((N_HEADS, KEY_TILES * QBLOCK, QBLOCK), F32),
            pltpu.VMEM((SCORE_LEAD + 1, KEY_TILES * QBLOCK, QBLOCK), F32),
            pltpu.VMEM((W_ATT, QBLOCK), F32),
        ],
        compiler_params=pltpu.CompilerParams(
            dimension_semantics=("arbitrary", "arbitrary"), vmem_limit_bytes=VMEM_LIMIT),
        name="attn_prompt",
    )(qt, kbf, vt, bias_t, gat)


def _back_kernel(x_ref, lru_ref, att_ref, p_ref, wo_ref, pg_ref, wpg_ref, wpe_ref, y_ref):
    mix = _dot(lru_ref[...], wo_ref[0:W_LRU, :]) + _dot(att_ref[...], wo_ref[W_LRU:, :])
    h = x_ref[...] + mix
    gate = jax.nn.sigmoid(_dot(_rms_rows(h, pg_ref[...]).astype(BF16), wpg_ref[...]))
    y_ref[...] = h + _dot(p_ref[...].astype(BF16), wpe_ref[...]) * gate


def _back(x2, lru2, att2, p2, wo, pg, wpg, wpe, rows):
    n = x2.shape[0]
    const = lambda shape: pl.BlockSpec(shape, lambda i: (0,) * len(shape))
    return pl.pallas_call(
        _back_kernel,
        grid=(n // rows,),
        in_specs=[
            pl.BlockSpec((rows, D_MODEL), lambda i: (i, 0)),
            pl.BlockSpec((rows, W_LRU), lambda i: (i, 0)),
            pl.BlockSpec((rows, W_ATT), lambda i: (i, 0)),
            pl.BlockSpec((rows, PLE_DIM), lambda i: (i, 0)),
            const((W_LRU + W_ATT, D_MODEL)),
            const((1, D_MODEL)),
            const((D_MODEL, D_MODEL)),
            const((PLE_DIM, D_MODEL)),
        ],
        out_specs=pl.BlockSpec((rows, D_MODEL), lambda i: (i, 0)),
        out_shape=jax.ShapeDtypeStruct((n, D_MODEL), F32),
        compiler_params=pltpu.CompilerParams(
            dimension_semantics=("arbitrary",), vmem_limit_bytes=VMEM_LIMIT),
        name="back",
    )(x2, lru2, att2, p2, wo, pg, wpg, wpe)


def _front_sample_kernel(x_ref, ng_ref, win_ref, cw_ref, cbias_ref, wg_ref, bg_ref, lam_ref,
                         qg_ref, kg_ref, ones_ref, sconv_ref, slru_ref,
                         lru_ref, q_ref, k_ref, v_ref, ga_ref, sc_ref, sh_ref,
                         cb_scr, xc_scr, a_scr, u_scr, h_scr, *, nb, tt):
    seg = tt + SUBLANES
    xn = _rms_rows(x_ref[...], ng_ref[...]).astype(BF16)
    z = _dot(xn, win_ref[...])
    xl = z[:, :W_LRU]
    for s in range(nb):
        cb_scr[s * seg:s * seg + SUBLANES, :] = sconv_ref[s]
        cb_scr[s * seg + SUBLANES:(s + 1) * seg, :] = xl[s * tt:(s + 1) * tt, :]
        xc_scr[s * tt:(s + 1) * tt, :] = _conv_rows(cb_scr, s * seg, tt, cw_ref, cbias_ref[...])
        sc_ref[s] = cb_scr[(s + 1) * seg - (CONV_W - 1):(s + 1) * seg, :]
    _lru_inputs(xc_scr[...], wg_ref, bg_ref, lam_ref, a_scr, u_scr)
    for s in range(nb):
        sh_ref[s] = _scan_rows(a_scr, u_scr, h_scr, s * tt, tt, slru_ref[s])
    lru_ref[...] = (h_scr[...] * jax.nn.silu(z[:, W_LRU:2 * W_LRU])).astype(BF16)

    o = 2 * W_LRU
    q = _head_norm_rows(z[:, o:o + W_ATT], ones_ref[...], qg_ref[...])
    q_ref[...] = (q * (HEAD_DIM ** -0.5)).astype(BF16)
    k_ref[...] = _head_norm_rows(z[:, o + W_ATT:o + 2 * W_ATT], ones_ref[...], kg_ref[...])
    v_ref[...] = z[:, o + 2 * W_ATT:o + 3 * W_ATT]
    ga_ref[...] = jax.nn.silu(z[:, o + 3 * W_ATT:])


def _front_sample(x2, ng, win, cw, cbias, wg, bg, lam, qg_t, kg_t, ones_bd, sconv_pad, slru, nb, tt):
    n = x2.shape[0]
    rows = nb * tt
    const = lambda shape: pl.BlockSpec(shape, lambda i: (0,) * len(shape))
    row_spec = lambda w: pl.BlockSpec((rows, w), lambda i: (i, 0))
    nseq = n // tt
    kern = functools.partial(_front_sample_kernel, nb=nb, tt=tt)
    return pl.pallas_call(
        kern,
        grid=(n // rows,),
        in_specs=[
            row_spec(D_MODEL),
            const((1, D_MODEL)),
            const((D_MODEL, 2 * W_LRU + 4 * W_ATT)),
            const((CONV_W, W_LRU)),
            const((1, W_LRU)),
            const((W_LRU // HEAD_GROUP, HEAD_GROUP, 2 * HEAD_GROUP)),
            const((W_LRU // HEAD_GROUP, 1, 2 * HEAD_GROUP)),
            const((1, W_LRU)),
            const((1, W_ATT)),
            const((1, W_ATT)),
            const((HEAD_GROUP, HEAD_GROUP)),
            pl.BlockSpec((nb, SUBLANES, W_LRU), lambda i: (i, 0, 0)),
            pl.BlockSpec((nb, 1, W_LRU), lambda i: (i, 0, 0)),
        ],
        out_specs=[
            row_spec(W_LRU), row_spec(W_ATT), row_spec(W_ATT), row_spec(W_ATT), row_spec(W_ATT),
            pl.BlockSpec((nb, CONV_W - 1, W_LRU), lambda i: (i, 0, 0)),
            pl.BlockSpec((nb, 1, W_LRU), lambda i: (i, 0, 0)),
        ],
        out_shape=[
            jax.ShapeDtypeStruct((n, W_LRU), BF16),
            jax.ShapeDtypeStruct((n, W_ATT), BF16),
            jax.ShapeDtypeStruct((n, W_ATT), F32),
            jax.ShapeDtypeStruct((n, W_ATT), F32),
            jax.ShapeDtypeStruct((n, W_ATT), F32),
            jax.ShapeDtypeStruct((nseq, CONV_W - 1, W_LRU), F32),
            jax.ShapeDtypeStruct((nseq, 1, W_LRU), F32),
        ],
        scratch_shapes=[
            pltpu.VMEM((nb * (tt + SUBLANES), W_LRU), F32),
            pltpu.VMEM((rows, W_LRU), F32),
            pltpu.VMEM((rows, W_LRU), F32),
            pltpu.VMEM((rows, W_LRU), F32),
            pltpu.VMEM((rows, W_LRU), F32),
        ],
        compiler_params=pltpu.CompilerParams(
            dimension_semantics=("arbitrary",), vmem_limit_bytes=VMEM_LIMIT),
        name="front_sample",
    )(x2, ng, win, cw, cbias, wg, bg, lam, qg_t, kg_t, ones_bd, sconv_pad, slru)


def _attn_sample_kernel(q_ref, kn_ref, vn_ref, kc_ref, vc_ref, w_ref, ga_ref, out_ref, bias_ref):
    tt = q_ref.shape[1]
    lc = kc_ref.shape[-1]

    @pl.when(pl.program_id(0) == 0)
    def _():
        for h in range(N_HEADS):
            g, hl = divmod(h, HEADS_PER_GROUP)
            bias_ref[g, hl * tt:(hl + 1) * tt, :] = _toeplitz_rows(w_ref[h], tt, 0)

    lanes = lax.broadcasted_iota(jnp.int32, (tt, HEAD_GROUP), 1)
    masks = [(lanes >= hl * HEAD_DIM) & (lanes < (hl + 1) * HEAD_DIM) for hl in range(HEADS_PER_GROUP)]
    units = [(s, g) for s in range(q_ref.shape[0]) for g in range(N_GROUPS)]

    def cached(ref, s, g):
        heads = ref[s, g * HEADS_PER_GROUP:(g + 1) * HEADS_PER_GROUP]
        return heads.reshape(HEAD_GROUP, lc).astype(BF16)

    def scores(s, g):
        gsl = slice(g * HEAD_GROUP, (g + 1) * HEAD_GROUP)
        qg = q_ref[s, :, gsl]
        qs = jnp.concatenate([jnp.where(m, qg, jnp.zeros((), BF16)) for m in masks], axis=0)
        sc = _dot(qs, cached(kc_ref, s, g)) + bias_ref[g, :, 0:lc]
        sn = _dot_nt(qs, kn_ref[s, :, gsl].astype(BF16)) + bias_ref[g, :, lc:lc + tt]
        return sc, sn

    def finish(s, g, sc, sn):
        gsl = slice(g * HEAD_GROUP, (g + 1) * HEAD_GROUP)
        m = jnp.maximum(jnp.max(sc, axis=-1, keepdims=True), jnp.max(sn, axis=-1, keepdims=True))
        pc = jnp.exp(sc - m)
        pn = jnp.exp(sn - m)
        l = jnp.sum(pc, axis=-1, keepdims=True) + jnp.sum(pn, axis=-1, keepdims=True)
        o = _dot_nt(pc.astype(BF16), cached(vc_ref, s, g))
        o = (o + _dot(pn.astype(BF16), vn_ref[s, :, gsl].astype(BF16))) * (1.0 / l)
        att = jnp.zeros((tt, HEAD_GROUP), F32)
        for hl in range(HEADS_PER_GROUP):
            att = att + jnp.where(masks[hl], o[hl * tt:(hl + 1) * tt, :], 0.0)
        out_ref[s, :, gsl] = (att * ga_ref[s, :, gsl]).astype(BF16)

    nxt = scores(*units[0])
    for n, (s, g) in enumerate(units):
        cur = nxt
        if n + 1 < len(units):
            nxt = scores(*units[n + 1])
        finish(s, g, *cur)


def _attn_sample(q3, k3, v3, kc_t, vc_t, w_bias, ga3):
    b, tt, _ = q3.shape
    l = kc_t.shape[-1]
    period = w_bias.shape[-1]
    ns = SAMPLE_SEQS_PER_STEP
    assert b % ns == 0
    seq = lambda r: pl.BlockSpec((ns, r, W_ATT), lambda i: (i, 0, 0))
    cache = pl.BlockSpec((ns, N_HEADS, HEAD_DIM, l), lambda i: (i, 0, 0, 0))
    return pl.pallas_call(
        _attn_sample_kernel,
        grid=(b // ns,),
        in_specs=[
            seq(tt), seq(tt), seq(tt), cache, cache,
            pl.BlockSpec((N_HEADS, 1, period), lambda i: (0, 0, 0)),
            seq(tt),
        ],
        out_specs=seq(tt),
        out_shape=jax.ShapeDtypeStruct((b, tt, W_ATT), BF16),
        scratch_shapes=[pltpu.VMEM((N_GROUPS, HEADS_PER_GROUP * tt, period), F32)],
        compiler_params=pltpu.CompilerParams(
            dimension_semantics=("arbitrary",), vmem_limit_bytes=VMEM_LIMIT),
        name="attn_sample",
    )(q3, k3, v3, kc_t, vc_t, w_bias, ga3)


def _block_diag(w):
    n, d, e = w.shape
    eye = jnp.eye(n, dtype=w.dtype)
    return (eye[:, None, :, None] * w[:, :, None, :]).reshape(n * d, n * e)


def _prompt_bias_period(table):
    assert BAND - MAX_REL == MAX_REL and PROMPT_BIAS_PERIOD == KEY_TILES * QBLOCK + QBLOCK
    last = table[2 * MAX_REL:]
    neg_d = jnp.concatenate([table, jnp.broadcast_to(last, (MAX_REL - 1, N_HEADS))])
    w = jnp.concatenate([jnp.broadcast_to(last, (QBLOCK, N_HEADS)), neg_d])
    return w.T.reshape(N_HEADS, 1, PROMPT_BIAS_PERIOD).astype(F32)


def _sample_bias_period(table, tt, l):
    assert l >= MAX_REL
    period = -(-(l + 2 * tt - 1) // LANES) * LANES
    last = table[2 * MAX_REL:]
    n_var = tt + MAX_REL - 1
    var = table[2 * MAX_REL - 1:2 * MAX_REL - 1 - n_var:-1]
    w = jnp.concatenate([jnp.broadcast_to(last, (l - MAX_REL + 1, N_HEADS)), var,
                         jnp.broadcast_to(last, (period - (l + tt), N_HEADS))])
    return w.T.reshape(N_HEADS, 1, period).astype(F32)


def kernel(x_prompt, x_sample, p_prompt, p_sample, cache_k, cache_v, state_conv, state_lru, norm_g, w_in, conv_w, conv_b, gate_a_w, gate_a_b, gate_x_w, gate_x_b, lru_lambda, q_norm_g, k_norm_g, rel_bias, w_out, ple_norm_g, w_ple_gate, w_ple_proj):
    depth = w_in.shape[0]
    b, s, _ = x_prompt.shape
    db, ds, _ = x_sample.shape
    lc = cache_k.shape[2]
    yp, ys = x_prompt, x_sample.reshape(db * ds, D_MODEL)
    ones_bd = _block_diag(jnp.full((HEADS_PER_GROUP, HEAD_DIM, HEAD_DIM), 1.0 / HEAD_DIM, F32)).astype(BF16)
    gate_halves = W_LRU // HEAD_GROUP
    blocks_per_half = LRU_BLOCKS // gate_halves
    outs = [[] for _ in range(8)]
    sample_nb = 8
    for l in range(depth):
        win = w_in[l].astype(BF16)
        o = 2 * W_LRU
        wnat = jnp.concatenate([win[:, :o], win[:, o + W_ATT:o + 2 * W_ATT]], axis=1)
        wt = jnp.concatenate([win[:, o:o + W_ATT], win[:, o + 2 * W_ATT:]], axis=1).T
        ng = norm_g[l].reshape(1, D_MODEL)
        cw = conv_w[l]
        cbias = conv_b[l].reshape(1, W_LRU)
        wg = jnp.stack([
            jnp.concatenate([_block_diag(w[j * blocks_per_half:(j + 1) * blocks_per_half])
                             for w in (gate_a_w[l], gate_x_w[l])], axis=1)
            for j in range(gate_halves)]).astype(BF16)
        bg = jnp.concatenate([gate_a_b[l].reshape(gate_halves, 1, HEAD_GROUP),
                              gate_x_b[l].reshape(gate_halves, 1, HEAD_GROUP)], axis=2)
        lam = lru_lambda[l].reshape(1, W_LRU)
        qg_col = q_norm_g[l].reshape(HEAD_DIM, 1)
        qg_t = jnp.tile(q_norm_g[l], N_HEADS).reshape(1, W_ATT)
        kg_t = jnp.tile(k_norm_g[l], N_HEADS).reshape(1, W_ATT)
        wo = w_out[l].astype(BF16)
        pg = ple_norm_g[l].reshape(1, D_MODEL)
        wpg = w_ple_gate[l].astype(BF16)
        wpe = w_ple_proj[l].astype(BF16)

        lru_g, qt, kbf, vt, gat, pk, pv, pc, ph = _front_prompt(
            yp, ng, wnat, wt, cw, cbias, wg, bg, lam, qg_col, kg_t, ones_bd)
        att_g = _attn_prompt(qt, kbf, vt, _prompt_bias_period(rel_bias[l]), gat)
        yp = _back(yp.reshape(b * s, D_MODEL), lru_g.reshape(b * s, W_LRU), att_g.reshape(b * s, W_ATT),
                   p_prompt[l].reshape(b * s, PLE_DIM), wo, pg, wpg, wpe, 512).reshape(b, s, D_MODEL)
        to_frames = lambda a: jnp.transpose(a.reshape(b, N_HEADS, HEAD_DIM, a.shape[-1]), (0, 3, 1, 2))
        outs[0].append(to_frames(pk))
        outs[1].append(to_frames(pv))
        outs[2].append(pc)
        outs[3].append(ph.reshape(b, W_LRU))

        sconv_pad = jnp.pad(state_conv[l], ((0, 0), (SUBLANES - (CONV_W - 1), 0), (0, 0)))
        lru_s, q_s, k_s, v_s, ga_s, sc, sh = _front_sample(
            ys, ng, win, cw, cbias, wg, bg, lam, qg_t, kg_t, ones_bd,
            sconv_pad, state_lru[l].reshape(db, 1, W_LRU), sample_nb, ds)
        att_s = _attn_sample(q_s.reshape(db, ds, W_ATT), k_s.reshape(db, ds, W_ATT), v_s.reshape(db, ds, W_ATT),
                             jnp.transpose(cache_k[l], (0, 2, 3, 1)), jnp.transpose(cache_v[l], (0, 2, 3, 1)),
                             _sample_bias_period(rel_bias[l], ds, lc), ga_s.reshape(db, ds, W_ATT))
        ys = _back(ys, lru_s, att_s.reshape(db * ds, W_ATT), p_sample[l].reshape(db * ds, PLE_DIM),
                   wo, pg, wpg, wpe, 256)
        outs[4].append(k_s.reshape(db, ds, N_HEADS, HEAD_DIM))
        outs[5].append(v_s.reshape(db, ds, N_HEADS, HEAD_DIM))
        outs[6].append(sc)
        outs[7].append(sh.reshape(db, W_LRU))
    return (yp, ys.reshape(db, ds, D_MODEL)) + tuple(jnp.stack(o) for o in outs)
```

```python
import functools

import jax
import jax.numpy as jnp
from jax import lax
from jax.experimental import pallas as pl
from jax.experimental.pallas import tpu as pltpu

D_MODEL = 1024
CHUNK = 64
PAST_CHUNKS = 8
BAND = PAST_CHUNKS * CHUNK
W_LRU = D_MODEL // 2
LRU_BLOCKS = 8
LRU_BLOCK = W_LRU // LRU_BLOCKS
CONV_W = 4
RG_C = 8.0
HEAD_DIM = 64
W_ATT = D_MODEL // 2
N_HEADS = W_ATT // HEAD_DIM
MAX_REL = 256
PLE_DIM = 256
EPS = 1e-6
NEG = -1e30
LOG2E = 1.4426950408889634

SUBLANES = 8
LANES = 128
HEAD_GROUP = 256
HEADS_PER_GROUP = HEAD_GROUP // HEAD_DIM
N_GROUPS = W_ATT // HEAD_GROUP
QBLOCK = 256
KEY_TILES = BAND // QBLOCK + 1
PROMPT_BIAS_PERIOD = (KEY_TILES + 1) * QBLOCK
FRONT_BLOCK = 1024
SAMPLE_SEQS_PER_STEP = 4
VMEM_LIMIT = 56 * 1024 * 1024

F32 = jnp.float32
BF16 = jnp.bfloat16


def _dot(a, b):
    return jnp.dot(a, b, preferred_element_type=F32)


def _dot_nt(a, b):
    return lax.dot_general(a, b, (((1,), (1,)), ((), ())), preferred_element_type=F32)


def _rms_rows(x, g):
    ms = jnp.mean(x * x, axis=-1, keepdims=True)
    return x * lax.rsqrt(ms + EPS) * g


def _head_norm_rows(x, ones_bd, g_tiled):
    x2 = x * x
    hi = x2.astype(BF16)
    lo = (x2 - hi.astype(F32)).astype(BF16)
    ms = jnp.concatenate(
        [_dot(hi[:, g * HEAD_GROUP:(g + 1) * HEAD_GROUP], ones_bd)
         + _dot(lo[:, g * HEAD_GROUP:(g + 1) * HEAD_GROUP], ones_bd) for g in range(N_GROUPS)], axis=1)
    return x * lax.rsqrt(ms + EPS) * g_tiled


def _scan_rows(a_ref, u_ref, h_ref, row0, nrows, h0, unroll=False):
    ridx = lax.broadcasted_iota(jnp.int32, (SUBLANES, W_LRU), 0)

    def body(i, hprev):
        r = pl.multiple_of(row0 + i * SUBLANES, SUBLANES)
        a = a_ref[pl.ds(r, SUBLANES), :]
        u = u_ref[pl.ds(r, SUBLANES), :]
        for s in (1, 2, 4):
            a_s = jnp.where(ridx >= s, pltpu.roll(a, s, 0), 1.0)
            u_s = jnp.where(ridx >= s, pltpu.roll(u, s, 0), 0.0)
            u = a * u_s + u
            a = a * a_s
        h = a * hprev + u
        h_ref[pl.ds(r, SUBLANES), :] = h
        return h[SUBLANES - 1:SUBLANES, :]

    return lax.fori_loop(0, nrows // SUBLANES, body, h0, unroll=unroll)


def _lru_inputs(xc, wg_ref, bg_ref, lam_ref, a_ref, u_ref):
    xcb = xc.astype(BF16)
    half = wg_ref.shape[1]
    for j in range(wg_ref.shape[0]):
        sl = slice(j * half, (j + 1) * half)
        gates = _dot(xcb[:, sl], wg_ref[j]) + bg_ref[j]
        r = jax.nn.sigmoid(gates[:, :half])
        i = jax.nn.sigmoid(gates[:, half:])
        log_a = -RG_C * r * jax.nn.softplus(-lam_ref[:, sl])
        a = jnp.exp(log_a)
        a_ref[:, sl] = a
        u_ref[:, sl] = jnp.sqrt(jnp.tanh(-log_a) * (1.0 + a * a)) * (i * xc[:, sl])


def _conv_rows(cb_ref, base, nrows, cw_ref, cb_bias):
    out = cb_bias + cw_ref[CONV_W - 1:CONV_W, :] * cb_ref[pl.ds(base + SUBLANES, nrows), :]
    for k in range(CONV_W - 1):
        shift = CONV_W - 1 - k
        out = out + cw_ref[k:k + 1, :] * cb_ref[pl.ds(base + SUBLANES - shift, nrows), :]
    return out


def _front_prompt_kernel(x_ref, ng_ref, wnat_ref, wt_ref, cw_ref, cbias_ref, wg_ref, bg_ref, lam_ref,
                         qg_ref, kg_ref, ones_ref,
                         lru_ref, qt_ref, kbf_ref, vt_ref, gat_ref, pk_ref, pv_ref, pc_ref, ph_ref,
                         zn_scr, zt_scr, cb_scr, a_scr, u_scr, h_scr, hlast_scr, *, keep_subs):
    sub = QBLOCK
    n_sub = x_ref.shape[1] // sub

    @pl.when(pl.program_id(1) == 0)
    def _():
        cb_scr[0:SUBLANES, :] = jnp.zeros((SUBLANES, W_LRU), F32)
        hlast_scr[...] = jnp.zeros((1, W_LRU), F32)

    def project(i):
        xn = _rms_rows(x_ref[0, i * sub:(i + 1) * sub, :], ng_ref[...]).astype(BF16)
        zn_scr[i % 2] = _dot(xn, wnat_ref[...])
        zt_scr[i % 2] = _dot_nt(wt_ref[...], xn)

    def finish(i):
        rows = slice(i * sub, (i + 1) * sub)
        zn = zn_scr.at[i % 2]
        zt = zt_scr.at[i % 2]
        cb_scr[SUBLANES:SUBLANES + sub, :] = zn[:, :W_LRU]
        xc = _conv_rows(cb_scr, 0, sub, cw_ref, cbias_ref[...])
        pc_ref[0] = cb_scr[sub + SUBLANES - (CONV_W - 1):sub + SUBLANES, :]
        cb_scr[0:SUBLANES, :] = cb_scr[sub:sub + SUBLANES, :]
        _lru_inputs(xc, wg_ref, bg_ref, lam_ref, a_scr, u_scr)
        h_last = _scan_rows(a_scr, u_scr, h_scr, 0, sub, hlast_scr[...], unroll=True)
        hlast_scr[...] = h_last
        ph_ref[0] = h_last
        lru_ref[0, rows, :] = (h_scr[...] * jax.nn.silu(zn[:, W_LRU:2 * W_LRU])).astype(BF16)
        k = _head_norm_rows(zn[:, 2 * W_LRU:], ones_ref[...], kg_ref[...])
        kbf_ref[0, rows, :] = k.astype(BF16)
        q3 = zt[0:W_ATT, :].reshape(N_HEADS, HEAD_DIM, sub)
        ms = jnp.mean(q3 * q3, axis=1, keepdims=True)
        qn = q3 * lax.rsqrt(ms + EPS) * (qg_ref[...] * (HEAD_DIM ** -0.5 * LOG2E))
        qt_ref[0, :, rows] = qn.reshape(W_ATT, sub).astype(BF16)
        vt = zt[W_ATT:2 * W_ATT, :]
        vt_ref[0, :, rows] = vt.astype(BF16)
        gat_ref[0, :, rows] = jax.nn.silu(zt[2 * W_ATT:, :])
        if i >= n_sub - keep_subs:
            first = (i - (n_sub - keep_subs)) * sub
            pk_ref[0, :, first:first + sub] = k.T
            pv_ref[0, :, first:first + sub] = vt

    project(0)
    for i in range(n_sub):
        if i + 1 < n_sub:
            project(i + 1)
        finish(i)


def _front_prompt(x, ng, wnat, wt, cw, cbias, wg, bg, lam, qg, kg, ones_bd):
    b, s, _ = x.shape
    tb = min(FRONT_BLOCK, s)
    keep = min(BAND, s)
    assert s % tb == 0 and tb % QBLOCK == 0 and keep % QBLOCK == 0 and keep <= tb
    const = lambda shape: pl.BlockSpec(shape, lambda i, j: (0,) * len(shape))
    rows_spec = lambda w: pl.BlockSpec((1, tb, w), lambda i, j: (i, j, 0))
    cols_spec = pl.BlockSpec((1, W_ATT, tb), lambda i, j: (i, 0, j))
    per_seq = lambda r, w: pl.BlockSpec((1, r, w), lambda i, j: (i, 0, 0))
    kern = functools.partial(_front_prompt_kernel, keep_subs=keep // QBLOCK)
    return pl.pallas_call(
        kern,
        grid=(b, s // tb),
        in_specs=[
            rows_spec(D_MODEL),
            const((1, D_MODEL)),
            const((D_MODEL, 3 * W_LRU)),
            const((3 * W_ATT, D_MODEL)),
            const((CONV_W, W_LRU)),
            const((1, W_LRU)),
            const((W_LRU // HEAD_GROUP, HEAD_GROUP, 2 * HEAD_GROUP)),
            const((W_LRU // HEAD_GROUP, 1, 2 * HEAD_GROUP)),
            const((1, W_LRU)),
            const((HEAD_DIM, 1)),
            const((1, W_ATT)),
            const((HEAD_GROUP, HEAD_GROUP)),
        ],
        out_specs=[
            rows_spec(W_LRU),
            cols_spec,
            rows_spec(W_ATT),
            cols_spec,
            cols_spec,
            per_seq(W_ATT, keep),
            per_seq(W_ATT, keep),
            per_seq(CONV_W - 1, W_LRU),
            per_seq(1, W_LRU),
        ],
        out_shape=[
            jax.ShapeDtypeStruct((b, s, W_LRU), BF16),
            jax.ShapeDtypeStruct((b, W_ATT, s), BF16),
            jax.ShapeDtypeStruct((b, s, W_ATT), BF16),
            jax.ShapeDtypeStruct((b, W_ATT, s), BF16),
            jax.ShapeDtypeStruct((b, W_ATT, s), F32),
            jax.ShapeDtypeStruct((b, W_ATT, keep), F32),
            jax.ShapeDtypeStruct((b, W_ATT, keep), F32),
            jax.ShapeDtypeStruct((b, CONV_W - 1, W_LRU), F32),
            jax.ShapeDtypeStruct((b, 1, W_LRU), F32),
        ],
        scratch_shapes=[
            pltpu.VMEM((2, QBLOCK, 3 * W_LRU), F32),
            pltpu.VMEM((2, 3 * W_ATT, QBLOCK), F32),
            pltpu.VMEM((QBLOCK + SUBLANES, W_LRU), F32),
            pltpu.VMEM((QBLOCK, W_LRU), F32),
            pltpu.VMEM((QBLOCK, W_LRU), F32),
            pltpu.VMEM((QBLOCK, W_LRU), F32),
            pltpu.VMEM((1, W_LRU), F32),
        ],
        compiler_params=pltpu.CompilerParams(
            dimension_semantics=("arbitrary", "arbitrary"), vmem_limit_bytes=VMEM_LIMIT),
        name="front_prompt",
    )(x, ng, wnat, wt, cw, cbias, wg, bg, lam, qg, kg, ones_bd)


def _toeplitz_rows(w_row, nrows, row0):
    x = jnp.broadcast_to(w_row, (nrows, w_row.shape[1]))
    return pltpu.roll(x, row0, 1, stride=1, stride_axis=0)


def _fill_prompt_bias(w_ref, bias_ref):
    q_chunk = (lax.broadcasted_iota(jnp.int32, (CHUNK, QBLOCK), 1) + BAND) // CHUNK
    for h in range(N_HEADS):
        def body(n, carry, h=h):
            r0 = pl.multiple_of(n * CHUNK, CHUNK)
            far = w_ref[h][:, 0:1]
            t = _toeplitz_rows(w_ref[h] - far, CHUNK, r0)[:, :QBLOCK]
            dc = q_chunk - n
            bias_ref[h, pl.ds(r0, CHUNK), :] = jnp.where((dc >= 0) & (dc <= PAST_CHUNKS), t * LOG2E, NEG)
            return carry
        lax.fori_loop(0, KEY_TILES * QBLOCK // CHUNK, body, 0)


SCORE_LEAD = 3
CHUNKS_PER_TILE = QBLOCK // CHUNK
FRAMES_PER_VREG = LANES // CHUNK


def _lane_cols(kc):
    cols = []
    for c in range(QBLOCK // LANES):
        q_lo = PAST_CHUNKS + c * FRAMES_PER_VREG
        q_hi = q_lo + FRAMES_PER_VREG - 1
        if q_lo - PAST_CHUNKS <= kc <= q_hi:
            cols.append(c)
    return cols


def _needs_bias(kc, c):
    q_lo = PAST_CHUNKS + c * FRAMES_PER_VREG
    q_hi = q_lo + FRAMES_PER_VREG - 1
    nearest = (q_lo - kc) * CHUNK - (CHUNK - 1)
    return not (nearest >= MAX_REL and q_hi - kc <= PAST_CHUNKS)


def _fold_rows(x):
    return x.reshape(x.shape[0] // SUBLANES, SUBLANES, x.shape[1])


def _attn_prompt_heads(tiles, key0, qt_ref, k_ref, vt_ref, bias_ref, s_scr, att_scr):
    n_cols = QBLOCK // LANES
    rows = lax.broadcasted_iota(jnp.int32, (HEAD_GROUP, QBLOCK), 0)
    pieces = []
    for i in tiles:
        for cc in range(CHUNKS_PER_TILE):
            kc = i * CHUNKS_PER_TILE + cc
            for c in _lane_cols(kc):
                pieces.append((i, cc, slice(kc * CHUNK, (kc + 1) * CHUNK), slice(c * LANES, (c + 1) * LANES), c,
                               _needs_bias(kc, c)))

    def scores(h):
        g, hl = divmod(h, HEADS_PER_GROUP)
        gsl = slice(g * HEAD_GROUP, (g + 1) * HEAD_GROUP)
        in_head = (rows >= hl * HEAD_DIM) & (rows < (hl + 1) * HEAD_DIM)
        qm = jnp.where(in_head, qt_ref[0, gsl, :], jnp.zeros((), BF16))
        s = {i: _dot(k_ref[0, pl.ds(key0 + (i - tiles[0]) * QBLOCK, QBLOCK), gsl], qm) for i in tiles}
        m_acc = [jnp.full((SUBLANES, LANES), NEG, F32) for _ in range(n_cols)]
        for i, cc, rsl, lsl, c, biased in pieces:
            sp = s[i][cc * CHUNK:(cc + 1) * CHUNK, lsl]
            if biased:
                sp = sp + bias_ref[h, rsl, lsl]
            s_scr[h % (SCORE_LEAD + 1), rsl, lsl] = sp
            m_acc[c] = jnp.maximum(m_acc[c], jnp.max(_fold_rows(sp), axis=0))
        return [jnp.max(a, axis=0, keepdims=True) for a in m_acc]

    def weights(h, m):
        tiles_p = {}
        for i in tiles:
            blocks = []
            for cc in range(CHUNKS_PER_TILE):
                kc = i * CHUNKS_PER_TILE + cc
                rsl = slice(kc * CHUNK, (kc + 1) * CHUNK)
                cols = []
                for c in range(n_cols):
                    if c in _lane_cols(kc):
                        lsl = slice(c * LANES, (c + 1) * LANES)
                        cols.append(jnp.exp2(s_scr[h % (SCORE_LEAD + 1), rsl, lsl] - m[c]).astype(BF16))
                    else:
                        cols.append(jnp.zeros((CHUNK, LANES), BF16))
                blocks.append(jnp.concatenate(cols, axis=1))
            tiles_p[i] = jnp.concatenate(blocks, axis=0)
        return tiles_p

    ones_rows = jnp.ones((2 * SUBLANES, QBLOCK), BF16)

    def values(h, tiles_p):
        hsl = slice(h * HEAD_DIM, (h + 1) * HEAD_DIM)
        o = None
        for i in tiles:
            vt = vt_ref[0, hsl, pl.ds(key0 + (i - tiles[0]) * QBLOCK, QBLOCK)]
            oi = _dot(jnp.concatenate([vt, ones_rows], axis=0), tiles_p[i])
            o = oi if o is None else o + oi
        att_scr[hsl, :] = o[0:HEAD_DIM, :] * (1.0 / o[HEAD_DIM:HEAD_DIM + 1, :])

    m = {h: scores(h) for h in range(min(SCORE_LEAD, N_HEADS))}
    for h in range(N_HEADS):
        if h + SCORE_LEAD < N_HEADS:
            m[h + SCORE_LEAD] = scores(h + SCORE_LEAD)
        values(h, weights(h, m.pop(h)))


def _attn_prompt_kernel(qt_ref, k_ref, vt_ref, w_ref, gat_ref, out_ref, bias_ref, s_scr, att_scr):
    j = pl.program_id(1)

    @pl.when((pl.program_id(0) == 0) & (j == 0))
    def _():
        _fill_prompt_bias(w_ref, bias_ref)

    for first in range(KEY_TILES):
        tiles = tuple(range(first, KEY_TILES))
        if first > 0:
            cond, key0 = (j == KEY_TILES - 1 - first), 0
        else:
            cond, key0 = (j >= KEY_TILES - 1), pl.multiple_of((j - (KEY_TILES - 1)) * QBLOCK, QBLOCK)

        @pl.when(cond)
        def _(tiles=tiles, key0=key0):
            _attn_prompt_heads(tiles, key0, qt_ref, k_ref, vt_ref, bias_ref, s_scr, att_scr)

    out_ref[0] = (att_scr[...] * gat_ref[0]).T.astype(BF16)


def _attn_prompt(qt, kbf, vt, bias_t, gat):
    b, _, s = qt.shape
    n_blocks = s // QBLOCK
    return pl.pallas_call(
        _attn_prompt_kernel,
        grid=(b, n_blocks),
        in_specs=[
            pl.BlockSpec((1, W_ATT, QBLOCK), lambda i, j: (i, 0, j)),
            pl.BlockSpec((1, s, W_ATT), lambda i, j: (i, 0, 0)),
            pl.BlockSpec((1, W_ATT, s), lambda i, j: (i, 0, 0)),
            pl.BlockSpec((N_HEADS, 1, PROMPT_BIAS_PERIOD), lambda i, j: (0, 0, 0)),
            pl.BlockSpec((1, W_ATT, QBLOCK), lambda i, j: (i, 0, j)),
        ],
        out_specs=pl.BlockSpec((1, QBLOCK, W_ATT), lambda i, j: (i, j, 0)),
        out_shape=jax.ShapeDtypeStruct((b, s, W_ATT), BF16),
        scratch_shapes=[
            pltpu.VMEM((N_HEADS, KEY_TILES * QBLOCK, QBLOCK), F32),
            pltpu.VMEM((SCORE_LEAD + 1, KEY_TILES * QBLOCK, QBLOCK), F32),
            pltpu.VMEM((W_ATT, QBLOCK), F32),
        ],
        compiler_params=pltpu.CompilerParams(
            dimension_semantics=("arbitrary", "arbitrary"), vmem_limit_bytes=VMEM_LIMIT),
        name="attn_prompt",
    )(qt, kbf, vt, bias_t, gat)


def _back_kernel(x_ref, lru_ref, att_ref, p_ref, wo_ref, pg_ref, wpg_ref, wpe_ref, y_ref):
    mix = _dot(lru_ref[...], wo_ref[0:W_LRU, :]) + _dot(att_ref[...], wo_ref[W_LRU:, :])
    h = x_ref[...] + mix
    gate = jax.nn.sigmoid(_dot(_rms_rows(h, pg_ref[...]).astype(BF16), wpg_ref[...]))
    y_ref[...] = h + _dot(p_ref[...].astype(BF16), wpe_ref[...]) * gate


def _back(x2, lru2, att2, p2, wo, pg, wpg, wpe, rows):
    n = x2.shape[0]
    const = lambda shape: pl.BlockSpec(shape, lambda i: (0,) * len(shape))
    return pl.pallas_call(
        _back_kernel,
        grid=(n // rows,),
        in_specs=[
            pl.BlockSpec((rows, D_MODEL), lambda i: (i, 0)),
            pl.BlockSpec((rows, W_LRU), lambda i: (i, 0)),
            pl.BlockSpec((rows, W_ATT), lambda i: (i, 0)),
            pl.BlockSpec((rows, PLE_DIM), lambda i: (i, 0)),
            const((W_LRU + W_ATT, D_MODEL)),
            const((1, D_MODEL)),
            const((D_MODEL, D_MODEL)),
            const((PLE_DIM, D_MODEL)),
        ],
        out_specs=pl.BlockSpec((rows, D_MODEL), lambda i: (i, 0)),
        out_shape=jax.ShapeDtypeStruct((n, D_MODEL), F32),
        compiler_params=pltpu.CompilerParams(
            dimension_semantics=("arbitrary",), vmem_limit_bytes=VMEM_LIMIT),
        name="back",
    )(x2, lru2, att2, p2, wo, pg, wpg, wpe)


def _front_sample_kernel(x_ref, ng_ref, win_ref, cw_ref, cbias_ref, wg_ref, bg_ref, lam_ref,
                         qg_ref, kg_ref, ones_ref, sconv_ref, slru_ref,
                         lru_ref, q_ref, k_ref, v_ref, ga_ref, sc_ref, sh_ref,
                         cb_scr, xc_scr, a_scr, u_scr, h_scr, *, nb, tt):
    seg = tt + SUBLANES
    xn = _rms_rows(x_ref[...], ng_ref[...]).astype(BF16)
    z = _dot(xn, win_ref[...])
    xl = z[:, :W_LRU]
    for s in range(nb):
        cb_scr[s * seg:s * seg + SUBLANES, :] = sconv_ref[s]
        cb_scr[s * seg + SUBLANES:(s + 1) * seg, :] = xl[s * tt:(s + 1) * tt, :]
        xc_scr[s * tt:(s + 1) * tt, :] = _conv_rows(cb_scr, s * seg, tt, cw_ref, cbias_ref[...])
        sc_ref[s] = cb_scr[(s + 1) * seg - (CONV_W - 1):(s + 1) * seg, :]
    _lru_inputs(xc_scr[...], wg_ref, bg_ref, lam_ref, a_scr, u_scr)
    for s in range(nb):
        sh_ref[s] = _scan_rows(a_scr, u_scr, h_scr, s * tt, tt, slru_ref[s])
    lru_ref[...] = (h_scr[...] * jax.nn.silu(z[:, W_LRU:2 * W_LRU])).astype(BF16)

    o = 2 * W_LRU
    q = _head_norm_rows(z[:, o:o + W_ATT], ones_ref[...], qg_ref[...])
    q_ref[...] = (q * (HEAD_DIM ** -0.5)).astype(BF16)
    k_ref[...] = _head_norm_rows(z[:, o + W_ATT:o + 2 * W_ATT], ones_ref[...], kg_ref[...])
    v_ref[...] = z[:, o + 2 * W_ATT:o + 3 * W_ATT]
    ga_ref[...] = jax.nn.silu(z[:, o + 3 * W_ATT:])


def _front_sample(x2, ng, win, cw, cbias, wg, bg, lam, qg_t, kg_t, ones_bd, sconv_pad, slru, nb, tt):
    n = x2.shape[0]
    rows = nb * tt
    const = lambda shape: pl.BlockSpec(shape, lambda i: (0,) * len(shape))
    row_spec = lambda w: pl.BlockSpec((rows, w), lambda i: (i, 0))
    nseq = n // tt
    kern = functools.partial(_front_sample_kernel, nb=nb, tt=tt)
    return pl.pallas_call(
        kern,
        grid=(n // rows,),
        in_specs=[
            row_spec(D_MODEL),
            const((1, D_MODEL)),
            const((D_MODEL, 2 * W_LRU + 4 * W_ATT)),
            const((CONV_W, W_LRU)),
            const((1, W_LRU)),
            const((W_LRU // HEAD_GROUP, HEAD_GROUP, 2 * HEAD_GROUP)),
            const((W_LRU // HEAD_GROUP, 1, 2 * HEAD_GROUP)),
            const((1, W_LRU)),
            const((1, W_ATT)),
            const((1, W_ATT)),
            const((HEAD_GROUP, HEAD_GROUP)),
            pl.BlockSpec((nb, SUBLANES, W_LRU), lambda i: (i, 0, 0)),
            pl.BlockSpec((nb, 1, W_LRU), lambda i: (i, 0, 0)),
        ],
        out_specs=[
            row_spec(W_LRU), row_spec(W_ATT), row_spec(W_ATT), row_spec(W_ATT), row_spec(W_ATT),
            pl.BlockSpec((nb, CONV_W - 1, W_LRU), lambda i: (i, 0, 0)),
            pl.BlockSpec((nb, 1, W_LRU), lambda i: (i, 0, 0)),
        ],
        out_shape=[
            jax.ShapeDtypeStruct((n, W_LRU), BF16),
            jax.ShapeDtypeStruct((n, W_ATT), BF16),
            jax.ShapeDtypeStruct((n, W_ATT), F32),
            jax.ShapeDtypeStruct((n, W_ATT), F32),
            jax.ShapeDtypeStruct((n, W_ATT), F32),
            jax.ShapeDtypeStruct((nseq, CONV_W - 1, W_LRU), F32),
            jax.ShapeDtypeStruct((nseq, 1, W_LRU), F32),
        ],
        scratch_shapes=[
            pltpu.VMEM((nb * (tt + SUBLANES), W_LRU), F32),
            pltpu.VMEM((rows, W_LRU), F32),
            pltpu.VMEM((rows, W_LRU), F32),
            pltpu.VMEM((rows, W_LRU), F32),
            pltpu.VMEM((rows, W_LRU), F32),
        ],
        compiler_params=pltpu.CompilerParams(
            dimension_semantics=("arbitrary",), vmem_limit_bytes=VMEM_LIMIT),
        name="front_sample",
    )(x2, ng, win, cw, cbias, wg, bg, lam, qg_t, kg_t, ones_bd, sconv_pad, slru)


def _attn_sample_kernel(q_ref, kn_ref, vn_ref, kc_ref, vc_ref, w_ref, ga_ref, out_ref, bias_ref):
    tt = q_ref.shape[1]
    lc = kc_ref.shape[-1]

    @pl.when(pl.program_id(0) == 0)
    def _():
        for h in range(N_HEADS):
            g, hl = divmod(h, HEADS_PER_GROUP)
            bias_ref[g, hl * tt:(hl + 1) * tt, :] = _toeplitz_rows(w_ref[h], tt, 0)

    lanes = lax.broadcasted_iota(jnp.int32, (tt, HEAD_GROUP), 1)
    masks = [(lanes >= hl * HEAD_DIM) & (lanes < (hl + 1) * HEAD_DIM) for hl in range(HEADS_PER_GROUP)]
    units = [(s, g) for s in range(q_ref.shape[0]) for g in range(N_GROUPS)]

    def cached(ref, s, g):
        heads = ref[s, g * HEADS_PER_GROUP:(g + 1) * HEADS_PER_GROUP]
        return heads.reshape(HEAD_GROUP, lc).astype(BF16)

    def scores(s, g):
        gsl = slice(g * HEAD_GROUP, (g + 1) * HEAD_GROUP)
        qg = q_ref[s, :, gsl]
        qs = jnp.concatenate([jnp.where(m, qg, jnp.zeros((), BF16)) for m in masks], axis=0)
        sc = _dot(qs, cached(kc_ref, s, g)) + bias_ref[g, :, 0:lc]
        sn = _dot_nt(qs, kn_ref[s, :, gsl].astype(BF16)) + bias_ref[g, :, lc:lc + tt]
        return sc, sn

    def finish(s, g, sc, sn):
        gsl = slice(g * HEAD_GROUP, (g + 1) * HEAD_GROUP)
        m = jnp.maximum(jnp.max(sc, axis=-1, keepdims=True), jnp.max(sn, axis=-1, keepdims=True))
        pc = jnp.exp(sc - m)
        pn = jnp.exp(sn - m)
        l = jnp.sum(pc, axis=-1, keepdims=True) + jnp.sum(pn, axis=-1, keepdims=True)
        o = _dot_nt(pc.astype(BF16), cached(vc_ref, s, g))
        o = (o + _dot(pn.astype(BF16), vn_ref[s, :, gsl].astype(BF16))) * (1.0 / l)
        att = jnp.zeros((tt, HEAD_GROUP), F32)
        for hl in range(HEADS_PER_GROUP):
            att = att + jnp.where(masks[hl], o[hl * tt:(hl + 1) * tt, :], 0.0)
        out_ref[s, :, gsl] = (att * ga_ref[s, :, gsl]).astype(BF16)

    nxt = scores(*units[0])
    for n, (s, g) in enumerate(units):
        cur = nxt
        if n + 1 < len(units):
            nxt = scores(*units[n + 1])
        finish(s, g, *cur)


def _attn_sample(q3, k3, v3, kc_t, vc_t, w_bias, ga3):
    b, tt, _ = q3.shape
    l = kc_t.shape[-1]
    period = w_bias.shape[-1]
    ns = SAMPLE_SEQS_PER_STEP
    assert b % ns == 0
    seq = lambda r: pl.BlockSpec((ns, r, W_ATT), lambda i: (i, 0, 0))
    cache = pl.BlockSpec((ns, N_HEADS, HEAD_DIM, l), lambda i: (i, 0, 0, 0))
    return pl.pallas_call(
        _attn_sample_kernel,
        grid=(b // ns,),
        in_specs=[
            seq(tt), seq(tt), seq(tt), cache, cache,
            pl.BlockSpec((N_HEADS, 1, period), lambda i: (0, 0, 0)),
            seq(tt),
        ],
        out_specs=seq(tt),
        out_shape=jax.ShapeDtypeStruct((b, tt, W_ATT), BF16),
        scratch_shapes=[pltpu.VMEM((N_GROUPS, HEADS_PER_GROUP * tt, period), F32)],
        compiler_params=pltpu.CompilerParams(
            dimension_semantics=("arbitrary",), vmem_limit_bytes=VMEM_LIMIT),
        name="attn_sample",
    )(q3, k3, v3, kc_t, vc_t, w_bias, ga3)


def _block_diag(w):
    n, d, e = w.shape
    eye = jnp.eye(n, dtype=w.dtype)
    return (eye[:, None, :, None] * w[:, :, None, :]).reshape(n * d, n * e)


def _prompt_bias_period(table):
    assert BAND - MAX_REL == MAX_REL and PROMPT_BIAS_PERIOD == KEY_TILES * QBLOCK + QBLOCK
    last = table[2 * MAX_REL:]
    neg_d = jnp.concatenate([table, jnp.broadcast_to(last, (MAX_REL - 1, N_HEADS))])
    w = jnp.concatenate([jnp.broadcast_to(last, (QBLOCK, N_HEADS)), neg_d])
    return w.T.reshape(N_HEADS, 1, PROMPT_BIAS_PERIOD).astype(F32)


def _sample_bias_period(table, tt, l):
    assert l >= MAX_REL
    period = -(-(l + 2 * tt - 1) // LANES) * LANES
    last = table[2 * MAX_REL:]
    n_var = tt + MAX_REL - 1
    var = table[2 * MAX_REL - 1:2 * MAX_REL - 1 - n_var:-1]
    w = jnp.concatenate([jnp.broadcast_to(last, (l - MAX_REL + 1, N_HEADS)), var,
                         jnp.broadcast_to(last, (period - (l + tt), N_HEADS))])
    return w.T.reshape(N_HEADS, 1, period).astype(F32)


def kernel(x_prompt, x_sample, p_prompt, p_sample, cache_k, cache_v, state_conv, state_lru, norm_g, w_in, conv_w, conv_b, gate_a_w, gate_a_b, gate_x_w, gate_x_b, lru_lambda, q_norm_g, k_norm_g, rel_bias, w_out, ple_norm_g, w_ple_gate, w_ple_proj):
    depth = w_in.shape[0]
    b, s, _ = x_prompt.shape
    db, ds, _ = x_sample.shape
    lc = cache_k.shape[2]
    yp, ys = x_prompt, x_sample.reshape(db * ds, D_MODEL)
    ones_bd = _block_diag(jnp.full((HEADS_PER_GROUP, HEAD_DIM, HEAD_DIM), 1.0 / HEAD_DIM, F32)).astype(BF16)
    gate_halves = W_LRU // HEAD_GROUP
    blocks_per_half = LRU_BLOCKS // gate_halves
    outs = [[] for _ in range(8)]
    sample_nb = 8
    for l in range(depth):
        win = w_in[l].astype(BF16)
        o = 2 * W_LRU
        wnat = jnp.concatenate([win[:, :o], win[:, o + W_ATT:o + 2 * W_ATT]], axis=1)
        wt = jnp.concatenate([win[:, o:o + W_ATT], win[:, o + 2 * W_ATT:]], axis=1).T
        ng = norm_g[l].reshape(1, D_MODEL)
        cw = conv_w[l]
        cbias = conv_b[l].reshape(1, W_LRU)
        wg = jnp.stack([
            jnp.concatenate([_block_diag(w[j * blocks_per_half:(j + 1) * blocks_per_half])
                             for w in (gate_a_w[l], gate_x_w[l])], axis=1)
            for j in range(gate_halves)]).astype(BF16)
        bg = jnp.concatenate([gate_a_b[l].reshape(gate_halves, 1, HEAD_GROUP),
                              gate_x_b[l].reshape(gate_halves, 1, HEAD_GROUP)], axis=2)
        lam = lru_lambda[l].reshape(1, W_LRU)
        qg_col = q_norm_g[l].reshape(HEAD_DIM, 1)
        qg_t = jnp.tile(q_norm_g[l], N_HEADS).reshape(1, W_ATT)
        kg_t = jnp.tile(k_norm_g[l], N_HEADS).reshape(1, W_ATT)
        wo = w_out[l].astype(BF16)
        pg = ple_norm_g[l].reshape(1, D_MODEL)
        wpg = w_ple_gate[l].astype(BF16)
        wpe = w_ple_proj[l].astype(BF16)

        lru_g, qt, kbf, vt, gat, pk, pv, pc, ph = _front_prompt(
            yp, ng, wnat, wt, cw, cbias, wg, bg, lam, qg_col, kg_t, ones_bd)
        att_g = _attn_prompt(qt, kbf, vt, _prompt_bias_period(rel_bias[l]), gat)
        yp = _back(yp.reshape(b * s, D_MODEL), lru_g.reshape(b * s, W_LRU), att_g.reshape(b * s, W_ATT),
                   p_prompt[l].reshape(b * s, PLE_DIM), wo, pg, wpg, wpe, 1024).reshape(b, s, D_MODEL)
        to_frames = lambda a: jnp.transpose(a.reshape(b, N_HEADS, HEAD_DIM, a.shape[-1]), (0, 3, 1, 2))
        outs[0].append(to_frames(pk))
        outs[1].append(to_frames(pv))
        outs[2].append(pc)
        outs[3].append(ph.reshape(b, W_LRU))

        sconv_pad = jnp.pad(state_conv[l], ((0, 0), (SUBLANES - (CONV_W - 1), 0), (0, 0)))
        lru_s, q_s, k_s, v_s, ga_s, sc, sh = _front_sample(
            ys, ng, win, cw, cbias, wg, bg, lam, qg_t, kg_t, ones_bd,
            sconv_pad, state_lru[l].reshape(db, 1, W_LRU), sample_nb, ds)
        att_s = _attn_sample(q_s.reshape(db, ds, W_ATT), k_s.reshape(db, ds, W_ATT), v_s.reshape(db, ds, W_ATT),
                             jnp.transpose(cache_k[l], (0, 2, 3, 1)), jnp.transpose(cache_v[l], (0, 2, 3, 1)),
                             _sample_bias_period(rel_bias[l], ds, lc), ga_s.reshape(db, ds, W_ATT))
        ys = _back(ys, lru_s, att_s.reshape(db * ds, W_ATT), p_sample[l].reshape(db * ds, PLE_DIM),
                   wo, pg, wpg, wpe, 256)
        outs[4].append(k_s.reshape(db, ds, N_HEADS, HEAD_DIM))
        outs[5].append(v_s.reshape(db, ds, N_HEADS, HEAD_DIM))
        outs[6].append(sc)
        outs[7].append(sh.reshape(db, W_LRU))
    return (yp, ys.reshape(db, ds, D_MODEL)) + tuple(jnp.stack(o) for o in outs)
```

```python
import functools

import jax
import jax.numpy as jnp
from jax import lax
from jax.experimental import pallas as pl
from jax.experimental.pallas import tpu as pltpu

D_MODEL = 1024
CHUNK = 64
PAST_CHUNKS = 8
BAND = PAST_CHUNKS * CHUNK
W_LRU = D_MODEL // 2
LRU_BLOCKS = 8
LRU_BLOCK = W_LRU // LRU_BLOCKS
CONV_W = 4
RG_C = 8.0
HEAD_DIM = 64
W_ATT = D_MODEL // 2
N_HEADS = W_ATT // HEAD_DIM
MAX_REL = 256
PLE_DIM = 256
EPS = 1e-6
NEG = -1e30
LOG2E = 1.4426950408889634

SUBLANES = 8
LANES = 128
HEAD_GROUP = 256
HEADS_PER_GROUP = HEAD_GROUP // HEAD_DIM
N_GROUPS = W_ATT // HEAD_GROUP
QBLOCK = 256
KEY_TILES = BAND // QBLOCK + 1
PROMPT_BIAS_PERIOD = (KEY_TILES + 1) * QBLOCK
FRONT_BLOCK = 1024
SAMPLE_SEQS_PER_STEP = 4
VMEM_LIMIT = 56 * 1024 * 1024

F32 = jnp.float32
BF16 = jnp.bfloat16


def _dot(a, b):
    return jnp.dot(a, b, preferred_element_type=F32)


def _dot_nt(a, b):
    return lax.dot_general(a, b, (((1,), (1,)), ((), ())), preferred_element_type=F32)


def _rms_rows(x, g):
    ms = jnp.mean(x * x, axis=-1, keepdims=True)
    return x * lax.rsqrt(ms + EPS) * g


def _head_norm_rows(x, ones_bd, g_tiled):
    x2 = x * x
    hi = x2.astype(BF16)
    lo = (x2 - hi.astype(F32)).astype(BF16)
    ms = jnp.concatenate(
        [_dot(hi[:, g * HEAD_GROUP:(g + 1) * HEAD_GROUP], ones_bd)
         + _dot(lo[:, g * HEAD_GROUP:(g + 1) * HEAD_GROUP], ones_bd) for g in range(N_GROUPS)], axis=1)
    return x * lax.rsqrt(ms + EPS) * g_tiled


def _scan_rows(a_ref, u_ref, h_ref, row0, nrows, h0, unroll=False):
    ridx = lax.broadcasted_iota(jnp.int32, (SUBLANES, W_LRU), 0)

    def body(i, hprev):
        r = pl.multiple_of(row0 + i * SUBLANES, SUBLANES)
        a = a_ref[pl.ds(r, SUBLANES), :]
        u = u_ref[pl.ds(r, SUBLANES), :]
        for s in (1, 2, 4):
            a_s = jnp.where(ridx >= s, pltpu.roll(a, s, 0), 1.0)
            u_s = jnp.where(ridx >= s, pltpu.roll(u, s, 0), 0.0)
            u = a * u_s + u
            a = a * a_s
        h = a * hprev + u
        h_ref[pl.ds(r, SUBLANES), :] = h
        return h[SUBLANES - 1:SUBLANES, :]

    return lax.fori_loop(0, nrows // SUBLANES, body, h0, unroll=unroll)


def _lru_inputs(xc, wg_ref, bg_ref, lam_ref, a_ref, u_ref):
    xcb = xc.astype(BF16)
    half = wg_ref.shape[1]
    for j in range(wg_ref.shape[0]):
        sl = slice(j * half, (j + 1) * half)
        gates = _dot(xcb[:, sl], wg_ref[j]) + bg_ref[j]
        r = jax.nn.sigmoid(gates[:, :half])
        i = jax.nn.sigmoid(gates[:, half:])
        log_a = -RG_C * r * jax.nn.softplus(-lam_ref[:, sl])
        a = jnp.exp(log_a)
        a_ref[:, sl] = a
        u_ref[:, sl] = jnp.sqrt(jnp.tanh(-log_a) * (1.0 + a * a)) * (i * xc[:, sl])


def _conv_rows(cb_ref, base, nrows, cw_ref, cb_bias):
    out = cb_bias + cw_ref[CONV_W - 1:CONV_W, :] * cb_ref[pl.ds(base + SUBLANES, nrows), :]
    for k in range(CONV_W - 1):
        shift = CONV_W - 1 - k
        out = out + cw_ref[k:k + 1, :] * cb_ref[pl.ds(base + SUBLANES - shift, nrows), :]
    return out


def _front_prompt_kernel(x_ref, ng_ref, wnat_ref, wt_ref, cw_ref, cbias_ref, wg_ref, bg_ref, lam_ref,
                         qg_ref, kg_ref, ones_ref,
                         lru_ref, qt_ref, kbf_ref, vt_ref, gat_ref, pk_ref, pv_ref, pc_ref, ph_ref,
                         zn_scr, zt_scr, cb_scr, a_scr, u_scr, h_scr, hlast_scr, *, keep_subs):
    sub = QBLOCK
    n_sub = x_ref.shape[1] // sub

    @pl.when(pl.program_id(1) == 0)
    def _():
        cb_scr[0:SUBLANES, :] = jnp.zeros((SUBLANES, W_LRU), F32)
        hlast_scr[...] = jnp.zeros((1, W_LRU), F32)

    def project(i):
        xn = _rms_rows(x_ref[0, i * sub:(i + 1) * sub, :], ng_ref[...]).astype(BF16)
        zn_scr[i % 2] = _dot(xn, wnat_ref[...])
        zt_scr[i % 2] = _dot_nt(wt_ref[...], xn)

    def finish(i):
        rows = slice(i * sub, (i + 1) * sub)
        zn = zn_scr.at[i % 2]
        zt = zt_scr.at[i % 2]
        cb_scr[SUBLANES:SUBLANES + sub, :] = zn[:, :W_LRU]
        xc = _conv_rows(cb_scr, 0, sub, cw_ref, cbias_ref[...])
        pc_ref[0] = cb_scr[sub + SUBLANES - (CONV_W - 1):sub + SUBLANES, :]
        cb_scr[0:SUBLANES, :] = cb_scr[sub:sub + SUBLANES, :]
        _lru_inputs(xc, wg_ref, bg_ref, lam_ref, a_scr, u_scr)
        h_last = _scan_rows(a_scr, u_scr, h_scr, 0, sub, hlast_scr[...], unroll=True)
        hlast_scr[...] = h_last
        ph_ref[0] = h_last
        lru_ref[0, rows, :] = (h_scr[...] * jax.nn.silu(zn[:, W_LRU:2 * W_LRU])).astype(BF16)
        k = _head_norm_rows(zn[:, 2 * W_LRU:], ones_ref[...], kg_ref[...])
        kbf_ref[0, rows, :] = k.astype(BF16)
        q3 = zt[0:W_ATT, :].reshape(N_HEADS, HEAD_DIM, sub)
        ms = jnp.mean(q3 * q3, axis=1, keepdims=True)
        qn = q3 * lax.rsqrt(ms + EPS) * (qg_ref[...] * (HEAD_DIM ** -0.5 * LOG2E))
        qt_ref[0, :, rows] = qn.reshape(W_ATT, sub).astype(BF16)
        vt = zt[W_ATT:2 * W_ATT, :]
        vt_ref[0, :, rows] = vt.astype(BF16)
        gat_ref[0, :, rows] = jax.nn.silu(zt[2 * W_ATT:, :])
        if i >= n_sub - keep_subs:
            first = (i - (n_sub - keep_subs)) * sub
            pk_ref[0, :, first:first + sub] = k.T
            pv_ref[0, :, first:first + sub] = vt

    project(0)
    for i in range(n_sub):
        if i + 1 < n_sub:
            project(i + 1)
        finish(i)


def _front_prompt(x, ng, wnat, wt, cw, cbias, wg, bg, lam, qg, kg, ones_bd):
    b, s, _ = x.shape
    tb = min(FRONT_BLOCK, s)
    keep = min(BAND, s)
    assert s % tb == 0 and tb % QBLOCK == 0 and keep % QBLOCK == 0 and keep <= tb
    const = lambda shape: pl.BlockSpec(shape, lambda i, j: (0,) * len(shape))
    rows_spec = lambda w: pl.BlockSpec((1, tb, w), lambda i, j: (i, j, 0))
    cols_spec = pl.BlockSpec((1, W_ATT, tb), lambda i, j: (i, 0, j))
    per_seq = lambda r, w: pl.BlockSpec((1, r, w), lambda i, j: (i, 0, 0))
    kern = functools.partial(_front_prompt_kernel, keep_subs=keep // QBLOCK)
    return pl.pallas_call(
        kern,
        grid=(b, s // tb),
        in_specs=[
            rows_spec(D_MODEL),
            const((1, D_MODEL)),
            const((D_MODEL, 3 * W_LRU)),
            const((3 * W_ATT, D_MODEL)),
            const((CONV_W, W_LRU)),
            const((1, W_LRU)),
            const((W_LRU // HEAD_GROUP, HEAD_GROUP, 2 * HEAD_GROUP)),
            const((W_LRU // HEAD_GROUP, 1, 2 * HEAD_GROUP)),
            const((1, W_LRU)),
            const((HEAD_DIM, 1)),
            const((1, W_ATT)),
            const((HEAD_GROUP, HEAD_GROUP)),
        ],
        out_specs=[
            rows_spec(W_LRU),
            cols_spec,
            rows_spec(W_ATT),
            cols_spec,
            cols_spec,
            per_seq(W_ATT, keep),
            per_seq(W_ATT, keep),
            per_seq(CONV_W - 1, W_LRU),
            per_seq(1, W_LRU),
        ],
        out_shape=[
            jax.ShapeDtypeStruct((b, s, W_LRU), BF16),
            jax.ShapeDtypeStruct((b, W_ATT, s), BF16),
            jax.ShapeDtypeStruct((b, s, W_ATT), BF16),
            jax.ShapeDtypeStruct((b, W_ATT, s), BF16),
            jax.ShapeDtypeStruct((b, W_ATT, s), F32),
            jax.ShapeDtypeStruct((b, W_ATT, keep), F32),
            jax.ShapeDtypeStruct((b, W_ATT, keep), F32),
            jax.ShapeDtypeStruct((b, CONV_W - 1, W_LRU), F32),
            jax.ShapeDtypeStruct((b, 1, W_LRU), F32),
        ],
        scratch_shapes=[
            pltpu.VMEM((2, QBLOCK, 3 * W_LRU), F32),
            pltpu.VMEM((2, 3 * W_ATT, QBLOCK), F32),
            pltpu.VMEM((QBLOCK + SUBLANES, W_LRU), F32),
            pltpu.VMEM((QBLOCK, W_LRU), F32),
            pltpu.VMEM((QBLOCK, W_LRU), F32),
            pltpu.VMEM((QBLOCK, W_LRU), F32),
            pltpu.VMEM((1, W_LRU), F32),
        ],
        compiler_params=pltpu.CompilerParams(
            dimension_semantics=("arbitrary", "arbitrary"), vmem_limit_bytes=VMEM_LIMIT),
        name="front_prompt",
    )(x, ng, wnat, wt, cw, cbias, wg, bg, lam, qg, kg, ones_bd)


def _toeplitz_rows(w_row, nrows, row0):
    x = jnp.broadcast_to(w_row, (nrows, w_row.shape[1]))
    return pltpu.roll(x, row0, 1, stride=1, stride_axis=0)


def _fill_prompt_bias(w_ref, bias_ref):
    q_chunk = (lax.broadcasted_iota(jnp.int32, (CHUNK, QBLOCK), 1) + BAND) // CHUNK
    for h in range(N_HEADS):
        def body(n, carry, h=h):
            r0 = pl.multiple_of(n * CHUNK, CHUNK)
            far = w_ref[h][:, 0:1]
            t = _toeplitz_rows(w_ref[h] - far, CHUNK, r0)[:, :QBLOCK]
            dc = q_chunk - n
            bias_ref[h, pl.ds(r0, CHUNK), :] = jnp.where((dc >= 0) & (dc <= PAST_CHUNKS), t * LOG2E, NEG)
            return carry
        lax.fori_loop(0, KEY_TILES * QBLOCK // CHUNK, body, 0)


SCORE_LEAD = 3
CHUNKS_PER_TILE = QBLOCK // CHUNK
FRAMES_PER_VREG = LANES // CHUNK


def _lane_cols(kc):
    cols = []
    for c in range(QBLOCK // LANES):
        q_lo = PAST_CHUNKS + c * FRAMES_PER_VREG
        q_hi = q_lo + FRAMES_PER_VREG - 1
        if q_lo - PAST_CHUNKS <= kc <= q_hi:
            cols.append(c)
    return cols


def _needs_bias(kc, c):
    q_lo = PAST_CHUNKS + c * FRAMES_PER_VREG
    q_hi = q_lo + FRAMES_PER_VREG - 1
    nearest = (q_lo - kc) * CHUNK - (CHUNK - 1)
    return not (nearest >= MAX_REL and q_hi - kc <= PAST_CHUNKS)


def _fold_rows(x):
    return x.reshape(x.shape[0] // SUBLANES, SUBLANES, x.shape[1])


def _attn_prompt_heads(tiles, key0, qt_ref, k_ref, vt_ref, bias_ref, s_scr, att_scr):
    n_cols = QBLOCK // LANES
    rows = lax.broadcasted_iota(jnp.int32, (HEAD_GROUP, QBLOCK), 0)
    pieces = []
    for i in tiles:
        for cc in range(CHUNKS_PER_TILE):
            kc = i * CHUNKS_PER_TILE + cc
            for c in _lane_cols(kc):
                pieces.append((i, cc, slice(kc * CHUNK, (kc + 1) * CHUNK), slice(c * LANES, (c + 1) * LANES), c,
                               _needs_bias(kc, c)))

    def scores(h):
        g, hl = divmod(h, HEADS_PER_GROUP)
        gsl = slice(g * HEAD_GROUP, (g + 1) * HEAD_GROUP)
        in_head = (rows >= hl * HEAD_DIM) & (rows < (hl + 1) * HEAD_DIM)
        qm = jnp.where(in_head, qt_ref[0, gsl, :], jnp.zeros((), BF16))
        s = {i: _dot(k_ref[0, pl.ds(key0 + (i - tiles[0]) * QBLOCK, QBLOCK), gsl], qm) for i in tiles}
        m_acc = [jnp.full((SUBLANES, LANES), NEG, F32) for _ in range(n_cols)]
        for i, cc, rsl, lsl, c, biased in pieces:
            sp = s[i][cc * CHUNK:(cc + 1) * CHUNK, lsl]
            if biased:
                sp = sp + bias_ref[h, rsl, lsl]
            s_scr[h % (SCORE_LEAD + 1), rsl, lsl] = sp
            m_acc[c] = jnp.maximum(m_acc[c], jnp.max(_fold_rows(sp), axis=0))
        return [jnp.max(a, axis=0, keepdims=True) for a in m_acc]

    def weights(h, m):
        tiles_p = {}
        for i in tiles:
            blocks = []
            for cc in range(CHUNKS_PER_TILE):
                kc = i * CHUNKS_PER_TILE + cc
                rsl = slice(kc * CHUNK, (kc + 1) * CHUNK)
                cols = []
                for c in range(n_cols):
                    if c in _lane_cols(kc):
                        lsl = slice(c * LANES, (c + 1) * LANES)
                        cols.append(jnp.exp2(s_scr[h % (SCORE_LEAD + 1), rsl, lsl] - m[c]).astype(BF16))
                    else:
                        cols.append(jnp.zeros((CHUNK, LANES), BF16))
                blocks.append(jnp.concatenate(cols, axis=1))
            tiles_p[i] = jnp.concatenate(blocks, axis=0)
        return tiles_p

    ones_rows = jnp.ones((2 * SUBLANES, QBLOCK), BF16)

    def values(h, tiles_p):
        hsl = slice(h * HEAD_DIM, (h + 1) * HEAD_DIM)
        o = None
        for i in tiles:
            vt = vt_ref[0, hsl, pl.ds(key0 + (i - tiles[0]) * QBLOCK, QBLOCK)]
            oi = _dot(jnp.concatenate([vt, ones_rows], axis=0), tiles_p[i])
            o = oi if o is None else o + oi
        att_scr[hsl, :] = o[0:HEAD_DIM, :] * (1.0 / o[HEAD_DIM:HEAD_DIM + 1, :])

    m = {h: scores(h) for h in range(min(SCORE_LEAD, N_HEADS))}
    w = {}
    for h in range(N_HEADS):
        if h + SCORE_LEAD < N_HEADS:
            m[h + SCORE_LEAD] = scores(h + SCORE_LEAD)
        w[h] = weights(h, m.pop(h))
        if h >= 1:
            values(h - 1, w.pop(h - 1))
    values(N_HEADS - 1, w.pop(N_HEADS - 1))


def _attn_prompt_kernel(qt_ref, k_ref, vt_ref, w_ref, gat_ref, out_ref, bias_ref, s_scr, att_scr):
    j = pl.program_id(1)

    @pl.when((pl.program_id(0) == 0) & (j == 0))
    def _():
        _fill_prompt_bias(w_ref, bias_ref)

    for first in range(KEY_TILES):
        tiles = tuple(range(first, KEY_TILES))
        if first > 0:
            cond, key0 = (j == KEY_TILES - 1 - first), 0
        else:
            cond, key0 = (j >= KEY_TILES - 1), pl.multiple_of((j - (KEY_TILES - 1)) * QBLOCK, QBLOCK)

        @pl.when(cond)
        def _(tiles=tiles, key0=key0):
            _attn_prompt_heads(tiles, key0, qt_ref, k_ref, vt_ref, bias_ref, s_scr, att_scr)

    out_ref[0] = (att_scr[...] * gat_ref[0]).T.astype(BF16)


def _attn_prompt(qt, kbf, vt, bias_t, gat):
    b, _, s = qt.shape
    n_blocks = s // QBLOCK
    return pl.pallas_call(
        _attn_prompt_kernel,
        grid=(b, n_blocks),
        in_specs=[
            pl.BlockSpec((1, W_ATT, QBLOCK), lambda i, j: (i, 0, j)),
            pl.BlockSpec((1, s, W_ATT), lambda i, j: (i, 0, 0)),
            pl.BlockSpec((1, W_ATT, s), lambda i, j: (i, 0, 0)),
            pl.BlockSpec((N_HEADS, 1, PROMPT_BIAS_PERIOD), lambda i, j: (0, 0, 0)),
            pl.BlockSpec((1, W_ATT, QBLOCK), lambda i, j: (i, 0, j)),
        ],
        out_specs=pl.BlockSpec((1, QBLOCK, W_ATT), lambda i, j: (i, j, 0)),
        out_shape=jax.ShapeDtypeStruct((b, s, W_ATT), BF16),
        scratch_shapes=[
            pltpu.VMEM((N_HEADS, KEY_TILES * QBLOCK, QBLOCK), F32),
            pltpu.VMEM((SCORE_LEAD + 1, KEY_TILES * QBLOCK, QBLOCK), F32),
            pltpu.VMEM((W_ATT, QBLOCK), F32),
        ],
        compiler_params=pltpu.CompilerParams(
            dimension_semantics=("arbitrary", "arbitrary"), vmem_limit_bytes=VMEM_LIMIT),
        name="attn_prompt",
    )(qt, kbf, vt, bias_t, gat)


def _back_kernel(x_ref, lru_ref, att_ref, p_ref, wo_ref, pg_ref, wpg_ref, wpe_ref, y_ref):
    mix = _dot(lru_ref[...], wo_ref[0:W_LRU, :]) + _dot(att_ref[...], wo_ref[W_LRU:, :])
    h = x_ref[...] + mix
    gate = jax.nn.sigmoid(_dot(_rms_rows(h, pg_ref[...]).astype(BF16), wpg_ref[...]))
    y_ref[...] = h + _dot(p_ref[...].astype(BF16), wpe_ref[...]) * gate


def _back(x2, lru2, att2, p2, wo, pg, wpg, wpe, rows):
    n = x2.shape[0]
    const = lambda shape: pl.BlockSpec(shape, lambda i: (0,) * len(shape))
    return pl.pallas_call(
        _back_kernel,
        grid=(n // rows,),
        in_specs=[
            pl.BlockSpec((rows, D_MODEL), lambda i: (i, 0)),
            pl.BlockSpec((rows, W_LRU), lambda i: (i, 0)),
            pl.BlockSpec((rows, W_ATT), lambda i: (i, 0)),
            pl.BlockSpec((rows, PLE_DIM), lambda i: (i, 0)),
            const((W_LRU + W_ATT, D_MODEL)),
            const((1, D_MODEL)),
            const((D_MODEL, D_MODEL)),
            const((PLE_DIM, D_MODEL)),
        ],
        out_specs=pl.BlockSpec((rows, D_MODEL), lambda i: (i, 0)),
        out_shape=jax.ShapeDtypeStruct((n, D_MODEL), F32),
        compiler_params=pltpu.CompilerParams(
            dimension_semantics=("arbitrary",), vmem_limit_bytes=VMEM_LIMIT),
        name="back",
    )(x2, lru2, att2, p2, wo, pg, wpg, wpe)


def _front_sample_kernel(x_ref, ng_ref, win_ref, cw_ref, cbias_ref, wg_ref, bg_ref, lam_ref,
                         qg_ref, kg_ref, ones_ref, sconv_ref, slru_ref,
                         lru_ref, q_ref, k_ref, v_ref, ga_ref, sc_ref, sh_ref,
                         cb_scr, xc_scr, a_scr, u_scr, h_scr, *, nb, tt):
    seg = tt + SUBLANES
    xn = _rms_rows(x_ref[...], ng_ref[...]).astype(BF16)
    z = _dot(xn, win_ref[...])
    xl = z[:, :W_LRU]
    for s in range(nb):
        cb_scr[s * seg:s * seg + SUBLANES, :] = sconv_ref[s]
        cb_scr[s * seg + SUBLANES:(s + 1) * seg, :] = xl[s * tt:(s + 1) * tt, :]
        xc_scr[s * tt:(s + 1) * tt, :] = _conv_rows(cb_scr, s * seg, tt, cw_ref, cbias_ref[...])
        sc_ref[s] = cb_scr[(s + 1) * seg - (CONV_W - 1):(s + 1) * seg, :]
    _lru_inputs(xc_scr[...], wg_ref, bg_ref, lam_ref, a_scr, u_scr)
    for s in range(nb):
        sh_ref[s] = _scan_rows(a_scr, u_scr, h_scr, s * tt, tt, slru_ref[s])
    lru_ref[...] = (h_scr[...] * jax.nn.silu(z[:, W_LRU:2 * W_LRU])).astype(BF16)

    o = 2 * W_LRU
    q = _head_norm_rows(z[:, o:o + W_ATT], ones_ref[...], qg_ref[...])
    q_ref[...] = (q * (HEAD_DIM ** -0.5)).astype(BF16)
    k_ref[...] = _head_norm_rows(z[:, o + W_ATT:o + 2 * W_ATT], ones_ref[...], kg_ref[...])
    v_ref[...] = z[:, o + 2 * W_ATT:o + 3 * W_ATT]
    ga_ref[...] = jax.nn.silu(z[:, o + 3 * W_ATT:])


def _front_sample(x2, ng, win, cw, cbias, wg, bg, lam, qg_t, kg_t, ones_bd, sconv_pad, slru, nb, tt):
    n = x2.shape[0]
    rows = nb * tt
    const = lambda shape: pl.BlockSpec(shape, lambda i: (0,) * len(shape))
    row_spec = lambda w: pl.BlockSpec((rows, w), lambda i: (i, 0))
    nseq = n // tt
    kern = functools.partial(_front_sample_kernel, nb=nb, tt=tt)
    return pl.pallas_call(
        kern,
        grid=(n // rows,),
        in_specs=[
            row_spec(D_MODEL),
            const((1, D_MODEL)),
            const((D_MODEL, 2 * W_LRU + 4 * W_ATT)),
            const((CONV_W, W_LRU)),
            const((1, W_LRU)),
            const((W_LRU // HEAD_GROUP, HEAD_GROUP, 2 * HEAD_GROUP)),
            const((W_LRU // HEAD_GROUP, 1, 2 * HEAD_GROUP)),
            const((1, W_LRU)),
            const((1, W_ATT)),
            const((1, W_ATT)),
            const((HEAD_GROUP, HEAD_GROUP)),
            pl.BlockSpec((nb, SUBLANES, W_LRU), lambda i: (i, 0, 0)),
            pl.BlockSpec((nb, 1, W_LRU), lambda i: (i, 0, 0)),
        ],
        out_specs=[
            row_spec(W_LRU), row_spec(W_ATT), row_spec(W_ATT), row_spec(W_ATT), row_spec(W_ATT),
            pl.BlockSpec((nb, CONV_W - 1, W_LRU), lambda i: (i, 0, 0)),
            pl.BlockSpec((nb, 1, W_LRU), lambda i: (i, 0, 0)),
        ],
        out_shape=[
            jax.ShapeDtypeStruct((n, W_LRU), BF16),
            jax.ShapeDtypeStruct((n, W_ATT), BF16),
            jax.ShapeDtypeStruct((n, W_ATT), F32),
            jax.ShapeDtypeStruct((n, W_ATT), F32),
            jax.ShapeDtypeStruct((n, W_ATT), F32),
            jax.ShapeDtypeStruct((nseq, CONV_W - 1, W_LRU), F32),
            jax.ShapeDtypeStruct((nseq, 1, W_LRU), F32),
        ],
        scratch_shapes=[
            pltpu.VMEM((nb * (tt + SUBLANES), W_LRU), F32),
            pltpu.VMEM((rows, W_LRU), F32),
            pltpu.VMEM((rows, W_LRU), F32),
            pltpu.VMEM((rows, W_LRU), F32),
            pltpu.VMEM((rows, W_LRU), F32),
        ],
        compiler_params=pltpu.CompilerParams(
            dimension_semantics=("arbitrary",), vmem_limit_bytes=VMEM_LIMIT),
        name="front_sample",
    )(x2, ng, win, cw, cbias, wg, bg, lam, qg_t, kg_t, ones_bd, sconv_pad, slru)


def _attn_sample_kernel(q_ref, kn_ref, vn_ref, kc_ref, vc_ref, w_ref, ga_ref, out_ref, bias_ref):
    tt = q_ref.shape[1]
    lc = kc_ref.shape[-1]

    @pl.when(pl.program_id(0) == 0)
    def _():
        for h in range(N_HEADS):
            g, hl = divmod(h, HEADS_PER_GROUP)
            bias_ref[g, hl * tt:(hl + 1) * tt, :] = _toeplitz_rows(w_ref[h], tt, 0)

    lanes = lax.broadcasted_iota(jnp.int32, (tt, HEAD_GROUP), 1)
    masks = [(lanes >= hl * HEAD_DIM) & (lanes < (hl + 1) * HEAD_DIM) for hl in range(HEADS_PER_GROUP)]
    units = [(s, g) for s in range(q_ref.shape[0]) for g in range(N_GROUPS)]

    def cached(ref, s, g):
        heads = ref[s, g * HEADS_PER_GROUP:(g + 1) * HEADS_PER_GROUP]
        return heads.reshape(HEAD_GROUP, lc).astype(BF16)

    def scores(s, g):
        gsl = slice(g * HEAD_GROUP, (g + 1) * HEAD_GROUP)
        qg = q_ref[s, :, gsl]
        qs = jnp.concatenate([jnp.where(m, qg, jnp.zeros((), BF16)) for m in masks], axis=0)
        sc = _dot(qs, cached(kc_ref, s, g)) + bias_ref[g, :, 0:lc]
        sn = _dot_nt(qs, kn_ref[s, :, gsl].astype(BF16)) + bias_ref[g, :, lc:lc + tt]
        return sc, sn

    def finish(s, g, sc, sn):
        gsl = slice(g * HEAD_GROUP, (g + 1) * HEAD_GROUP)
        m = jnp.maximum(jnp.max(sc, axis=-1, keepdims=True), jnp.max(sn, axis=-1, keepdims=True))
        pc = jnp.exp(sc - m)
        pn = jnp.exp(sn - m)
        l = jnp.sum(pc, axis=-1, keepdims=True) + jnp.sum(pn, axis=-1, keepdims=True)
        o = _dot_nt(pc.astype(BF16), cached(vc_ref, s, g))
        o = (o + _dot(pn.astype(BF16), vn_ref[s, :, gsl].astype(BF16))) * (1.0 / l)
        att = jnp.zeros((tt, HEAD_GROUP), F32)
        for hl in range(HEADS_PER_GROUP):
            att = att + jnp.where(masks[hl], o[hl * tt:(hl + 1) * tt, :], 0.0)
        out_ref[s, :, gsl] = (att * ga_ref[s, :, gsl]).astype(BF16)

    nxt = scores(*units[0])
    for n, (s, g) in enumerate(units):
        cur = nxt
        if n + 1 < len(units):
            nxt = scores(*units[n + 1])
        finish(s, g, *cur)


def _attn_sample(q3, k3, v3, kc_t, vc_t, w_bias, ga3):
    b, tt, _ = q3.shape
    l = kc_t.shape[-1]
    period = w_bias.shape[-1]
    ns = SAMPLE_SEQS_PER_STEP
    assert b % ns == 0
    seq = lambda r: pl.BlockSpec((ns, r, W_ATT), lambda i: (i, 0, 0))
    cache = pl.BlockSpec((ns, N_HEADS, HEAD_DIM, l), lambda i: (i, 0, 0, 0))
    return pl.pallas_call(
        _attn_sample_kernel,
        grid=(b // ns,),
        in_specs=[
            seq(tt), seq(tt), seq(tt), cache, cache,
            pl.BlockSpec((N_HEADS, 1, period), lambda i: (0, 0, 0)),
            seq(tt),
        ],
        out_specs=seq(tt),
        out_shape=jax.ShapeDtypeStruct((b, tt, W_ATT), BF16),
        scratch_shapes=[pltpu.VMEM((N_GROUPS, HEADS_PER_GROUP * tt, period), F32)],
        compiler_params=pltpu.CompilerParams(
            dimension_semantics=("arbitrary",), vmem_limit_bytes=VMEM_LIMIT),
        name="attn_sample",
    )(q3, k3, v3, kc_t, vc_t, w_bias, ga3)


def _block_diag(w):
    n, d, e = w.shape
    eye = jnp.eye(n, dtype=w.dtype)
    return (eye[:, None, :, None] * w[:, :, None, :]).reshape(n * d, n * e)


def _prompt_bias_period(table):
    assert BAND - MAX_REL == MAX_REL and PROMPT_BIAS_PERIOD == KEY_TILES * QBLOCK + QBLOCK
    last = table[2 * MAX_REL:]
    neg_d = jnp.concatenate([table, jnp.broadcast_to(last, (MAX_REL - 1, N_HEADS))])
    w = jnp.concatenate([jnp.broadcast_to(last, (QBLOCK, N_HEADS)), neg_d])
    return w.T.reshape(N_HEADS, 1, PROMPT_BIAS_PERIOD).astype(F32)


def _sample_bias_period(table, tt, l):
    assert l >= MAX_REL
    period = -(-(l + 2 * tt - 1) // LANES) * LANES
    last = table[2 * MAX_REL:]
    n_var = tt + MAX_REL - 1
    var = table[2 * MAX_REL - 1:2 * MAX_REL - 1 - n_var:-1]
    w = jnp.concatenate([jnp.broadcast_to(last, (l - MAX_REL + 1, N_HEADS)), var,
                         jnp.broadcast_to(last, (period - (l + tt), N_HEADS))])
    return w.T.reshape(N_HEADS, 1, period).astype(F32)


def kernel(x_prompt, x_sample, p_prompt, p_sample, cache_k, cache_v, state_conv, state_lru, norm_g, w_in, conv_w, conv_b, gate_a_w, gate_a_b, gate_x_w, gate_x_b, lru_lambda, q_norm_g, k_norm_g, rel_bias, w_out, ple_norm_g, w_ple_gate, w_ple_proj):
    depth = w_in.shape[0]
    b, s, _ = x_prompt.shape
    db, ds, _ = x_sample.shape
    lc = cache_k.shape[2]
    yp, ys = x_prompt, x_sample.reshape(db * ds, D_MODEL)
    ones_bd = _block_diag(jnp.full((HEADS_PER_GROUP, HEAD_DIM, HEAD_DIM), 1.0 / HEAD_DIM, F32)).astype(BF16)
    gate_halves = W_LRU // HEAD_GROUP
    blocks_per_half = LRU_BLOCKS // gate_halves
    outs = [[] for _ in range(8)]
    sample_nb = 8
    for l in range(depth):
        win = w_in[l].astype(BF16)
        o = 2 * W_LRU
        wnat = jnp.concatenate([win[:, :o], win[:, o + W_ATT:o + 2 * W_ATT]], axis=1)
        wt = jnp.concatenate([win[:, o:o + W_ATT], win[:, o + 2 * W_ATT:]], axis=1).T
        ng = norm_g[l].reshape(1, D_MODEL)
        cw = conv_w[l]
        cbias = conv_b[l].reshape(1, W_LRU)
        wg = jnp.stack([
            jnp.concatenate([_block_diag(w[j * blocks_per_half:(j + 1) * blocks_per_half])
                             for w in (gate_a_w[l], gate_x_w[l])], axis=1)
            for j in range(gate_halves)]).astype(BF16)
        bg = jnp.concatenate([gate_a_b[l].reshape(gate_halves, 1, HEAD_GROUP),
                              gate_x_b[l].reshape(gate_halves, 1, HEAD_GROUP)], axis=2)
        lam = lru_lambda[l].reshape(1, W_LRU)
        qg_col = q_norm_g[l].reshape(HEAD_DIM, 1)
        qg_t = jnp.tile(q_norm_g[l], N_HEADS).reshape(1, W_ATT)
        kg_t = jnp.tile(k_norm_g[l], N_HEADS).reshape(1, W_ATT)
        wo = w_out[l].astype(BF16)
        pg = ple_norm_g[l].reshape(1, D_MODEL)
        wpg = w_ple_gate[l].astype(BF16)
        wpe = w_ple_proj[l].astype(BF16)

        lru_g, qt, kbf, vt, gat, pk, pv, pc, ph = _front_prompt(
            yp, ng, wnat, wt, cw, cbias, wg, bg, lam, qg_col, kg_t, ones_bd)
        att_g = _attn_prompt(qt, kbf, vt, _prompt_bias_period(rel_bias[l]), gat)
        yp = _back(yp.reshape(b * s, D_MODEL), lru_g.reshape(b * s, W_LRU), att_g.reshape(b * s, W_ATT),
                   p_prompt[l].reshape(b * s, PLE_DIM), wo, pg, wpg, wpe, 1024).reshape(b, s, D_MODEL)
        to_frames = lambda a: jnp.transpose(a.reshape(b, N_HEADS, HEAD_DIM, a.shape[-1]), (0, 3, 1, 2))
        outs[0].append(to_frames(pk))
        outs[1].append(to_frames(pv))
        outs[2].append(pc)
        outs[3].append(ph.reshape(b, W_LRU))

        sconv_pad = jnp.pad(state_conv[l], ((0, 0), (SUBLANES - (CONV_W - 1), 0), (0, 0)))
        lru_s, q_s, k_s, v_s, ga_s, sc, sh = _front_sample(
            ys, ng, win, cw, cbias, wg, bg, lam, qg_t, kg_t, ones_bd,
            sconv_pad, state_lru[l].reshape(db, 1, W_LRU), sample_nb, ds)
        att_s = _attn_sample(q_s.reshape(db, ds, W_ATT), k_s.reshape(db, ds, W_ATT), v_s.reshape(db, ds, W_ATT),
                             jnp.transpose(cache_k[l], (0, 2, 3, 1)), jnp.transpose(cache_v[l], (0, 2, 3, 1)),
                             _sample_bias_period(rel_bias[l], ds, lc), ga_s.reshape(db, ds, W_ATT))
        ys = _back(ys, lru_s, att_s.reshape(db * ds, W_ATT), p_sample[l].reshape(db * ds, PLE_DIM),
                   wo, pg, wpg, wpe, 256)
        outs[4].append(k_s.reshape(db, ds, N_HEADS, HEAD_DIM))
        outs[5].append(v_s.reshape(db, ds, N_HEADS, HEAD_DIM))
        outs[6].append(sc)
        outs[7].append(sh.reshape(db, W_LRU))
    return (yp, ys.reshape(db, ds, D_MODEL)) + tuple(jnp.stack(o) for o in outs)
```

```python
import functools

import jax
import jax.numpy as jnp
from jax import lax
from jax.experimental import pallas as pl
from jax.experimental.pallas import tpu as pltpu

D_MODEL = 1024
CHUNK = 64
PAST_CHUNKS = 8
BAND = PAST_CHUNKS * CHUNK
W_LRU = D_MODEL // 2
LRU_BLOCKS = 8
LRU_BLOCK = W_LRU // LRU_BLOCKS
CONV_W = 4
RG_C = 8.0
HEAD_DIM = 64
W_ATT = D_MODEL // 2
N_HEADS = W_ATT // HEAD_DIM
MAX_REL = 256
PLE_DIM = 256
EPS = 1e-6
NEG = -1e30
LOG2E = 1.4426950408889634

SUBLANES = 8
LANES = 128
HEAD_GROUP = 256
HEADS_PER_GROUP = HEAD_GROUP // HEAD_DIM
N_GROUPS = W_ATT // HEAD_GROUP
QBLOCK = 256
KEY_TILES = BAND // QBLOCK + 1
PROMPT_BIAS_PERIOD = (KEY_TILES + 1) * QBLOCK
FRONT_BLOCK = 1024
SAMPLE_SEQS_PER_STEP = 4
VMEM_LIMIT = 56 * 1024 * 1024

F32 = jnp.float32
BF16 = jnp.bfloat16


def _dot(a, b):
    return jnp.dot(a, b, preferred_element_type=F32)


def _dot_nt(a, b):
    return lax.dot_general(a, b, (((1,), (1,)), ((), ())), preferred_element_type=F32)


def _rms_rows(x, g):
    ms = jnp.mean(x * x, axis=-1, keepdims=True)
    return x * lax.rsqrt(ms + EPS) * g


def _head_norm_rows(x, ones_bd, g_tiled):
    x2 = x * x
    hi = x2.astype(BF16)
    lo = (x2 - hi.astype(F32)).astype(BF16)
    ms = jnp.concatenate(
        [_dot(hi[:, g * HEAD_GROUP:(g + 1) * HEAD_GROUP], ones_bd)
         + _dot(lo[:, g * HEAD_GROUP:(g + 1) * HEAD_GROUP], ones_bd) for g in range(N_GROUPS)], axis=1)
    return x * lax.rsqrt(ms + EPS) * g_tiled


def _scan_rows(a_ref, u_ref, h_ref, row0, nrows, h0, unroll=False):
    ridx = lax.broadcasted_iota(jnp.int32, (SUBLANES, W_LRU), 0)

    def body(i, hprev):
        r = pl.multiple_of(row0 + i * SUBLANES, SUBLANES)
        a = a_ref[pl.ds(r, SUBLANES), :]
        u = u_ref[pl.ds(r, SUBLANES), :]
        for s in (1, 2, 4):
            a_s = jnp.where(ridx >= s, pltpu.roll(a, s, 0), 1.0)
            u_s = jnp.where(ridx >= s, pltpu.roll(u, s, 0), 0.0)
            u = a * u_s + u
            a = a * a_s
        h = a * hprev + u
        h_ref[pl.ds(r, SUBLANES), :] = h
        return h[SUBLANES - 1:SUBLANES, :]

    return lax.fori_loop(0, nrows // SUBLANES, body, h0, unroll=unroll)


def _lru_inputs(xc, wg_ref, bg_ref, lam_ref, a_ref, u_ref):
    xcb = xc.astype(BF16)
    half = wg_ref.shape[1]
    for j in range(wg_ref.shape[0]):
        sl = slice(j * half, (j + 1) * half)
        gates = _dot(xcb[:, sl], wg_ref[j]) + bg_ref[j]
        r = jax.nn.sigmoid(gates[:, :half])
        i = jax.nn.sigmoid(gates[:, half:])
        log_a = -RG_C * r * jax.nn.softplus(-lam_ref[:, sl])
        a = jnp.exp(log_a)
        a_ref[:, sl] = a
        u_ref[:, sl] = jnp.sqrt(jnp.tanh(-log_a) * (1.0 + a * a)) * (i * xc[:, sl])


def _conv_rows(cb_ref, base, nrows, cw_ref, cb_bias):
    out = cb_bias + cw_ref[CONV_W - 1:CONV_W, :] * cb_ref[pl.ds(base + SUBLANES, nrows), :]
    for k in range(CONV_W - 1):
        shift = CONV_W - 1 - k
        out = out + cw_ref[k:k + 1, :] * cb_ref[pl.ds(base + SUBLANES - shift, nrows), :]
    return out


def _front_prompt_kernel(x_ref, ng_ref, wnat_ref, wt_ref, cw_ref, cbias_ref, wg_ref, bg_ref, lam_ref,
                         qg_ref, kg_ref, ones_ref,
                         lru_ref, qt_ref, kbf_ref, vt_ref, gat_ref, pk_ref, pv_ref, pc_ref, ph_ref,
                         zn_scr, zt_scr, cb_scr, a_scr, u_scr, h_scr, hlast_scr, *, keep_subs):
    sub = QBLOCK
    n_sub = x_ref.shape[1] // sub

    @pl.when(pl.program_id(1) == 0)
    def _():
        cb_scr[0:SUBLANES, :] = jnp.zeros((SUBLANES, W_LRU), F32)
        hlast_scr[...] = jnp.zeros((1, W_LRU), F32)

    def project(i):
        xn = _rms_rows(x_ref[0, i * sub:(i + 1) * sub, :], ng_ref[...]).astype(BF16)
        zn_scr[i % 2] = _dot(xn, wnat_ref[...])
        zt_scr[i % 2] = _dot_nt(wt_ref[...], xn)

    def finish(i):
        rows = slice(i * sub, (i + 1) * sub)
        zn = zn_scr.at[i % 2]
        zt = zt_scr.at[i % 2]
        cb_scr[SUBLANES:SUBLANES + sub, :] = zn[:, :W_LRU]
        xc = _conv_rows(cb_scr, 0, sub, cw_ref, cbias_ref[...])
        pc_ref[0] = cb_scr[sub + SUBLANES - (CONV_W - 1):sub + SUBLANES, :]
        cb_scr[0:SUBLANES, :] = cb_scr[sub:sub + SUBLANES, :]
        _lru_inputs(xc, wg_ref, bg_ref, lam_ref, a_scr, u_scr)
        h_last = _scan_rows(a_scr, u_scr, h_scr, 0, sub, hlast_scr[...], unroll=True)
        hlast_scr[...] = h_last
        ph_ref[0] = h_last
        lru_ref[0, rows, :] = (h_scr[...] * jax.nn.silu(zn[:, W_LRU:2 * W_LRU])).astype(BF16)
        k = _head_norm_rows(zn[:, 2 * W_LRU:], ones_ref[...], kg_ref[...])
        kbf_ref[0, rows, :] = k.astype(BF16)
        q3 = zt[0:W_ATT, :].reshape(N_HEADS, HEAD_DIM, sub)
        ms = jnp.mean(q3 * q3, axis=1, keepdims=True)
        qn = q3 * lax.rsqrt(ms + EPS) * (qg_ref[...] * (HEAD_DIM ** -0.5 * LOG2E))
        qt_ref[0, :, rows] = qn.reshape(W_ATT, sub).astype(BF16)
        vt = zt[W_ATT:2 * W_ATT, :]
        vt_ref[0, :, rows] = vt.astype(BF16)
        gat_ref[0, :, rows] = jax.nn.silu(zt[2 * W_ATT:, :])
        if i >= n_sub - keep_subs:
            first = (i - (n_sub - keep_subs)) * sub
            pk_ref[0, :, first:first + sub] = k.T
            pv_ref[0, :, first:first + sub] = vt

    project(0)
    for i in range(n_sub):
        if i + 1 < n_sub:
            project(i + 1)
        finish(i)


def _front_prompt(x, ng, wnat, wt, cw, cbias, wg, bg, lam, qg, kg, ones_bd):
    b, s, _ = x.shape
    tb = min(FRONT_BLOCK, s)
    keep = min(BAND, s)
    assert s % tb == 0 and tb % QBLOCK == 0 and keep % QBLOCK == 0 and keep <= tb
    const = lambda shape: pl.BlockSpec(shape, lambda i, j: (0,) * len(shape))
    rows_spec = lambda w: pl.BlockSpec((1, tb, w), lambda i, j: (i, j, 0))
    cols_spec = pl.BlockSpec((1, W_ATT, tb), lambda i, j: (i, 0, j))
    per_seq = lambda r, w: pl.BlockSpec((1, r, w), lambda i, j: (i, 0, 0))
    kern = functools.partial(_front_prompt_kernel, keep_subs=keep // QBLOCK)
    return pl.pallas_call(
        kern,
        grid=(b, s // tb),
        in_specs=[
            rows_spec(D_MODEL),
            const((1, D_MODEL)),
            const((D_MODEL, 3 * W_LRU)),
            const((3 * W_ATT, D_MODEL)),
            const((CONV_W, W_LRU)),
            const((1, W_LRU)),
            const((W_LRU // HEAD_GROUP, HEAD_GROUP, 2 * HEAD_GROUP)),
            const((W_LRU // HEAD_GROUP, 1, 2 * HEAD_GROUP)),
            const((1, W_LRU)),
            const((HEAD_DIM, 1)),
            const((1, W_ATT)),
            const((HEAD_GROUP, HEAD_GROUP)),
        ],
        out_specs=[
            rows_spec(W_LRU),
            cols_spec,
            rows_spec(W_ATT),
            cols_spec,
            cols_spec,
            per_seq(W_ATT, keep),
            per_seq(W_ATT, keep),
            per_seq(CONV_W - 1, W_LRU),
            per_seq(1, W_LRU),
        ],
        out_shape=[
            jax.ShapeDtypeStruct((b, s, W_LRU), BF16),
            jax.ShapeDtypeStruct((b, W_ATT, s), BF16),
            jax.ShapeDtypeStruct((b, s, W_ATT), BF16),
            jax.ShapeDtypeStruct((b, W_ATT, s), BF16),
            jax.ShapeDtypeStruct((b, W_ATT, s), F32),
            jax.ShapeDtypeStruct((b, W_ATT, keep), F32),
            jax.ShapeDtypeStruct((b, W_ATT, keep), F32),
            jax.ShapeDtypeStruct((b, CONV_W - 1, W_LRU), F32),
            jax.ShapeDtypeStruct((b, 1, W_LRU), F32),
        ],
        scratch_shapes=[
            pltpu.VMEM((2, QBLOCK, 3 * W_LRU), F32),
            pltpu.VMEM((2, 3 * W_ATT, QBLOCK), F32),
            pltpu.VMEM((QBLOCK + SUBLANES, W_LRU), F32),
            pltpu.VMEM((QBLOCK, W_LRU), F32),
            pltpu.VMEM((QBLOCK, W_LRU), F32),
            pltpu.VMEM((QBLOCK, W_LRU), F32),
            pltpu.VMEM((1, W_LRU), F32),
        ],
        compiler_params=pltpu.CompilerParams(
            dimension_semantics=("arbitrary", "arbitrary"), vmem_limit_bytes=VMEM_LIMIT),
        name="front_prompt",
    )(x, ng, wnat, wt, cw, cbias, wg, bg, lam, qg, kg, ones_bd)


def _toeplitz_rows(w_row, nrows, row0):
    x = jnp.broadcast_to(w_row, (nrows, w_row.shape[1]))
    return pltpu.roll(x, row0, 1, stride=1, stride_axis=0)


def _fill_prompt_bias(w_ref, bias_ref):
    q_chunk = (lax.broadcasted_iota(jnp.int32, (CHUNK, QBLOCK), 1) + BAND) // CHUNK
    for h in range(N_HEADS):
        def body(n, carry, h=h):
            r0 = pl.multiple_of(n * CHUNK, CHUNK)
            t = _toeplitz_rows(w_ref[h], CHUNK, r0)[:, :QBLOCK]
            dc = q_chunk - n
            bias_ref[h, pl.ds(r0, CHUNK), :] = jnp.where((dc >= 0) & (dc <= PAST_CHUNKS), t * LOG2E, NEG)
            return carry
        lax.fori_loop(0, KEY_TILES * QBLOCK // CHUNK, body, 0)


SCORE_LEAD = 3
ATTN_BLOCKS = 2
CHUNKS_PER_TILE = QBLOCK // CHUNK
FRAMES_PER_VREG = LANES // CHUNK


def _lane_cols(kc):
    cols = []
    for c in range(QBLOCK // LANES):
        q_lo = PAST_CHUNKS + c * FRAMES_PER_VREG
        q_hi = q_lo + FRAMES_PER_VREG - 1
        if q_lo - PAST_CHUNKS <= kc <= q_hi:
            cols.append(c)
    return cols


def _fold_rows(x):
    return x.reshape(x.shape[0] // SUBLANES, SUBLANES, x.shape[1])


def _attn_prompt_blocks(blocks, qt_ref, k_ref, vt_ref, bias_ref, s_scr, att_scr):
    n_cols = QBLOCK // LANES
    rows = lax.broadcasted_iota(jnp.int32, (HEAD_GROUP, QBLOCK), 0)
    units = [(n, h) for n in range(len(blocks)) for h in range(N_HEADS)]

    def pieces(tiles):
        return [(i, cc, slice((i * CHUNKS_PER_TILE + cc) * CHUNK, (i * CHUNKS_PER_TILE + cc + 1) * CHUNK),
                 slice(c * LANES, (c + 1) * LANES), c)
                for i in tiles for cc in range(CHUNKS_PER_TILE) for c in _lane_cols(i * CHUNKS_PER_TILE + cc)]

    def key_rows(n, i):
        tiles, key0 = blocks[n]
        return pl.ds(key0 + (i - tiles[0]) * QBLOCK, QBLOCK)

    def scores(u):
        n, h = units[u]
        g, hl = divmod(h, HEADS_PER_GROUP)
        gsl = slice(g * HEAD_GROUP, (g + 1) * HEAD_GROUP)
        in_head = (rows >= hl * HEAD_DIM) & (rows < (hl + 1) * HEAD_DIM)
        qm = jnp.where(in_head, qt_ref[0, gsl, n * QBLOCK:(n + 1) * QBLOCK], jnp.zeros((), BF16))
        s = {i: _dot(k_ref[0, key_rows(n, i), gsl], qm) for i in blocks[n][0]}
        m_acc = [jnp.full((SUBLANES, LANES), NEG, F32) for _ in range(n_cols)]
        for i, cc, rsl, lsl, c in pieces(blocks[n][0]):
            sp = s[i][cc * CHUNK:(cc + 1) * CHUNK, lsl] + bias_ref[h, rsl, lsl]
            s_scr[u % (SCORE_LEAD + 1), rsl, lsl] = sp
            m_acc[c] = jnp.maximum(m_acc[c], jnp.max(_fold_rows(sp), axis=0))
        return [jnp.max(a, axis=0, keepdims=True) for a in m_acc]

    def weights(u, m):
        tiles_p = {}
        for i in blocks[units[u][0]][0]:
            chunks = []
            for cc in range(CHUNKS_PER_TILE):
                kc = i * CHUNKS_PER_TILE + cc
                rsl = slice(kc * CHUNK, (kc + 1) * CHUNK)
                cols = []
                for c in range(n_cols):
                    if c in _lane_cols(kc):
                        lsl = slice(c * LANES, (c + 1) * LANES)
                        cols.append(jnp.exp2(s_scr[u % (SCORE_LEAD + 1), rsl, lsl] - m[c]).astype(BF16))
                    else:
                        cols.append(jnp.zeros((CHUNK, LANES), BF16))
                chunks.append(jnp.concatenate(cols, axis=1))
            tiles_p[i] = jnp.concatenate(chunks, axis=0)
        return tiles_p

    ones_rows = jnp.ones((2 * SUBLANES, QBLOCK), BF16)

    def values(u, tiles_p):
        n, h = units[u]
        hsl = slice(h * HEAD_DIM, (h + 1) * HEAD_DIM)
        o = None
        for i in blocks[n][0]:
            oi = _dot(jnp.concatenate([vt_ref[0, hsl, key_rows(n, i)], ones_rows], axis=0), tiles_p[i])
            o = oi if o is None else o + oi
        att_scr[n, hsl, :] = o[0:HEAD_DIM, :] * (1.0 / o[HEAD_DIM:HEAD_DIM + 1, :])

    m = {u: scores(u) for u in range(min(SCORE_LEAD, len(units)))}
    w = {}
    for u in range(len(units)):
        if u + SCORE_LEAD < len(units):
            m[u + SCORE_LEAD] = scores(u + SCORE_LEAD)
        w[u] = weights(u, m.pop(u))
        if u >= 1:
            values(u - 1, w.pop(u - 1))
    values(len(units) - 1, w.pop(len(units) - 1))


def _attn_prompt_kernel(qt_ref, k_ref, vt_ref, w_ref, gat_ref, out_ref, bias_ref, s_scr, att_scr):
    step = pl.program_id(1)

    @pl.when((pl.program_id(0) == 0) & (step == 0))
    def _():
        _fill_prompt_bias(w_ref, bias_ref)

    assert ATTN_BLOCKS >= KEY_TILES - 1
    full = tuple(range(KEY_TILES))

    @pl.when(step == 0)
    def _():
        blocks = [(full[max(KEY_TILES - 1 - n, 0):], max(n - (KEY_TILES - 1), 0) * QBLOCK)
                  for n in range(ATTN_BLOCKS)]
        _attn_prompt_blocks(blocks, qt_ref, k_ref, vt_ref, bias_ref, s_scr, att_scr)

    @pl.when(step > 0)
    def _():
        first = step * ATTN_BLOCKS - (KEY_TILES - 1)
        blocks = [(full, pl.multiple_of((first + n) * QBLOCK, QBLOCK)) for n in range(ATTN_BLOCKS)]
        _attn_prompt_blocks(blocks, qt_ref, k_ref, vt_ref, bias_ref, s_scr, att_scr)

    for n in range(ATTN_BLOCKS):
        fr = slice(n * QBLOCK, (n + 1) * QBLOCK)
        out_ref[0, fr, :] = (att_scr[n] * gat_ref[0, :, fr]).T.astype(BF16)


def _attn_prompt(qt, kbf, vt, bias_t, gat):
    b, _, s = qt.shape
    fb = ATTN_BLOCKS * QBLOCK
    assert s % fb == 0
    return pl.pallas_call(
        _attn_prompt_kernel,
        grid=(b, s // fb),
        in_specs=[
            pl.BlockSpec((1, W_ATT, fb), lambda i, j: (i, 0, j)),
            pl.BlockSpec((1, s, W_ATT), lambda i, j: (i, 0, 0)),
            pl.BlockSpec((1, W_ATT, s), lambda i, j: (i, 0, 0)),
            pl.BlockSpec((N_HEADS, 1, PROMPT_BIAS_PERIOD), lambda i, j: (0, 0, 0)),
            pl.BlockSpec((1, W_ATT, fb), lambda i, j: (i, 0, j)),
        ],
        out_specs=pl.BlockSpec((1, fb, W_ATT), lambda i, j: (i, j, 0)),
        out_shape=jax.ShapeDtypeStruct((b, s, W_ATT), BF16),
        scratch_shapes=[
            pltpu.VMEM((N_HEADS, KEY_TILES * QBLOCK, QBLOCK), F32),
            pltpu.VMEM((SCORE_LEAD + 1, KEY_TILES * QBLOCK, QBLOCK), F32),
            pltpu.VMEM((ATTN_BLOCKS, W_ATT, QBLOCK), F32),
        ],
        compiler_params=pltpu.CompilerParams(
            dimension_semantics=("arbitrary", "arbitrary"), vmem_limit_bytes=VMEM_LIMIT),
        name="attn_prompt",
    )(qt, kbf, vt, bias_t, gat)


def _back_kernel(x_ref, lru_ref, att_ref, p_ref, wo_ref, pg_ref, wpg_ref, wpe_ref, y_ref):
    mix = _dot(lru_ref[...], wo_ref[0:W_LRU, :]) + _dot(att_ref[...], wo_ref[W_LRU:, :])
    h = x_ref[...] + mix
    gate = jax.nn.sigmoid(_dot(_rms_rows(h, pg_ref[...]).astype(BF16), wpg_ref[...]))
    y_ref[...] = h + _dot(p_ref[...].astype(BF16), wpe_ref[...]) * gate


def _back(x2, lru2, att2, p2, wo, pg, wpg, wpe, rows):
    n = x2.shape[0]
    const = lambda shape: pl.BlockSpec(shape, lambda i: (0,) * len(shape))
    return pl.pallas_call(
        _back_kernel,
        grid=(n // rows,),
        in_specs=[
            pl.BlockSpec((rows, D_MODEL), lambda i: (i, 0)),
            pl.BlockSpec((rows, W_LRU), lambda i: (i, 0)),
            pl.BlockSpec((rows, W_ATT), lambda i: (i, 0)),
            pl.BlockSpec((rows, PLE_DIM), lambda i: (i, 0)),
            const((W_LRU + W_ATT, D_MODEL)),
            const((1, D_MODEL)),
            const((D_MODEL, D_MODEL)),
            const((PLE_DIM, D_MODEL)),
        ],
        out_specs=pl.BlockSpec((rows, D_MODEL), lambda i: (i, 0)),
        out_shape=jax.ShapeDtypeStruct((n, D_MODEL), F32),
        compiler_params=pltpu.CompilerParams(
            dimension_semantics=("arbitrary",), vmem_limit_bytes=VMEM_LIMIT),
        name="back",
    )(x2, lru2, att2, p2, wo, pg, wpg, wpe)


def _front_sample_kernel(x_ref, ng_ref, win_ref, cw_ref, cbias_ref, wg_ref, bg_ref, lam_ref,
                         qg_ref, kg_ref, ones_ref, sconv_ref, slru_ref,
                         lru_ref, q_ref, k_ref, v_ref, ga_ref, sc_ref, sh_ref,
                         cb_scr, xc_scr, a_scr, u_scr, h_scr, *, nb, tt):
    seg = tt + SUBLANES
    xn = _rms_rows(x_ref[...], ng_ref[...]).astype(BF16)
    z = _dot(xn, win_ref[...])
    xl = z[:, :W_LRU]
    for s in range(nb):
        cb_scr[s * seg:s * seg + SUBLANES, :] = sconv_ref[s]
        cb_scr[s * seg + SUBLANES:(s + 1) * seg, :] = xl[s * tt:(s + 1) * tt, :]
        xc_scr[s * tt:(s + 1) * tt, :] = _conv_rows(cb_scr, s * seg, tt, cw_ref, cbias_ref[...])
        sc_ref[s] = cb_scr[(s + 1) * seg - (CONV_W - 1):(s + 1) * seg, :]
    _lru_inputs(xc_scr[...], wg_ref, bg_ref, lam_ref, a_scr, u_scr)
    for s in range(nb):
        sh_ref[s] = _scan_rows(a_scr, u_scr, h_scr, s * tt, tt, slru_ref[s])
    lru_ref[...] = (h_scr[...] * jax.nn.silu(z[:, W_LRU:2 * W_LRU])).astype(BF16)

    o = 2 * W_LRU
    q = _head_norm_rows(z[:, o:o + W_ATT], ones_ref[...], qg_ref[...])
    q_ref[...] = (q * (HEAD_DIM ** -0.5)).astype(BF16)
    k_ref[...] = _head_norm_rows(z[:, o + W_ATT:o + 2 * W_ATT], ones_ref[...], kg_ref[...])
    v_ref[...] = z[:, o + 2 * W_ATT:o + 3 * W_ATT]
    ga_ref[...] = jax.nn.silu(z[:, o + 3 * W_ATT:])


def _front_sample(x2, ng, win, cw, cbias, wg, bg, lam, qg_t, kg_t, ones_bd, sconv_pad, slru, nb, tt):
    n = x2.shape[0]
    rows = nb * tt
    const = lambda shape: pl.BlockSpec(shape, lambda i: (0,) * len(shape))
    row_spec = lambda w: pl.BlockSpec((rows, w), lambda i: (i, 0))
    nseq = n // tt
    kern = functools.partial(_front_sample_kernel, nb=nb, tt=tt)
    return pl.pallas_call(
        kern,
        grid=(n // rows,),
        in_specs=[
            row_spec(D_MODEL),
            const((1, D_MODEL)),
            const((D_MODEL, 2 * W_LRU + 4 * W_ATT)),
            const((CONV_W, W_LRU)),
            const((1, W_LRU)),
            const((W_LRU // HEAD_GROUP, HEAD_GROUP, 2 * HEAD_GROUP)),
            const((W_LRU // HEAD_GROUP, 1, 2 * HEAD_GROUP)),
            const((1, W_LRU)),
            const((1, W_ATT)),
            const((1, W_ATT)),
            const((HEAD_GROUP, HEAD_GROUP)),
            pl.BlockSpec((nb, SUBLANES, W_LRU), lambda i: (i, 0, 0)),
            pl.BlockSpec((nb, 1, W_LRU), lambda i: (i, 0, 0)),
        ],
        out_specs=[
            row_spec(W_LRU), row_spec(W_ATT), row_spec(W_ATT), row_spec(W_ATT), row_spec(W_ATT),
            pl.BlockSpec((nb, CONV_W - 1, W_LRU), lambda i: (i, 0, 0)),
            pl.BlockSpec((nb, 1, W_LRU), lambda i: (i, 0, 0)),
        ],
        out_shape=[
            jax.ShapeDtypeStruct((n, W_LRU), BF16),
            jax.ShapeDtypeStruct((n, W_ATT), BF16),
            jax.ShapeDtypeStruct((n, W_ATT), F32),
            jax.ShapeDtypeStruct((n, W_ATT), F32),
            jax.ShapeDtypeStruct((n, W_ATT), F32),
            jax.ShapeDtypeStruct((nseq, CONV_W - 1, W_LRU), F32),
            jax.ShapeDtypeStruct((nseq, 1, W_LRU), F32),
        ],
        scratch_shapes=[
            pltpu.VMEM((nb * (tt + SUBLANES), W_LRU), F32),
            pltpu.VMEM((rows, W_LRU), F32),
            pltpu.VMEM((rows, W_LRU), F32),
            pltpu.VMEM((rows, W_LRU), F32),
            pltpu.VMEM((rows, W_LRU), F32),
        ],
        compiler_params=pltpu.CompilerParams(
            dimension_semantics=("arbitrary",), vmem_limit_bytes=VMEM_LIMIT),
        name="front_sample",
    )(x2, ng, win, cw, cbias, wg, bg, lam, qg_t, kg_t, ones_bd, sconv_pad, slru)


def _attn_sample_kernel(q_ref, kn_ref, vn_ref, kc_ref, vc_ref, w_ref, ga_ref, out_ref, bias_ref):
    tt = q_ref.shape[1]
    lc = kc_ref.shape[-1]

    @pl.when(pl.program_id(0) == 0)
    def _():
        for h in range(N_HEADS):
            g, hl = divmod(h, HEADS_PER_GROUP)
            bias_ref[g, hl * tt:(hl + 1) * tt, :] = _toeplitz_rows(w_ref[h], tt, 0)

    lanes = lax.broadcasted_iota(jnp.int32, (tt, HEAD_GROUP), 1)
    masks = [(lanes >= hl * HEAD_DIM) & (lanes < (hl + 1) * HEAD_DIM) for hl in range(HEADS_PER_GROUP)]
    units = [(s, g) for s in range(q_ref.shape[0]) for g in range(N_GROUPS)]

    def cached(ref, s, g):
        heads = ref[s, g * HEADS_PER_GROUP:(g + 1) * HEADS_PER_GROUP]
        return heads.reshape(HEAD_GROUP, lc).astype(BF16)

    def scores(s, g):
        gsl = slice(g * HEAD_GROUP, (g + 1) * HEAD_GROUP)
        qg = q_ref[s, :, gsl]
        qs = jnp.concatenate([jnp.where(m, qg, jnp.zeros((), BF16)) for m in masks], axis=0)
        sc = _dot(qs, cached(kc_ref, s, g)) + bias_ref[g, :, 0:lc]
        sn = _dot_nt(qs, kn_ref[s, :, gsl].astype(BF16)) + bias_ref[g, :, lc:lc + tt]
        return sc, sn

    def finish(s, g, sc, sn):
        gsl = slice(g * HEAD_GROUP, (g + 1) * HEAD_GROUP)
        m = jnp.maximum(jnp.max(sc, axis=-1, keepdims=True), jnp.max(sn, axis=-1, keepdims=True))
        pc = jnp.exp(sc - m)
        pn = jnp.exp(sn - m)
        l = jnp.sum(pc, axis=-1, keepdims=True) + jnp.sum(pn, axis=-1, keepdims=True)
        o = _dot_nt(pc.astype(BF16), cached(vc_ref, s, g))
        o = (o + _dot(pn.astype(BF16), vn_ref[s, :, gsl].astype(BF16))) * (1.0 / l)
        att = jnp.zeros((tt, HEAD_GROUP), F32)
        for hl in range(HEADS_PER_GROUP):
            att = att + jnp.where(masks[hl], o[hl * tt:(hl + 1) * tt, :], 0.0)
        out_ref[s, :, gsl] = (att * ga_ref[s, :, gsl]).astype(BF16)

    nxt = scores(*units[0])
    for n, (s, g) in enumerate(units):
        cur = nxt
        if n + 1 < len(units):
            nxt = scores(*units[n + 1])
        finish(s, g, *cur)


def _attn_sample(q3, k3, v3, kc_t, vc_t, w_bias, ga3):
    b, tt, _ = q3.shape
    l = kc_t.shape[-1]
    period = w_bias.shape[-1]
    ns = SAMPLE_SEQS_PER_STEP
    assert b % ns == 0
    seq = lambda r: pl.BlockSpec((ns, r, W_ATT), lambda i: (i, 0, 0))
    cache = pl.BlockSpec((ns, N_HEADS, HEAD_DIM, l), lambda i: (i, 0, 0, 0))
    return pl.pallas_call(
        _attn_sample_kernel,
        grid=(b // ns,),
        in_specs=[
            seq(tt), seq(tt), seq(tt), cache, cache,
            pl.BlockSpec((N_HEADS, 1, period), lambda i: (0, 0, 0)),
            seq(tt),
        ],
        out_specs=seq(tt),
        out_shape=jax.ShapeDtypeStruct((b, tt, W_ATT), BF16),
        scratch_shapes=[pltpu.VMEM((N_GROUPS, HEADS_PER_GROUP * tt, period), F32)],
        compiler_params=pltpu.CompilerParams(
            dimension_semantics=("arbitrary",), vmem_limit_bytes=VMEM_LIMIT),
        name="attn_sample",
    )(q3, k3, v3, kc_t, vc_t, w_bias, ga3)


def _block_diag(w):
    n, d, e = w.shape
    eye = jnp.eye(n, dtype=w.dtype)
    return (eye[:, None, :, None] * w[:, :, None, :]).reshape(n * d, n * e)


def _prompt_bias_period(table):
    assert BAND - MAX_REL == MAX_REL and PROMPT_BIAS_PERIOD == KEY_TILES * QBLOCK + QBLOCK
    last = table[2 * MAX_REL:]
    neg_d = jnp.concatenate([table, jnp.broadcast_to(last, (MAX_REL - 1, N_HEADS))])
    w = jnp.concatenate([jnp.broadcast_to(last, (QBLOCK, N_HEADS)), neg_d])
    return w.T.reshape(N_HEADS, 1, PROMPT_BIAS_PERIOD).astype(F32)


def _sample_bias_period(table, tt, l):
    assert l >= MAX_REL
    period = -(-(l + 2 * tt - 1) // LANES) * LANES
    last = table[2 * MAX_REL:]
    n_var = tt + MAX_REL - 1
    var = table[2 * MAX_REL - 1:2 * MAX_REL - 1 - n_var:-1]
    w = jnp.concatenate([jnp.broadcast_to(last, (l - MAX_REL + 1, N_HEADS)), var,
                         jnp.broadcast_to(last, (period - (l + tt), N_HEADS))])
    return w.T.reshape(N_HEADS, 1, period).astype(F32)


def kernel(x_prompt, x_sample, p_prompt, p_sample, cache_k, cache_v, state_conv, state_lru, norm_g, w_in, conv_w, conv_b, gate_a_w, gate_a_b, gate_x_w, gate_x_b, lru_lambda, q_norm_g, k_norm_g, rel_bias, w_out, ple_norm_g, w_ple_gate, w_ple_proj):
    depth = w_in.shape[0]
    b, s, _ = x_prompt.shape
    db, ds, _ = x_sample.shape
    lc = cache_k.shape[2]
    yp, ys = x_prompt, x_sample.reshape(db * ds, D_MODEL)
    ones_bd = _block_diag(jnp.full((HEADS_PER_GROUP, HEAD_DIM, HEAD_DIM), 1.0 / HEAD_DIM, F32)).astype(BF16)
    gate_halves = W_LRU // HEAD_GROUP
    blocks_per_half = LRU_BLOCKS // gate_halves
    outs = [[] for _ in range(8)]
    sample_nb = 8
    for l in range(depth):
        win = w_in[l].astype(BF16)
        o = 2 * W_LRU
        wnat = jnp.concatenate([win[:, :o], win[:, o + W_ATT:o + 2 * W_ATT]], axis=1)
        wt = jnp.concatenate([win[:, o:o + W_ATT], win[:, o + 2 * W_ATT:]], axis=1).T
        ng = norm_g[l].reshape(1, D_MODEL)
        cw = conv_w[l]
        cbias = conv_b[l].reshape(1, W_LRU)
        wg = jnp.stack([
            jnp.concatenate([_block_diag(w[j * blocks_per_half:(j + 1) * blocks_per_half])
                             for w in (gate_a_w[l], gate_x_w[l])], axis=1)
            for j in range(gate_halves)]).astype(BF16)
        bg = jnp.concatenate([gate_a_b[l].reshape(gate_halves, 1, HEAD_GROUP),
                              gate_x_b[l].reshape(gate_halves, 1, HEAD_GROUP)], axis=2)
        lam = lru_lambda[l].reshape(1, W_LRU)
        qg_col = q_norm_g[l].reshape(HEAD_DIM, 1)
        qg_t = jnp.tile(q_norm_g[l], N_HEADS).reshape(1, W_ATT)
        kg_t = jnp.tile(k_norm_g[l], N_HEADS).reshape(1, W_ATT)
        wo = w_out[l].astype(BF16)
        pg = ple_norm_g[l].reshape(1, D_MODEL)
        wpg = w_ple_gate[l].astype(BF16)
        wpe = w_ple_proj[l].astype(BF16)

        lru_g, qt, kbf, vt, gat, pk, pv, pc, ph = _front_prompt(
            yp, ng, wnat, wt, cw, cbias, wg, bg, lam, qg_col, kg_t, ones_bd)
        att_g = _attn_prompt(qt, kbf, vt, _prompt_bias_period(rel_bias[l]), gat)
        yp = _back(yp.reshape(b * s, D_MODEL), lru_g.reshape(b * s, W_LRU), att_g.reshape(b * s, W_ATT),
                   p_prompt[l].reshape(b * s, PLE_DIM), wo, pg, wpg, wpe, 1024).reshape(b, s, D_MODEL)
        to_frames = lambda a: jnp.transpose(a.reshape(b, N_HEADS, HEAD_DIM, a.shape[-1]), (0, 3, 1, 2))
        outs[0].append(to_frames(pk))
        outs[1].append(to_frames(pv))
        outs[2].append(pc)
        outs[3].append(ph.reshape(b, W_LRU))

        sconv_pad = jnp.pad(state_conv[l], ((0, 0), (SUBLANES - (CONV_W - 1), 0), (0, 0)))
        lru_s, q_s, k_s, v_s, ga_s, sc, sh = _front_sample(
            ys, ng, win, cw, cbias, wg, bg, lam, qg_t, kg_t, ones_bd,
            sconv_pad, state_lru[l].reshape(db, 1, W_LRU), sample_nb, ds)
        att_s = _attn_sample(q_s.reshape(db, ds, W_ATT), k_s.reshape(db, ds, W_ATT), v_s.reshape(db, ds, W_ATT),
                             jnp.transpose(cache_k[l], (0, 2, 3, 1)), jnp.transpose(cache_v[l], (0, 2, 3, 1)),
                             _sample_bias_period(rel_bias[l], ds, lc), ga_s.reshape(db, ds, W_ATT))
        ys = _back(ys, lru_s, att_s.reshape(db * ds, W_ATT), p_sample[l].reshape(db * ds, PLE_DIM),
                   wo, pg, wpg, wpe, 256)
        outs[4].append(k_s.reshape(db, ds, N_HEADS, HEAD_DIM))
        outs[5].append(v_s.reshape(db, ds, N_HEADS, HEAD_DIM))
        outs[6].append(sc)
        outs[7].append(sh.reshape(db, W_LRU))
    return (yp, ys.reshape(db, ds, D_MODEL)) + tuple(jnp.stack(o) for o in outs)
```

```python
import functools

import jax
import jax.numpy as jnp
from jax import lax
from jax.experimental import pallas as pl
from jax.experimental.pallas import tpu as pltpu

D_MODEL = 1024
CHUNK = 64
PAST_CHUNKS = 8
BAND = PAST_CHUNKS * CHUNK
W_LRU = D_MODEL // 2
LRU_BLOCKS = 8
LRU_BLOCK = W_LRU // LRU_BLOCKS
CONV_W = 4
RG_C = 8.0
HEAD_DIM = 64
W_ATT = D_MODEL // 2
N_HEADS = W_ATT // HEAD_DIM
MAX_REL = 256
PLE_DIM = 256
EPS = 1e-6
NEG = -1e30
LOG2E = 1.4426950408889634

SUBLANES = 8
LANES = 128
HEAD_GROUP = 256
HEADS_PER_GROUP = HEAD_GROUP // HEAD_DIM
N_GROUPS = W_ATT // HEAD_GROUP
QBLOCK = 256
KEY_TILES = BAND // QBLOCK + 1
PROMPT_BIAS_PERIOD = (KEY_TILES + 1) * QBLOCK
FRONT_BLOCK = 1024
SAMPLE_SEQS_PER_STEP = 4
BACK_ROWS = 1024
VMEM_LIMIT = 56 * 1024 * 1024

F32 = jnp.float32
BF16 = jnp.bfloat16


def _dot(a, b):
    return jnp.dot(a, b, preferred_element_type=F32)


def _dot_nt(a, b):
    return lax.dot_general(a, b, (((1,), (1,)), ((), ())), preferred_element_type=F32)


def _rms_rows(x, g):
    ms = jnp.mean(x * x, axis=-1, keepdims=True)
    return x * lax.rsqrt(ms + EPS) * g


def _head_norm_rows(x, ones_bd, g_tiled):
    x2 = x * x
    hi = x2.astype(BF16)
    lo = (x2 - hi.astype(F32)).astype(BF16)
    ms = jnp.concatenate(
        [_dot(hi[:, g * HEAD_GROUP:(g + 1) * HEAD_GROUP], ones_bd)
         + _dot(lo[:, g * HEAD_GROUP:(g + 1) * HEAD_GROUP], ones_bd) for g in range(N_GROUPS)], axis=1)
    return x * lax.rsqrt(ms + EPS) * g_tiled


def _scan_rows(a_ref, u_ref, h_ref, row0, nrows, h0, unroll=False):
    ridx = lax.broadcasted_iota(jnp.int32, (SUBLANES, W_LRU), 0)

    def body(i, hprev):
        r = pl.multiple_of(row0 + i * SUBLANES, SUBLANES)
        a = a_ref[pl.ds(r, SUBLANES), :]
        u = u_ref[pl.ds(r, SUBLANES), :]
        for s in (1, 2, 4):
            a_s = jnp.where(ridx >= s, pltpu.roll(a, s, 0), 1.0)
            u_s = jnp.where(ridx >= s, pltpu.roll(u, s, 0), 0.0)
            u = a * u_s + u
            a = a * a_s
        h = a * hprev + u
        h_ref[pl.ds(r, SUBLANES), :] = h
        return h[SUBLANES - 1:SUBLANES, :]

    return lax.fori_loop(0, nrows // SUBLANES, body, h0, unroll=unroll)


def _lru_inputs(xc, wg_ref, bg_ref, lam_ref, a_ref, u_ref):
    xcb = xc.astype(BF16)
    half = wg_ref.shape[1]
    for j in range(wg_ref.shape[0]):
        sl = slice(j * half, (j + 1) * half)
        gates = _dot(xcb[:, sl], wg_ref[j]) + bg_ref[j]
        r = jax.nn.sigmoid(gates[:, :half])
        i = jax.nn.sigmoid(gates[:, half:])
        log_a = -RG_C * r * jax.nn.softplus(-lam_ref[:, sl])
        a = jnp.exp(log_a)
        a_ref[:, sl] = a
        u_ref[:, sl] = jnp.sqrt(jnp.tanh(-log_a) * (1.0 + a * a)) * (i * xc[:, sl])


def _conv_rows(cb_ref, base, nrows, cw_ref, cb_bias):
    out = cb_bias + cw_ref[CONV_W - 1:CONV_W, :] * cb_ref[pl.ds(base + SUBLANES, nrows), :]
    for k in range(CONV_W - 1):
        shift = CONV_W - 1 - k
        out = out + cw_ref[k:k + 1, :] * cb_ref[pl.ds(base + SUBLANES - shift, nrows), :]
    return out


def _front_prompt_kernel(x_ref, ng_ref, wnat_ref, wt_ref, cw_ref, cbias_ref, wg_ref, bg_ref, lam_ref,
                         qg_ref, kg_ref, ones_ref,
                         lru_ref, qt_ref, kbf_ref, vt_ref, gat_ref, pk_ref, pv_ref, pc_ref, ph_ref,
                         zn_scr, zt_scr, cb_scr, a_scr, u_scr, h_scr, hlast_scr, *, keep_subs):
    sub = QBLOCK
    n_sub = x_ref.shape[1] // sub

    @pl.when(pl.program_id(1) == 0)
    def _():
        cb_scr[0:SUBLANES, :] = jnp.zeros((SUBLANES, W_LRU), F32)
        hlast_scr[...] = jnp.zeros((1, W_LRU), F32)

    def project(i):
        xn = _rms_rows(x_ref[0, i * sub:(i + 1) * sub, :], ng_ref[...]).astype(BF16)
        zn_scr[i % 2] = _dot(xn, wnat_ref[...])
        zt_scr[i % 2] = _dot_nt(wt_ref[...], xn)

    def finish(i):
        rows = slice(i * sub, (i + 1) * sub)
        zn = zn_scr.at[i % 2]
        zt = zt_scr.at[i % 2]
        cb_scr[SUBLANES:SUBLANES + sub, :] = zn[:, :W_LRU]
        xc = _conv_rows(cb_scr, 0, sub, cw_ref, cbias_ref[...])
        pc_ref[0] = cb_scr[sub + SUBLANES - (CONV_W - 1):sub + SUBLANES, :]
        cb_scr[0:SUBLANES, :] = cb_scr[sub:sub + SUBLANES, :]
        _lru_inputs(xc, wg_ref, bg_ref, lam_ref, a_scr, u_scr)
        h_last = _scan_rows(a_scr, u_scr, h_scr, 0, sub, hlast_scr[...], unroll=True)
        hlast_scr[...] = h_last
        ph_ref[0] = h_last
        lru_ref[0, rows, :] = (h_scr[...] * jax.nn.silu(zn[:, W_LRU:2 * W_LRU])).astype(BF16)
        k = _head_norm_rows(zn[:, 2 * W_LRU:], ones_ref[...], kg_ref[...])
        kbf_ref[0, rows, :] = k.astype(BF16)
        q3 = zt[0:W_ATT, :].reshape(N_HEADS, HEAD_DIM, sub)
        ms = jnp.mean(q3 * q3, axis=1, keepdims=True)
        qn = q3 * lax.rsqrt(ms + EPS) * (qg_ref[...] * (HEAD_DIM ** -0.5 * LOG2E))
        qt_ref[0, :, rows] = qn.reshape(W_ATT, sub).astype(BF16)
        vt = zt[W_ATT:2 * W_ATT, :]
        vt_ref[0, :, rows] = vt.astype(BF16)
        gat_ref[0, :, rows] = jax.nn.silu(zt[2 * W_ATT:, :])
        if i >= n_sub - keep_subs:
            first = (i - (n_sub - keep_subs)) * sub
            pk_ref[0, :, first:first + sub] = k.T
            pv_ref[0, :, first:first + sub] = vt

    project(0)
    for i in range(n_sub):
        if i + 1 < n_sub:
            project(i + 1)
        finish(i)


def _front_prompt(x, ng, wnat, wt, cw, cbias, wg, bg, lam, qg, kg, ones_bd):
    b, s, _ = x.shape
    tb = min(FRONT_BLOCK, s)
    keep = min(BAND, s)
    assert s % tb == 0 and tb % QBLOCK == 0 and keep % QBLOCK == 0 and keep <= tb
    const = lambda shape: pl.BlockSpec(shape, lambda i, j: (0,) * len(shape))
    rows_spec = lambda w: pl.BlockSpec((1, tb, w), lambda i, j: (i, j, 0))
    cols_spec = pl.BlockSpec((1, W_ATT, tb), lambda i, j: (i, 0, j))
    per_seq = lambda r, w: pl.BlockSpec((1, r, w), lambda i, j: (i, 0, 0))
    kern = functools.partial(_front_prompt_kernel, keep_subs=keep // QBLOCK)
    return pl.pallas_call(
        kern,
        grid=(b, s // tb),
        in_specs=[
            rows_spec(D_MODEL),
            const((1, D_MODEL)),
            const((D_MODEL, 3 * W_LRU)),
            const((3 * W_ATT, D_MODEL)),
            const((CONV_W, W_LRU)),
            const((1, W_LRU)),
            const((W_LRU // HEAD_GROUP, HEAD_GROUP, 2 * HEAD_GROUP)),
            const((W_LRU // HEAD_GROUP, 1, 2 * HEAD_GROUP)),
            const((1, W_LRU)),
            const((HEAD_DIM, 1)),
            const((1, W_ATT)),
            const((HEAD_GROUP, HEAD_GROUP)),
        ],
        out_specs=[
            rows_spec(W_LRU),
            cols_spec,
            rows_spec(W_ATT),
            cols_spec,
            cols_spec,
            per_seq(W_ATT, keep),
            per_seq(W_ATT, keep),
            per_seq(CONV_W - 1, W_LRU),
            per_seq(1, W_LRU),
        ],
        out_shape=[
            jax.ShapeDtypeStruct((b, s, W_LRU), BF16),
            jax.ShapeDtypeStruct((b, W_ATT, s), BF16),
            jax.ShapeDtypeStruct((b, s, W_ATT), BF16),
            jax.ShapeDtypeStruct((b, W_ATT, s), BF16),
            jax.ShapeDtypeStruct((b, W_ATT, s), F32),
            jax.ShapeDtypeStruct((b, W_ATT, keep), F32),
            jax.ShapeDtypeStruct((b, W_ATT, keep), F32),
            jax.ShapeDtypeStruct((b, CONV_W - 1, W_LRU), F32),
            jax.ShapeDtypeStruct((b, 1, W_LRU), F32),
        ],
        scratch_shapes=[
            pltpu.VMEM((2, QBLOCK, 3 * W_LRU), F32),
            pltpu.VMEM((2, 3 * W_ATT, QBLOCK), F32),
            pltpu.VMEM((QBLOCK + SUBLANES, W_LRU), F32),
            pltpu.VMEM((QBLOCK, W_LRU), F32),
            pltpu.VMEM((QBLOCK, W_LRU), F32),
            pltpu.VMEM((QBLOCK, W_LRU), F32),
            pltpu.VMEM((1, W_LRU), F32),
        ],
        compiler_params=pltpu.CompilerParams(
            dimension_semantics=("arbitrary", "arbitrary"), vmem_limit_bytes=VMEM_LIMIT),
        name="front_prompt",
    )(x, ng, wnat, wt, cw, cbias, wg, bg, lam, qg, kg, ones_bd)


def _toeplitz_rows(w_row, nrows, row0):
    x = jnp.broadcast_to(w_row, (nrows, w_row.shape[1]))
    return pltpu.roll(x, row0, 1, stride=1, stride_axis=0)


def _fill_prompt_bias(w_ref, bias_ref):
    q_chunk = (lax.broadcasted_iota(jnp.int32, (CHUNK, QBLOCK), 1) + BAND) // CHUNK
    for h in range(N_HEADS):
        def body(n, carry, h=h):
            r0 = pl.multiple_of(n * CHUNK, CHUNK)
            t = _toeplitz_rows(w_ref[h], CHUNK, r0)[:, :QBLOCK]
            dc = q_chunk - n
            bias_ref[h, pl.ds(r0, CHUNK), :] = jnp.where((dc >= 0) & (dc <= PAST_CHUNKS), t * LOG2E, NEG)
            return carry
        lax.fori_loop(0, KEY_TILES * QBLOCK // CHUNK, body, 0)


SCORE_LEAD = 3
ATTN_BLOCKS = 4
CHUNKS_PER_TILE = QBLOCK // CHUNK
FRAMES_PER_VREG = LANES // CHUNK


def _lane_cols(kc):
    cols = []
    for c in range(QBLOCK // LANES):
        q_lo = PAST_CHUNKS + c * FRAMES_PER_VREG
        q_hi = q_lo + FRAMES_PER_VREG - 1
        if q_lo - PAST_CHUNKS <= kc <= q_hi:
            cols.append(c)
    return cols


def _fold_rows(x):
    return x.reshape(x.shape[0] // SUBLANES, SUBLANES, x.shape[1])


def _attn_prompt_blocks(blocks, qt_ref, k_ref, vt_ref, bias_ref, s_scr, att_scr):
    n_cols = QBLOCK // LANES
    rows = lax.broadcasted_iota(jnp.int32, (HEAD_GROUP, QBLOCK), 0)
    units = [(n, h) for n in range(len(blocks)) for h in range(N_HEADS)]

    def pieces(tiles):
        return [(i, cc, slice((i * CHUNKS_PER_TILE + cc) * CHUNK, (i * CHUNKS_PER_TILE + cc + 1) * CHUNK),
                 slice(c * LANES, (c + 1) * LANES), c)
                for i in tiles for cc in range(CHUNKS_PER_TILE) for c in _lane_cols(i * CHUNKS_PER_TILE + cc)]

    def key_rows(n, i):
        tiles, key0 = blocks[n]
        return pl.ds(key0 + (i - tiles[0]) * QBLOCK, QBLOCK)

    def scores(u):
        n, h = units[u]
        g, hl = divmod(h, HEADS_PER_GROUP)
        gsl = slice(g * HEAD_GROUP, (g + 1) * HEAD_GROUP)
        in_head = (rows >= hl * HEAD_DIM) & (rows < (hl + 1) * HEAD_DIM)
        qm = jnp.where(in_head, qt_ref[0, gsl, n * QBLOCK:(n + 1) * QBLOCK], jnp.zeros((), BF16))
        s = {i: _dot(k_ref[0, key_rows(n, i), gsl], qm) for i in blocks[n][0]}
        m_acc = [jnp.full((SUBLANES, LANES), NEG, F32) for _ in range(n_cols)]
        for i, cc, rsl, lsl, c in pieces(blocks[n][0]):
            sp = s[i][cc * CHUNK:(cc + 1) * CHUNK, lsl] + bias_ref[h, rsl, lsl]
            s_scr[u % (SCORE_LEAD + 1), rsl, lsl] = sp
            m_acc[c] = jnp.maximum(m_acc[c], jnp.max(_fold_rows(sp), axis=0))
        return [jnp.max(a, axis=0, keepdims=True) for a in m_acc]

    def weights(u, m):
        tiles_p = {}
        for i in blocks[units[u][0]][0]:
            chunks = []
            for cc in range(CHUNKS_PER_TILE):
                kc = i * CHUNKS_PER_TILE + cc
                rsl = slice(kc * CHUNK, (kc + 1) * CHUNK)
                cols = []
                for c in range(n_cols):
                    if c in _lane_cols(kc):
                        lsl = slice(c * LANES, (c + 1) * LANES)
                        cols.append(jnp.exp2(s_scr[u % (SCORE_LEAD + 1), rsl, lsl] - m[c]).astype(BF16))
                    else:
                        cols.append(jnp.zeros((CHUNK, LANES), BF16))
                chunks.append(jnp.concatenate(cols, axis=1))
            tiles_p[i] = jnp.concatenate(chunks, axis=0)
        return tiles_p

    ones_rows = jnp.ones((2 * SUBLANES, QBLOCK), BF16)

    def values(u, tiles_p):
        n, h = units[u]
        hsl = slice(h * HEAD_DIM, (h + 1) * HEAD_DIM)
        o = None
        for i in blocks[n][0]:
            oi = _dot(jnp.concatenate([vt_ref[0, hsl, key_rows(n, i)], ones_rows], axis=0), tiles_p[i])
            o = oi if o is None else o + oi
        att_scr[n, hsl, :] = o[0:HEAD_DIM, :] * (1.0 / o[HEAD_DIM:HEAD_DIM + 1, :])

    m = {u: scores(u) for u in range(min(SCORE_LEAD, len(units)))}
    w = {}
    for u in range(len(units)):
        if u + SCORE_LEAD < len(units):
            m[u + SCORE_LEAD] = scores(u + SCORE_LEAD)
        w[u] = weights(u, m.pop(u))
        if u >= 1:
            values(u - 1, w.pop(u - 1))
    values(len(units) - 1, w.pop(len(units) - 1))


def _attn_prompt_kernel(qt_ref, k_ref, vt_ref, w_ref, gat_ref, out_ref, bias_ref, s_scr, att_scr):
    step = pl.program_id(1)

    @pl.when((pl.program_id(0) == 0) & (step == 0))
    def _():
        _fill_prompt_bias(w_ref, bias_ref)

    assert ATTN_BLOCKS >= KEY_TILES - 1
    full = tuple(range(KEY_TILES))

    @pl.when(step == 0)
    def _():
        blocks = [(full[max(KEY_TILES - 1 - n, 0):], max(n - (KEY_TILES - 1), 0) * QBLOCK)
                  for n in range(ATTN_BLOCKS)]
        _attn_prompt_blocks(blocks, qt_ref, k_ref, vt_ref, bias_ref, s_scr, att_scr)

    if k_ref.shape[1] > ATTN_BLOCKS * QBLOCK:
        @pl.when(step > 0)
        def _():
            first = step * ATTN_BLOCKS - (KEY_TILES - 1)
            blocks = [(full, pl.multiple_of((first + n) * QBLOCK, QBLOCK)) for n in range(ATTN_BLOCKS)]
            _attn_prompt_blocks(blocks, qt_ref, k_ref, vt_ref, bias_ref, s_scr, att_scr)

    for n in range(ATTN_BLOCKS):
        fr = slice(n * QBLOCK, (n + 1) * QBLOCK)
        out_ref[0, fr, :] = (att_scr[n] * gat_ref[0, :, fr]).T.astype(BF16)


def _attn_prompt(qt, kbf, vt, bias_t, gat):
    b, _, s = qt.shape
    fb = ATTN_BLOCKS * QBLOCK
    assert s % fb == 0
    return pl.pallas_call(
        _attn_prompt_kernel,
        grid=(b, s // fb),
        in_specs=[
            pl.BlockSpec((1, W_ATT, fb), lambda i, j: (i, 0, j)),
            pl.BlockSpec((1, s, W_ATT), lambda i, j: (i, 0, 0)),
            pl.BlockSpec((1, W_ATT, s), lambda i, j: (i, 0, 0)),
            pl.BlockSpec((N_HEADS, 1, PROMPT_BIAS_PERIOD), lambda i, j: (0, 0, 0)),
            pl.BlockSpec((1, W_ATT, fb), lambda i, j: (i, 0, j)),
        ],
        out_specs=pl.BlockSpec((1, fb, W_ATT), lambda i, j: (i, j, 0)),
        out_shape=jax.ShapeDtypeStruct((b, s, W_ATT), BF16),
        scratch_shapes=[
            pltpu.VMEM((N_HEADS, KEY_TILES * QBLOCK, QBLOCK), F32),
            pltpu.VMEM((SCORE_LEAD + 1, KEY_TILES * QBLOCK, QBLOCK), F32),
            pltpu.VMEM((ATTN_BLOCKS, W_ATT, QBLOCK), F32),
        ],
        compiler_params=pltpu.CompilerParams(
            dimension_semantics=("arbitrary", "arbitrary"), vmem_limit_bytes=VMEM_LIMIT),
        name="attn_prompt",
    )(qt, kbf, vt, bias_t, gat)


def _back_rows(x_ref, lru_ref, att_ref, p_ref, wo_ref, pg_ref, wpg_ref, wpe_ref, y_ref):
    mix = _dot(lru_ref[...], wo_ref[0:W_LRU, :]) + _dot(att_ref[...], wo_ref[W_LRU:, :])
    h = x_ref[...] + mix
    gate = jax.nn.sigmoid(_dot(_rms_rows(h, pg_ref[...]).astype(BF16), wpg_ref[...]))
    y_ref[...] = h + _dot(p_ref[...].astype(BF16), wpe_ref[...]) * gate


def _back_kernel(xp_ref, lrup_ref, attp_ref, pp_ref, xs_ref, lrus_ref, atts_ref, ps_ref,
                 wo_ref, pg_ref, wpg_ref, wpe_ref, yp_ref, ys_ref, *, prompt_steps):
    weights = (wo_ref, pg_ref, wpg_ref, wpe_ref)

    @pl.when(pl.program_id(0) < prompt_steps)
    def _():
        _back_rows(xp_ref, lrup_ref, attp_ref, pp_ref, *weights, yp_ref)

    @pl.when(pl.program_id(0) == prompt_steps)
    def _():
        _back_rows(xs_ref, lrus_ref, atts_ref, ps_ref, *weights, ys_ref)


def _back(prompt, sample, wo, pg, wpg, wpe, rows):
    n, ns = prompt[0].shape[0], sample[0].shape[0]
    assert n % rows == 0
    steps = n // rows
    const = lambda shape: pl.BlockSpec(shape, lambda i: (0,) * len(shape))
    walk = lambda w: pl.BlockSpec((rows, w), lambda i: (jnp.minimum(i, steps - 1), 0))
    whole = lambda w: pl.BlockSpec((ns, w), lambda i: (0, 0), pipeline_mode=pl.Buffered(1))
    widths = (D_MODEL, W_LRU, W_ATT, PLE_DIM)
    return pl.pallas_call(
        functools.partial(_back_kernel, prompt_steps=steps),
        grid=(steps + 1,),
        in_specs=[walk(w) for w in widths] + [whole(w) for w in widths] + [
            const((W_LRU + W_ATT, D_MODEL)),
            const((1, D_MODEL)),
            const((D_MODEL, D_MODEL)),
            const((PLE_DIM, D_MODEL)),
        ],
        out_specs=[walk(D_MODEL), pl.BlockSpec((ns, D_MODEL), lambda i: (0, 0))],
        out_shape=[jax.ShapeDtypeStruct((n, D_MODEL), F32), jax.ShapeDtypeStruct((ns, D_MODEL), F32)],
        compiler_params=pltpu.CompilerParams(
            dimension_semantics=("arbitrary",), vmem_limit_bytes=VMEM_LIMIT),
        name="back",
    )(*prompt, *sample, wo, pg, wpg, wpe)


def _front_sample_kernel(x_ref, ng_ref, win_ref, cw_ref, cbias_ref, wg_ref, bg_ref, lam_ref,
                         qg_ref, kg_ref, ones_ref, sconv_ref, slru_ref,
                         lru_ref, q_ref, k_ref, v_ref, ga_ref, sc_ref, sh_ref,
                         cb_scr, xc_scr, a_scr, u_scr, h_scr, *, nb, tt):
    seg = tt + SUBLANES
    xn = _rms_rows(x_ref[...], ng_ref[...]).astype(BF16)
    z = _dot(xn, win_ref[...])
    xl = z[:, :W_LRU]
    for s in range(nb):
        cb_scr[s * seg:s * seg + SUBLANES, :] = sconv_ref[s]
        cb_scr[s * seg + SUBLANES:(s + 1) * seg, :] = xl[s * tt:(s + 1) * tt, :]
        xc_scr[s * tt:(s + 1) * tt, :] = _conv_rows(cb_scr, s * seg, tt, cw_ref, cbias_ref[...])
        sc_ref[s] = cb_scr[(s + 1) * seg - (CONV_W - 1):(s + 1) * seg, :]
    _lru_inputs(xc_scr[...], wg_ref, bg_ref, lam_ref, a_scr, u_scr)
    for s in range(nb):
        sh_ref[s] = _scan_rows(a_scr, u_scr, h_scr, s * tt, tt, slru_ref[s])
    lru_ref[...] = (h_scr[...] * jax.nn.silu(z[:, W_LRU:2 * W_LRU])).astype(BF16)

    o = 2 * W_LRU
    q = _head_norm_rows(z[:, o:o + W_ATT], ones_ref[...], qg_ref[...])
    q_ref[...] = (q * (HEAD_DIM ** -0.5)).astype(BF16)
    k_ref[...] = _head_norm_rows(z[:, o + W_ATT:o + 2 * W_ATT], ones_ref[...], kg_ref[...])
    v_ref[...] = z[:, o + 2 * W_ATT:o + 3 * W_ATT]
    ga_ref[...] = jax.nn.silu(z[:, o + 3 * W_ATT:])


def _front_sample(x2, ng, win, cw, cbias, wg, bg, lam, qg_t, kg_t, ones_bd, sconv_pad, slru, nb, tt):
    n = x2.shape[0]
    rows = nb * tt
    const = lambda shape: pl.BlockSpec(shape, lambda i: (0,) * len(shape))
    row_spec = lambda w: pl.BlockSpec((rows, w), lambda i: (i, 0))
    nseq = n // tt
    kern = functools.partial(_front_sample_kernel, nb=nb, tt=tt)
    return pl.pallas_call(
        kern,
        grid=(n // rows,),
        in_specs=[
            row_spec(D_MODEL),
            const((1, D_MODEL)),
            const((D_MODEL, 2 * W_LRU + 4 * W_ATT)),
            const((CONV_W, W_LRU)),
            const((1, W_LRU)),
            const((W_LRU // HEAD_GROUP, HEAD_GROUP, 2 * HEAD_GROUP)),
            const((W_LRU // HEAD_GROUP, 1, 2 * HEAD_GROUP)),
            const((1, W_LRU)),
            const((1, W_ATT)),
            const((1, W_ATT)),
            const((HEAD_GROUP, HEAD_GROUP)),
            pl.BlockSpec((nb, SUBLANES, W_LRU), lambda i: (i, 0, 0)),
            pl.BlockSpec((nb, 1, W_LRU), lambda i: (i, 0, 0)),
        ],
        out_specs=[
            row_spec(W_LRU), row_spec(W_ATT), row_spec(W_ATT), row_spec(W_ATT), row_spec(W_ATT),
            pl.BlockSpec((nb, CONV_W - 1, W_LRU), lambda i: (i, 0, 0)),
            pl.BlockSpec((nb, 1, W_LRU), lambda i: (i, 0, 0)),
        ],
        out_shape=[
            jax.ShapeDtypeStruct((n, W_LRU), BF16),
            jax.ShapeDtypeStruct((n, W_ATT), BF16),
            jax.ShapeDtypeStruct((n, W_ATT), F32),
            jax.ShapeDtypeStruct((n, W_ATT), F32),
            jax.ShapeDtypeStruct((n, W_ATT), F32),
            jax.ShapeDtypeStruct((nseq, CONV_W - 1, W_LRU), F32),
            jax.ShapeDtypeStruct((nseq, 1, W_LRU), F32),
        ],
        scratch_shapes=[
            pltpu.VMEM((nb * (tt + SUBLANES), W_LRU), F32),
            pltpu.VMEM((rows, W_LRU), F32),
            pltpu.VMEM((rows, W_LRU), F32),
            pltpu.VMEM((rows, W_LRU), F32),
            pltpu.VMEM((rows, W_LRU), F32),
        ],
        compiler_params=pltpu.CompilerParams(
            dimension_semantics=("arbitrary",), vmem_limit_bytes=VMEM_LIMIT),
        name="front_sample",
    )(x2, ng, win, cw, cbias, wg, bg, lam, qg_t, kg_t, ones_bd, sconv_pad, slru)


def _attn_sample_kernel(q_ref, kn_ref, vn_ref, kc_ref, vc_ref, w_ref, ga_ref, out_ref, bias_ref):
    tt = q_ref.shape[1]
    lc = kc_ref.shape[-1]

    @pl.when(pl.program_id(0) == 0)
    def _():
        for h in range(N_HEADS):
            g, hl = divmod(h, HEADS_PER_GROUP)
            bias_ref[g, hl * tt:(hl + 1) * tt, :] = _toeplitz_rows(w_ref[h], tt, 0)

    lanes = lax.broadcasted_iota(jnp.int32, (tt, HEAD_GROUP), 1)
    masks = [(lanes >= hl * HEAD_DIM) & (lanes < (hl + 1) * HEAD_DIM) for hl in range(HEADS_PER_GROUP)]
    units = [(s, g) for s in range(q_ref.shape[0]) for g in range(N_GROUPS)]

    def cached(ref, s, g):
        heads = ref[s, g * HEADS_PER_GROUP:(g + 1) * HEADS_PER_GROUP]
        return heads.reshape(HEAD_GROUP, lc).astype(BF16)

    def scores(s, g):
        gsl = slice(g * HEAD_GROUP, (g + 1) * HEAD_GROUP)
        qg = q_ref[s, :, gsl]
        qs = jnp.concatenate([jnp.where(m, qg, jnp.zeros((), BF16)) for m in masks], axis=0)
        sc = _dot(qs, cached(kc_ref, s, g)) + bias_ref[g, :, 0:lc]
        sn = _dot_nt(qs, kn_ref[s, :, gsl].astype(BF16)) + bias_ref[g, :, lc:lc + tt]
        return sc, sn

    def finish(s, g, sc, sn):
        gsl = slice(g * HEAD_GROUP, (g + 1) * HEAD_GROUP)
        m = jnp.maximum(jnp.max(sc, axis=-1, keepdims=True), jnp.max(sn, axis=-1, keepdims=True))
        pc = jnp.exp(sc - m)
        pn = jnp.exp(sn - m)
        l = jnp.sum(pc, axis=-1, keepdims=True) + jnp.sum(pn, axis=-1, keepdims=True)
        o = _dot_nt(pc.astype(BF16), cached(vc_ref, s, g))
        o = (o + _dot(pn.astype(BF16), vn_ref[s, :, gsl].astype(BF16))) * (1.0 / l)
        att = jnp.zeros((tt, HEAD_GROUP), F32)
        for hl in range(HEADS_PER_GROUP):
            att = att + jnp.where(masks[hl], o[hl * tt:(hl + 1) * tt, :], 0.0)
        out_ref[s, :, gsl] = (att * ga_ref[s, :, gsl]).astype(BF16)

    nxt = scores(*units[0])
    for n, (s, g) in enumerate(units):
        cur = nxt
        if n + 1 < len(units):
            nxt = scores(*units[n + 1])
        finish(s, g, *cur)


def _attn_sample(q3, k3, v3, kc_t, vc_t, w_bias, ga3):
    b, tt, _ = q3.shape
    l = kc_t.shape[-1]
    period = w_bias.shape[-1]
    ns = SAMPLE_SEQS_PER_STEP
    assert b % ns == 0
    seq = lambda r: pl.BlockSpec((ns, r, W_ATT), lambda i: (i, 0, 0))
    cache = pl.BlockSpec((ns, N_HEADS, HEAD_DIM, l), lambda i: (i, 0, 0, 0))
    return pl.pallas_call(
        _attn_sample_kernel,
        grid=(b // ns,),
        in_specs=[
            seq(tt), seq(tt), seq(tt), cache, cache,
            pl.BlockSpec((N_HEADS, 1, period), lambda i: (0, 0, 0)),
            seq(tt),
        ],
        out_specs=seq(tt),
        out_shape=jax.ShapeDtypeStruct((b, tt, W_ATT), BF16),
        scratch_shapes=[pltpu.VMEM((N_GROUPS, HEADS_PER_GROUP * tt, period), F32)],
        compiler_params=pltpu.CompilerParams(
            dimension_semantics=("arbitrary",), vmem_limit_bytes=VMEM_LIMIT),
        name="attn_sample",
    )(q3, k3, v3, kc_t, vc_t, w_bias, ga3)


def _block_diag(w):
    n, d, e = w.shape
    eye = jnp.eye(n, dtype=w.dtype)
    return (eye[:, None, :, None] * w[:, :, None, :]).reshape(n * d, n * e)


def _prompt_bias_period(table):
    assert BAND - MAX_REL == MAX_REL and PROMPT_BIAS_PERIOD == KEY_TILES * QBLOCK + QBLOCK
    last = table[2 * MAX_REL:]
    neg_d = jnp.concatenate([table, jnp.broadcast_to(last, (MAX_REL - 1, N_HEADS))])
    w = jnp.concatenate([jnp.broadcast_to(last, (QBLOCK, N_HEADS)), neg_d])
    return w.T.reshape(N_HEADS, 1, PROMPT_BIAS_PERIOD).astype(F32)


def _sample_bias_period(table, tt, l):
    assert l >= MAX_REL
    period = -(-(l + 2 * tt - 1) // LANES) * LANES
    last = table[2 * MAX_REL:]
    n_var = tt + MAX_REL - 1
    var = table[2 * MAX_REL - 1:2 * MAX_REL - 1 - n_var:-1]
    w = jnp.concatenate([jnp.broadcast_to(last, (l - MAX_REL + 1, N_HEADS)), var,
                         jnp.broadcast_to(last, (period - (l + tt), N_HEADS))])
    return w.T.reshape(N_HEADS, 1, period).astype(F32)


def kernel(x_prompt, x_sample, p_prompt, p_sample, cache_k, cache_v, state_conv, state_lru, norm_g, w_in, conv_w, conv_b, gate_a_w, gate_a_b, gate_x_w, gate_x_b, lru_lambda, q_norm_g, k_norm_g, rel_bias, w_out, ple_norm_g, w_ple_gate, w_ple_proj):
    depth = w_in.shape[0]
    b, s, _ = x_prompt.shape
    db, ds, _ = x_sample.shape
    lc = cache_k.shape[2]
    yp, ys = x_prompt, x_sample.reshape(db * ds, D_MODEL)
    ones_bd = _block_diag(jnp.full((HEADS_PER_GROUP, HEAD_DIM, HEAD_DIM), 1.0 / HEAD_DIM, F32)).astype(BF16)
    gate_halves = W_LRU // HEAD_GROUP
    blocks_per_half = LRU_BLOCKS // gate_halves
    outs = [[] for _ in range(8)]
    sample_nb = 8
    for l in range(depth):
        win = w_in[l].astype(BF16)
        o = 2 * W_LRU
        wnat = jnp.concatenate([win[:, :o], win[:, o + W_ATT:o + 2 * W_ATT]], axis=1)
        wt = jnp.concatenate([win[:, o:o + W_ATT], win[:, o + 2 * W_ATT:]], axis=1).T
        ng = norm_g[l].reshape(1, D_MODEL)
        cw = conv_w[l]
        cbias = conv_b[l].reshape(1, W_LRU)
        wg = jnp.stack([
            jnp.concatenate([_block_diag(w[j * blocks_per_half:(j + 1) * blocks_per_half])
                             for w in (gate_a_w[l], gate_x_w[l])], axis=1)
            for j in range(gate_halves)]).astype(BF16)
        bg = jnp.concatenate([gate_a_b[l].reshape(gate_halves, 1, HEAD_GROUP),
                              gate_x_b[l].reshape(gate_halves, 1, HEAD_GROUP)], axis=2)
        lam = lru_lambda[l].reshape(1, W_LRU)
        qg_col = q_norm_g[l].reshape(HEAD_DIM, 1)
        qg_t = jnp.tile(q_norm_g[l], N_HEADS).reshape(1, W_ATT)
        kg_t = jnp.tile(k_norm_g[l], N_HEADS).reshape(1, W_ATT)
        wo = w_out[l].astype(BF16)
        pg = ple_norm_g[l].reshape(1, D_MODEL)
        wpg = w_ple_gate[l].astype(BF16)
        wpe = w_ple_proj[l].astype(BF16)

        lru_g, qt, kbf, vt, gat, pk, pv, pc, ph = _front_prompt(
            yp, ng, wnat, wt, cw, cbias, wg, bg, lam, qg_col, kg_t, ones_bd)
        att_g = _attn_prompt(qt, kbf, vt, _prompt_bias_period(rel_bias[l]), gat)
        to_frames = lambda a: jnp.transpose(a.reshape(b, N_HEADS, HEAD_DIM, a.shape[-1]), (0, 3, 1, 2))
        outs[0].append(to_frames(pk))
        outs[1].append(to_frames(pv))
        outs[2].append(pc)
        outs[3].append(ph.reshape(b, W_LRU))

        sconv_pad = jnp.pad(state_conv[l], ((0, 0), (SUBLANES - (CONV_W - 1), 0), (0, 0)))
        lru_s, q_s, k_s, v_s, ga_s, sc, sh = _front_sample(
            ys, ng, win, cw, cbias, wg, bg, lam, qg_t, kg_t, ones_bd,
            sconv_pad, state_lru[l].reshape(db, 1, W_LRU), sample_nb, ds)
        att_s = _attn_sample(q_s.reshape(db, ds, W_ATT), k_s.reshape(db, ds, W_ATT), v_s.reshape(db, ds, W_ATT),
                             jnp.transpose(cache_k[l], (0, 2, 3, 1)), jnp.transpose(cache_v[l], (0, 2, 3, 1)),
                             _sample_bias_period(rel_bias[l], ds, lc), ga_s.reshape(db, ds, W_ATT))
        yp, ys = _back(
            (yp.reshape(b * s, D_MODEL), lru_g.reshape(b * s, W_LRU), att_g.reshape(b * s, W_ATT),
             p_prompt[l].reshape(b * s, PLE_DIM)),
            (ys, lru_s, att_s.reshape(db * ds, W_ATT), p_sample[l].reshape(db * ds, PLE_DIM)),
            wo, pg, wpg, wpe, BACK_ROWS)
        yp = yp.reshape(b, s, D_MODEL)
        outs[4].append(k_s.reshape(db, ds, N_HEADS, HEAD_DIM))
        outs[5].append(v_s.reshape(db, ds, N_HEADS, HEAD_DIM))
        outs[6].append(sc)
        outs[7].append(sh.reshape(db, W_LRU))
    return (yp, ys.reshape(db, ds, D_MODEL)) + tuple(jnp.stack(o) for o in outs)
```

```python
import functools

import jax
import jax.numpy as jnp
from jax import lax
from jax.experimental import pallas as pl
from jax.experimental.pallas import tpu as pltpu

D_MODEL = 1024
CHUNK = 64
PAST_CHUNKS = 8
BAND = PAST_CHUNKS * CHUNK
W_LRU = D_MODEL // 2
LRU_BLOCKS = 8
LRU_BLOCK = W_LRU // LRU_BLOCKS
CONV_W = 4
RG_C = 8.0
HEAD_DIM = 64
W_ATT = D_MODEL // 2
N_HEADS = W_ATT // HEAD_DIM
MAX_REL = 256
PLE_DIM = 256
EPS = 1e-6
NEG = -1e30
LOG2E = 1.4426950408889634

SUBLANES = 8
LANES = 128
HEAD_GROUP = 256
HEADS_PER_GROUP = HEAD_GROUP // HEAD_DIM
N_GROUPS = W_ATT // HEAD_GROUP
QBLOCK = 256
KEY_TILES = BAND // QBLOCK + 1
PROMPT_BIAS_PERIOD = (KEY_TILES + 1) * QBLOCK
FRONT_BLOCK = 1024
BACK_ROWS = 1024
VMEM_LIMIT = 56 * 1024 * 1024

F32 = jnp.float32
BF16 = jnp.bfloat16


def _dot(a, b):
    return jnp.dot(a, b, preferred_element_type=F32)


def _dot_nt(a, b):
    return lax.dot_general(a, b, (((1,), (1,)), ((), ())), preferred_element_type=F32)


def _rms_rows(x, g):
    ms = jnp.mean(x * x, axis=-1, keepdims=True)
    return x * lax.rsqrt(ms + EPS) * g


def _head_norm_rows(x, ones_bd, g_tiled):
    x2 = x * x
    hi = x2.astype(BF16)
    lo = (x2 - hi.astype(F32)).astype(BF16)
    ms = jnp.concatenate(
        [_dot(hi[:, g * HEAD_GROUP:(g + 1) * HEAD_GROUP], ones_bd)
         + _dot(lo[:, g * HEAD_GROUP:(g + 1) * HEAD_GROUP], ones_bd) for g in range(N_GROUPS)], axis=1)
    return x * lax.rsqrt(ms + EPS) * g_tiled


def _scan_rows(a_ref, u_ref, h_ref, row0, nrows, h0, unroll=False):
    ridx = lax.broadcasted_iota(jnp.int32, (SUBLANES, W_LRU), 0)

    def body(i, hprev):
        r = pl.multiple_of(row0 + i * SUBLANES, SUBLANES)
        a = a_ref[pl.ds(r, SUBLANES), :]
        u = u_ref[pl.ds(r, SUBLANES), :]
        for s in (1, 2, 4):
            a_s = jnp.where(ridx >= s, pltpu.roll(a, s, 0), 1.0)
            u_s = jnp.where(ridx >= s, pltpu.roll(u, s, 0), 0.0)
            u = a * u_s + u
            a = a * a_s
        h = a * hprev + u
        h_ref[pl.ds(r, SUBLANES), :] = h
        return h[SUBLANES - 1:SUBLANES, :]

    return lax.fori_loop(0, nrows // SUBLANES, body, h0, unroll=unroll)


def _lru_inputs(xc, wg_ref, bg_ref, lam_ref, a_ref, u_ref):
    xcb = xc.astype(BF16)
    half = wg_ref.shape[1]
    for j in range(wg_ref.shape[0]):
        sl = slice(j * half, (j + 1) * half)
        gates = _dot(xcb[:, sl], wg_ref[j]) + bg_ref[j]
        r = jax.nn.sigmoid(gates[:, :half])
        i = jax.nn.sigmoid(gates[:, half:])
        log_a = -RG_C * r * jax.nn.softplus(-lam_ref[:, sl])
        a = jnp.exp(log_a)
        a_ref[:, sl] = a
        u_ref[:, sl] = jnp.sqrt(jnp.tanh(-log_a) * (1.0 + a * a)) * (i * xc[:, sl])


def _conv_rows(cb_ref, base, nrows, cw_ref, cb_bias):
    out = cb_bias + cw_ref[CONV_W - 1:CONV_W, :] * cb_ref[pl.ds(base + SUBLANES, nrows), :]
    for k in range(CONV_W - 1):
        shift = CONV_W - 1 - k
        out = out + cw_ref[k:k + 1, :] * cb_ref[pl.ds(base + SUBLANES - shift, nrows), :]
    return out


def _front_prompt_kernel(x_ref, ng_ref, wnat_ref, wt_ref, cw_ref, cbias_ref, wg_ref, bg_ref, lam_ref,
                         qg_ref, kg_ref, ones_ref,
                         lru_ref, qt_ref, kbf_ref, vt_ref, gat_ref, pk_ref, pv_ref, pc_ref, ph_ref,
                         zn_scr, zt_scr, cb_scr, a_scr, u_scr, h_scr, hlast_scr, *, keep_subs):
    sub = QBLOCK
    n_sub = x_ref.shape[1] // sub

    @pl.when(pl.program_id(1) == 0)
    def _():
        cb_scr[0:SUBLANES, :] = jnp.zeros((SUBLANES, W_LRU), F32)
        hlast_scr[...] = jnp.zeros((1, W_LRU), F32)

    def project(i):
        xn = _rms_rows(x_ref[0, i * sub:(i + 1) * sub, :], ng_ref[...]).astype(BF16)
        zn_scr[i % 2] = _dot(xn, wnat_ref[...])
        zt_scr[i % 2] = _dot_nt(wt_ref[...], xn)

    def finish(i):
        rows = slice(i * sub, (i + 1) * sub)
        zn = zn_scr.at[i % 2]
        zt = zt_scr.at[i % 2]
        cb_scr[SUBLANES:SUBLANES + sub, :] = zn[:, :W_LRU]
        xc = _conv_rows(cb_scr, 0, sub, cw_ref, cbias_ref[...])
        pc_ref[0] = cb_scr[sub + SUBLANES - (CONV_W - 1):sub + SUBLANES, :]
        cb_scr[0:SUBLANES, :] = cb_scr[sub:sub + SUBLANES, :]
        _lru_inputs(xc, wg_ref, bg_ref, lam_ref, a_scr, u_scr)
        h_last = _scan_rows(a_scr, u_scr, h_scr, 0, sub, hlast_scr[...], unroll=True)
        hlast_scr[...] = h_last
        ph_ref[0] = h_last
        lru_ref[0, rows, :] = (h_scr[...] * jax.nn.silu(zn[:, W_LRU:2 * W_LRU])).astype(BF16)
        k = _head_norm_rows(zn[:, 2 * W_LRU:], ones_ref[...], kg_ref[...])
        kbf_ref[0, rows, :] = k.astype(BF16)
        q3 = zt[0:W_ATT, :].reshape(N_HEADS, HEAD_DIM, sub)
        ms = jnp.mean(q3 * q3, axis=1, keepdims=True)
        qn = q3 * lax.rsqrt(ms + EPS) * (qg_ref[...] * (HEAD_DIM ** -0.5 * LOG2E))
        qt_ref[0, :, rows] = qn.reshape(W_ATT, sub).astype(BF16)
        vt = zt[W_ATT:2 * W_ATT, :]
        vt_ref[0, :, rows] = vt.astype(BF16)
        gat_ref[0, :, rows] = jax.nn.silu(zt[2 * W_ATT:, :])
        if i >= n_sub - keep_subs:
            first = (i - (n_sub - keep_subs)) * sub
            pk_ref[0, :, first:first + sub] = k.T
            pv_ref[0, :, first:first + sub] = vt

    project(0)
    for i in range(n_sub):
        if i + 1 < n_sub:
            project(i + 1)
        finish(i)


def _front_prompt(x, ng, wnat, wt, cw, cbias, wg, bg, lam, qg, kg, ones_bd):
    b, s, _ = x.shape
    tb = min(FRONT_BLOCK, s)
    keep = min(BAND, s)
    assert s % tb == 0 and tb % QBLOCK == 0 and keep % QBLOCK == 0 and keep <= tb
    const = lambda shape: pl.BlockSpec(shape, lambda i, j: (0,) * len(shape))
    rows_spec = lambda w: pl.BlockSpec((1, tb, w), lambda i, j: (i, j, 0))
    cols_spec = pl.BlockSpec((1, W_ATT, tb), lambda i, j: (i, 0, j))
    per_seq = lambda r, w: pl.BlockSpec((1, r, w), lambda i, j: (i, 0, 0))
    kern = functools.partial(_front_prompt_kernel, keep_subs=keep // QBLOCK)
    return pl.pallas_call(
        kern,
        grid=(b, s // tb),
        in_specs=[
            rows_spec(D_MODEL),
            const((1, D_MODEL)),
            const((D_MODEL, 3 * W_LRU)),
            const((3 * W_ATT, D_MODEL)),
            const((CONV_W, W_LRU)),
            const((1, W_LRU)),
            const((W_LRU // HEAD_GROUP, HEAD_GROUP, 2 * HEAD_GROUP)),
            const((W_LRU // HEAD_GROUP, 1, 2 * HEAD_GROUP)),
            const((1, W_LRU)),
            const((HEAD_DIM, 1)),
            const((1, W_ATT)),
            const((HEAD_GROUP, HEAD_GROUP)),
        ],
        out_specs=[
            rows_spec(W_LRU),
            cols_spec,
            rows_spec(W_ATT),
            cols_spec,
            cols_spec,
            per_seq(W_ATT, keep),
            per_seq(W_ATT, keep),
            per_seq(CONV_W - 1, W_LRU),
            per_seq(1, W_LRU),
        ],
        out_shape=[
            jax.ShapeDtypeStruct((b, s, W_LRU), BF16),
            jax.ShapeDtypeStruct((b, W_ATT, s), BF16),
            jax.ShapeDtypeStruct((b, s, W_ATT), BF16),
            jax.ShapeDtypeStruct((b, W_ATT, s), BF16),
            jax.ShapeDtypeStruct((b, W_ATT, s), F32),
            jax.ShapeDtypeStruct((b, W_ATT, keep), F32),
            jax.ShapeDtypeStruct((b, W_ATT, keep), F32),
            jax.ShapeDtypeStruct((b, CONV_W - 1, W_LRU), F32),
            jax.ShapeDtypeStruct((b, 1, W_LRU), F32),
        ],
        scratch_shapes=[
            pltpu.VMEM((2, QBLOCK, 3 * W_LRU), F32),
            pltpu.VMEM((2, 3 * W_ATT, QBLOCK), F32),
            pltpu.VMEM((QBLOCK + SUBLANES, W_LRU), F32),
            pltpu.VMEM((QBLOCK, W_LRU), F32),
            pltpu.VMEM((QBLOCK, W_LRU), F32),
            pltpu.VMEM((QBLOCK, W_LRU), F32),
            pltpu.VMEM((1, W_LRU), F32),
        ],
        compiler_params=pltpu.CompilerParams(
            dimension_semantics=("arbitrary", "arbitrary"), vmem_limit_bytes=VMEM_LIMIT),
        name="front_prompt",
    )(x, ng, wnat, wt, cw, cbias, wg, bg, lam, qg, kg, ones_bd)


def _toeplitz_rows(w_row, nrows, row0):
    x = jnp.broadcast_to(w_row, (nrows, w_row.shape[1]))
    return pltpu.roll(x, row0, 1, stride=1, stride_axis=0)


def _fill_prompt_bias(w_ref, bias_ref):
    q_chunk = (lax.broadcasted_iota(jnp.int32, (CHUNK, QBLOCK), 1) + BAND) // CHUNK
    for h in range(N_HEADS):
        def body(n, carry, h=h):
            r0 = pl.multiple_of(n * CHUNK, CHUNK)
            t = _toeplitz_rows(w_ref[h], CHUNK, r0)[:, :QBLOCK]
            dc = q_chunk - n
            bias_ref[h, pl.ds(r0, CHUNK), :] = jnp.where((dc >= 0) & (dc <= PAST_CHUNKS), t * LOG2E, NEG)
            return carry
        lax.fori_loop(0, KEY_TILES * QBLOCK // CHUNK, body, 0)


SCORE_LEAD = 3
ATTN_BLOCKS = 4
CHUNKS_PER_TILE = QBLOCK // CHUNK
FRAMES_PER_VREG = LANES // CHUNK


def _lane_cols(kc):
    cols = []
    for c in range(QBLOCK // LANES):
        q_lo = PAST_CHUNKS + c * FRAMES_PER_VREG
        q_hi = q_lo + FRAMES_PER_VREG - 1
        if q_lo - PAST_CHUNKS <= kc <= q_hi:
            cols.append(c)
    return cols


def _fold_rows(x):
    return x.reshape(x.shape[0] // SUBLANES, SUBLANES, x.shape[1])


def _attn_prompt_blocks(blocks, qt_ref, k_ref, vt_ref, bias_ref, s_scr, att_scr):
    n_cols = QBLOCK // LANES
    rows = lax.broadcasted_iota(jnp.int32, (HEAD_GROUP, QBLOCK), 0)
    units = [(n, h) for n in range(len(blocks)) for h in range(N_HEADS)]

    def pieces(tiles):
        return [(i, cc, slice((i * CHUNKS_PER_TILE + cc) * CHUNK, (i * CHUNKS_PER_TILE + cc + 1) * CHUNK),
                 slice(c * LANES, (c + 1) * LANES), c)
                for i in tiles for cc in range(CHUNKS_PER_TILE) for c in _lane_cols(i * CHUNKS_PER_TILE + cc)]

    def key_rows(n, i):
        tiles, key0 = blocks[n]
        return pl.ds(key0 + (i - tiles[0]) * QBLOCK, QBLOCK)

    def scores(u):
        n, h = units[u]
        g, hl = divmod(h, HEADS_PER_GROUP)
        gsl = slice(g * HEAD_GROUP, (g + 1) * HEAD_GROUP)
        in_head = (rows >= hl * HEAD_DIM) & (rows < (hl + 1) * HEAD_DIM)
        qm = jnp.where(in_head, qt_ref[0, gsl, n * QBLOCK:(n + 1) * QBLOCK], jnp.zeros((), BF16))
        s = {i: _dot(k_ref[0, key_rows(n, i), gsl], qm) for i in blocks[n][0]}
        m_acc = [jnp.full((SUBLANES, LANES), NEG, F32) for _ in range(n_cols)]
        for i, cc, rsl, lsl, c in pieces(blocks[n][0]):
            sp = s[i][cc * CHUNK:(cc + 1) * CHUNK, lsl] + bias_ref[h, rsl, lsl]
            s_scr[u % (SCORE_LEAD + 1), rsl, lsl] = sp
            m_acc[c] = jnp.maximum(m_acc[c], jnp.max(_fold_rows(sp), axis=0))
        return [jnp.max(a, axis=0, keepdims=True) for a in m_acc]

    def weights(u, m):
        tiles_p = {}
        for i in blocks[units[u][0]][0]:
            chunks = []
            for cc in range(CHUNKS_PER_TILE):
                kc = i * CHUNKS_PER_TILE + cc
                rsl = slice(kc * CHUNK, (kc + 1) * CHUNK)
                cols = []
                for c in range(n_cols):
                    if c in _lane_cols(kc):
                        lsl = slice(c * LANES, (c + 1) * LANES)
                        cols.append(jnp.exp2(s_scr[u % (SCORE_LEAD + 1), rsl, lsl] - m[c]).astype(BF16))
                    else:
                        cols.append(jnp.zeros((CHUNK, LANES), BF16))
                chunks.append(jnp.concatenate(cols, axis=1))
            tiles_p[i] = jnp.concatenate(chunks, axis=0)
        return tiles_p

    ones_rows = jnp.ones((2 * SUBLANES, QBLOCK), BF16)

    def values(u, tiles_p):
        n, h = units[u]
        hsl = slice(h * HEAD_DIM, (h + 1) * HEAD_DIM)
        o = None
        for i in blocks[n][0]:
            oi = _dot(jnp.concatenate([vt_ref[0, hsl, key_rows(n, i)], ones_rows], axis=0), tiles_p[i])
            o = oi if o is None else o + oi
        att_scr[n, hsl, :] = o[0:HEAD_DIM, :] * (1.0 / o[HEAD_DIM:HEAD_DIM + 1, :])

    m = {u: scores(u) for u in range(min(SCORE_LEAD, len(units)))}
    w = {}
    for u in range(len(units)):
        if u + SCORE_LEAD < len(units):
            m[u + SCORE_LEAD] = scores(u + SCORE_LEAD)
        w[u] = weights(u, m.pop(u))
        if u >= 1:
            values(u - 1, w.pop(u - 1))
    values(len(units) - 1, w.pop(len(units) - 1))


def _attn_prompt_kernel(qt_ref, k_ref, vt_ref, w_ref, gat_ref, out_ref, bias_ref, s_scr, att_scr):
    step = pl.program_id(1)

    @pl.when((pl.program_id(0) == 0) & (step == 0))
    def _():
        _fill_prompt_bias(w_ref, bias_ref)

    assert ATTN_BLOCKS >= KEY_TILES - 1
    full = tuple(range(KEY_TILES))

    @pl.when(step == 0)
    def _():
        blocks = [(full[max(KEY_TILES - 1 - n, 0):], max(n - (KEY_TILES - 1), 0) * QBLOCK)
                  for n in range(ATTN_BLOCKS)]
        _attn_prompt_blocks(blocks, qt_ref, k_ref, vt_ref, bias_ref, s_scr, att_scr)

    if k_ref.shape[1] > ATTN_BLOCKS * QBLOCK:
        @pl.when(step > 0)
        def _():
            first = step * ATTN_BLOCKS - (KEY_TILES - 1)
            blocks = [(full, pl.multiple_of((first + n) * QBLOCK, QBLOCK)) for n in range(ATTN_BLOCKS)]
            _attn_prompt_blocks(blocks, qt_ref, k_ref, vt_ref, bias_ref, s_scr, att_scr)

    for n in range(ATTN_BLOCKS):
        fr = slice(n * QBLOCK, (n + 1) * QBLOCK)
        out_ref[0, fr, :] = (att_scr[n] * gat_ref[0, :, fr]).T.astype(BF16)


def _attn_prompt(qt, kbf, vt, bias_t, gat):
    b, _, s = qt.shape
    fb = ATTN_BLOCKS * QBLOCK
    assert s % fb == 0
    return pl.pallas_call(
        _attn_prompt_kernel,
        grid=(b, s // fb),
        in_specs=[
            pl.BlockSpec((1, W_ATT, fb), lambda i, j: (i, 0, j)),
            pl.BlockSpec((1, s, W_ATT), lambda i, j: (i, 0, 0)),
            pl.BlockSpec((1, W_ATT, s), lambda i, j: (i, 0, 0)),
            pl.BlockSpec((N_HEADS, 1, PROMPT_BIAS_PERIOD), lambda i, j: (0, 0, 0)),
            pl.BlockSpec((1, W_ATT, fb), lambda i, j: (i, 0, j)),
        ],
        out_specs=pl.BlockSpec((1, fb, W_ATT), lambda i, j: (i, j, 0)),
        out_shape=jax.ShapeDtypeStruct((b, s, W_ATT), BF16),
        scratch_shapes=[
            pltpu.VMEM((N_HEADS, KEY_TILES * QBLOCK, QBLOCK), F32),
            pltpu.VMEM((SCORE_LEAD + 1, KEY_TILES * QBLOCK, QBLOCK), F32),
            pltpu.VMEM((ATTN_BLOCKS, W_ATT, QBLOCK), F32),
        ],
        compiler_params=pltpu.CompilerParams(
            dimension_semantics=("arbitrary", "arbitrary"), vmem_limit_bytes=VMEM_LIMIT),
        name="attn_prompt",
    )(qt, kbf, vt, bias_t, gat)


def _back_rows(x_ref, lru_ref, att_ref, p_ref, wo_ref, pg_ref, wpg_ref, wpe_ref, y_ref):
    mix = _dot(lru_ref[...], wo_ref[0:W_LRU, :]) + _dot(att_ref[...], wo_ref[W_LRU:, :])
    h = x_ref[...] + mix
    gate = jax.nn.sigmoid(_dot(_rms_rows(h, pg_ref[...]).astype(BF16), wpg_ref[...]))
    y_ref[...] = h + _dot(p_ref[...].astype(BF16), wpe_ref[...]) * gate


def _back_kernel(xp_ref, lrup_ref, attp_ref, pp_ref, xs_ref, lrus_ref, ps_ref,
                 q_ref, kn_ref, vn_ref, kc_ref, vc_ref, ga_ref, wb_ref,
                 wo_ref, pg_ref, wpg_ref, wpe_ref, yp_ref, ys_ref, bias_scr, atts_scr, *, prompt_steps):
    step = pl.program_id(0)
    weights = (wo_ref, pg_ref, wpg_ref, wpe_ref)
    streams, tt = q_ref.shape[0], q_ref.shape[1]

    @pl.when(step == 0)
    def _():
        _fill_sample_bias(wb_ref, bias_scr, tt)

    @pl.when(step < prompt_steps)
    def _():
        def write(s, lanes, rows):
            atts_scr[pl.ds(pl.multiple_of((step * streams + s) * tt, tt), tt), lanes] = rows

        _attn_sample_streams(q_ref, kn_ref, vn_ref, kc_ref, vc_ref, ga_ref, bias_scr, write)
        _back_rows(xp_ref, lrup_ref, attp_ref, pp_ref, *weights, yp_ref)

    @pl.when(step == prompt_steps)
    def _():
        _back_rows(xs_ref, lrus_ref, atts_scr, ps_ref, *weights, ys_ref)


def _back(prompt, sample, sample_attn, wo, pg, wpg, wpe, rows):
    n, ns = prompt[0].shape[0], sample[0].shape[0]
    q3, k3, v3, kc_t, vc_t, ga3, w_bias = sample_attn
    db, tt, _ = q3.shape
    assert n % rows == 0 and db * tt == ns
    steps = n // rows
    assert db % steps == 0
    per_step = db // steps
    const = lambda shape: pl.BlockSpec(shape, lambda i: (0,) * len(shape))
    walk = lambda w: pl.BlockSpec((rows, w), lambda i: (jnp.minimum(i, steps - 1), 0))
    whole = lambda w: pl.BlockSpec((ns, w), lambda i: (0, 0), pipeline_mode=pl.Buffered(1))
    seq = pl.BlockSpec((per_step, tt, W_ATT), lambda i: (jnp.minimum(i, steps - 1), 0, 0))
    cache = pl.BlockSpec((per_step, N_HEADS, HEAD_DIM, kc_t.shape[-1]),
                         lambda i: (jnp.minimum(i, steps - 1), 0, 0, 0))
    period = w_bias.shape[-1]
    return pl.pallas_call(
        functools.partial(_back_kernel, prompt_steps=steps),
        grid=(steps + 1,),
        in_specs=[walk(w) for w in (D_MODEL, W_LRU, W_ATT, PLE_DIM)]
        + [whole(w) for w in (D_MODEL, W_LRU, PLE_DIM)]
        + [seq, seq, seq, cache, cache, seq, const((N_HEADS, 1, period))]
        + [const((W_LRU + W_ATT, D_MODEL)), const((1, D_MODEL)), const((D_MODEL, D_MODEL)),
           const((PLE_DIM, D_MODEL))],
        out_specs=[walk(D_MODEL), pl.BlockSpec((ns, D_MODEL), lambda i: (0, 0))],
        out_shape=[jax.ShapeDtypeStruct((n, D_MODEL), F32), jax.ShapeDtypeStruct((ns, D_MODEL), F32)],
        scratch_shapes=[
            pltpu.VMEM((N_GROUPS, HEADS_PER_GROUP * tt, period), F32),
            pltpu.VMEM((ns, W_ATT), BF16),
        ],
        compiler_params=pltpu.CompilerParams(
            dimension_semantics=("arbitrary",), vmem_limit_bytes=VMEM_LIMIT),
        name="back",
    )(*prompt, *sample, q3, k3, v3, kc_t, vc_t, ga3, w_bias, wo, pg, wpg, wpe)


def _front_sample_kernel(x_ref, ng_ref, win_ref, cw_ref, cbias_ref, wg_ref, bg_ref, lam_ref,
                         qg_ref, kg_ref, ones_ref, sconv_ref, slru_ref,
                         lru_ref, q_ref, k_ref, v_ref, ga_ref, sc_ref, sh_ref,
                         cb_scr, xc_scr, a_scr, u_scr, h_scr, *, nb, tt):
    seg = tt + SUBLANES
    xn = _rms_rows(x_ref[...], ng_ref[...]).astype(BF16)
    z = _dot(xn, win_ref[...])
    xl = z[:, :W_LRU]
    for s in range(nb):
        cb_scr[s * seg:s * seg + SUBLANES, :] = sconv_ref[s]
        cb_scr[s * seg + SUBLANES:(s + 1) * seg, :] = xl[s * tt:(s + 1) * tt, :]
        xc_scr[s * tt:(s + 1) * tt, :] = _conv_rows(cb_scr, s * seg, tt, cw_ref, cbias_ref[...])
        sc_ref[s] = cb_scr[(s + 1) * seg - (CONV_W - 1):(s + 1) * seg, :]
    _lru_inputs(xc_scr[...], wg_ref, bg_ref, lam_ref, a_scr, u_scr)
    for s in range(nb):
        sh_ref[s] = _scan_rows(a_scr, u_scr, h_scr, s * tt, tt, slru_ref[s])
    lru_ref[...] = (h_scr[...] * jax.nn.silu(z[:, W_LRU:2 * W_LRU])).astype(BF16)

    o = 2 * W_LRU
    q = _head_norm_rows(z[:, o:o + W_ATT], ones_ref[...], qg_ref[...])
    q_ref[...] = (q * (HEAD_DIM ** -0.5)).astype(BF16)
    k_ref[...] = _head_norm_rows(z[:, o + W_ATT:o + 2 * W_ATT], ones_ref[...], kg_ref[...])
    v_ref[...] = z[:, o + 2 * W_ATT:o + 3 * W_ATT]
    ga_ref[...] = jax.nn.silu(z[:, o + 3 * W_ATT:])


def _front_sample(x2, ng, win, cw, cbias, wg, bg, lam, qg_t, kg_t, ones_bd, sconv_pad, slru, nb, tt):
    n = x2.shape[0]
    rows = nb * tt
    const = lambda shape: pl.BlockSpec(shape, lambda i: (0,) * len(shape))
    row_spec = lambda w: pl.BlockSpec((rows, w), lambda i: (i, 0))
    nseq = n // tt
    kern = functools.partial(_front_sample_kernel, nb=nb, tt=tt)
    return pl.pallas_call(
        kern,
        grid=(n // rows,),
        in_specs=[
            row_spec(D_MODEL),
            const((1, D_MODEL)),
            const((D_MODEL, 2 * W_LRU + 4 * W_ATT)),
            const((CONV_W, W_LRU)),
            const((1, W_LRU)),
            const((W_LRU // HEAD_GROUP, HEAD_GROUP, 2 * HEAD_GROUP)),
            const((W_LRU // HEAD_GROUP, 1, 2 * HEAD_GROUP)),
            const((1, W_LRU)),
            const((1, W_ATT)),
            const((1, W_ATT)),
            const((HEAD_GROUP, HEAD_GROUP)),
            pl.BlockSpec((nb, SUBLANES, W_LRU), lambda i: (i, 0, 0)),
            pl.BlockSpec((nb, 1, W_LRU), lambda i: (i, 0, 0)),
        ],
        out_specs=[
            row_spec(W_LRU), row_spec(W_ATT), row_spec(W_ATT), row_spec(W_ATT), row_spec(W_ATT),
            pl.BlockSpec((nb, CONV_W - 1, W_LRU), lambda i: (i, 0, 0)),
            pl.BlockSpec((nb, 1, W_LRU), lambda i: (i, 0, 0)),
        ],
        out_shape=[
            jax.ShapeDtypeStruct((n, W_LRU), BF16),
            jax.ShapeDtypeStruct((n, W_ATT), BF16),
            jax.ShapeDtypeStruct((n, W_ATT), F32),
            jax.ShapeDtypeStruct((n, W_ATT), F32),
            jax.ShapeDtypeStruct((n, W_ATT), F32),
            jax.ShapeDtypeStruct((nseq, CONV_W - 1, W_LRU), F32),
            jax.ShapeDtypeStruct((nseq, 1, W_LRU), F32),
        ],
        scratch_shapes=[
            pltpu.VMEM((nb * (tt + SUBLANES), W_LRU), F32),
            pltpu.VMEM((rows, W_LRU), F32),
            pltpu.VMEM((rows, W_LRU), F32),
            pltpu.VMEM((rows, W_LRU), F32),
            pltpu.VMEM((rows, W_LRU), F32),
        ],
        compiler_params=pltpu.CompilerParams(
            dimension_semantics=("arbitrary",), vmem_limit_bytes=VMEM_LIMIT),
        name="front_sample",
    )(x2, ng, win, cw, cbias, wg, bg, lam, qg_t, kg_t, ones_bd, sconv_pad, slru)


def _fill_sample_bias(w_ref, bias_ref, tt):
    for h in range(N_HEADS):
        g, hl = divmod(h, HEADS_PER_GROUP)
        bias_ref[g, hl * tt:(hl + 1) * tt, :] = _toeplitz_rows(w_ref[h], tt, 0)


def _attn_sample_streams(q_ref, kn_ref, vn_ref, kc_ref, vc_ref, ga_ref, bias_ref, write):
    tt = q_ref.shape[1]
    lc = kc_ref.shape[-1]
    lanes = lax.broadcasted_iota(jnp.int32, (tt, HEAD_GROUP), 1)
    masks = [(lanes >= hl * HEAD_DIM) & (lanes < (hl + 1) * HEAD_DIM) for hl in range(HEADS_PER_GROUP)]
    units = [(s, g) for s in range(q_ref.shape[0]) for g in range(N_GROUPS)]

    def cached(ref, s, g):
        heads = ref[s, g * HEADS_PER_GROUP:(g + 1) * HEADS_PER_GROUP]
        return heads.reshape(HEAD_GROUP, lc).astype(BF16)

    def scores(s, g):
        gsl = slice(g * HEAD_GROUP, (g + 1) * HEAD_GROUP)
        qg = q_ref[s, :, gsl]
        qs = jnp.concatenate([jnp.where(m, qg, jnp.zeros((), BF16)) for m in masks], axis=0)
        sc = _dot(qs, cached(kc_ref, s, g)) + bias_ref[g, :, 0:lc]
        sn = _dot_nt(qs, kn_ref[s, :, gsl].astype(BF16)) + bias_ref[g, :, lc:lc + tt]
        return sc, sn

    def finish(s, g, sc, sn):
        gsl = slice(g * HEAD_GROUP, (g + 1) * HEAD_GROUP)
        m = jnp.maximum(jnp.max(sc, axis=-1, keepdims=True), jnp.max(sn, axis=-1, keepdims=True))
        pc = jnp.exp(sc - m)
        pn = jnp.exp(sn - m)
        l = jnp.sum(pc, axis=-1, keepdims=True) + jnp.sum(pn, axis=-1, keepdims=True)
        o = _dot_nt(pc.astype(BF16), cached(vc_ref, s, g))
        o = (o + _dot(pn.astype(BF16), vn_ref[s, :, gsl].astype(BF16))) * (1.0 / l)
        att = jnp.zeros((tt, HEAD_GROUP), F32)
        for hl in range(HEADS_PER_GROUP):
            att = att + jnp.where(masks[hl], o[hl * tt:(hl + 1) * tt, :], 0.0)
        write(s, gsl, (att * ga_ref[s, :, gsl]).astype(BF16))

    nxt = scores(*units[0])
    for n, (s, g) in enumerate(units):
        cur = nxt
        if n + 1 < len(units):
            nxt = scores(*units[n + 1])
        finish(s, g, *cur)


def _block_diag(w):
    n, d, e = w.shape
    eye = jnp.eye(n, dtype=w.dtype)
    return (eye[:, None, :, None] * w[:, :, None, :]).reshape(n * d, n * e)


def _prompt_bias_period(table):
    assert BAND - MAX_REL == MAX_REL and PROMPT_BIAS_PERIOD == KEY_TILES * QBLOCK + QBLOCK
    last = table[2 * MAX_REL:]
    neg_d = jnp.concatenate([table, jnp.broadcast_to(last, (MAX_REL - 1, N_HEADS))])
    w = jnp.concatenate([jnp.broadcast_to(last, (QBLOCK, N_HEADS)), neg_d])
    return w.T.reshape(N_HEADS, 1, PROMPT_BIAS_PERIOD).astype(F32)


def _sample_bias_period(table, tt, l):
    assert l >= MAX_REL
    period = -(-(l + 2 * tt - 1) // LANES) * LANES
    last = table[2 * MAX_REL:]
    n_var = tt + MAX_REL - 1
    var = table[2 * MAX_REL - 1:2 * MAX_REL - 1 - n_var:-1]
    w = jnp.concatenate([jnp.broadcast_to(last, (l - MAX_REL + 1, N_HEADS)), var,
                         jnp.broadcast_to(last, (period - (l + tt), N_HEADS))])
    return w.T.reshape(N_HEADS, 1, period).astype(F32)


def kernel(x_prompt, x_sample, p_prompt, p_sample, cache_k, cache_v, state_conv, state_lru, norm_g, w_in, conv_w, conv_b, gate_a_w, gate_a_b, gate_x_w, gate_x_b, lru_lambda, q_norm_g, k_norm_g, rel_bias, w_out, ple_norm_g, w_ple_gate, w_ple_proj):
    depth = w_in.shape[0]
    b, s, _ = x_prompt.shape
    db, ds, _ = x_sample.shape
    lc = cache_k.shape[2]
    yp, ys = x_prompt, x_sample.reshape(db * ds, D_MODEL)
    ones_bd = _block_diag(jnp.full((HEADS_PER_GROUP, HEAD_DIM, HEAD_DIM), 1.0 / HEAD_DIM, F32)).astype(BF16)
    gate_halves = W_LRU // HEAD_GROUP
    blocks_per_half = LRU_BLOCKS // gate_halves
    outs = [[] for _ in range(8)]
    sample_nb = 8
    for l in range(depth):
        win = w_in[l].astype(BF16)
        o = 2 * W_LRU
        wnat = jnp.concatenate([win[:, :o], win[:, o + W_ATT:o + 2 * W_ATT]], axis=1)
        wt = jnp.concatenate([win[:, o:o + W_ATT], win[:, o + 2 * W_ATT:]], axis=1).T
        ng = norm_g[l].reshape(1, D_MODEL)
        cw = conv_w[l]
        cbias = conv_b[l].reshape(1, W_LRU)
        wg = jnp.stack([
            jnp.concatenate([_block_diag(w[j * blocks_per_half:(j + 1) * blocks_per_half])
                             for w in (gate_a_w[l], gate_x_w[l])], axis=1)
            for j in range(gate_halves)]).astype(BF16)
        bg = jnp.concatenate([gate_a_b[l].reshape(gate_halves, 1, HEAD_GROUP),
                              gate_x_b[l].reshape(gate_halves, 1, HEAD_GROUP)], axis=2)
        lam = lru_lambda[l].reshape(1, W_LRU)
        qg_col = q_norm_g[l].reshape(HEAD_DIM, 1)
        qg_t = jnp.tile(q_norm_g[l], N_HEADS).reshape(1, W_ATT)
        kg_t = jnp.tile(k_norm_g[l], N_HEADS).reshape(1, W_ATT)
        wo = w_out[l].astype(BF16)
        pg = ple_norm_g[l].reshape(1, D_MODEL)
        wpg = w_ple_gate[l].astype(BF16)
        wpe = w_ple_proj[l].astype(BF16)

        lru_g, qt, kbf, vt, gat, pk, pv, pc, ph = _front_prompt(
            yp, ng, wnat, wt, cw, cbias, wg, bg, lam, qg_col, kg_t, ones_bd)
        att_g = _attn_prompt(qt, kbf, vt, _prompt_bias_period(rel_bias[l]), gat)
        to_frames = lambda a: jnp.transpose(a.reshape(b, N_HEADS, HEAD_DIM, a.shape[-1]), (0, 3, 1, 2))
        outs[0].append(to_frames(pk))
        outs[1].append(to_frames(pv))
        outs[2].append(pc)
        outs[3].append(ph.reshape(b, W_LRU))

        sconv_pad = jnp.pad(state_conv[l], ((0, 0), (SUBLANES - (CONV_W - 1), 0), (0, 0)))
        lru_s, q_s, k_s, v_s, ga_s, sc, sh = _front_sample(
            ys, ng, win, cw, cbias, wg, bg, lam, qg_t, kg_t, ones_bd,
            sconv_pad, state_lru[l].reshape(db, 1, W_LRU), sample_nb, ds)
        yp, ys = _back(
            (yp.reshape(b * s, D_MODEL), lru_g.reshape(b * s, W_LRU), att_g.reshape(b * s, W_ATT),
             p_prompt[l].reshape(b * s, PLE_DIM)),
            (ys, lru_s, p_sample[l].reshape(db * ds, PLE_DIM)),
            (q_s.reshape(db, ds, W_ATT), k_s.reshape(db, ds, W_ATT), v_s.reshape(db, ds, W_ATT),
             jnp.transpose(cache_k[l], (0, 2, 3, 1)), jnp.transpose(cache_v[l], (0, 2, 3, 1)),
             ga_s.reshape(db, ds, W_ATT), _sample_bias_period(rel_bias[l], ds, lc)),
            wo, pg, wpg, wpe, BACK_ROWS)
        yp = yp.reshape(b, s, D_MODEL)
        outs[4].append(k_s.reshape(db, ds, N_HEADS, HEAD_DIM))
        outs[5].append(v_s.reshape(db, ds, N_HEADS, HEAD_DIM))
        outs[6].append(sc)
        outs[7].append(sh.reshape(db, W_LRU))
    return (yp, ys.reshape(db, ds, D_MODEL)) + tuple(jnp.stack(o) for o in outs)
```

```python
import functools

import jax
import jax.numpy as jnp
from jax import lax
from jax.experimental import pallas as pl
from jax.experimental.pallas import tpu as pltpu

D_MODEL = 1024
CHUNK = 64
PAST_CHUNKS = 8
BAND = PAST_CHUNKS * CHUNK
W_LRU = D_MODEL // 2
LRU_BLOCKS = 8
CONV_W = 4
RG_C = 8.0
HEAD_DIM = 64
W_ATT = D_MODEL // 2
N_HEADS = W_ATT // HEAD_DIM
MAX_REL = 256
PLE_DIM = 256
EPS = 1e-6
NEG = -1e30
LOG2E = 1.4426950408889634

SUBLANES = 8
LANES = 128
HEAD_GROUP = 256
HEADS_PER_GROUP = HEAD_GROUP // HEAD_DIM
N_GROUPS = W_ATT // HEAD_GROUP
QBLOCK = 256
KEY_TILES = BAND // QBLOCK + 1
PROMPT_BIAS_PERIOD = (KEY_TILES + 1) * QBLOCK
FRONT_BLOCK = 1024
FRONT_SLOTS = 2
BACK_ROWS = 1024
SAMPLE_STREAMS_PER_BLOCK = 8
VMEM_LIMIT = 56 * 1024 * 1024

F32 = jnp.float32
BF16 = jnp.bfloat16


def _dot(a, b):
    return jnp.dot(a, b, preferred_element_type=F32)


def _dot_nt(a, b):
    return lax.dot_general(a, b, (((1,), (1,)), ((), ())), preferred_element_type=F32)


def _rms_rows(x, g):
    ms = jnp.mean(x * x, axis=-1, keepdims=True)
    return x * lax.rsqrt(ms + EPS) * g


def _head_norm_rows(x, ones_bd, g_tiled):
    x2 = x * x
    hi = x2.astype(BF16)
    lo = (x2 - hi.astype(F32)).astype(BF16)
    ms = jnp.concatenate(
        [_dot(hi[:, g * HEAD_GROUP:(g + 1) * HEAD_GROUP], ones_bd)
         + _dot(lo[:, g * HEAD_GROUP:(g + 1) * HEAD_GROUP], ones_bd) for g in range(N_GROUPS)], axis=1)
    return x * lax.rsqrt(ms + EPS) * g_tiled


def _scan_rows(a_ref, u_ref, h_ref, row0, nrows, h0, unroll=False):
    ridx = lax.broadcasted_iota(jnp.int32, (SUBLANES, W_LRU), 0)

    def body(i, hprev):
        r = pl.multiple_of(row0 + i * SUBLANES, SUBLANES)
        a = a_ref[pl.ds(r, SUBLANES), :]
        u = u_ref[pl.ds(r, SUBLANES), :]
        for s in (1, 2, 4):
            a_s = jnp.where(ridx >= s, pltpu.roll(a, s, 0), 1.0)
            u_s = jnp.where(ridx >= s, pltpu.roll(u, s, 0), 0.0)
            u = a * u_s + u
            a = a * a_s
        h = a * hprev + u
        h_ref[pl.ds(r, SUBLANES), :] = h
        return h[SUBLANES - 1:SUBLANES, :]

    return lax.fori_loop(0, nrows // SUBLANES, body, h0, unroll=unroll)


def _lru_inputs(xc, wg_ref, bg_ref, lam_ref, a_ref, u_ref):
    xcb = xc.astype(BF16)
    half = wg_ref.shape[1]
    for j in range(wg_ref.shape[0]):
        sl = slice(j * half, (j + 1) * half)
        gates = _dot(xcb[:, sl], wg_ref[j]) + bg_ref[j]
        r = jax.nn.sigmoid(gates[:, :half])
        i = jax.nn.sigmoid(gates[:, half:])
        log_a = -RG_C * r * jax.nn.softplus(-lam_ref[:, sl])
        a = jnp.exp(log_a)
        a_ref[:, sl] = a
        u_ref[:, sl] = jnp.sqrt(jnp.tanh(-log_a) * (1.0 + a * a)) * (i * xc[:, sl])


def _conv_rows(cb_ref, base, nrows, cw_ref, cb_bias):
    out = cb_bias + cw_ref[CONV_W - 1:CONV_W, :] * cb_ref[pl.ds(base + SUBLANES, nrows), :]
    for k in range(CONV_W - 1):
        shift = CONV_W - 1 - k
        out = out + cw_ref[k:k + 1, :] * cb_ref[pl.ds(base + SUBLANES - shift, nrows), :]
    return out


def _front_prompt_kernel(x_ref, ng_ref, win_ref, cw_ref, cbias_ref, wg_ref, bg_ref, lam_ref,
                         qg_ref, kg_ref, ones_ref,
                         lru_ref, qt_ref, kbf_ref, vt_ref, gat_ref, pk_ref, pv_ref, pc_ref, ph_ref,
                         wnat_scr, wt_scr, zn_scr, zt_scr, cb_scr, a_scr, u_scr, h_scr, hlast_scr, *, keep_subs):
    sub = QBLOCK
    n_sub = x_ref.shape[1] // sub

    @pl.when((pl.program_id(0) == 0) & (pl.program_id(1) == 0))
    def _():
        o = 2 * W_LRU
        wnat_scr[:, 0:o] = win_ref[:, 0:o]
        wnat_scr[:, o:o + W_ATT] = win_ref[:, o + W_ATT:o + 2 * W_ATT]
        for n, c0 in enumerate((o, o + 2 * W_ATT, o + 3 * W_ATT)):
            wt_scr[n * W_ATT:(n + 1) * W_ATT, :] = win_ref[:, c0:c0 + W_ATT].T

    @pl.when(pl.program_id(1) == 0)
    def _():
        cb_scr[0:SUBLANES, :] = jnp.zeros((SUBLANES, W_LRU), F32)
        hlast_scr[...] = jnp.zeros((1, W_LRU), F32)

    def project(i):
        xn = _rms_rows(x_ref[0, i * sub:(i + 1) * sub, :], ng_ref[...]).astype(BF16)
        zn_scr[i % FRONT_SLOTS] = _dot(xn, wnat_scr[...])
        zt_scr[i % FRONT_SLOTS] = _dot_nt(wt_scr[...], xn)

    def finish(i):
        rows = slice(i * sub, (i + 1) * sub)
        zn = zn_scr.at[i % FRONT_SLOTS]
        zt = zt_scr.at[i % FRONT_SLOTS]
        cb_scr[SUBLANES:SUBLANES + sub, :] = zn[:, :W_LRU]
        xc = _conv_rows(cb_scr, 0, sub, cw_ref, cbias_ref[...])
        pc_ref[0] = cb_scr[sub + SUBLANES - (CONV_W - 1):sub + SUBLANES, :]
        cb_scr[0:SUBLANES, :] = cb_scr[sub:sub + SUBLANES, :]
        _lru_inputs(xc, wg_ref, bg_ref, lam_ref, a_scr, u_scr)
        h_last = _scan_rows(a_scr, u_scr, h_scr, 0, sub, hlast_scr[...], unroll=True)
        hlast_scr[...] = h_last
        ph_ref[0] = h_last
        lru_ref[0, rows, :] = (h_scr[...] * jax.nn.silu(zn[:, W_LRU:2 * W_LRU])).astype(BF16)
        k = _head_norm_rows(zn[:, 2 * W_LRU:], ones_ref[...], kg_ref[...])
        kbf_ref[0, rows, :] = k.astype(BF16)
        q3 = zt[0:W_ATT, :].reshape(N_HEADS, HEAD_DIM, sub)
        ms = jnp.mean(q3 * q3, axis=1, keepdims=True)
        qn = q3 * lax.rsqrt(ms + EPS) * (qg_ref[...] * (HEAD_DIM ** -0.5 * LOG2E))
        qt_ref[0, :, rows] = qn.reshape(W_ATT, sub).astype(BF16)
        vt = zt[W_ATT:2 * W_ATT, :]
        vt_ref[0, :, rows] = vt.astype(BF16)
        gat_ref[0, :, rows] = jax.nn.silu(zt[2 * W_ATT:, :])
        if i >= n_sub - keep_subs:
            first = (i - (n_sub - keep_subs)) * sub
            pk_ref[0, :, first:first + sub] = k.T
            pv_ref[0, :, first:first + sub] = vt

    lead = FRONT_SLOTS - 1
    for i in range(min(lead, n_sub)):
        project(i)
    for i in range(n_sub):
        if i + lead < n_sub:
            project(i + lead)
        finish(i)


def _front_prompt(x, ng, win, cw, cbias, wg, bg, lam, qg, kg, ones_bd):
    b, s, _ = x.shape
    tb = min(FRONT_BLOCK, s)
    keep = min(BAND, s)
    assert s % tb == 0 and tb % QBLOCK == 0 and keep % QBLOCK == 0 and keep <= tb
    const = lambda shape: pl.BlockSpec(shape, lambda i, j: (0,) * len(shape))
    rows_spec = lambda w: pl.BlockSpec((1, tb, w), lambda i, j: (i, j, 0))
    cols_spec = pl.BlockSpec((1, W_ATT, tb), lambda i, j: (i, 0, j))
    per_seq = lambda r, w: pl.BlockSpec((1, r, w), lambda i, j: (i, 0, 0))
    kern = functools.partial(_front_prompt_kernel, keep_subs=keep // QBLOCK)
    return pl.pallas_call(
        kern,
        grid=(b, s // tb),
        in_specs=[
            rows_spec(D_MODEL),
            const((1, D_MODEL)),
            pl.BlockSpec((D_MODEL, 2 * W_LRU + 4 * W_ATT), lambda i, j: (0, 0), pipeline_mode=pl.Buffered(1)),
            const((CONV_W, W_LRU)),
            const((1, W_LRU)),
            const((W_LRU // HEAD_GROUP, HEAD_GROUP, 2 * HEAD_GROUP)),
            const((W_LRU // HEAD_GROUP, 1, 2 * HEAD_GROUP)),
            const((1, W_LRU)),
            const((HEAD_DIM, 1)),
            const((1, W_ATT)),
            const((HEAD_GROUP, HEAD_GROUP)),
        ],
        out_specs=[
            rows_spec(W_LRU),
            cols_spec,
            rows_spec(W_ATT),
            cols_spec,
            cols_spec,
            per_seq(W_ATT, keep),
            per_seq(W_ATT, keep),
            per_seq(CONV_W - 1, W_LRU),
            per_seq(1, W_LRU),
        ],
        out_shape=[
            jax.ShapeDtypeStruct((b, s, W_LRU), BF16),
            jax.ShapeDtypeStruct((b, W_ATT, s), BF16),
            jax.ShapeDtypeStruct((b, s, W_ATT), BF16),
            jax.ShapeDtypeStruct((b, W_ATT, s), BF16),
            jax.ShapeDtypeStruct((b, W_ATT, s), F32),
            jax.ShapeDtypeStruct((b, W_ATT, keep), F32),
            jax.ShapeDtypeStruct((b, W_ATT, keep), F32),
            jax.ShapeDtypeStruct((b, CONV_W - 1, W_LRU), F32),
            jax.ShapeDtypeStruct((b, 1, W_LRU), F32),
        ],
        scratch_shapes=[
            pltpu.VMEM((D_MODEL, 3 * W_LRU), BF16),
            pltpu.VMEM((3 * W_ATT, D_MODEL), BF16),
            pltpu.VMEM((FRONT_SLOTS, QBLOCK, 3 * W_LRU), F32),
            pltpu.VMEM((FRONT_SLOTS, 3 * W_ATT, QBLOCK), F32),
            pltpu.VMEM((QBLOCK + SUBLANES, W_LRU), F32),
            pltpu.VMEM((QBLOCK, W_LRU), F32),
            pltpu.VMEM((QBLOCK, W_LRU), F32),
            pltpu.VMEM((QBLOCK, W_LRU), F32),
            pltpu.VMEM((1, W_LRU), F32),
        ],
        compiler_params=pltpu.CompilerParams(
            dimension_semantics=("arbitrary", "arbitrary"), vmem_limit_bytes=VMEM_LIMIT),
        name="front_prompt",
    )(x, ng, win, cw, cbias, wg, bg, lam, qg, kg, ones_bd)


def _toeplitz_rows(w_row, nrows, row0):
    x = jnp.broadcast_to(w_row, (nrows, w_row.shape[1]))
    return pltpu.roll(x, row0, 1, stride=1, stride_axis=0)


def _fill_prompt_bias(w_ref, bias_ref):
    q_chunk = (lax.broadcasted_iota(jnp.int32, (CHUNK, QBLOCK), 1) + BAND) // CHUNK
    for h in range(N_HEADS):
        def body(n, carry, h=h):
            r0 = pl.multiple_of(n * CHUNK, CHUNK)
            t = _toeplitz_rows(w_ref[h], CHUNK, r0)[:, :QBLOCK]
            dc = q_chunk - n
            bias_ref[h, pl.ds(r0, CHUNK), :] = jnp.where((dc >= 0) & (dc <= PAST_CHUNKS), t * LOG2E, NEG)
            return carry
        lax.fori_loop(0, KEY_TILES * QBLOCK // CHUNK, body, 0)


SCORE_LEAD = 3
ATTN_BLOCKS = 4
CHUNKS_PER_TILE = QBLOCK // CHUNK
FRAMES_PER_VREG = LANES // CHUNK


def _lane_cols(kc):
    cols = []
    for c in range(QBLOCK // LANES):
        q_lo = PAST_CHUNKS + c * FRAMES_PER_VREG
        q_hi = q_lo + FRAMES_PER_VREG - 1
        if q_lo - PAST_CHUNKS <= kc <= q_hi:
            cols.append(c)
    return cols


def _fold_rows(x):
    return x.reshape(x.shape[0] // SUBLANES, SUBLANES, x.shape[1])


def _attn_prompt_blocks(blocks, qt_ref, k_ref, vt_ref, bias_ref, s_scr, att_scr):
    n_cols = QBLOCK // LANES
    rows = lax.broadcasted_iota(jnp.int32, (HEAD_GROUP, QBLOCK), 0)
    units = [(n, h) for n in range(len(blocks)) for h in range(N_HEADS)]

    def pieces(tiles):
        return [(i, cc, slice((i * CHUNKS_PER_TILE + cc) * CHUNK, (i * CHUNKS_PER_TILE + cc + 1) * CHUNK),
                 slice(c * LANES, (c + 1) * LANES), c)
                for i in tiles for cc in range(CHUNKS_PER_TILE) for c in _lane_cols(i * CHUNKS_PER_TILE + cc)]

    def key_rows(n, i):
        tiles, key0 = blocks[n]
        return pl.ds(key0 + (i - tiles[0]) * QBLOCK, QBLOCK)

    def scores(u):
        n, h = units[u]
        g, hl = divmod(h, HEADS_PER_GROUP)
        gsl = slice(g * HEAD_GROUP, (g + 1) * HEAD_GROUP)
        in_head = (rows >= hl * HEAD_DIM) & (rows < (hl + 1) * HEAD_DIM)
        qm = jnp.where(in_head, qt_ref[0, gsl, n * QBLOCK:(n + 1) * QBLOCK], jnp.zeros((), BF16))
        s = {i: _dot(k_ref[0, key_rows(n, i), gsl], qm) for i in blocks[n][0]}
        m_acc = [jnp.full((SUBLANES, LANES), NEG, F32) for _ in range(n_cols)]
        for i, cc, rsl, lsl, c in pieces(blocks[n][0]):
            sp = s[i][cc * CHUNK:(cc + 1) * CHUNK, lsl] + bias_ref[h, rsl, lsl]
            s_scr[u % (SCORE_LEAD + 1), rsl, lsl] = sp
            m_acc[c] = jnp.maximum(m_acc[c], jnp.max(_fold_rows(sp), axis=0))
        return [jnp.max(a, axis=0, keepdims=True) for a in m_acc]

    def weights(u, m):
        tiles_p = {}
        for i in blocks[units[u][0]][0]:
            chunks = []
            for cc in range(CHUNKS_PER_TILE):
                kc = i * CHUNKS_PER_TILE + cc
                rsl = slice(kc * CHUNK, (kc + 1) * CHUNK)
                cols = []
                for c in range(n_cols):
                    if c in _lane_cols(kc):
                        lsl = slice(c * LANES, (c + 1) * LANES)
                        cols.append(jnp.exp2(s_scr[u % (SCORE_LEAD + 1), rsl, lsl] - m[c]).astype(BF16))
                    else:
                        cols.append(jnp.zeros((CHUNK, LANES), BF16))
                chunks.append(jnp.concatenate(cols, axis=1))
            tiles_p[i] = jnp.concatenate(chunks, axis=0)
        return tiles_p

    ones_rows = jnp.ones((2 * SUBLANES, QBLOCK), BF16)

    def values(u, tiles_p):
        n, h = units[u]
        hsl = slice(h * HEAD_DIM, (h + 1) * HEAD_DIM)
        o = None
        for i in blocks[n][0]:
            oi = _dot(jnp.concatenate([vt_ref[0, hsl, key_rows(n, i)], ones_rows], axis=0), tiles_p[i])
            o = oi if o is None else o + oi
        att_scr[n, hsl, :] = o[0:HEAD_DIM, :] * (1.0 / o[HEAD_DIM:HEAD_DIM + 1, :])

    m = {u: scores(u) for u in range(min(SCORE_LEAD, len(units)))}
    w = {}
    for u in range(len(units)):
        if u + SCORE_LEAD < len(units):
            m[u + SCORE_LEAD] = scores(u + SCORE_LEAD)
        w[u] = weights(u, m.pop(u))
        if u >= 1:
            values(u - 1, w.pop(u - 1))
    values(len(units) - 1, w.pop(len(units) - 1))


def _attn_prompt_kernel(qt_ref, k_ref, vt_ref, w_ref, gat_ref, out_ref, bias_ref, s_scr, att_scr):
    step = pl.program_id(1)

    @pl.when((pl.program_id(0) == 0) & (step == 0))
    def _():
        _fill_prompt_bias(w_ref, bias_ref)

    assert ATTN_BLOCKS >= KEY_TILES - 1
    full = tuple(range(KEY_TILES))

    @pl.when(step == 0)
    def _():
        blocks = [(full[max(KEY_TILES - 1 - n, 0):], max(n - (KEY_TILES - 1), 0) * QBLOCK)
                  for n in range(ATTN_BLOCKS)]
        _attn_prompt_blocks(blocks, qt_ref, k_ref, vt_ref, bias_ref, s_scr, att_scr)

    if k_ref.shape[1] > ATTN_BLOCKS * QBLOCK:
        @pl.when(step > 0)
        def _():
            first = step * ATTN_BLOCKS - (KEY_TILES - 1)
            blocks = [(full, pl.multiple_of((first + n) * QBLOCK, QBLOCK)) for n in range(ATTN_BLOCKS)]
            _attn_prompt_blocks(blocks, qt_ref, k_ref, vt_ref, bias_ref, s_scr, att_scr)

    for n in range(ATTN_BLOCKS):
        fr = slice(n * QBLOCK, (n + 1) * QBLOCK)
        out_ref[0, fr, :] = (att_scr[n] * gat_ref[0, :, fr]).T.astype(BF16)


def _attn_prompt(qt, kbf, vt, w_bias, gat):
    b, _, s = qt.shape
    fb = ATTN_BLOCKS * QBLOCK
    assert s % fb == 0
    return pl.pallas_call(
        _attn_prompt_kernel,
        grid=(b, s // fb),
        in_specs=[
            pl.BlockSpec((1, W_ATT, fb), lambda i, j: (i, 0, j)),
            pl.BlockSpec((1, s, W_ATT), lambda i, j: (i, 0, 0)),
            pl.BlockSpec((1, W_ATT, s), lambda i, j: (i, 0, 0)),
            pl.BlockSpec((N_HEADS, 1, PROMPT_BIAS_PERIOD), lambda i, j: (0, 0, 0)),
            pl.BlockSpec((1, W_ATT, fb), lambda i, j: (i, 0, j)),
        ],
        out_specs=pl.BlockSpec((1, fb, W_ATT), lambda i, j: (i, j, 0)),
        out_shape=jax.ShapeDtypeStruct((b, s, W_ATT), BF16),
        scratch_shapes=[
            pltpu.VMEM((N_HEADS, KEY_TILES * QBLOCK, QBLOCK), F32),
            pltpu.VMEM((SCORE_LEAD + 1, KEY_TILES * QBLOCK, QBLOCK), F32),
            pltpu.VMEM((ATTN_BLOCKS, W_ATT, QBLOCK), F32),
        ],
        compiler_params=pltpu.CompilerParams(
            dimension_semantics=("arbitrary", "arbitrary"), vmem_limit_bytes=VMEM_LIMIT),
        name="attn_prompt",
    )(qt, kbf, vt, w_bias, gat)


def _back_rows(x_ref, lru_ref, att_ref, p_ref, wo_ref, pg_ref, wpg_ref, wpe_ref, y_ref):
    mix = _dot(lru_ref[...], wo_ref[0:W_LRU, :]) + _dot(att_ref[...], wo_ref[W_LRU:, :])
    h = x_ref[...] + mix
    gate = jax.nn.sigmoid(_dot(_rms_rows(h, pg_ref[...]).astype(BF16), wpg_ref[...]))
    y_ref[...] = h + _dot(p_ref[...].astype(BF16), wpe_ref[...]) * gate


def _back_kernel(xp_ref, lrup_ref, attp_ref, pp_ref, xs_ref, lrus_ref, ps_ref,
                 q_ref, kn_ref, vn_ref, kc_ref, vc_ref, ga_ref, wb_ref,
                 wo_ref, pg_ref, wpg_ref, wpe_ref, yp_ref, ys_ref, bias_scr, atts_scr, *, prompt_steps):
    step = pl.program_id(0)
    weights = (wo_ref, pg_ref, wpg_ref, wpe_ref)
    streams, tt = q_ref.shape[0], q_ref.shape[1]

    @pl.when(step == 0)
    def _():
        _fill_sample_bias(wb_ref, bias_scr, tt)

    @pl.when(step < prompt_steps)
    def _():
        def write(s, lanes, rows):
            atts_scr[pl.ds(pl.multiple_of((step * streams + s) * tt, tt), tt), lanes] = rows

        _attn_sample_streams(q_ref, kn_ref, vn_ref, kc_ref, vc_ref, ga_ref, bias_scr, write)
        _back_rows(xp_ref, lrup_ref, attp_ref, pp_ref, *weights, yp_ref)

    @pl.when(step == prompt_steps)
    def _():
        _back_rows(xs_ref, lrus_ref, atts_scr, ps_ref, *weights, ys_ref)


def _back(prompt, sample, sample_attn, wo, pg, wpg, wpe, rows):
    n, ns = prompt[0].shape[0], sample[0].shape[0]
    q3, k3, v3, kc_t, vc_t, ga3, w_bias = sample_attn
    db, tt, _ = q3.shape
    assert n % rows == 0 and db * tt == ns
    steps = n // rows
    assert db % steps == 0
    per_step = db // steps
    const = lambda shape: pl.BlockSpec(shape, lambda i: (0,) * len(shape))
    walk = lambda w: pl.BlockSpec((rows, w), lambda i: (jnp.minimum(i, steps - 1), 0))
    whole = lambda w: pl.BlockSpec((ns, w), lambda i: (0, 0), pipeline_mode=pl.Buffered(1))
    seq = pl.BlockSpec((per_step, tt, W_ATT), lambda i: (jnp.minimum(i, steps - 1), 0, 0))
    cache = pl.BlockSpec((per_step, N_HEADS, HEAD_DIM, kc_t.shape[-1]),
                         lambda i: (jnp.minimum(i, steps - 1), 0, 0, 0))
    period = w_bias.shape[-1]
    return pl.pallas_call(
        functools.partial(_back_kernel, prompt_steps=steps),
        grid=(steps + 1,),
        in_specs=[walk(w) for w in (D_MODEL, W_LRU, W_ATT, PLE_DIM)]
        + [whole(w) for w in (D_MODEL, W_LRU, PLE_DIM)]
        + [seq, seq, seq, cache, cache, seq, const((N_HEADS, 1, period))]
        + [const((W_LRU + W_ATT, D_MODEL)), const((1, D_MODEL)), const((D_MODEL, D_MODEL)),
           const((PLE_DIM, D_MODEL))],
        out_specs=[walk(D_MODEL), pl.BlockSpec((ns, D_MODEL), lambda i: (0, 0))],
        out_shape=[jax.ShapeDtypeStruct((n, D_MODEL), F32), jax.ShapeDtypeStruct((ns, D_MODEL), F32)],
        scratch_shapes=[
            pltpu.VMEM((N_GROUPS, HEADS_PER_GROUP * tt, period), F32),
            pltpu.VMEM((ns, W_ATT), BF16),
        ],
        compiler_params=pltpu.CompilerParams(
            dimension_semantics=("arbitrary",), vmem_limit_bytes=VMEM_LIMIT),
        name="back",
    )(*prompt, *sample, q3, k3, v3, kc_t, vc_t, ga3, w_bias, wo, pg, wpg, wpe)


def _front_sample_kernel(x_ref, ng_ref, win_ref, cw_ref, cbias_ref, wg_ref, bg_ref, lam_ref,
                         qg_ref, kg_ref, ones_ref, sconv_ref, slru_ref,
                         lru_ref, q_ref, k_ref, v_ref, ga_ref, sc_ref, sh_ref,
                         cb_scr, xc_scr, a_scr, u_scr, h_scr, *, nb, tt):
    seg = tt + SUBLANES
    xn = _rms_rows(x_ref[...], ng_ref[...]).astype(BF16)
    z = _dot(xn, win_ref[...])
    xl = z[:, :W_LRU]
    for s in range(nb):
        cb_scr[s * seg:s * seg + SUBLANES, :] = sconv_ref[s]
        cb_scr[s * seg + SUBLANES:(s + 1) * seg, :] = xl[s * tt:(s + 1) * tt, :]
        xc_scr[s * tt:(s + 1) * tt, :] = _conv_rows(cb_scr, s * seg, tt, cw_ref, cbias_ref[...])
        sc_ref[s] = cb_scr[(s + 1) * seg - (CONV_W - 1):(s + 1) * seg, :]
    _lru_inputs(xc_scr[...], wg_ref, bg_ref, lam_ref, a_scr, u_scr)
    for s in range(nb):
        sh_ref[s] = _scan_rows(a_scr, u_scr, h_scr, s * tt, tt, slru_ref[s])
    lru_ref[...] = (h_scr[...] * jax.nn.silu(z[:, W_LRU:2 * W_LRU])).astype(BF16)

    o = 2 * W_LRU
    q = _head_norm_rows(z[:, o:o + W_ATT], ones_ref[...], qg_ref[...])
    q_ref[...] = (q * (HEAD_DIM ** -0.5)).astype(BF16)
    k_ref[...] = _head_norm_rows(z[:, o + W_ATT:o + 2 * W_ATT], ones_ref[...], kg_ref[...])
    v_ref[...] = z[:, o + 2 * W_ATT:o + 3 * W_ATT]
    ga_ref[...] = jax.nn.silu(z[:, o + 3 * W_ATT:])


def _front_sample(x2, ng, win, cw, cbias, wg, bg, lam, qg_t, kg_t, ones_bd, sconv_pad, slru, nb, tt):
    n = x2.shape[0]
    rows = nb * tt
    const = lambda shape: pl.BlockSpec(shape, lambda i: (0,) * len(shape))
    row_spec = lambda w: pl.BlockSpec((rows, w), lambda i: (i, 0))
    nseq = n // tt
    kern = functools.partial(_front_sample_kernel, nb=nb, tt=tt)
    return pl.pallas_call(
        kern,
        grid=(n // rows,),
        in_specs=[
            row_spec(D_MODEL),
            const((1, D_MODEL)),
            const((D_MODEL, 2 * W_LRU + 4 * W_ATT)),
            const((CONV_W, W_LRU)),
            const((1, W_LRU)),
            const((W_LRU // HEAD_GROUP, HEAD_GROUP, 2 * HEAD_GROUP)),
            const((W_LRU // HEAD_GROUP, 1, 2 * HEAD_GROUP)),
            const((1, W_LRU)),
            const((1, W_ATT)),
            const((1, W_ATT)),
            const((HEAD_GROUP, HEAD_GROUP)),
            pl.BlockSpec((nb, SUBLANES, W_LRU), lambda i: (i, 0, 0)),
            pl.BlockSpec((nb, 1, W_LRU), lambda i: (i, 0, 0)),
        ],
        out_specs=[
            row_spec(W_LRU), row_spec(W_ATT), row_spec(W_ATT), row_spec(W_ATT), row_spec(W_ATT),
            pl.BlockSpec((nb, CONV_W - 1, W_LRU), lambda i: (i, 0, 0)),
            pl.BlockSpec((nb, 1, W_LRU), lambda i: (i, 0, 0)),
        ],
        out_shape=[
            jax.ShapeDtypeStruct((n, W_LRU), BF16),
            jax.ShapeDtypeStruct((n, W_ATT), BF16),
            jax.ShapeDtypeStruct((n, W_ATT), F32),
            jax.ShapeDtypeStruct((n, W_ATT), F32),
            jax.ShapeDtypeStruct((n, W_ATT), F32),
            jax.ShapeDtypeStruct((nseq, CONV_W - 1, W_LRU), F32),
            jax.ShapeDtypeStruct((nseq, 1, W_LRU), F32),
        ],
        scratch_shapes=[
            pltpu.VMEM((nb * (tt + SUBLANES), W_LRU), F32),
            pltpu.VMEM((rows, W_LRU), F32),
            pltpu.VMEM((rows, W_LRU), F32),
            pltpu.VMEM((rows, W_LRU), F32),
            pltpu.VMEM((rows, W_LRU), F32),
        ],
        compiler_params=pltpu.CompilerParams(
            dimension_semantics=("arbitrary",), vmem_limit_bytes=VMEM_LIMIT),
        name="front_sample",
    )(x2, ng, win, cw, cbias, wg, bg, lam, qg_t, kg_t, ones_bd, sconv_pad, slru)


def _fill_sample_bias(w_ref, bias_ref, tt):
    for h in range(N_HEADS):
        g, hl = divmod(h, HEADS_PER_GROUP)
        bias_ref[g, hl * tt:(hl + 1) * tt, :] = _toeplitz_rows(w_ref[h], tt, 0)


def _attn_sample_streams(q_ref, kn_ref, vn_ref, kc_ref, vc_ref, ga_ref, bias_ref, write):
    tt = q_ref.shape[1]
    lc = kc_ref.shape[-1]
    lanes = lax.broadcasted_iota(jnp.int32, (tt, HEAD_GROUP), 1)
    masks = [(lanes >= hl * HEAD_DIM) & (lanes < (hl + 1) * HEAD_DIM) for hl in range(HEADS_PER_GROUP)]
    units = [(s, g) for s in range(q_ref.shape[0]) for g in range(N_GROUPS)]

    def cached(ref, s, g):
        heads = ref[s, g * HEADS_PER_GROUP:(g + 1) * HEADS_PER_GROUP]
        return heads.reshape(HEAD_GROUP, lc).astype(BF16)

    def scores(s, g):
        gsl = slice(g * HEAD_GROUP, (g + 1) * HEAD_GROUP)
        qg = q_ref[s, :, gsl]
        qs = jnp.concatenate([jnp.where(m, qg, jnp.zeros((), BF16)) for m in masks], axis=0)
        sc = _dot(qs, cached(kc_ref, s, g)) + bias_ref[g, :, 0:lc]
        sn = _dot_nt(qs, kn_ref[s, :, gsl].astype(BF16)) + bias_ref[g, :, lc:lc + tt]
        return sc, sn

    def finish(s, g, sc, sn):
        gsl = slice(g * HEAD_GROUP, (g + 1) * HEAD_GROUP)
        m = jnp.maximum(jnp.max(sc, axis=-1, keepdims=True), jnp.max(sn, axis=-1, keepdims=True))
        pc = jnp.exp(sc - m)
        pn = jnp.exp(sn - m)
        l = jnp.sum(pc, axis=-1, keepdims=True) + jnp.sum(pn, axis=-1, keepdims=True)
        o = _dot_nt(pc.astype(BF16), cached(vc_ref, s, g))
        o = (o + _dot(pn.astype(BF16), vn_ref[s, :, gsl].astype(BF16))) * (1.0 / l)
        att = jnp.zeros((tt, HEAD_GROUP), F32)
        for hl in range(HEADS_PER_GROUP):
            att = att + jnp.where(masks[hl], o[hl * tt:(hl + 1) * tt, :], 0.0)
        write(s, gsl, (att * ga_ref[s, :, gsl]).astype(BF16))

    nxt = scores(*units[0])
    for n, (s, g) in enumerate(units):
        cur = nxt
        if n + 1 < len(units):
            nxt = scores(*units[n + 1])
        finish(s, g, *cur)


def _block_diag(w):
    n, d, e = w.shape
    eye = jnp.eye(n, dtype=w.dtype)
    return (eye[:, None, :, None] * w[:, :, None, :]).reshape(n * d, n * e)


def _prompt_bias_period(table):
    assert BAND - MAX_REL == MAX_REL and PROMPT_BIAS_PERIOD == KEY_TILES * QBLOCK + QBLOCK
    last = table[2 * MAX_REL:]
    neg_d = jnp.concatenate([table, jnp.broadcast_to(last, (MAX_REL - 1, N_HEADS))])
    w = jnp.concatenate([jnp.broadcast_to(last, (QBLOCK, N_HEADS)), neg_d])
    return w.T.reshape(N_HEADS, 1, PROMPT_BIAS_PERIOD).astype(F32)


def _sample_bias_period(table, tt, l):
    assert l >= MAX_REL
    period = -(-(l + 2 * tt - 1) // LANES) * LANES
    last = table[2 * MAX_REL:]
    n_var = tt + MAX_REL - 1
    var = table[2 * MAX_REL - 1:2 * MAX_REL - 1 - n_var:-1]
    w = jnp.concatenate([jnp.broadcast_to(last, (l - MAX_REL + 1, N_HEADS)), var,
                         jnp.broadcast_to(last, (period - (l + tt), N_HEADS))])
    return w.T.reshape(N_HEADS, 1, period).astype(F32)


def kernel(x_prompt, x_sample, p_prompt, p_sample, cache_k, cache_v, state_conv, state_lru, norm_g, w_in, conv_w, conv_b, gate_a_w, gate_a_b, gate_x_w, gate_x_b, lru_lambda, q_norm_g, k_norm_g, rel_bias, w_out, ple_norm_g, w_ple_gate, w_ple_proj):
    depth = w_in.shape[0]
    b, s, _ = x_prompt.shape
    db, ds, _ = x_sample.shape
    lc = cache_k.shape[2]
    yp, ys = x_prompt, x_sample.reshape(db * ds, D_MODEL)
    ones_bd = _block_diag(jnp.full((HEADS_PER_GROUP, HEAD_DIM, HEAD_DIM), 1.0 / HEAD_DIM, F32)).astype(BF16)
    gate_halves = W_LRU // HEAD_GROUP
    blocks_per_half = LRU_BLOCKS // gate_halves
    outs = [[] for _ in range(8)]
    for l in range(depth):
        win = w_in[l].astype(BF16)
        ng = norm_g[l].reshape(1, D_MODEL)
        cw = conv_w[l]
        cbias = conv_b[l].reshape(1, W_LRU)
        wg = jnp.stack([
            jnp.concatenate([_block_diag(w[j * blocks_per_half:(j + 1) * blocks_per_half])
                             for w in (gate_a_w[l], gate_x_w[l])], axis=1)
            for j in range(gate_halves)]).astype(BF16)
        bg = jnp.concatenate([gate_a_b[l].reshape(gate_halves, 1, HEAD_GROUP),
                              gate_x_b[l].reshape(gate_halves, 1, HEAD_GROUP)], axis=2)
        lam = lru_lambda[l].reshape(1, W_LRU)
        qg_col = q_norm_g[l].reshape(HEAD_DIM, 1)
        qg_t = jnp.tile(q_norm_g[l], N_HEADS).reshape(1, W_ATT)
        kg_t = jnp.tile(k_norm_g[l], N_HEADS).reshape(1, W_ATT)
        wo = w_out[l].astype(BF16)
        pg = ple_norm_g[l].reshape(1, D_MODEL)
        wpg = w_ple_gate[l].astype(BF16)
        wpe = w_ple_proj[l].astype(BF16)

        lru_g, qt, kbf, vt, gat, pk, pv, pc, ph = _front_prompt(
            yp, ng, win, cw, cbias, wg, bg, lam, qg_col, kg_t, ones_bd)
        att_g = _attn_prompt(qt, kbf, vt, _prompt_bias_period(rel_bias[l]), gat)
        to_frames = lambda a: jnp.transpose(a.reshape(b, N_HEADS, HEAD_DIM, a.shape[-1]), (0, 3, 1, 2))
        outs[0].append(to_frames(pk))
        outs[1].append(to_frames(pv))
        outs[2].append(pc)
        outs[3].append(ph.reshape(b, W_LRU))

        sconv_pad = jnp.pad(state_conv[l], ((0, 0), (SUBLANES - (CONV_W - 1), 0), (0, 0)))
        lru_s, q_s, k_s, v_s, ga_s, sc, sh = _front_sample(
            ys, ng, win, cw, cbias, wg, bg, lam, qg_t, kg_t, ones_bd,
            sconv_pad, state_lru[l].reshape(db, 1, W_LRU), SAMPLE_STREAMS_PER_BLOCK, ds)
        yp, ys = _back(
            (yp.reshape(b * s, D_MODEL), lru_g.reshape(b * s, W_LRU), att_g.reshape(b * s, W_ATT),
             p_prompt[l].reshape(b * s, PLE_DIM)),
            (ys, lru_s, p_sample[l].reshape(db * ds, PLE_DIM)),
            (q_s.reshape(db, ds, W_ATT), k_s.reshape(db, ds, W_ATT), v_s.reshape(db, ds, W_ATT),
             jnp.transpose(cache_k[l], (0, 2, 3, 1)), jnp.transpose(cache_v[l], (0, 2, 3, 1)),
             ga_s.reshape(db, ds, W_ATT), _sample_bias_period(rel_bias[l], ds, lc)),
            wo, pg, wpg, wpe, BACK_ROWS)
        yp = yp.reshape(b, s, D_MODEL)
        outs[4].append(k_s.reshape(db, ds, N_HEADS, HEAD_DIM))
        outs[5].append(v_s.reshape(db, ds, N_HEADS, HEAD_DIM))
        outs[6].append(sc)
        outs[7].append(sh.reshape(db, W_LRU))
    return (yp, ys.reshape(db, ds, D_MODEL)) + tuple(jnp.stack(o) for o in outs)
```

```python
import functools

import jax
import jax.numpy as jnp
from jax import lax
from jax.experimental import pallas as pl
from jax.experimental.pallas import tpu as pltpu

D_MODEL = 1024
CHUNK = 64
PAST_CHUNKS = 8
BAND = PAST_CHUNKS * CHUNK
W_LRU = D_MODEL // 2
LRU_BLOCKS = 8
CONV_W = 4
RG_C = 8.0
HEAD_DIM = 64
W_ATT = D_MODEL // 2
N_HEADS = W_ATT // HEAD_DIM
MAX_REL = 256
PLE_DIM = 256
EPS = 1e-6
NEG = -1e30
LOG2E = 1.4426950408889634

SUBLANES = 8
LANES = 128
HEAD_GROUP = 256
HEADS_PER_GROUP = HEAD_GROUP // HEAD_DIM
N_GROUPS = W_ATT // HEAD_GROUP
QBLOCK = 256
KEY_TILES = BAND // QBLOCK + 1
PROMPT_BIAS_PERIOD = (KEY_TILES + 1) * QBLOCK
FRONT_BLOCK = 1024
BACK_ROWS = 1024
SAMPLE_STREAMS_PER_BLOCK = 8
VMEM_LIMIT = 56 * 1024 * 1024

F32 = jnp.float32
BF16 = jnp.bfloat16


def _dot(a, b):
    return jnp.dot(a, b, preferred_element_type=F32)


def _dot_nt(a, b):
    return lax.dot_general(a, b, (((1,), (1,)), ((), ())), preferred_element_type=F32)


def _rms_rows(x, g):
    ms = jnp.mean(x * x, axis=-1, keepdims=True)
    return x * lax.rsqrt(ms + EPS) * g


def _head_norm_rows(x, ones_bd, g_tiled):
    x2 = x * x
    hi = x2.astype(BF16)
    lo = (x2 - hi.astype(F32)).astype(BF16)
    ms = jnp.concatenate(
        [_dot(hi[:, g * HEAD_GROUP:(g + 1) * HEAD_GROUP], ones_bd)
         + _dot(lo[:, g * HEAD_GROUP:(g + 1) * HEAD_GROUP], ones_bd) for g in range(N_GROUPS)], axis=1)
    return x * lax.rsqrt(ms + EPS) * g_tiled


def _scan_rows(a_ref, u_ref, h_ref, row0, nrows, h0, unroll=False):
    ridx = lax.broadcasted_iota(jnp.int32, (SUBLANES, W_LRU), 0)

    def body(i, hprev):
        r = pl.multiple_of(row0 + i * SUBLANES, SUBLANES)
        a = a_ref[pl.ds(r, SUBLANES), :]
        u = u_ref[pl.ds(r, SUBLANES), :]
        for s in (1, 2, 4):
            a_s = jnp.where(ridx >= s, pltpu.roll(a, s, 0), 1.0)
            u_s = jnp.where(ridx >= s, pltpu.roll(u, s, 0), 0.0)
            u = a * u_s + u
            a = a * a_s
        h = a * hprev + u
        h_ref[pl.ds(r, SUBLANES), :] = h
        return h[SUBLANES - 1:SUBLANES, :]

    return lax.fori_loop(0, nrows // SUBLANES, body, h0, unroll=unroll)


def _lru_inputs(xc, wg_ref, bg_ref, lam_ref, a_ref, u_ref):
    xcb = xc.astype(BF16)
    half = wg_ref.shape[1]
    for j in range(wg_ref.shape[0]):
        sl = slice(j * half, (j + 1) * half)
        gates = _dot(xcb[:, sl], wg_ref[j]) + bg_ref[j]
        r = jax.nn.sigmoid(gates[:, :half])
        i = jax.nn.sigmoid(gates[:, half:])
        log_a = -RG_C * r * jax.nn.softplus(-lam_ref[:, sl])
        a = jnp.exp(log_a)
        a_ref[:, sl] = a
        u_ref[:, sl] = jnp.sqrt(jnp.tanh(-log_a) * (1.0 + a * a)) * (i * xc[:, sl])


def _conv_rows(cb_ref, base, nrows, cw_ref, cb_bias):
    out = cb_bias + cw_ref[CONV_W - 1:CONV_W, :] * cb_ref[pl.ds(base + SUBLANES, nrows), :]
    for k in range(CONV_W - 1):
        shift = CONV_W - 1 - k
        out = out + cw_ref[k:k + 1, :] * cb_ref[pl.ds(base + SUBLANES - shift, nrows), :]
    return out


def _front_prompt_kernel(x_ref, ng_ref, wnat_ref, wt_ref, cw_ref, cbias_ref, wg_ref, bg_ref, lam_ref,
                         qg_ref, kg_ref, ones_ref,
                         lru_ref, qt_ref, kbf_ref, vt_ref, gat_ref, pk_ref, pv_ref, pc_ref, ph_ref,
                         zn_scr, zt_scr, cb_scr, a_scr, u_scr, h_scr, hlast_scr, *, keep_subs):
    sub = QBLOCK
    n_sub = x_ref.shape[1] // sub

    @pl.when(pl.program_id(1) == 0)
    def _():
        cb_scr[0:SUBLANES, :] = jnp.zeros((SUBLANES, W_LRU), F32)
        hlast_scr[...] = jnp.zeros((1, W_LRU), F32)

    def project(i):
        xn = _rms_rows(x_ref[0, i * sub:(i + 1) * sub, :], ng_ref[...]).astype(BF16)
        zn_scr[i % 2] = _dot(xn, wnat_ref[...])
        zt_scr[i % 2] = _dot_nt(wt_ref[...], xn)

    def finish(i):
        rows = slice(i * sub, (i + 1) * sub)
        zn = zn_scr.at[i % 2]
        zt = zt_scr.at[i % 2]
        cb_scr[SUBLANES:SUBLANES + sub, :] = zn[:, :W_LRU]
        xc = _conv_rows(cb_scr, 0, sub, cw_ref, cbias_ref[...])
        pc_ref[0] = cb_scr[sub + SUBLANES - (CONV_W - 1):sub + SUBLANES, :]
        cb_scr[0:SUBLANES, :] = cb_scr[sub:sub + SUBLANES, :]
        _lru_inputs(xc, wg_ref, bg_ref, lam_ref, a_scr, u_scr)
        h_last = _scan_rows(a_scr, u_scr, h_scr, 0, sub, hlast_scr[...], unroll=True)
        hlast_scr[...] = h_last
        ph_ref[0] = h_last
        lru_ref[0, rows, :] = (h_scr[...] * jax.nn.silu(zn[:, W_LRU:2 * W_LRU])).astype(BF16)
        k = _head_norm_rows(zn[:, 2 * W_LRU:], ones_ref[...], kg_ref[...])
        kbf_ref[0, rows, :] = k.astype(BF16)
        q3 = zt[0:W_ATT, :].reshape(N_HEADS, HEAD_DIM, sub)
        ms = jnp.mean(q3 * q3, axis=1, keepdims=True)
        qn = q3 * lax.rsqrt(ms + EPS) * (qg_ref[...] * (HEAD_DIM ** -0.5 * LOG2E))
        qt_ref[0, :, rows] = qn.reshape(W_ATT, sub).astype(BF16)
        vt = zt[W_ATT:2 * W_ATT, :]
        vt_ref[0, :, rows] = vt.astype(BF16)
        gat_ref[0, :, rows] = jax.nn.silu(zt[2 * W_ATT:, :])
        if i >= n_sub - keep_subs:
            first = (i - (n_sub - keep_subs)) * sub
            pk_ref[0, :, first:first + sub] = k.T
            pv_ref[0, :, first:first + sub] = vt

    project(0)
    for i in range(n_sub):
        if i + 1 < n_sub:
            project(i + 1)
        finish(i)


def _front_prompt(x, ng, wnat, wt, cw, cbias, wg, bg, lam, qg, kg, ones_bd):
    b, s, _ = x.shape
    tb = min(FRONT_BLOCK, s)
    keep = min(BAND, s)
    assert s % tb == 0 and tb % QBLOCK == 0 and keep % QBLOCK == 0 and keep <= tb
    const = lambda shape: pl.BlockSpec(shape, lambda i, j: (0,) * len(shape))
    rows_spec = lambda w: pl.BlockSpec((1, tb, w), lambda i, j: (i, j, 0))
    cols_spec = pl.BlockSpec((1, W_ATT, tb), lambda i, j: (i, 0, j))
    per_seq = lambda r, w: pl.BlockSpec((1, r, w), lambda i, j: (i, 0, 0))
    kern = functools.partial(_front_prompt_kernel, keep_subs=keep // QBLOCK)
    return pl.pallas_call(
        kern,
        grid=(b, s // tb),
        in_specs=[
            rows_spec(D_MODEL),
            const((1, D_MODEL)),
            const((D_MODEL, 3 * W_LRU)),
            const((3 * W_ATT, D_MODEL)),
            const((CONV_W, W_LRU)),
            const((1, W_LRU)),
            const((W_LRU // HEAD_GROUP, HEAD_GROUP, 2 * HEAD_GROUP)),
            const((W_LRU // HEAD_GROUP, 1, 2 * HEAD_GROUP)),
            const((1, W_LRU)),
            const((HEAD_DIM, 1)),
            const((1, W_ATT)),
            const((HEAD_GROUP, HEAD_GROUP)),
        ],
        out_specs=[
            rows_spec(W_LRU),
            cols_spec,
            rows_spec(W_ATT),
            cols_spec,
            cols_spec,
            per_seq(W_ATT, keep),
            per_seq(W_ATT, keep),
            per_seq(CONV_W - 1, W_LRU),
            per_seq(1, W_LRU),
        ],
        out_shape=[
            jax.ShapeDtypeStruct((b, s, W_LRU), BF16),
            jax.ShapeDtypeStruct((b, W_ATT, s), BF16),
            jax.ShapeDtypeStruct((b, s, W_ATT), BF16),
            jax.ShapeDtypeStruct((b, W_ATT, s), BF16),
            jax.ShapeDtypeStruct((b, W_ATT, s), F32),
            jax.ShapeDtypeStruct((b, W_ATT, keep), F32),
            jax.ShapeDtypeStruct((b, W_ATT, keep), F32),
            jax.ShapeDtypeStruct((b, CONV_W - 1, W_LRU), F32),
            jax.ShapeDtypeStruct((b, 1, W_LRU), F32),
        ],
        scratch_shapes=[
            pltpu.VMEM((2, QBLOCK, 3 * W_LRU), F32),
            pltpu.VMEM((2, 3 * W_ATT, QBLOCK), F32),
            pltpu.VMEM((QBLOCK + SUBLANES, W_LRU), F32),
            pltpu.VMEM((QBLOCK, W_LRU), F32),
            pltpu.VMEM((QBLOCK, W_LRU), F32),
            pltpu.VMEM((QBLOCK, W_LRU), F32),
            pltpu.VMEM((1, W_LRU), F32),
        ],
        compiler_params=pltpu.CompilerParams(
            dimension_semantics=("arbitrary", "arbitrary"), vmem_limit_bytes=VMEM_LIMIT),
        name="front_prompt",
    )(x, ng, wnat, wt, cw, cbias, wg, bg, lam, qg, kg, ones_bd)


def _toeplitz_rows(w_row, nrows, row0):
    x = jnp.broadcast_to(w_row, (nrows, w_row.shape[1]))
    return pltpu.roll(x, row0, 1, stride=1, stride_axis=0)


def _fill_prompt_bias(w_ref, bias_ref):
    q_chunk = (lax.broadcasted_iota(jnp.int32, (CHUNK, QBLOCK), 1) + BAND) // CHUNK
    for h in range(N_HEADS):
        def body(n, carry, h=h):
            r0 = pl.multiple_of(n * CHUNK, CHUNK)
            t = _toeplitz_rows(w_ref[h], CHUNK, r0)[:, :QBLOCK]
            dc = q_chunk - n
            bias_ref[h, pl.ds(r0, CHUNK), :] = jnp.where((dc >= 0) & (dc <= PAST_CHUNKS), t * LOG2E, NEG)
            return carry
        lax.fori_loop(0, KEY_TILES * QBLOCK // CHUNK, body, 0)


SCORE_LEAD = 4
ATTN_BLOCKS = 4
CHUNKS_PER_TILE = QBLOCK // CHUNK
FRAMES_PER_VREG = LANES // CHUNK


def _lane_cols(kc):
    cols = []
    for c in range(QBLOCK // LANES):
        q_lo = PAST_CHUNKS + c * FRAMES_PER_VREG
        q_hi = q_lo + FRAMES_PER_VREG - 1
        if q_lo - PAST_CHUNKS <= kc <= q_hi:
            cols.append(c)
    return cols


def _fold_rows(x):
    return x.reshape(x.shape[0] // SUBLANES, SUBLANES, x.shape[1])


def _attn_prompt_blocks(blocks, qt_ref, k_ref, vt_ref, bias_ref, s_scr, att_scr):
    n_cols = QBLOCK // LANES
    rows = lax.broadcasted_iota(jnp.int32, (HEAD_GROUP, QBLOCK), 0)
    units = [(n, h) for n in range(len(blocks)) for h in range(N_HEADS)]

    def pieces(tiles):
        return [(i, cc, slice((i * CHUNKS_PER_TILE + cc) * CHUNK, (i * CHUNKS_PER_TILE + cc + 1) * CHUNK),
                 slice(c * LANES, (c + 1) * LANES), c)
                for i in tiles for cc in range(CHUNKS_PER_TILE) for c in _lane_cols(i * CHUNKS_PER_TILE + cc)]

    def key_rows(n, i):
        tiles, key0 = blocks[n]
        return pl.ds(key0 + (i - tiles[0]) * QBLOCK, QBLOCK)

    def scores(u):
        n, h = units[u]
        g, hl = divmod(h, HEADS_PER_GROUP)
        gsl = slice(g * HEAD_GROUP, (g + 1) * HEAD_GROUP)
        in_head = (rows >= hl * HEAD_DIM) & (rows < (hl + 1) * HEAD_DIM)
        qm = jnp.where(in_head, qt_ref[0, gsl, n * QBLOCK:(n + 1) * QBLOCK], jnp.zeros((), BF16))
        s = {i: _dot(k_ref[0, key_rows(n, i), gsl], qm) for i in blocks[n][0]}
        m_acc = [jnp.full((SUBLANES, LANES), NEG, F32) for _ in range(n_cols)]
        for i, cc, rsl, lsl, c in pieces(blocks[n][0]):
            sp = s[i][cc * CHUNK:(cc + 1) * CHUNK, lsl] + bias_ref[h, rsl, lsl]
            s_scr[u % (SCORE_LEAD + 1), rsl, lsl] = sp
            m_acc[c] = jnp.maximum(m_acc[c], jnp.max(_fold_rows(sp), axis=0))
        return [jnp.max(a, axis=0, keepdims=True) for a in m_acc]

    def weights(u, m):
        tiles_p = {}
        for i in blocks[units[u][0]][0]:
            chunks = []
            for cc in range(CHUNKS_PER_TILE):
                kc = i * CHUNKS_PER_TILE + cc
                rsl = slice(kc * CHUNK, (kc + 1) * CHUNK)
                cols = []
                for c in range(n_cols):
                    if c in _lane_cols(kc):
                        lsl = slice(c * LANES, (c + 1) * LANES)
                        cols.append(jnp.exp2(s_scr[u % (SCORE_LEAD + 1), rsl, lsl] - m[c]).astype(BF16))
                    else:
                        cols.append(jnp.zeros((CHUNK, LANES), BF16))
                chunks.append(jnp.concatenate(cols, axis=1))
            tiles_p[i] = jnp.concatenate(chunks, axis=0)
        return tiles_p

    ones_rows = jnp.ones((2 * SUBLANES, QBLOCK), BF16)

    def values(u, tiles_p):
        n, h = units[u]
        hsl = slice(h * HEAD_DIM, (h + 1) * HEAD_DIM)
        o = None
        for i in blocks[n][0]:
            oi = _dot(jnp.concatenate([vt_ref[0, hsl, key_rows(n, i)], ones_rows], axis=0), tiles_p[i])
            o = oi if o is None else o + oi
        att_scr[n, hsl, :] = o[0:HEAD_DIM, :] * (1.0 / o[HEAD_DIM:HEAD_DIM + 1, :])

    m = {u: scores(u) for u in range(min(SCORE_LEAD, len(units)))}
    w = {}
    for u in range(len(units)):
        if u + SCORE_LEAD < len(units):
            m[u + SCORE_LEAD] = scores(u + SCORE_LEAD)
        w[u] = weights(u, m.pop(u))
        if u >= 1:
            values(u - 1, w.pop(u - 1))
    values(len(units) - 1, w.pop(len(units) - 1))


def _attn_prompt_kernel(qt_ref, k_ref, vt_ref, w_ref, gat_ref, out_ref, bias_ref, s_scr, att_scr):
    step = pl.program_id(1)

    @pl.when((pl.program_id(0) == 0) & (step == 0))
    def _():
        _fill_prompt_bias(w_ref, bias_ref)

    assert ATTN_BLOCKS >= KEY_TILES - 1
    full = tuple(range(KEY_TILES))

    @pl.when(step == 0)
    def _():
        blocks = [(full[max(KEY_TILES - 1 - n, 0):], max(n - (KEY_TILES - 1), 0) * QBLOCK)
                  for n in range(ATTN_BLOCKS)]
        _attn_prompt_blocks(blocks, qt_ref, k_ref, vt_ref, bias_ref, s_scr, att_scr)

    if k_ref.shape[1] > ATTN_BLOCKS * QBLOCK:
        @pl.when(step > 0)
        def _():
            first = step * ATTN_BLOCKS - (KEY_TILES - 1)
            blocks = [(full, pl.multiple_of((first + n) * QBLOCK, QBLOCK)) for n in range(ATTN_BLOCKS)]
            _attn_prompt_blocks(blocks, qt_ref, k_ref, vt_ref, bias_ref, s_scr, att_scr)

    for n in range(ATTN_BLOCKS):
        fr = slice(n * QBLOCK, (n + 1) * QBLOCK)
        out_ref[0, fr, :] = (att_scr[n] * gat_ref[0, :, fr]).T.astype(BF16)


def _attn_prompt(qt, kbf, vt, w_bias, gat):
    b, _, s = qt.shape
    fb = ATTN_BLOCKS * QBLOCK
    assert s % fb == 0
    return pl.pallas_call(
        _attn_prompt_kernel,
        grid=(b, s // fb),
        in_specs=[
            pl.BlockSpec((1, W_ATT, fb), lambda i, j: (i, 0, j)),
            pl.BlockSpec((1, s, W_ATT), lambda i, j: (i, 0, 0)),
            pl.BlockSpec((1, W_ATT, s), lambda i, j: (i, 0, 0)),
            pl.BlockSpec((N_HEADS, 1, PROMPT_BIAS_PERIOD), lambda i, j: (0, 0, 0)),
            pl.BlockSpec((1, W_ATT, fb), lambda i, j: (i, 0, j)),
        ],
        out_specs=pl.BlockSpec((1, fb, W_ATT), lambda i, j: (i, j, 0)),
        out_shape=jax.ShapeDtypeStruct((b, s, W_ATT), BF16),
        scratch_shapes=[
            pltpu.VMEM((N_HEADS, KEY_TILES * QBLOCK, QBLOCK), F32),
            pltpu.VMEM((SCORE_LEAD + 1, KEY_TILES * QBLOCK, QBLOCK), F32),
            pltpu.VMEM((ATTN_BLOCKS, W_ATT, QBLOCK), F32),
        ],
        compiler_params=pltpu.CompilerParams(
            dimension_semantics=("arbitrary", "arbitrary"), vmem_limit_bytes=VMEM_LIMIT),
        name="attn_prompt",
    )(qt, kbf, vt, w_bias, gat)


def _back_rows(x_ref, lru_ref, att_ref, p_ref, wo_ref, pg_ref, wpg_ref, wpe_ref, y_ref):
    mix = _dot(lru_ref[...], wo_ref[0:W_LRU, :]) + _dot(att_ref[...], wo_ref[W_LRU:, :])
    h = x_ref[...] + mix
    gate = jax.nn.sigmoid(_dot(_rms_rows(h, pg_ref[...]).astype(BF16), wpg_ref[...]))
    y_ref[...] = h + _dot(p_ref[...].astype(BF16), wpe_ref[...]) * gate


def _back_kernel(xp_ref, lrup_ref, attp_ref, pp_ref, xs_ref, lrus_ref, ps_ref,
                 q_ref, kn_ref, vn_ref, kc_ref, vc_ref, ga_ref, wb_ref,
                 wo_ref, pg_ref, wpg_ref, wpe_ref, yp_ref, ys_ref, bias_scr, atts_scr, *, prompt_steps):
    step = pl.program_id(0)
    weights = (wo_ref, pg_ref, wpg_ref, wpe_ref)
    streams, tt = q_ref.shape[0], q_ref.shape[1]

    @pl.when(step == 0)
    def _():
        _fill_sample_bias(wb_ref, bias_scr, tt)

    @pl.when(step < prompt_steps)
    def _():
        def write(s, lanes, rows):
            atts_scr[pl.ds(pl.multiple_of((step * streams + s) * tt, tt), tt), lanes] = rows

        _attn_sample_streams(q_ref, kn_ref, vn_ref, kc_ref, vc_ref, ga_ref, bias_scr, write)
        _back_rows(xp_ref, lrup_ref, attp_ref, pp_ref, *weights, yp_ref)

    @pl.when(step == prompt_steps)
    def _():
        _back_rows(xs_ref, lrus_ref, atts_scr, ps_ref, *weights, ys_ref)


def _back(prompt, sample, sample_attn, wo, pg, wpg, wpe, rows):
    n, ns = prompt[0].shape[0], sample[0].shape[0]
    q3, k3, v3, kc_t, vc_t, ga3, w_bias = sample_attn
    db, tt, _ = q3.shape
    assert n % rows == 0 and db * tt == ns
    steps = n // rows
    assert db % steps == 0
    per_step = db // steps
    const = lambda shape: pl.BlockSpec(shape, lambda i: (0,) * len(shape))
    walk = lambda w: pl.BlockSpec((rows, w), lambda i: (jnp.minimum(i, steps - 1), 0))
    whole = lambda w: pl.BlockSpec((ns, w), lambda i: (0, 0), pipeline_mode=pl.Buffered(1))
    seq = pl.BlockSpec((per_step, tt, W_ATT), lambda i: (jnp.minimum(i, steps - 1), 0, 0))
    cache = pl.BlockSpec((per_step, N_HEADS, HEAD_DIM, kc_t.shape[-1]),
                         lambda i: (jnp.minimum(i, steps - 1), 0, 0, 0))
    period = w_bias.shape[-1]
    return pl.pallas_call(
        functools.partial(_back_kernel, prompt_steps=steps),
        grid=(steps + 1,),
        in_specs=[walk(w) for w in (D_MODEL, W_LRU, W_ATT, PLE_DIM)]
        + [whole(w) for w in (D_MODEL, W_LRU, PLE_DIM)]
        + [seq, seq, seq, cache, cache, seq, const((N_HEADS, 1, period))]
        + [const((W_LRU + W_ATT, D_MODEL)), const((1, D_MODEL)), const((D_MODEL, D_MODEL)),
           const((PLE_DIM, D_MODEL))],
        out_specs=[walk(D_MODEL), pl.BlockSpec((ns, D_MODEL), lambda i: (0, 0))],
        out_shape=[jax.ShapeDtypeStruct((n, D_MODEL), F32), jax.ShapeDtypeStruct((ns, D_MODEL), F32)],
        scratch_shapes=[
            pltpu.VMEM((N_GROUPS, HEADS_PER_GROUP * tt, period), F32),
            pltpu.VMEM((ns, W_ATT), BF16),
        ],
        compiler_params=pltpu.CompilerParams(
            dimension_semantics=("arbitrary",), vmem_limit_bytes=VMEM_LIMIT),
        name="back",
    )(*prompt, *sample, q3, k3, v3, kc_t, vc_t, ga3, w_bias, wo, pg, wpg, wpe)


def _front_sample_kernel(x_ref, ng_ref, win_ref, cw_ref, cbias_ref, wg_ref, bg_ref, lam_ref,
                         qg_ref, kg_ref, ones_ref, sconv_ref, slru_ref,
                         lru_ref, q_ref, k_ref, v_ref, ga_ref, sc_ref, sh_ref,
                         cb_scr, xc_scr, a_scr, u_scr, h_scr, *, nb, tt):
    seg = tt + SUBLANES
    xn = _rms_rows(x_ref[...], ng_ref[...]).astype(BF16)
    z = _dot(xn, win_ref[...])
    xl = z[:, :W_LRU]
    for s in range(nb):
        cb_scr[s * seg:s * seg + SUBLANES, :] = sconv_ref[s]
        cb_scr[s * seg + SUBLANES:(s + 1) * seg, :] = xl[s * tt:(s + 1) * tt, :]
        xc_scr[s * tt:(s + 1) * tt, :] = _conv_rows(cb_scr, s * seg, tt, cw_ref, cbias_ref[...])
        sc_ref[s] = cb_scr[(s + 1) * seg - (CONV_W - 1):(s + 1) * seg, :]
    _lru_inputs(xc_scr[...], wg_ref, bg_ref, lam_ref, a_scr, u_scr)
    for s in range(nb):
        sh_ref[s] = _scan_rows(a_scr, u_scr, h_scr, s * tt, tt, slru_ref[s], unroll=True)
    lru_ref[...] = (h_scr[...] * jax.nn.silu(z[:, W_LRU:2 * W_LRU])).astype(BF16)

    o = 2 * W_LRU
    q = _head_norm_rows(z[:, o:o + W_ATT], ones_ref[...], qg_ref[...])
    q_ref[...] = (q * (HEAD_DIM ** -0.5)).astype(BF16)
    k_ref[...] = _head_norm_rows(z[:, o + W_ATT:o + 2 * W_ATT], ones_ref[...], kg_ref[...])
    v_ref[...] = z[:, o + 2 * W_ATT:o + 3 * W_ATT]
    ga_ref[...] = jax.nn.silu(z[:, o + 3 * W_ATT:])


def _front_sample(x2, ng, win, cw, cbias, wg, bg, lam, qg_t, kg_t, ones_bd, sconv_pad, slru, nb, tt):
    n = x2.shape[0]
    rows = nb * tt
    const = lambda shape: pl.BlockSpec(shape, lambda i: (0,) * len(shape))
    row_spec = lambda w: pl.BlockSpec((rows, w), lambda i: (i, 0))
    nseq = n // tt
    kern = functools.partial(_front_sample_kernel, nb=nb, tt=tt)
    return pl.pallas_call(
        kern,
        grid=(n // rows,),
        in_specs=[
            row_spec(D_MODEL),
            const((1, D_MODEL)),
            const((D_MODEL, 2 * W_LRU + 4 * W_ATT)),
            const((CONV_W, W_LRU)),
            const((1, W_LRU)),
            const((W_LRU // HEAD_GROUP, HEAD_GROUP, 2 * HEAD_GROUP)),
            const((W_LRU // HEAD_GROUP, 1, 2 * HEAD_GROUP)),
            const((1, W_LRU)),
            const((1, W_ATT)),
            const((1, W_ATT)),
            const((HEAD_GROUP, HEAD_GROUP)),
            pl.BlockSpec((nb, SUBLANES, W_LRU), lambda i: (i, 0, 0)),
            pl.BlockSpec((nb, 1, W_LRU), lambda i: (i, 0, 0)),
        ],
        out_specs=[
            row_spec(W_LRU), row_spec(W_ATT), row_spec(W_ATT), row_spec(W_ATT), row_spec(W_ATT),
            pl.BlockSpec((nb, CONV_W - 1, W_LRU), lambda i: (i, 0, 0)),
            pl.BlockSpec((nb, 1, W_LRU), lambda i: (i, 0, 0)),
        ],
        out_shape=[
            jax.ShapeDtypeStruct((n, W_LRU), BF16),
            jax.ShapeDtypeStruct((n, W_ATT), BF16),
            jax.ShapeDtypeStruct((n, W_ATT), F32),
            jax.ShapeDtypeStruct((n, W_ATT), F32),
            jax.ShapeDtypeStruct((n, W_ATT), F32),
            jax.ShapeDtypeStruct((nseq, CONV_W - 1, W_LRU), F32),
            jax.ShapeDtypeStruct((nseq, 1, W_LRU), F32),
        ],
        scratch_shapes=[
            pltpu.VMEM((nb * (tt + SUBLANES), W_LRU), F32),
            pltpu.VMEM((rows, W_LRU), F32),
            pltpu.VMEM((rows, W_LRU), F32),
            pltpu.VMEM((rows, W_LRU), F32),
            pltpu.VMEM((rows, W_LRU), F32),
        ],
        compiler_params=pltpu.CompilerParams(
            dimension_semantics=("arbitrary",), vmem_limit_bytes=VMEM_LIMIT),
        name="front_sample",
    )(x2, ng, win, cw, cbias, wg, bg, lam, qg_t, kg_t, ones_bd, sconv_pad, slru)


def _fill_sample_bias(w_ref, bias_ref, tt):
    for h in range(N_HEADS):
        g, hl = divmod(h, HEADS_PER_GROUP)
        bias_ref[g, hl * tt:(hl + 1) * tt, :] = _toeplitz_rows(w_ref[h], tt, 0)


def _attn_sample_streams(q_ref, kn_ref, vn_ref, kc_ref, vc_ref, ga_ref, bias_ref, write):
    tt = q_ref.shape[1]
    lc = kc_ref.shape[-1]
    lanes = lax.broadcasted_iota(jnp.int32, (tt, HEAD_GROUP), 1)
    masks = [(lanes >= hl * HEAD_DIM) & (lanes < (hl + 1) * HEAD_DIM) for hl in range(HEADS_PER_GROUP)]
    units = [(s, g) for s in range(q_ref.shape[0]) for g in range(N_GROUPS)]

    def cached(ref, s, g):
        heads = ref[s, g * HEADS_PER_GROUP:(g + 1) * HEADS_PER_GROUP]
        return heads.reshape(HEAD_GROUP, lc).astype(BF16)

    def scores(s, g):
        gsl = slice(g * HEAD_GROUP, (g + 1) * HEAD_GROUP)
        qg = q_ref[s, :, gsl]
        qs = jnp.concatenate([jnp.where(m, qg, jnp.zeros((), BF16)) for m in masks], axis=0)
        sc = _dot(qs, cached(kc_ref, s, g)) + bias_ref[g, :, 0:lc]
        sn = _dot_nt(qs, kn_ref[s, :, gsl].astype(BF16)) + bias_ref[g, :, lc:lc + tt]
        return sc, sn

    def finish(s, g, sc, sn):
        gsl = slice(g * HEAD_GROUP, (g + 1) * HEAD_GROUP)
        m = jnp.maximum(jnp.max(sc, axis=-1, keepdims=True), jnp.max(sn, axis=-1, keepdims=True))
        pc = jnp.exp(sc - m)
        pn = jnp.exp(sn - m)
        l = jnp.sum(pc, axis=-1, keepdims=True) + jnp.sum(pn, axis=-1, keepdims=True)
        o = _dot_nt(pc.astype(BF16), cached(vc_ref, s, g))
        o = (o + _dot(pn.astype(BF16), vn_ref[s, :, gsl].astype(BF16))) * (1.0 / l)
        att = jnp.zeros((tt, HEAD_GROUP), F32)
        for hl in range(HEADS_PER_GROUP):
            att = att + jnp.where(masks[hl], o[hl * tt:(hl + 1) * tt, :], 0.0)
        write(s, gsl, (att * ga_ref[s, :, gsl]).astype(BF16))

    nxt = scores(*units[0])
    for n, (s, g) in enumerate(units):
        cur = nxt
        if n + 1 < len(units):
            nxt = scores(*units[n + 1])
        finish(s, g, *cur)


def _block_diag(w):
    n, d, e = w.shape
    eye = jnp.eye(n, dtype=w.dtype)
    return (eye[:, None, :, None] * w[:, :, None, :]).reshape(n * d, n * e)


def _prompt_bias_period(table):
    assert BAND - MAX_REL == MAX_REL and PROMPT_BIAS_PERIOD == KEY_TILES * QBLOCK + QBLOCK
    last = table[2 * MAX_REL:]
    neg_d = jnp.concatenate([table, jnp.broadcast_to(last, (MAX_REL - 1, N_HEADS))])
    w = jnp.concatenate([jnp.broadcast_to(last, (QBLOCK, N_HEADS)), neg_d])
    return w.T.reshape(N_HEADS, 1, PROMPT_BIAS_PERIOD).astype(F32)


def _sample_bias_period(table, tt, l):
    assert l >= MAX_REL
    period = -(-(l + 2 * tt - 1) // LANES) * LANES
    last = table[2 * MAX_REL:]
    n_var = tt + MAX_REL - 1
    var = table[2 * MAX_REL - 1:2 * MAX_REL - 1 - n_var:-1]
    w = jnp.concatenate([jnp.broadcast_to(last, (l - MAX_REL + 1, N_HEADS)), var,
                         jnp.broadcast_to(last, (period - (l + tt), N_HEADS))])
    return w.T.reshape(N_HEADS, 1, period).astype(F32)


def kernel(x_prompt, x_sample, p_prompt, p_sample, cache_k, cache_v, state_conv, state_lru, norm_g, w_in, conv_w, conv_b, gate_a_w, gate_a_b, gate_x_w, gate_x_b, lru_lambda, q_norm_g, k_norm_g, rel_bias, w_out, ple_norm_g, w_ple_gate, w_ple_proj):
    depth = w_in.shape[0]
    b, s, _ = x_prompt.shape
    db, ds, _ = x_sample.shape
    lc = cache_k.shape[2]
    yp, ys = x_prompt, x_sample.reshape(db * ds, D_MODEL)
    ones_bd = _block_diag(jnp.full((HEADS_PER_GROUP, HEAD_DIM, HEAD_DIM), 1.0 / HEAD_DIM, F32)).astype(BF16)
    gate_halves = W_LRU // HEAD_GROUP
    blocks_per_half = LRU_BLOCKS // gate_halves
    outs = [[] for _ in range(8)]
    for l in range(depth):
        win = w_in[l].astype(BF16)
        o = 2 * W_LRU
        wnat = jnp.concatenate([win[:, :o], win[:, o + W_ATT:o + 2 * W_ATT]], axis=1)
        wt = jnp.concatenate([win[:, o:o + W_ATT], win[:, o + 2 * W_ATT:]], axis=1).T
        ng = norm_g[l].reshape(1, D_MODEL)
        cw = conv_w[l]
        cbias = conv_b[l].reshape(1, W_LRU)
        wg = jnp.stack([
            jnp.concatenate([_block_diag(w[j * blocks_per_half:(j + 1) * blocks_per_half])
                             for w in (gate_a_w[l], gate_x_w[l])], axis=1)
            for j in range(gate_halves)]).astype(BF16)
        bg = jnp.concatenate([gate_a_b[l].reshape(gate_halves, 1, HEAD_GROUP),
                              gate_x_b[l].reshape(gate_halves, 1, HEAD_GROUP)], axis=2)
        lam = lru_lambda[l].reshape(1, W_LRU)
        qg_col = q_norm_g[l].reshape(HEAD_DIM, 1)
        qg_t = jnp.tile(q_norm_g[l], N_HEADS).reshape(1, W_ATT)
        kg_t = jnp.tile(k_norm_g[l], N_HEADS).reshape(1, W_ATT)
        wo = w_out[l].astype(BF16)
        pg = ple_norm_g[l].reshape(1, D_MODEL)
        wpg = w_ple_gate[l].astype(BF16)
        wpe = w_ple_proj[l].astype(BF16)

        lru_g, qt, kbf, vt, gat, pk, pv, pc, ph = _front_prompt(
            yp, ng, wnat, wt, cw, cbias, wg, bg, lam, qg_col, kg_t, ones_bd)
        att_g = _attn_prompt(qt, kbf, vt, _prompt_bias_period(rel_bias[l]), gat)
        to_frames = lambda a: jnp.transpose(a.reshape(b, N_HEADS, HEAD_DIM, a.shape[-1]), (0, 3, 1, 2))
        outs[0].append(to_frames(pk))
        outs[1].append(to_frames(pv))
        outs[2].append(pc)
        outs[3].append(ph.reshape(b, W_LRU))

        sconv_pad = jnp.pad(state_conv[l], ((0, 0), (SUBLANES - (CONV_W - 1), 0), (0, 0)))
        lru_s, q_s, k_s, v_s, ga_s, sc, sh = _front_sample(
            ys, ng, win, cw, cbias, wg, bg, lam, qg_t, kg_t, ones_bd,
            sconv_pad, state_lru[l].reshape(db, 1, W_LRU), SAMPLE_STREAMS_PER_BLOCK, ds)
        yp, ys = _back(
            (yp.reshape(b * s, D_MODEL), lru_g.reshape(b * s, W_LRU), att_g.reshape(b * s, W_ATT),
             p_prompt[l].reshape(b * s, PLE_DIM)),
            (ys, lru_s, p_sample[l].reshape(db * ds, PLE_DIM)),
            (q_s.reshape(db, ds, W_ATT), k_s.reshape(db, ds, W_ATT), v_s.reshape(db, ds, W_ATT),
             jnp.transpose(cache_k[l], (0, 2, 3, 1)), jnp.transpose(cache_v[l], (0, 2, 3, 1)),
             ga_s.reshape(db, ds, W_ATT), _sample_bias_period(rel_bias[l], ds, lc)),
            wo, pg, wpg, wpe, BACK_ROWS)
        yp = yp.reshape(b, s, D_MODEL)
        outs[4].append(k_s.reshape(db, ds, N_HEADS, HEAD_DIM))
        outs[5].append(v_s.reshape(db, ds, N_HEADS, HEAD_DIM))
        outs[6].append(sc)
        outs[7].append(sh.reshape(db, W_LRU))
    return (yp, ys.reshape(db, ds, D_MODEL)) + tuple(jnp.stack(o) for o in outs)
```

```python
import functools

import jax
import jax.numpy as jnp
from jax import lax
from jax.experimental import pallas as pl
from jax.experimental.pallas import tpu as pltpu

D_MODEL = 1024
CHUNK = 64
PAST_CHUNKS = 8
BAND = PAST_CHUNKS * CHUNK
W_LRU = D_MODEL // 2
LRU_BLOCKS = 8
CONV_W = 4
RG_C = 8.0
HEAD_DIM = 64
W_ATT = D_MODEL // 2
N_HEADS = W_ATT // HEAD_DIM
MAX_REL = 256
PLE_DIM = 256
EPS = 1e-6
NEG = -1e30
LOG2E = 1.4426950408889634

SUBLANES = 8
LANES = 128
HEAD_GROUP = 256
HEADS_PER_GROUP = HEAD_GROUP // HEAD_DIM
N_GROUPS = W_ATT // HEAD_GROUP
QBLOCK = 256
KEY_TILES = BAND // QBLOCK + 1
PROMPT_BIAS_PERIOD = (KEY_TILES + 1) * QBLOCK
FRONT_BLOCK = 1024
BACK_ROWS = 1024
SAMPLE_STREAMS_PER_BLOCK = 8
VMEM_LIMIT = 56 * 1024 * 1024

F32 = jnp.float32
BF16 = jnp.bfloat16


def _dot(a, b):
    return jnp.dot(a, b, preferred_element_type=F32)


def _dot_nt(a, b):
    return lax.dot_general(a, b, (((1,), (1,)), ((), ())), preferred_element_type=F32)


def _rms_rows(x, g):
    ms = jnp.mean(x * x, axis=-1, keepdims=True)
    return x * lax.rsqrt(ms + EPS) * g


def _head_norm_rows(x, ones_bd, g_tiled):
    x2 = x * x
    hi = x2.astype(BF16)
    lo = (x2 - hi.astype(F32)).astype(BF16)
    ms = jnp.concatenate(
        [_dot(hi[:, g * HEAD_GROUP:(g + 1) * HEAD_GROUP], ones_bd)
         + _dot(lo[:, g * HEAD_GROUP:(g + 1) * HEAD_GROUP], ones_bd) for g in range(N_GROUPS)], axis=1)
    return x * lax.rsqrt(ms + EPS) * g_tiled


def _scan_rows(a_ref, u_ref, h_ref, row0, nrows, h0, unroll=False):
    ridx = lax.broadcasted_iota(jnp.int32, (SUBLANES, W_LRU), 0)

    def body(i, hprev):
        r = pl.multiple_of(row0 + i * SUBLANES, SUBLANES)
        a = a_ref[pl.ds(r, SUBLANES), :]
        u = u_ref[pl.ds(r, SUBLANES), :]
        for s in (1, 2, 4):
            a_s = jnp.where(ridx >= s, pltpu.roll(a, s, 0), 1.0)
            u_s = jnp.where(ridx >= s, pltpu.roll(u, s, 0), 0.0)
            u = a * u_s + u
            a = a * a_s
        h = a * hprev + u
        h_ref[pl.ds(r, SUBLANES), :] = h
        return h[SUBLANES - 1:SUBLANES, :]

    return lax.fori_loop(0, nrows // SUBLANES, body, h0, unroll=unroll)


def _lru_inputs(xc, wg_ref, bg_ref, lam_ref, a_ref, u_ref):
    xcb = xc.astype(BF16)
    half = wg_ref.shape[1]
    for j in range(wg_ref.shape[0]):
        sl = slice(j * half, (j + 1) * half)
        gates = _dot(xcb[:, sl], wg_ref[j]) + bg_ref[j]
        r = jax.nn.sigmoid(gates[:, :half])
        i = jax.nn.sigmoid(gates[:, half:])
        log_a = -RG_C * r * jax.nn.softplus(-lam_ref[:, sl])
        a = jnp.exp(log_a)
        a_ref[:, sl] = a
        u_ref[:, sl] = jnp.sqrt(jnp.tanh(-log_a) * (1.0 + a * a)) * (i * xc[:, sl])


SEG = 32
N_SEG = SUBLANES
SEG_PITCH = 40
SLABS = W_LRU // LANES


def _seg_rows(s):
    return slice(s * SEG_PITCH, s * SEG_PITCH + SEG)


def _segments_pass(a_s, u_s, slab, h, with_product=False, store=False):
    prod = jnp.ones((N_SEG, LANES), F32) if with_product else None
    for p in range(SEG):
        at = pl.ds(p, N_SEG, stride=SEG_PITCH)
        a = a_s[slab, at, :]
        h = a * h + u_s[slab, at, :]
        if with_product:
            prod = a * prod
        if store:
            u_s[slab, at, :] = h
    return h, prod


def _scan_chained(a_s, u_s, h_prev):
    ridx = lax.broadcasted_iota(jnp.int32, (N_SEG, LANES), 0)
    last = []
    for slab in range(SLABS):
        hp = h_prev[:, slab * LANES:(slab + 1) * LANES]
        u, a = _segments_pass(a_s, u_s, slab, jnp.zeros((N_SEG, LANES), F32), with_product=True)
        for s in (1, 2, 4):
            a_sh = jnp.where(ridx >= s, pltpu.roll(a, s, 0), 1.0)
            u_sh = jnp.where(ridx >= s, pltpu.roll(u, s, 0), 0.0)
            u = a * u_sh + u
            a = a * a_sh
        ends = a * hp + u
        starts = jnp.where(ridx == 0, hp, pltpu.roll(ends, 1, 0))
        _segments_pass(a_s, u_s, slab, starts, store=True)
        last.append(ends[N_SEG - 1:N_SEG, :])
    return jnp.concatenate(last, axis=1)


class _SegmentStore:
    def __init__(self, ref):
        self.ref = ref

    def __setitem__(self, idx, value):
        _, lanes = idx
        assert lanes.start % LANES == 0 and lanes.stop % LANES == 0
        for s in range(N_SEG):
            for k in range((lanes.stop - lanes.start) // LANES):
                self.ref[lanes.start // LANES + k, _seg_rows(s), :] = value[s * SEG:(s + 1) * SEG,
                                                                            k * LANES:(k + 1) * LANES]


def _gated_lru(u_s, gate, col0, out_ref, row0):
    for s in range(N_SEG):
        for slab in range(SLABS):
            g = gate[s * SEG:(s + 1) * SEG, col0 + slab * LANES:col0 + (slab + 1) * LANES]
            out_ref[row0 + s * SEG:row0 + (s + 1) * SEG, slab * LANES:(slab + 1) * LANES] = (
                u_s[slab, _seg_rows(s), :] * jax.nn.silu(g)).astype(BF16)


def _conv_rows(cb_ref, base, nrows, cw_ref, cb_bias):
    out = cb_bias + cw_ref[CONV_W - 1:CONV_W, :] * cb_ref[pl.ds(base + SUBLANES, nrows), :]
    for k in range(CONV_W - 1):
        shift = CONV_W - 1 - k
        out = out + cw_ref[k:k + 1, :] * cb_ref[pl.ds(base + SUBLANES - shift, nrows), :]
    return out


def _front_prompt_kernel(x_ref, ng_ref, wnat_ref, wt_ref, cw_ref, cbias_ref, wg_ref, bg_ref, lam_ref,
                         qg_ref, kg_ref, ones_ref,
                         lru_ref, qt_ref, kbf_ref, vt_ref, gat_ref, pk_ref, pv_ref, pc_ref, ph_ref,
                         zn_scr, zt_scr, cb_scr, a_scr, u_scr, hlast_scr, *, keep_subs):
    sub = QBLOCK
    assert sub == N_SEG * SEG
    n_sub = x_ref.shape[1] // sub

    @pl.when(pl.program_id(1) == 0)
    def _():
        cb_scr[0:SUBLANES, :] = jnp.zeros((SUBLANES, W_LRU), F32)
        hlast_scr[...] = jnp.zeros((1, W_LRU), F32)

    def project(i):
        xn = _rms_rows(x_ref[0, i * sub:(i + 1) * sub, :], ng_ref[...]).astype(BF16)
        zn_scr[i % 2] = _dot(xn, wnat_ref[...])
        zt_scr[i % 2] = _dot_nt(wt_ref[...], xn)

    def finish(i):
        rows = slice(i * sub, (i + 1) * sub)
        zn = zn_scr.at[i % 2]
        zt = zt_scr.at[i % 2]
        cb_scr[SUBLANES:SUBLANES + sub, :] = zn[:, :W_LRU]
        xc = _conv_rows(cb_scr, 0, sub, cw_ref, cbias_ref[...])
        pc_ref[0] = cb_scr[sub + SUBLANES - (CONV_W - 1):sub + SUBLANES, :]
        cb_scr[0:SUBLANES, :] = cb_scr[sub:sub + SUBLANES, :]
        _lru_inputs(xc, wg_ref, bg_ref, lam_ref, _SegmentStore(a_scr), _SegmentStore(u_scr))
        h_last = _scan_chained(a_scr, u_scr, hlast_scr[...])
        hlast_scr[...] = h_last
        ph_ref[0] = h_last
        _gated_lru(u_scr, zn, W_LRU, lru_ref.at[0], i * sub)
        k = _head_norm_rows(zn[:, 2 * W_LRU:], ones_ref[...], kg_ref[...])
        kbf_ref[0, rows, :] = k.astype(BF16)
        q3 = zt[0:W_ATT, :].reshape(N_HEADS, HEAD_DIM, sub)
        ms = jnp.mean(q3 * q3, axis=1, keepdims=True)
        qn = q3 * lax.rsqrt(ms + EPS) * (qg_ref[...] * (HEAD_DIM ** -0.5 * LOG2E))
        qt_ref[0, :, rows] = qn.reshape(W_ATT, sub).astype(BF16)
        vt = zt[W_ATT:2 * W_ATT, :]
        vt_ref[0, :, rows] = vt.astype(BF16)
        gat_ref[0, :, rows] = jax.nn.silu(zt[2 * W_ATT:, :])
        if i >= n_sub - keep_subs:
            first = (i - (n_sub - keep_subs)) * sub
            pk_ref[0, :, first:first + sub] = k.T
            pv_ref[0, :, first:first + sub] = vt

    project(0)
    for i in range(n_sub):
        if i + 1 < n_sub:
            project(i + 1)
        finish(i)


def _front_prompt(x, ng, wnat, wt, cw, cbias, wg, bg, lam, qg, kg, ones_bd):
    b, s, _ = x.shape
    tb = min(FRONT_BLOCK, s)
    keep = min(BAND, s)
    assert s % tb == 0 and tb % QBLOCK == 0 and keep % QBLOCK == 0 and keep <= tb
    const = lambda shape: pl.BlockSpec(shape, lambda i, j: (0,) * len(shape))
    rows_spec = lambda w: pl.BlockSpec((1, tb, w), lambda i, j: (i, j, 0))
    cols_spec = pl.BlockSpec((1, W_ATT, tb), lambda i, j: (i, 0, j))
    per_seq = lambda r, w: pl.BlockSpec((1, r, w), lambda i, j: (i, 0, 0))
    kern = functools.partial(_front_prompt_kernel, keep_subs=keep // QBLOCK)
    return pl.pallas_call(
        kern,
        grid=(b, s // tb),
        in_specs=[
            rows_spec(D_MODEL),
            const((1, D_MODEL)),
            const((D_MODEL, 3 * W_LRU)),
            const((3 * W_ATT, D_MODEL)),
            const((CONV_W, W_LRU)),
            const((1, W_LRU)),
            const((W_LRU // HEAD_GROUP, HEAD_GROUP, 2 * HEAD_GROUP)),
            const((W_LRU // HEAD_GROUP, 1, 2 * HEAD_GROUP)),
            const((1, W_LRU)),
            const((HEAD_DIM, 1)),
            const((1, W_ATT)),
            const((HEAD_GROUP, HEAD_GROUP)),
        ],
        out_specs=[
            rows_spec(W_LRU),
            cols_spec,
            rows_spec(W_ATT),
            cols_spec,
            cols_spec,
            per_seq(W_ATT, keep),
            per_seq(W_ATT, keep),
            per_seq(CONV_W - 1, W_LRU),
            per_seq(1, W_LRU),
        ],
        out_shape=[
            jax.ShapeDtypeStruct((b, s, W_LRU), BF16),
            jax.ShapeDtypeStruct((b, W_ATT, s), BF16),
            jax.ShapeDtypeStruct((b, s, W_ATT), BF16),
            jax.ShapeDtypeStruct((b, W_ATT, s), BF16),
            jax.ShapeDtypeStruct((b, W_ATT, s), F32),
            jax.ShapeDtypeStruct((b, W_ATT, keep), F32),
            jax.ShapeDtypeStruct((b, W_ATT, keep), F32),
            jax.ShapeDtypeStruct((b, CONV_W - 1, W_LRU), F32),
            jax.ShapeDtypeStruct((b, 1, W_LRU), F32),
        ],
        scratch_shapes=[
            pltpu.VMEM((2, QBLOCK, 3 * W_LRU), F32),
            pltpu.VMEM((2, 3 * W_ATT, QBLOCK), F32),
            pltpu.VMEM((QBLOCK + SUBLANES, W_LRU), F32),
            pltpu.VMEM((SLABS, N_SEG * SEG_PITCH, LANES), F32),
            pltpu.VMEM((SLABS, N_SEG * SEG_PITCH, LANES), F32),
            pltpu.VMEM((1, W_LRU), F32),
        ],
        compiler_params=pltpu.CompilerParams(
            dimension_semantics=("arbitrary", "arbitrary"), vmem_limit_bytes=VMEM_LIMIT),
        name="front_prompt",
    )(x, ng, wnat, wt, cw, cbias, wg, bg, lam, qg, kg, ones_bd)


def _toeplitz_rows(w_row, nrows, row0):
    x = jnp.broadcast_to(w_row, (nrows, w_row.shape[1]))
    return pltpu.roll(x, row0, 1, stride=1, stride_axis=0)


def _fill_prompt_bias(w_ref, bias_ref):
    q_chunk = (lax.broadcasted_iota(jnp.int32, (CHUNK, QBLOCK), 1) + BAND) // CHUNK
    for h in range(N_HEADS):
        def body(n, carry, h=h):
            r0 = pl.multiple_of(n * CHUNK, CHUNK)
            t = _toeplitz_rows(w_ref[h], CHUNK, r0)[:, :QBLOCK]
            dc = q_chunk - n
            bias_ref[h, pl.ds(r0, CHUNK), :] = jnp.where((dc >= 0) & (dc <= PAST_CHUNKS), t * LOG2E, NEG)
            return carry
        lax.fori_loop(0, KEY_TILES * QBLOCK // CHUNK, body, 0)


SCORE_LEAD = 4
ATTN_BLOCKS = 4
CHUNKS_PER_TILE = QBLOCK // CHUNK
FRAMES_PER_VREG = LANES // CHUNK


def _lane_cols(kc):
    cols = []
    for c in range(QBLOCK // LANES):
        q_lo = PAST_CHUNKS + c * FRAMES_PER_VREG
        q_hi = q_lo + FRAMES_PER_VREG - 1
        if q_lo - PAST_CHUNKS <= kc <= q_hi:
            cols.append(c)
    return cols


def _fold_rows(x):
    return x.reshape(x.shape[0] // SUBLANES, SUBLANES, x.shape[1])


def _attn_prompt_blocks(blocks, qt_ref, k_ref, vt_ref, bias_ref, s_scr, att_scr):
    n_cols = QBLOCK // LANES
    rows = lax.broadcasted_iota(jnp.int32, (HEAD_GROUP, QBLOCK), 0)
    units = [(n, h) for n in range(len(blocks)) for h in range(N_HEADS)]

    def pieces(tiles):
        return [(i, cc, slice((i * CHUNKS_PER_TILE + cc) * CHUNK, (i * CHUNKS_PER_TILE + cc + 1) * CHUNK),
                 slice(c * LANES, (c + 1) * LANES), c)
                for i in tiles for cc in range(CHUNKS_PER_TILE) for c in _lane_cols(i * CHUNKS_PER_TILE + cc)]

    def key_rows(n, i):
        tiles, key0 = blocks[n]
        return pl.ds(key0 + (i - tiles[0]) * QBLOCK, QBLOCK)

    def scores(u):
        n, h = units[u]
        g, hl = divmod(h, HEADS_PER_GROUP)
        gsl = slice(g * HEAD_GROUP, (g + 1) * HEAD_GROUP)
        in_head = (rows >= hl * HEAD_DIM) & (rows < (hl + 1) * HEAD_DIM)
        qm = jnp.where(in_head, qt_ref[0, gsl, n * QBLOCK:(n + 1) * QBLOCK], jnp.zeros((), BF16))
        s = {i: _dot(k_ref[0, key_rows(n, i), gsl], qm) for i in blocks[n][0]}
        m_acc = [jnp.full((SUBLANES, LANES), NEG, F32) for _ in range(n_cols)]
        for i, cc, rsl, lsl, c in pieces(blocks[n][0]):
            sp = s[i][cc * CHUNK:(cc + 1) * CHUNK, lsl] + bias_ref[h, rsl, lsl]
            s_scr[u % (SCORE_LEAD + 1), rsl, lsl] = sp
            m_acc[c] = jnp.maximum(m_acc[c], jnp.max(_fold_rows(sp), axis=0))
        return [jnp.max(a, axis=0, keepdims=True) for a in m_acc]

    def weights(u, m):
        tiles_p = {}
        for i in blocks[units[u][0]][0]:
            chunks = []
            for cc in range(CHUNKS_PER_TILE):
                kc = i * CHUNKS_PER_TILE + cc
                rsl = slice(kc * CHUNK, (kc + 1) * CHUNK)
                cols = []
                for c in range(n_cols):
                    if c in _lane_cols(kc):
                        lsl = slice(c * LANES, (c + 1) * LANES)
                        cols.append(jnp.exp2(s_scr[u % (SCORE_LEAD + 1), rsl, lsl] - m[c]).astype(BF16))
                    else:
                        cols.append(jnp.zeros((CHUNK, LANES), BF16))
                chunks.append(jnp.concatenate(cols, axis=1))
            tiles_p[i] = jnp.concatenate(chunks, axis=0)
        return tiles_p

    ones_rows = jnp.ones((2 * SUBLANES, QBLOCK), BF16)

    def values(u, tiles_p):
        n, h = units[u]
        hsl = slice(h * HEAD_DIM, (h + 1) * HEAD_DIM)
        o = None
        for i in blocks[n][0]:
            oi = _dot(jnp.concatenate([vt_ref[0, hsl, key_rows(n, i)], ones_rows], axis=0), tiles_p[i])
            o = oi if o is None else o + oi
        att_scr[n, hsl, :] = o[0:HEAD_DIM, :] * (1.0 / o[HEAD_DIM:HEAD_DIM + 1, :])

    m = {u: scores(u) for u in range(min(SCORE_LEAD, len(units)))}
    w = {}
    for u in range(len(units)):
        if u + SCORE_LEAD < len(units):
            m[u + SCORE_LEAD] = scores(u + SCORE_LEAD)
        w[u] = weights(u, m.pop(u))
        if u >= 1:
            values(u - 1, w.pop(u - 1))
    values(len(units) - 1, w.pop(len(units) - 1))


def _attn_prompt_kernel(qt_ref, k_ref, vt_ref, w_ref, gat_ref, out_ref, bias_ref, s_scr, att_scr):
    step = pl.program_id(1)

    @pl.when((pl.program_id(0) == 0) & (step == 0))
    def _():
        _fill_prompt_bias(w_ref, bias_ref)

    assert ATTN_BLOCKS >= KEY_TILES - 1
    full = tuple(range(KEY_TILES))

    @pl.when(step == 0)
    def _():
        blocks = [(full[max(KEY_TILES - 1 - n, 0):], max(n - (KEY_TILES - 1), 0) * QBLOCK)
                  for n in range(ATTN_BLOCKS)]
        _attn_prompt_blocks(blocks, qt_ref, k_ref, vt_ref, bias_ref, s_scr, att_scr)

    if k_ref.shape[1] > ATTN_BLOCKS * QBLOCK:
        @pl.when(step > 0)
        def _():
            first = step * ATTN_BLOCKS - (KEY_TILES - 1)
            blocks = [(full, pl.multiple_of((first + n) * QBLOCK, QBLOCK)) for n in range(ATTN_BLOCKS)]
            _attn_prompt_blocks(blocks, qt_ref, k_ref, vt_ref, bias_ref, s_scr, att_scr)

    for n in range(ATTN_BLOCKS):
        fr = slice(n * QBLOCK, (n + 1) * QBLOCK)
        out_ref[0, fr, :] = (att_scr[n] * gat_ref[0, :, fr]).T.astype(BF16)


def _attn_prompt(qt, kbf, vt, w_bias, gat):
    b, _, s = qt.shape
    fb = ATTN_BLOCKS * QBLOCK
    assert s % fb == 0
    return pl.pallas_call(
        _attn_prompt_kernel,
        grid=(b, s // fb),
        in_specs=[
            pl.BlockSpec((1, W_ATT, fb), lambda i, j: (i, 0, j)),
            pl.BlockSpec((1, s, W_ATT), lambda i, j: (i, 0, 0)),
            pl.BlockSpec((1, W_ATT, s), lambda i, j: (i, 0, 0)),
            pl.BlockSpec((N_HEADS, 1, PROMPT_BIAS_PERIOD), lambda i, j: (0, 0, 0)),
            pl.BlockSpec((1, W_ATT, fb), lambda i, j: (i, 0, j)),
        ],
        out_specs=pl.BlockSpec((1, fb, W_ATT), lambda i, j: (i, j, 0)),
        out_shape=jax.ShapeDtypeStruct((b, s, W_ATT), BF16),
        scratch_shapes=[
            pltpu.VMEM((N_HEADS, KEY_TILES * QBLOCK, QBLOCK), F32),
            pltpu.VMEM((SCORE_LEAD + 1, KEY_TILES * QBLOCK, QBLOCK), F32),
            pltpu.VMEM((ATTN_BLOCKS, W_ATT, QBLOCK), F32),
        ],
        compiler_params=pltpu.CompilerParams(
            dimension_semantics=("arbitrary", "arbitrary"), vmem_limit_bytes=VMEM_LIMIT),
        name="attn_prompt",
    )(qt, kbf, vt, w_bias, gat)


def _back_rows(x_ref, lru_ref, att_ref, p_ref, wo_ref, pg_ref, wpg_ref, wpe_ref, y_ref):
    mix = _dot(lru_ref[...], wo_ref[0:W_LRU, :]) + _dot(att_ref[...], wo_ref[W_LRU:, :])
    h = x_ref[...] + mix
    gate = jax.nn.sigmoid(_dot(_rms_rows(h, pg_ref[...]).astype(BF16), wpg_ref[...]))
    y_ref[...] = h + _dot(p_ref[...].astype(BF16), wpe_ref[...]) * gate


def _back_kernel(xp_ref, lrup_ref, attp_ref, pp_ref, xs_ref, lrus_ref, ps_ref,
                 q_ref, kn_ref, vn_ref, kc_ref, vc_ref, ga_ref, wb_ref,
                 wo_ref, pg_ref, wpg_ref, wpe_ref, yp_ref, ys_ref, bias_scr, atts_scr, *, prompt_steps):
    step = pl.program_id(0)
    weights = (wo_ref, pg_ref, wpg_ref, wpe_ref)
    streams, tt = q_ref.shape[0], q_ref.shape[1]

    @pl.when(step == 0)
    def _():
        _fill_sample_bias(wb_ref, bias_scr, tt)

    @pl.when(step < prompt_steps)
    def _():
        def write(s, lanes, rows):
            atts_scr[pl.ds(pl.multiple_of((step * streams + s) * tt, tt), tt), lanes] = rows

        _attn_sample_streams(q_ref, kn_ref, vn_ref, kc_ref, vc_ref, ga_ref, bias_scr, write)
        _back_rows(xp_ref, lrup_ref, attp_ref, pp_ref, *weights, yp_ref)

    @pl.when(step == prompt_steps)
    def _():
        _back_rows(xs_ref, lrus_ref, atts_scr, ps_ref, *weights, ys_ref)


def _back(prompt, sample, sample_attn, wo, pg, wpg, wpe, rows):
    n, ns = prompt[0].shape[0], sample[0].shape[0]
    q3, k3, v3, kc_t, vc_t, ga3, w_bias = sample_attn
    db, tt, _ = q3.shape
    assert n % rows == 0 and db * tt == ns
    steps = n // rows
    assert db % steps == 0
    per_step = db // steps
    const = lambda shape: pl.BlockSpec(shape, lambda i: (0,) * len(shape))
    walk = lambda w: pl.BlockSpec((rows, w), lambda i: (jnp.minimum(i, steps - 1), 0))
    whole = lambda w: pl.BlockSpec((ns, w), lambda i: (0, 0), pipeline_mode=pl.Buffered(1))
    seq = pl.BlockSpec((per_step, tt, W_ATT), lambda i: (jnp.minimum(i, steps - 1), 0, 0))
    cache = pl.BlockSpec((per_step, N_HEADS, HEAD_DIM, kc_t.shape[-1]),
                         lambda i: (jnp.minimum(i, steps - 1), 0, 0, 0))
    period = w_bias.shape[-1]
    return pl.pallas_call(
        functools.partial(_back_kernel, prompt_steps=steps),
        grid=(steps + 1,),
        in_specs=[walk(w) for w in (D_MODEL, W_LRU, W_ATT, PLE_DIM)]
        + [whole(w) for w in (D_MODEL, W_LRU, PLE_DIM)]
        + [seq, seq, seq, cache, cache, seq, const((N_HEADS, 1, period))]
        + [const((W_LRU + W_ATT, D_MODEL)), const((1, D_MODEL)), const((D_MODEL, D_MODEL)),
           const((PLE_DIM, D_MODEL))],
        out_specs=[walk(D_MODEL), pl.BlockSpec((ns, D_MODEL), lambda i: (0, 0))],
        out_shape=[jax.ShapeDtypeStruct((n, D_MODEL), F32), jax.ShapeDtypeStruct((ns, D_MODEL), F32)],
        scratch_shapes=[
            pltpu.VMEM((N_GROUPS, HEADS_PER_GROUP * tt, period), F32),
            pltpu.VMEM((ns, W_ATT), BF16),
        ],
        compiler_params=pltpu.CompilerParams(
            dimension_semantics=("arbitrary",), vmem_limit_bytes=VMEM_LIMIT),
        name="back",
    )(*prompt, *sample, q3, k3, v3, kc_t, vc_t, ga3, w_bias, wo, pg, wpg, wpe)


def _front_sample_kernel(x_ref, ng_ref, win_ref, cw_ref, cbias_ref, wg_ref, bg_ref, lam_ref,
                         qg_ref, kg_ref, ones_ref, sconv_ref, slru_ref,
                         lru_ref, q_ref, k_ref, v_ref, ga_ref, sc_ref, sh_ref,
                         cb_scr, xc_scr, a_scr, u_scr, h_scr, *, nb, tt):
    seg = tt + SUBLANES
    xn = _rms_rows(x_ref[...], ng_ref[...]).astype(BF16)
    z = _dot(xn, win_ref[...])
    xl = z[:, :W_LRU]
    for s in range(nb):
        cb_scr[s * seg:s * seg + SUBLANES, :] = sconv_ref[s]
        cb_scr[s * seg + SUBLANES:(s + 1) * seg, :] = xl[s * tt:(s + 1) * tt, :]
        xc_scr[s * tt:(s + 1) * tt, :] = _conv_rows(cb_scr, s * seg, tt, cw_ref, cbias_ref[...])
        sc_ref[s] = cb_scr[(s + 1) * seg - (CONV_W - 1):(s + 1) * seg, :]
    _lru_inputs(xc_scr[...], wg_ref, bg_ref, lam_ref, a_scr, u_scr)
    for s in range(nb):
        sh_ref[s] = _scan_rows(a_scr, u_scr, h_scr, s * tt, tt, slru_ref[s], unroll=True)
    lru_ref[...] = (h_scr[...] * jax.nn.silu(z[:, W_LRU:2 * W_LRU])).astype(BF16)

    o = 2 * W_LRU
    q = _head_norm_rows(z[:, o:o + W_ATT], ones_ref[...], qg_ref[...])
    q_ref[...] = (q * (HEAD_DIM ** -0.5)).astype(BF16)
    k_ref[...] = _head_norm_rows(z[:, o + W_ATT:o + 2 * W_ATT], ones_ref[...], kg_ref[...])
    v_ref[...] = z[:, o + 2 * W_ATT:o + 3 * W_ATT]
    ga_ref[...] = jax.nn.silu(z[:, o + 3 * W_ATT:])


def _front_sample(x2, ng, win, cw, cbias, wg, bg, lam, qg_t, kg_t, ones_bd, sconv_pad, slru, nb, tt):
    n = x2.shape[0]
    rows = nb * tt
    const = lambda shape: pl.BlockSpec(shape, lambda i: (0,) * len(shape))
    row_spec = lambda w: pl.BlockSpec((rows, w), lambda i: (i, 0))
    nseq = n // tt
    kern = functools.partial(_front_sample_kernel, nb=nb, tt=tt)
    return pl.pallas_call(
        kern,
        grid=(n // rows,),
        in_specs=[
            row_spec(D_MODEL),
            const((1, D_MODEL)),
            const((D_MODEL, 2 * W_LRU + 4 * W_ATT)),
            const((CONV_W, W_LRU)),
            const((1, W_LRU)),
            const((W_LRU // HEAD_GROUP, HEAD_GROUP, 2 * HEAD_GROUP)),
            const((W_LRU // HEAD_GROUP, 1, 2 * HEAD_GROUP)),
            const((1, W_LRU)),
            const((1, W_ATT)),
            const((1, W_ATT)),
            const((HEAD_GROUP, HEAD_GROUP)),
            pl.BlockSpec((nb, SUBLANES, W_LRU), lambda i: (i, 0, 0)),
            pl.BlockSpec((nb, 1, W_LRU), lambda i: (i, 0, 0)),
        ],
        out_specs=[
            row_spec(W_LRU), row_spec(W_ATT), row_spec(W_ATT), row_spec(W_ATT), row_spec(W_ATT),
            pl.BlockSpec((nb, CONV_W - 1, W_LRU), lambda i: (i, 0, 0)),
            pl.BlockSpec((nb, 1, W_LRU), lambda i: (i, 0, 0)),
        ],
        out_shape=[
            jax.ShapeDtypeStruct((n, W_LRU), BF16),
            jax.ShapeDtypeStruct((n, W_ATT), BF16),
            jax.ShapeDtypeStruct((n, W_ATT), F32),
            jax.ShapeDtypeStruct((n, W_ATT), F32),
            jax.ShapeDtypeStruct((n, W_ATT), F32),
            jax.ShapeDtypeStruct((nseq, CONV_W - 1, W_LRU), F32),
            jax.ShapeDtypeStruct((nseq, 1, W_LRU), F32),
        ],
        scratch_shapes=[
            pltpu.VMEM((nb * (tt + SUBLANES), W_LRU), F32),
            pltpu.VMEM((rows, W_LRU), F32),
            pltpu.VMEM((rows, W_LRU), F32),
            pltpu.VMEM((rows, W_LRU), F32),
            pltpu.VMEM((rows, W_LRU), F32),
        ],
        compiler_params=pltpu.CompilerParams(
            dimension_semantics=("arbitrary",), vmem_limit_bytes=VMEM_LIMIT),
        name="front_sample",
    )(x2, ng, win, cw, cbias, wg, bg, lam, qg_t, kg_t, ones_bd, sconv_pad, slru)


def _fill_sample_bias(w_ref, bias_ref, tt):
    for h in range(N_HEADS):
        g, hl = divmod(h, HEADS_PER_GROUP)
        bias_ref[g, hl * tt:(hl + 1) * tt, :] = _toeplitz_rows(w_ref[h], tt, 0)


def _attn_sample_streams(q_ref, kn_ref, vn_ref, kc_ref, vc_ref, ga_ref, bias_ref, write):
    tt = q_ref.shape[1]
    lc = kc_ref.shape[-1]
    lanes = lax.broadcasted_iota(jnp.int32, (tt, HEAD_GROUP), 1)
    masks = [(lanes >= hl * HEAD_DIM) & (lanes < (hl + 1) * HEAD_DIM) for hl in range(HEADS_PER_GROUP)]
    units = [(s, g) for s in range(q_ref.shape[0]) for g in range(N_GROUPS)]

    def cached(ref, s, g):
        heads = ref[s, g * HEADS_PER_GROUP:(g + 1) * HEADS_PER_GROUP]
        return heads.reshape(HEAD_GROUP, lc).astype(BF16)

    def scores(s, g):
        gsl = slice(g * HEAD_GROUP, (g + 1) * HEAD_GROUP)
        qg = q_ref[s, :, gsl]
        qs = jnp.concatenate([jnp.where(m, qg, jnp.zeros((), BF16)) for m in masks], axis=0)
        sc = _dot(qs, cached(kc_ref, s, g)) + bias_ref[g, :, 0:lc]
        sn = _dot_nt(qs, kn_ref[s, :, gsl].astype(BF16)) + bias_ref[g, :, lc:lc + tt]
        return sc, sn

    def finish(s, g, sc, sn):
        gsl = slice(g * HEAD_GROUP, (g + 1) * HEAD_GROUP)
        m = jnp.maximum(jnp.max(sc, axis=-1, keepdims=True), jnp.max(sn, axis=-1, keepdims=True))
        pc = jnp.exp(sc - m)
        pn = jnp.exp(sn - m)
        l = jnp.sum(pc, axis=-1, keepdims=True) + jnp.sum(pn, axis=-1, keepdims=True)
        o = _dot_nt(pc.astype(BF16), cached(vc_ref, s, g))
        o = (o + _dot(pn.astype(BF16), vn_ref[s, :, gsl].astype(BF16))) * (1.0 / l)
        att = jnp.zeros((tt, HEAD_GROUP), F32)
        for hl in range(HEADS_PER_GROUP):
            att = att + jnp.where(masks[hl], o[hl * tt:(hl + 1) * tt, :], 0.0)
        write(s, gsl, (att * ga_ref[s, :, gsl]).astype(BF16))

    nxt = scores(*units[0])
    for n, (s, g) in enumerate(units):
        cur = nxt
        if n + 1 < len(units):
            nxt = scores(*units[n + 1])
        finish(s, g, *cur)


def _block_diag(w):
    n, d, e = w.shape
    eye = jnp.eye(n, dtype=w.dtype)
    return (eye[:, None, :, None] * w[:, :, None, :]).reshape(n * d, n * e)


def _prompt_bias_period(table):
    assert BAND - MAX_REL == MAX_REL and PROMPT_BIAS_PERIOD == KEY_TILES * QBLOCK + QBLOCK
    last = table[2 * MAX_REL:]
    neg_d = jnp.concatenate([table, jnp.broadcast_to(last, (MAX_REL - 1, N_HEADS))])
    w = jnp.concatenate([jnp.broadcast_to(last, (QBLOCK, N_HEADS)), neg_d])
    return w.T.reshape(N_HEADS, 1, PROMPT_BIAS_PERIOD).astype(F32)


def _sample_bias_period(table, tt, l):
    assert l >= MAX_REL
    period = -(-(l + 2 * tt - 1) // LANES) * LANES
    last = table[2 * MAX_REL:]
    n_var = tt + MAX_REL - 1
    var = table[2 * MAX_REL - 1:2 * MAX_REL - 1 - n_var:-1]
    w = jnp.concatenate([jnp.broadcast_to(last, (l - MAX_REL + 1, N_HEADS)), var,
                         jnp.broadcast_to(last, (period - (l + tt), N_HEADS))])
    return w.T.reshape(N_HEADS, 1, period).astype(F32)


def kernel(x_prompt, x_sample, p_prompt, p_sample, cache_k, cache_v, state_conv, state_lru, norm_g, w_in, conv_w, conv_b, gate_a_w, gate_a_b, gate_x_w, gate_x_b, lru_lambda, q_norm_g, k_norm_g, rel_bias, w_out, ple_norm_g, w_ple_gate, w_ple_proj):
    depth = w_in.shape[0]
    b, s, _ = x_prompt.shape
    db, ds, _ = x_sample.shape
    lc = cache_k.shape[2]
    yp, ys = x_prompt, x_sample.reshape(db * ds, D_MODEL)
    ones_bd = _block_diag(jnp.full((HEADS_PER_GROUP, HEAD_DIM, HEAD_DIM), 1.0 / HEAD_DIM, F32)).astype(BF16)
    gate_halves = W_LRU // HEAD_GROUP
    blocks_per_half = LRU_BLOCKS // gate_halves
    outs = [[] for _ in range(8)]
    for l in range(depth):
        win = w_in[l].astype(BF16)
        o = 2 * W_LRU
        wnat = jnp.concatenate([win[:, :o], win[:, o + W_ATT:o + 2 * W_ATT]], axis=1)
        wt = jnp.concatenate([win[:, o:o + W_ATT], win[:, o + 2 * W_ATT:]], axis=1).T
        ng = norm_g[l].reshape(1, D_MODEL)
        cw = conv_w[l]
        cbias = conv_b[l].reshape(1, W_LRU)
        wg = jnp.stack([
            jnp.concatenate([_block_diag(w[j * blocks_per_half:(j + 1) * blocks_per_half])
                             for w in (gate_a_w[l], gate_x_w[l])], axis=1)
            for j in range(gate_halves)]).astype(BF16)
        bg = jnp.concatenate([gate_a_b[l].reshape(gate_halves, 1, HEAD_GROUP),
                              gate_x_b[l].reshape(gate_halves, 1, HEAD_GROUP)], axis=2)
        lam = lru_lambda[l].reshape(1, W_LRU)
        qg_col = q_norm_g[l].reshape(HEAD_DIM, 1)
        qg_t = jnp.tile(q_norm_g[l], N_HEADS).reshape(1, W_ATT)
        kg_t = jnp.tile(k_norm_g[l], N_HEADS).reshape(1, W_ATT)
        wo = w_out[l].astype(BF16)
        pg = ple_norm_g[l].reshape(1, D_MODEL)
        wpg = w_ple_gate[l].astype(BF16)
        wpe = w_ple_proj[l].astype(BF16)

        lru_g, qt, kbf, vt, gat, pk, pv, pc, ph = _front_prompt(
            yp, ng, wnat, wt, cw, cbias, wg, bg, lam, qg_col, kg_t, ones_bd)
        att_g = _attn_prompt(qt, kbf, vt, _prompt_bias_period(rel_bias[l]), gat)
        to_frames = lambda a: jnp.transpose(a.reshape(b, N_HEADS, HEAD_DIM, a.shape[-1]), (0, 3, 1, 2))
        outs[0].append(to_frames(pk))
        outs[1].append(to_frames(pv))
        outs[2].append(pc)
        outs[3].append(ph.reshape(b, W_LRU))

        sconv_pad = jnp.pad(state_conv[l], ((0, 0), (SUBLANES - (CONV_W - 1), 0), (0, 0)))
        lru_s, q_s, k_s, v_s, ga_s, sc, sh = _front_sample(
            ys, ng, win, cw, cbias, wg, bg, lam, qg_t, kg_t, ones_bd,
            sconv_pad, state_lru[l].reshape(db, 1, W_LRU), SAMPLE_STREAMS_PER_BLOCK, ds)
        yp, ys = _back(
            (yp.reshape(b * s, D_MODEL), lru_g.reshape(b * s, W_LRU), att_g.reshape(b * s, W_ATT),
             p_prompt[l].reshape(b * s, PLE_DIM)),
            (ys, lru_s, p_sample[l].reshape(db * ds, PLE_DIM)),
            (q_s.reshape(db, ds, W_ATT), k_s.reshape(db, ds, W_ATT), v_s.reshape(db, ds, W_ATT),
             jnp.transpose(cache_k[l], (0, 2, 3, 1)), jnp.transpose(cache_v[l], (0, 2, 3, 1)),
             ga_s.reshape(db, ds, W_ATT), _sample_bias_period(rel_bias[l], ds, lc)),
            wo, pg, wpg, wpe, BACK_ROWS)
        yp = yp.reshape(b, s, D_MODEL)
        outs[4].append(k_s.reshape(db, ds, N_HEADS, HEAD_DIM))
        outs[5].append(v_s.reshape(db, ds, N_HEADS, HEAD_DIM))
        outs[6].append(sc)
        outs[7].append(sh.reshape(db, W_LRU))
    return (yp, ys.reshape(db, ds, D_MODEL)) + tuple(jnp.stack(o) for o in outs)
```

```python
import functools

import jax
import jax.numpy as jnp
from jax import lax
from jax.experimental import pallas as pl
from jax.experimental.pallas import tpu as pltpu

D_MODEL = 1024
CHUNK = 64
PAST_CHUNKS = 8
BAND = PAST_CHUNKS * CHUNK
W_LRU = D_MODEL // 2
LRU_BLOCKS = 8
CONV_W = 4
RG_C = 8.0
HEAD_DIM = 64
W_ATT = D_MODEL // 2
N_HEADS = W_ATT // HEAD_DIM
MAX_REL = 256
PLE_DIM = 256
EPS = 1e-6
NEG = -1e30
LOG2E = 1.4426950408889634

SUBLANES = 8
LANES = 128
HEAD_GROUP = 256
HEADS_PER_GROUP = HEAD_GROUP // HEAD_DIM
N_GROUPS = W_ATT // HEAD_GROUP
QBLOCK = 256
KEY_TILES = BAND // QBLOCK + 1
PROMPT_BIAS_PERIOD = (KEY_TILES + 1) * QBLOCK
FRONT_BLOCK = 1024
BACK_ROWS = 1024
SAMPLE_STREAMS_PER_BLOCK = 8
VMEM_LIMIT = 56 * 1024 * 1024

F32 = jnp.float32
BF16 = jnp.bfloat16


def _dot(a, b):
    return jnp.dot(a, b, preferred_element_type=F32)


def _dot_nt(a, b):
    return lax.dot_general(a, b, (((1,), (1,)), ((), ())), preferred_element_type=F32)


def _rms_rows(x, g):
    ms = jnp.mean(x * x, axis=-1, keepdims=True)
    return x * lax.rsqrt(ms + EPS) * g


def _head_norm_rows(x, ones_bd, g_tiled):
    x2 = x * x
    hi = x2.astype(BF16)
    lo = (x2 - hi.astype(F32)).astype(BF16)
    ms = jnp.concatenate(
        [_dot(hi[:, g * HEAD_GROUP:(g + 1) * HEAD_GROUP], ones_bd)
         + _dot(lo[:, g * HEAD_GROUP:(g + 1) * HEAD_GROUP], ones_bd) for g in range(N_GROUPS)], axis=1)
    return x * lax.rsqrt(ms + EPS) * g_tiled


def _scan_rows(a_ref, u_ref, h_ref, row0, nrows, h0, unroll=False):
    ridx = lax.broadcasted_iota(jnp.int32, (SUBLANES, W_LRU), 0)

    def body(i, hprev):
        r = pl.multiple_of(row0 + i * SUBLANES, SUBLANES)
        a = a_ref[pl.ds(r, SUBLANES), :]
        u = u_ref[pl.ds(r, SUBLANES), :]
        for s in (1, 2, 4):
            a_s = jnp.where(ridx >= s, pltpu.roll(a, s, 0), 1.0)
            u_s = jnp.where(ridx >= s, pltpu.roll(u, s, 0), 0.0)
            u = a * u_s + u
            a = a * a_s
        h = a * hprev + u
        h_ref[pl.ds(r, SUBLANES), :] = h
        return h[SUBLANES - 1:SUBLANES, :]

    return lax.fori_loop(0, nrows // SUBLANES, body, h0, unroll=unroll)


def _lru_inputs(xc, wg_ref, bg_ref, lam_ref, a_ref, u_ref):
    xcb = xc.astype(BF16)
    half = wg_ref.shape[1]
    for j in range(wg_ref.shape[0]):
        sl = slice(j * half, (j + 1) * half)
        gates = _dot(xcb[:, sl], wg_ref[j]) + bg_ref[j]
        r = jax.nn.sigmoid(gates[:, :half])
        i = jax.nn.sigmoid(gates[:, half:])
        log_a = -RG_C * r * jax.nn.softplus(-lam_ref[:, sl])
        a = jnp.exp(log_a)
        a_ref[:, sl] = a
        u_ref[:, sl] = jnp.sqrt(jnp.tanh(-log_a) * (1.0 + a * a)) * (i * xc[:, sl])


def _conv_rows(cb_ref, base, nrows, cw_ref, cb_bias):
    out = cb_bias + cw_ref[CONV_W - 1:CONV_W, :] * cb_ref[pl.ds(base + SUBLANES, nrows), :]
    for k in range(CONV_W - 1):
        shift = CONV_W - 1 - k
        out = out + cw_ref[k:k + 1, :] * cb_ref[pl.ds(base + SUBLANES - shift, nrows), :]
    return out


def _front_prompt_kernel(x_ref, ng_ref, wnat_ref, wt_ref, cw_ref, cbias_ref, wg_ref, bg_ref, lam_ref,
                         qg_ref, kg_ref, ones_ref,
                         lru_ref, qt_ref, kbf_ref, vt_ref, gat_ref, pk_ref, pv_ref, pc_ref, ph_ref,
                         zn_scr, zt_scr, cb_scr, a_scr, u_scr, h_scr, hlast_scr, *, keep_subs):
    sub = QBLOCK
    n_sub = x_ref.shape[1] // sub

    @pl.when(pl.program_id(1) == 0)
    def _():
        cb_scr[0:SUBLANES, :] = jnp.zeros((SUBLANES, W_LRU), F32)
        hlast_scr[...] = jnp.zeros((1, W_LRU), F32)

    def project(i):
        xn = _rms_rows(x_ref[0, i * sub:(i + 1) * sub, :], ng_ref[...]).astype(BF16)
        zn_scr[i % 2] = _dot(xn, wnat_ref[...])
        zt_scr[i % 2] = _dot_nt(wt_ref[...], xn)

    def finish(i):
        rows = slice(i * sub, (i + 1) * sub)
        zn = zn_scr.at[i % 2]
        zt = zt_scr.at[i % 2]
        cb_scr[SUBLANES:SUBLANES + sub, :] = zn[:, :W_LRU]
        xc = _conv_rows(cb_scr, 0, sub, cw_ref, cbias_ref[...])
        pc_ref[0] = cb_scr[sub + SUBLANES - (CONV_W - 1):sub + SUBLANES, :]
        cb_scr[0:SUBLANES, :] = cb_scr[sub:sub + SUBLANES, :]
        _lru_inputs(xc, wg_ref, bg_ref, lam_ref, a_scr, u_scr)
        h_last = _scan_rows(a_scr, u_scr, h_scr, 0, sub, hlast_scr[...], unroll=True)
        hlast_scr[...] = h_last
        ph_ref[0] = h_last
        lru_ref[0, rows, :] = (h_scr[...] * jax.nn.silu(zn[:, W_LRU:2 * W_LRU])).astype(BF16)
        k = _head_norm_rows(zn[:, 2 * W_LRU:], ones_ref[...], kg_ref[...])
        kbf_ref[0, rows, :] = k.astype(BF16)
        q3 = zt[0:W_ATT, :].reshape(N_HEADS, HEAD_DIM, sub)
        ms = jnp.mean(q3 * q3, axis=1, keepdims=True)
        qn = q3 * lax.rsqrt(ms + EPS) * (qg_ref[...] * (HEAD_DIM ** -0.5 * LOG2E))
        qt_ref[0, :, rows] = qn.reshape(W_ATT, sub).astype(BF16)
        vt = zt[W_ATT:2 * W_ATT, :]
        vt_ref[0, :, rows] = vt.astype(BF16)
        gat_ref[0, :, rows] = jax.nn.silu(zt[2 * W_ATT:, :])
        if i >= n_sub - keep_subs:
            first = (i - (n_sub - keep_subs)) * sub
            pk_ref[0, :, first:first + sub] = k.T
            pv_ref[0, :, first:first + sub] = vt

    project(0)
    for i in range(n_sub):
        if i + 1 < n_sub:
            project(i + 1)
        finish(i)


def _front_prompt(x, ng, wnat, wt, cw, cbias, wg, bg, lam, qg, kg, ones_bd):
    b, s, _ = x.shape
    tb = min(FRONT_BLOCK, s)
    keep = min(BAND, s)
    assert s % tb == 0 and tb % QBLOCK == 0 and keep % QBLOCK == 0 and keep <= tb
    const = lambda shape: pl.BlockSpec(shape, lambda i, j: (0,) * len(shape))
    rows_spec = lambda w: pl.BlockSpec((1, tb, w), lambda i, j: (i, j, 0))
    cols_spec = pl.BlockSpec((1, W_ATT, tb), lambda i, j: (i, 0, j))
    per_seq = lambda r, w: pl.BlockSpec((1, r, w), lambda i, j: (i, 0, 0))
    kern = functools.partial(_front_prompt_kernel, keep_subs=keep // QBLOCK)
    return pl.pallas_call(
        kern,
        grid=(b, s // tb),
        in_specs=[
            rows_spec(D_MODEL),
            const((1, D_MODEL)),
            const((D_MODEL, 3 * W_LRU)),
            const((3 * W_ATT, D_MODEL)),
            const((CONV_W, W_LRU)),
            const((1, W_LRU)),
            const((W_LRU // HEAD_GROUP, HEAD_GROUP, 2 * HEAD_GROUP)),
            const((W_LRU // HEAD_GROUP, 1, 2 * HEAD_GROUP)),
            const((1, W_LRU)),
            const((HEAD_DIM, 1)),
            const((1, W_ATT)),
            const((HEAD_GROUP, HEAD_GROUP)),
        ],
        out_specs=[
            rows_spec(W_LRU),
            cols_spec,
            rows_spec(W_ATT),
            cols_spec,
            cols_spec,
            per_seq(W_ATT, keep),
            per_seq(W_ATT, keep),
            per_seq(CONV_W - 1, W_LRU),
            per_seq(1, W_LRU),
        ],
        out_shape=[
            jax.ShapeDtypeStruct((b, s, W_LRU), BF16),
            jax.ShapeDtypeStruct((b, W_ATT, s), BF16),
            jax.ShapeDtypeStruct((b, s, W_ATT), BF16),
            jax.ShapeDtypeStruct((b, W_ATT, s), BF16),
            jax.ShapeDtypeStruct((b, W_ATT, s), F32),
            jax.ShapeDtypeStruct((b, W_ATT, keep), F32),
            jax.ShapeDtypeStruct((b, W_ATT, keep), F32),
            jax.ShapeDtypeStruct((b, CONV_W - 1, W_LRU), F32),
            jax.ShapeDtypeStruct((b, 1, W_LRU), F32),
        ],
        scratch_shapes=[
            pltpu.VMEM((2, QBLOCK, 3 * W_LRU), F32),
            pltpu.VMEM((2, 3 * W_ATT, QBLOCK), F32),
            pltpu.VMEM((QBLOCK + SUBLANES, W_LRU), F32),
            pltpu.VMEM((QBLOCK, W_LRU), F32),
            pltpu.VMEM((QBLOCK, W_LRU), F32),
            pltpu.VMEM((QBLOCK, W_LRU), F32),
            pltpu.VMEM((1, W_LRU), F32),
        ],
        compiler_params=pltpu.CompilerParams(
            dimension_semantics=("arbitrary", "arbitrary"), vmem_limit_bytes=VMEM_LIMIT),
        name="front_prompt",
    )(x, ng, wnat, wt, cw, cbias, wg, bg, lam, qg, kg, ones_bd)


def _toeplitz_rows(w_row, nrows, row0):
    x = jnp.broadcast_to(w_row, (nrows, w_row.shape[1]))
    return pltpu.roll(x, row0, 1, stride=1, stride_axis=0)


def _fill_prompt_bias(w_ref, bias_ref):
    q_chunk = (lax.broadcasted_iota(jnp.int32, (CHUNK, QBLOCK), 1) + BAND) // CHUNK
    for h in range(N_HEADS):
        def body(n, carry, h=h):
            r0 = pl.multiple_of(n * CHUNK, CHUNK)
            t = _toeplitz_rows(w_ref[h], CHUNK, r0)[:, :QBLOCK]
            dc = q_chunk - n
            bias_ref[h, pl.ds(r0, CHUNK), :] = jnp.where((dc >= 0) & (dc <= PAST_CHUNKS), t * LOG2E, NEG)
            return carry
        lax.fori_loop(0, KEY_TILES * QBLOCK // CHUNK, body, 0)


SCORE_LEAD = 4
ATTN_BLOCKS = 4
CHUNKS_PER_TILE = QBLOCK // CHUNK
FRAMES_PER_VREG = LANES // CHUNK


def _lane_cols(kc):
    cols = []
    for c in range(QBLOCK // LANES):
        q_lo = PAST_CHUNKS + c * FRAMES_PER_VREG
        q_hi = q_lo + FRAMES_PER_VREG - 1
        if q_lo - PAST_CHUNKS <= kc <= q_hi:
            cols.append(c)
    return cols


def _fold_rows(x):
    return x.reshape(x.shape[0] // SUBLANES, SUBLANES, x.shape[1])


def _attn_prompt_blocks(blocks, qt_ref, k_ref, vt_ref, gat_ref, out_ref, bias_ref, s_scr, att_scr):
    n_cols = QBLOCK // LANES
    rows = lax.broadcasted_iota(jnp.int32, (HEAD_GROUP, QBLOCK), 0)
    units = [(n, h) for n in range(len(blocks)) for h in range(N_HEADS)]

    def pieces(tiles):
        return [(i, cc, slice((i * CHUNKS_PER_TILE + cc) * CHUNK, (i * CHUNKS_PER_TILE + cc + 1) * CHUNK),
                 slice(c * LANES, (c + 1) * LANES), c)
                for i in tiles for cc in range(CHUNKS_PER_TILE) for c in _lane_cols(i * CHUNKS_PER_TILE + cc)]

    def key_rows(n, i):
        tiles, key0 = blocks[n]
        return pl.ds(key0 + (i - tiles[0]) * QBLOCK, QBLOCK)

    def scores(u):
        n, h = units[u]
        g, hl = divmod(h, HEADS_PER_GROUP)
        gsl = slice(g * HEAD_GROUP, (g + 1) * HEAD_GROUP)
        in_head = (rows >= hl * HEAD_DIM) & (rows < (hl + 1) * HEAD_DIM)
        qm = jnp.where(in_head, qt_ref[0, gsl, n * QBLOCK:(n + 1) * QBLOCK], jnp.zeros((), BF16))
        s = {i: _dot(k_ref[0, key_rows(n, i), gsl], qm) for i in blocks[n][0]}
        m_acc = [jnp.full((SUBLANES, LANES), NEG, F32) for _ in range(n_cols)]
        for i, cc, rsl, lsl, c in pieces(blocks[n][0]):
            sp = s[i][cc * CHUNK:(cc + 1) * CHUNK, lsl] + bias_ref[h, rsl, lsl]
            s_scr[u % (SCORE_LEAD + 1), rsl, lsl] = sp
            m_acc[c] = jnp.maximum(m_acc[c], jnp.max(_fold_rows(sp), axis=0))
        return [jnp.max(a, axis=0, keepdims=True) for a in m_acc]

    def weights(u, m):
        tiles_p = {}
        for i in blocks[units[u][0]][0]:
            chunks = []
            for cc in range(CHUNKS_PER_TILE):
                kc = i * CHUNKS_PER_TILE + cc
                rsl = slice(kc * CHUNK, (kc + 1) * CHUNK)
                cols = []
                for c in range(n_cols):
                    if c in _lane_cols(kc):
                        lsl = slice(c * LANES, (c + 1) * LANES)
                        cols.append(jnp.exp2(s_scr[u % (SCORE_LEAD + 1), rsl, lsl] - m[c]).astype(BF16))
                    else:
                        cols.append(jnp.zeros((CHUNK, LANES), BF16))
                chunks.append(jnp.concatenate(cols, axis=1))
            tiles_p[i] = jnp.concatenate(chunks, axis=0)
        return tiles_p

    ones_rows = jnp.ones((2 * SUBLANES, QBLOCK), BF16)

    def values(u, tiles_p):
        n, h = units[u]
        hsl = slice(h * HEAD_DIM, (h + 1) * HEAD_DIM)
        o = None
        for i in blocks[n][0]:
            oi = _dot(jnp.concatenate([vt_ref[0, hsl, key_rows(n, i)], ones_rows], axis=0), tiles_p[i])
            o = oi if o is None else o + oi
        att_scr[n, hsl, :] = o[0:HEAD_DIM, :] * (1.0 / o[HEAD_DIM:HEAD_DIM + 1, :])
        if h == N_HEADS - 1:
            fr = slice(n * QBLOCK, (n + 1) * QBLOCK)
            out_ref[0, fr, :] = (att_scr[n] * gat_ref[0, :, fr]).T.astype(BF16)

    m = {u: scores(u) for u in range(min(SCORE_LEAD, len(units)))}
    w = {}
    for u in range(len(units)):
        if u + SCORE_LEAD < len(units):
            m[u + SCORE_LEAD] = scores(u + SCORE_LEAD)
        w[u] = weights(u, m.pop(u))
        if u >= 1:
            values(u - 1, w.pop(u - 1))
    values(len(units) - 1, w.pop(len(units) - 1))


def _attn_prompt_kernel(qt_ref, k_ref, vt_ref, w_ref, gat_ref, out_ref, bias_ref, s_scr, att_scr):
    step = pl.program_id(1)

    @pl.when((pl.program_id(0) == 0) & (step == 0))
    def _():
        _fill_prompt_bias(w_ref, bias_ref)

    assert ATTN_BLOCKS >= KEY_TILES - 1
    full = tuple(range(KEY_TILES))

    @pl.when(step == 0)
    def _():
        blocks = [(full[max(KEY_TILES - 1 - n, 0):], max(n - (KEY_TILES - 1), 0) * QBLOCK)
                  for n in range(ATTN_BLOCKS)]
        _attn_prompt_blocks(blocks, qt_ref, k_ref, vt_ref, gat_ref, out_ref, bias_ref, s_scr, att_scr)

    if k_ref.shape[1] > ATTN_BLOCKS * QBLOCK:
        @pl.when(step > 0)
        def _():
            first = step * ATTN_BLOCKS - (KEY_TILES - 1)
            blocks = [(full, pl.multiple_of((first + n) * QBLOCK, QBLOCK)) for n in range(ATTN_BLOCKS)]
            _attn_prompt_blocks(blocks, qt_ref, k_ref, vt_ref, gat_ref, out_ref, bias_ref, s_scr, att_scr)


def _attn_prompt(qt, kbf, vt, w_bias, gat):
    b, _, s = qt.shape
    fb = ATTN_BLOCKS * QBLOCK
    assert s % fb == 0
    return pl.pallas_call(
        _attn_prompt_kernel,
        grid=(b, s // fb),
        in_specs=[
            pl.BlockSpec((1, W_ATT, fb), lambda i, j: (i, 0, j)),
            pl.BlockSpec((1, s, W_ATT), lambda i, j: (i, 0, 0)),
            pl.BlockSpec((1, W_ATT, s), lambda i, j: (i, 0, 0)),
            pl.BlockSpec((N_HEADS, 1, PROMPT_BIAS_PERIOD), lambda i, j: (0, 0, 0)),
            pl.BlockSpec((1, W_ATT, fb), lambda i, j: (i, 0, j)),
        ],
        out_specs=pl.BlockSpec((1, fb, W_ATT), lambda i, j: (i, j, 0)),
        out_shape=jax.ShapeDtypeStruct((b, s, W_ATT), BF16),
        scratch_shapes=[
            pltpu.VMEM((N_HEADS, KEY_TILES * QBLOCK, QBLOCK), F32),
            pltpu.VMEM((SCORE_LEAD + 1, KEY_TILES * QBLOCK, QBLOCK), F32),
            pltpu.VMEM((ATTN_BLOCKS, W_ATT, QBLOCK), F32),
        ],
        compiler_params=pltpu.CompilerParams(
            dimension_semantics=("arbitrary", "arbitrary"), vmem_limit_bytes=VMEM_LIMIT),
        name="attn_prompt",
    )(qt, kbf, vt, w_bias, gat)


def _back_rows(x_ref, lru_ref, att_ref, p_ref, wo_ref, pg_ref, wpg_ref, wpe_ref, y_ref):
    mix = _dot(lru_ref[...], wo_ref[0:W_LRU, :]) + _dot(att_ref[...], wo_ref[W_LRU:, :])
    h = x_ref[...] + mix
    gate = jax.nn.sigmoid(_dot(_rms_rows(h, pg_ref[...]).astype(BF16), wpg_ref[...]))
    y_ref[...] = h + _dot(p_ref[...].astype(BF16), wpe_ref[...]) * gate


def _back_kernel(xp_ref, lrup_ref, attp_ref, pp_ref, xs_ref, lrus_ref, ps_ref,
                 q_ref, kn_ref, vn_ref, kc_ref, vc_ref, ga_ref, wb_ref,
                 wo_ref, pg_ref, wpg_ref, wpe_ref, yp_ref, ys_ref, bias_scr, atts_scr, *, prompt_steps):
    step = pl.program_id(0)
    weights = (wo_ref, pg_ref, wpg_ref, wpe_ref)
    streams, tt = q_ref.shape[0], q_ref.shape[1]

    @pl.when(step == 0)
    def _():
        _fill_sample_bias(wb_ref, bias_scr, tt)

    @pl.when(step < prompt_steps)
    def _():
        def write(s, lanes, rows):
            atts_scr[pl.ds(pl.multiple_of((step * streams + s) * tt, tt), tt), lanes] = rows

        _attn_sample_streams(q_ref, kn_ref, vn_ref, kc_ref, vc_ref, ga_ref, bias_scr, write)
        _back_rows(xp_ref, lrup_ref, attp_ref, pp_ref, *weights, yp_ref)

    @pl.when(step == prompt_steps)
    def _():
        _back_rows(xs_ref, lrus_ref, atts_scr, ps_ref, *weights, ys_ref)


def _back(prompt, sample, sample_attn, wo, pg, wpg, wpe, rows):
    n, ns = prompt[0].shape[0], sample[0].shape[0]
    q3, k3, v3, kc_t, vc_t, ga3, w_bias = sample_attn
    db, tt, _ = q3.shape
    assert n % rows == 0 and db * tt == ns
    steps = n // rows
    assert db % steps == 0
    per_step = db // steps
    const = lambda shape: pl.BlockSpec(shape, lambda i: (0,) * len(shape))
    walk = lambda w: pl.BlockSpec((rows, w), lambda i: (jnp.minimum(i, steps - 1), 0))
    whole = lambda w: pl.BlockSpec((ns, w), lambda i: (0, 0), pipeline_mode=pl.Buffered(1))
    seq = pl.BlockSpec((per_step, tt, W_ATT), lambda i: (jnp.minimum(i, steps - 1), 0, 0))
    cache = pl.BlockSpec((per_step, N_HEADS, HEAD_DIM, kc_t.shape[-1]),
                         lambda i: (jnp.minimum(i, steps - 1), 0, 0, 0))
    period = w_bias.shape[-1]
    return pl.pallas_call(
        functools.partial(_back_kernel, prompt_steps=steps),
        grid=(steps + 1,),
        in_specs=[walk(w) for w in (D_MODEL, W_LRU, W_ATT, PLE_DIM)]
        + [whole(w) for w in (D_MODEL, W_LRU, PLE_DIM)]
        + [seq, seq, seq, cache, cache, seq, const((N_HEADS, 1, period))]
        + [const((W_LRU + W_ATT, D_MODEL)), const((1, D_MODEL)), const((D_MODEL, D_MODEL)),
           const((PLE_DIM, D_MODEL))],
        out_specs=[walk(D_MODEL), pl.BlockSpec((ns, D_MODEL), lambda i: (0, 0))],
        out_shape=[jax.ShapeDtypeStruct((n, D_MODEL), F32), jax.ShapeDtypeStruct((ns, D_MODEL), F32)],
        scratch_shapes=[
            pltpu.VMEM((N_GROUPS, HEADS_PER_GROUP * tt, period), F32),
            pltpu.VMEM((ns, W_ATT), BF16),
        ],
        compiler_params=pltpu.CompilerParams(
            dimension_semantics=("arbitrary",), vmem_limit_bytes=VMEM_LIMIT),
        name="back",
    )(*prompt, *sample, q3, k3, v3, kc_t, vc_t, ga3, w_bias, wo, pg, wpg, wpe)


def _front_sample_kernel(x_ref, ng_ref, win_ref, cw_ref, cbias_ref, wg_ref, bg_ref, lam_ref,
                         qg_ref, kg_ref, ones_ref, sconv_ref, slru_ref,
                         lru_ref, q_ref, k_ref, v_ref, ga_ref, sc_ref, sh_ref,
                         cb_scr, xc_scr, a_scr, u_scr, h_scr, *, nb, tt):
    seg = tt + SUBLANES
    xn = _rms_rows(x_ref[...], ng_ref[...]).astype(BF16)
    z = _dot(xn, win_ref[...])
    xl = z[:, :W_LRU]
    for s in range(nb):
        cb_scr[s * seg:s * seg + SUBLANES, :] = sconv_ref[s]
        cb_scr[s * seg + SUBLANES:(s + 1) * seg, :] = xl[s * tt:(s + 1) * tt, :]
        xc_scr[s * tt:(s + 1) * tt, :] = _conv_rows(cb_scr, s * seg, tt, cw_ref, cbias_ref[...])
        sc_ref[s] = cb_scr[(s + 1) * seg - (CONV_W - 1):(s + 1) * seg, :]
    _lru_inputs(xc_scr[...], wg_ref, bg_ref, lam_ref, a_scr, u_scr)
    for s in range(nb):
        sh_ref[s] = _scan_rows(a_scr, u_scr, h_scr, s * tt, tt, slru_ref[s], unroll=True)
    lru_ref[...] = (h_scr[...] * jax.nn.silu(z[:, W_LRU:2 * W_LRU])).astype(BF16)

    o = 2 * W_LRU
    q = _head_norm_rows(z[:, o:o + W_ATT], ones_ref[...], qg_ref[...])
    q_ref[...] = (q * (HEAD_DIM ** -0.5)).astype(BF16)
    k_ref[...] = _head_norm_rows(z[:, o + W_ATT:o + 2 * W_ATT], ones_ref[...], kg_ref[...])
    v_ref[...] = z[:, o + 2 * W_ATT:o + 3 * W_ATT]
    ga_ref[...] = jax.nn.silu(z[:, o + 3 * W_ATT:])


def _front_sample(x2, ng, win, cw, cbias, wg, bg, lam, qg_t, kg_t, ones_bd, sconv_pad, slru, nb, tt):
    n = x2.shape[0]
    rows = nb * tt
    const = lambda shape: pl.BlockSpec(shape, lambda i: (0,) * len(shape))
    row_spec = lambda w: pl.BlockSpec((rows, w), lambda i: (i, 0))
    nseq = n // tt
    kern = functools.partial(_front_sample_kernel, nb=nb, tt=tt)
    return pl.pallas_call(
        kern,
        grid=(n // rows,),
        in_specs=[
            row_spec(D_MODEL),
            const((1, D_MODEL)),
            const((D_MODEL, 2 * W_LRU + 4 * W_ATT)),
            const((CONV_W, W_LRU)),
            const((1, W_LRU)),
            const((W_LRU // HEAD_GROUP, HEAD_GROUP, 2 * HEAD_GROUP)),
            const((W_LRU // HEAD_GROUP, 1, 2 * HEAD_GROUP)),
            const((1, W_LRU)),
            const((1, W_ATT)),
            const((1, W_ATT)),
            const((HEAD_GROUP, HEAD_GROUP)),
            pl.BlockSpec((nb, SUBLANES, W_LRU), lambda i: (i, 0, 0)),
            pl.BlockSpec((nb, 1, W_LRU), lambda i: (i, 0, 0)),
        ],
        out_specs=[
            row_spec(W_LRU), row_spec(W_ATT), row_spec(W_ATT), row_spec(W_ATT), row_spec(W_ATT),
            pl.BlockSpec((nb, CONV_W - 1, W_LRU), lambda i: (i, 0, 0)),
            pl.BlockSpec((nb, 1, W_LRU), lambda i: (i, 0, 0)),
        ],
        out_shape=[
            jax.ShapeDtypeStruct((n, W_LRU), BF16),
            jax.ShapeDtypeStruct((n, W_ATT), BF16),
            jax.ShapeDtypeStruct((n, W_ATT), F32),
            jax.ShapeDtypeStruct((n, W_ATT), F32),
            jax.ShapeDtypeStruct((n, W_ATT), F32),
            jax.ShapeDtypeStruct((nseq, CONV_W - 1, W_LRU), F32),
            jax.ShapeDtypeStruct((nseq, 1, W_LRU), F32),
        ],
        scratch_shapes=[
            pltpu.VMEM((nb * (tt + SUBLANES), W_LRU), F32),
            pltpu.VMEM((rows, W_LRU), F32),
            pltpu.VMEM((rows, W_LRU), F32),
            pltpu.VMEM((rows, W_LRU), F32),
            pltpu.VMEM((rows, W_LRU), F32),
        ],
        compiler_params=pltpu.CompilerParams(
            dimension_semantics=("arbitrary",), vmem_limit_bytes=VMEM_LIMIT),
        name="front_sample",
    )(x2, ng, win, cw, cbias, wg, bg, lam, qg_t, kg_t, ones_bd, sconv_pad, slru)


def _fill_sample_bias(w_ref, bias_ref, tt):
    for h in range(N_HEADS):
        g, hl = divmod(h, HEADS_PER_GROUP)
        bias_ref[g, hl * tt:(hl + 1) * tt, :] = _toeplitz_rows(w_ref[h], tt, 0)


def _attn_sample_streams(q_ref, kn_ref, vn_ref, kc_ref, vc_ref, ga_ref, bias_ref, write):
    tt = q_ref.shape[1]
    lc = kc_ref.shape[-1]
    lanes = lax.broadcasted_iota(jnp.int32, (tt, HEAD_GROUP), 1)
    masks = [(lanes >= hl * HEAD_DIM) & (lanes < (hl + 1) * HEAD_DIM) for hl in range(HEADS_PER_GROUP)]
    units = [(s, g) for s in range(q_ref.shape[0]) for g in range(N_GROUPS)]

    def cached(ref, s, g):
        heads = ref[s, g * HEADS_PER_GROUP:(g + 1) * HEADS_PER_GROUP]
        return heads.reshape(HEAD_GROUP, lc).astype(BF16)

    def scores(s, g):
        gsl = slice(g * HEAD_GROUP, (g + 1) * HEAD_GROUP)
        qg = q_ref[s, :, gsl]
        qs = jnp.concatenate([jnp.where(m, qg, jnp.zeros((), BF16)) for m in masks], axis=0)
        sc = _dot(qs, cached(kc_ref, s, g)) + bias_ref[g, :, 0:lc]
        sn = _dot_nt(qs, kn_ref[s, :, gsl].astype(BF16)) + bias_ref[g, :, lc:lc + tt]
        return sc, sn

    def finish(s, g, sc, sn):
        gsl = slice(g * HEAD_GROUP, (g + 1) * HEAD_GROUP)
        m = jnp.maximum(jnp.max(sc, axis=-1, keepdims=True), jnp.max(sn, axis=-1, keepdims=True))
        pc = jnp.exp(sc - m)
        pn = jnp.exp(sn - m)
        l = jnp.sum(pc, axis=-1, keepdims=True) + jnp.sum(pn, axis=-1, keepdims=True)
        o = _dot_nt(pc.astype(BF16), cached(vc_ref, s, g))
        o = (o + _dot(pn.astype(BF16), vn_ref[s, :, gsl].astype(BF16))) * (1.0 / l)
        att = jnp.zeros((tt, HEAD_GROUP), F32)
        for hl in range(HEADS_PER_GROUP):
            att = att + jnp.where(masks[hl], o[hl * tt:(hl + 1) * tt, :], 0.0)
        write(s, gsl, (att * ga_ref[s, :, gsl]).astype(BF16))

    nxt = scores(*units[0])
    for n, (s, g) in enumerate(units):
        cur = nxt
        if n + 1 < len(units):
            nxt = scores(*units[n + 1])
        finish(s, g, *cur)


def _block_diag(w):
    n, d, e = w.shape
    eye = jnp.eye(n, dtype=w.dtype)
    return (eye[:, None, :, None] * w[:, :, None, :]).reshape(n * d, n * e)


def _prompt_bias_period(table):
    assert BAND - MAX_REL == MAX_REL and PROMPT_BIAS_PERIOD == KEY_TILES * QBLOCK + QBLOCK
    last = table[2 * MAX_REL:]
    neg_d = jnp.concatenate([table, jnp.broadcast_to(last, (MAX_REL - 1, N_HEADS))])
    w = jnp.concatenate([jnp.broadcast_to(last, (QBLOCK, N_HEADS)), neg_d])
    return w.T.reshape(N_HEADS, 1, PROMPT_BIAS_PERIOD).astype(F32)


def _sample_bias_period(table, tt, l):
    assert l >= MAX_REL
    period = -(-(l + 2 * tt - 1) // LANES) * LANES
    last = table[2 * MAX_REL:]
    n_var = tt + MAX_REL - 1
    var = table[2 * MAX_REL - 1:2 * MAX_REL - 1 - n_var:-1]
    w = jnp.concatenate([jnp.broadcast_to(last, (l - MAX_REL + 1, N_HEADS)), var,
                         jnp.broadcast_to(last, (period - (l + tt), N_HEADS))])
    return w.T.reshape(N_HEADS, 1, period).astype(F32)


def kernel(x_prompt, x_sample, p_prompt, p_sample, cache_k, cache_v, state_conv, state_lru, norm_g, w_in, conv_w, conv_b, gate_a_w, gate_a_b, gate_x_w, gate_x_b, lru_lambda, q_norm_g, k_norm_g, rel_bias, w_out, ple_norm_g, w_ple_gate, w_ple_proj):
    depth = w_in.shape[0]
    b, s, _ = x_prompt.shape
    db, ds, _ = x_sample.shape
    lc = cache_k.shape[2]
    yp, ys = x_prompt, x_sample.reshape(db * ds, D_MODEL)
    ones_bd = _block_diag(jnp.full((HEADS_PER_GROUP, HEAD_DIM, HEAD_DIM), 1.0 / HEAD_DIM, F32)).astype(BF16)
    gate_halves = W_LRU // HEAD_GROUP
    blocks_per_half = LRU_BLOCKS // gate_halves
    outs = [[] for _ in range(8)]
    for l in range(depth):
        win = w_in[l].astype(BF16)
        o = 2 * W_LRU
        wnat = jnp.concatenate([win[:, :o], win[:, o + W_ATT:o + 2 * W_ATT]], axis=1)
        wt = jnp.concatenate([win[:, o:o + W_ATT], win[:, o + 2 * W_ATT:]], axis=1).T
        ng = norm_g[l].reshape(1, D_MODEL)
        cw = conv_w[l]
        cbias = conv_b[l].reshape(1, W_LRU)
        wg = jnp.stack([
            jnp.concatenate([_block_diag(w[j * blocks_per_half:(j + 1) * blocks_per_half])
                             for w in (gate_a_w[l], gate_x_w[l])], axis=1)
            for j in range(gate_halves)]).astype(BF16)
        bg = jnp.concatenate([gate_a_b[l].reshape(gate_halves, 1, HEAD_GROUP),
                              gate_x_b[l].reshape(gate_halves, 1, HEAD_GROUP)], axis=2)
        lam = lru_lambda[l].reshape(1, W_LRU)
        qg_col = q_norm_g[l].reshape(HEAD_DIM, 1)
        qg_t = jnp.tile(q_norm_g[l], N_HEADS).reshape(1, W_ATT)
        kg_t = jnp.tile(k_norm_g[l], N_HEADS).reshape(1, W_ATT)
        wo = w_out[l].astype(BF16)
        pg = ple_norm_g[l].reshape(1, D_MODEL)
        wpg = w_ple_gate[l].astype(BF16)
        wpe = w_ple_proj[l].astype(BF16)

        lru_g, qt, kbf, vt, gat, pk, pv, pc, ph = _front_prompt(
            yp, ng, wnat, wt, cw, cbias, wg, bg, lam, qg_col, kg_t, ones_bd)
        att_g = _attn_prompt(qt, kbf, vt, _prompt_bias_period(rel_bias[l]), gat)
        to_frames = lambda a: jnp.transpose(a.reshape(b, N_HEADS, HEAD_DIM, a.shape[-1]), (0, 3, 1, 2))
        outs[0].append(to_frames(pk))
        outs[1].append(to_frames(pv))
        outs[2].append(pc)
        outs[3].append(ph.reshape(b, W_LRU))

        sconv_pad = jnp.pad(state_conv[l], ((0, 0), (SUBLANES - (CONV_W - 1), 0), (0, 0)))
        lru_s, q_s, k_s, v_s, ga_s, sc, sh = _front_sample(
            ys, ng, win, cw, cbias, wg, bg, lam, qg_t, kg_t, ones_bd,
            sconv_pad, state_lru[l].reshape(db, 1, W_LRU), SAMPLE_STREAMS_PER_BLOCK, ds)
        yp, ys = _back(
            (yp.reshape(b * s, D_MODEL), lru_g.reshape(b * s, W_LRU), att_g.reshape(b * s, W_ATT),
             p_prompt[l].reshape(b * s, PLE_DIM)),
            (ys, lru_s, p_sample[l].reshape(db * ds, PLE_DIM)),
            (q_s.reshape(db, ds, W_ATT), k_s.reshape(db, ds, W_ATT), v_s.reshape(db, ds, W_ATT),
             jnp.transpose(cache_k[l], (0, 2, 3, 1)), jnp.transpose(cache_v[l], (0, 2, 3, 1)),
             ga_s.reshape(db, ds, W_ATT), _sample_bias_period(rel_bias[l], ds, lc)),
            wo, pg, wpg, wpe, BACK_ROWS)
        yp = yp.reshape(b, s, D_MODEL)
        outs[4].append(k_s.reshape(db, ds, N_HEADS, HEAD_DIM))
        outs[5].append(v_s.reshape(db, ds, N_HEADS, HEAD_DIM))
        outs[6].append(sc)
        outs[7].append(sh.reshape(db, W_LRU))
    return (yp, ys.reshape(db, ds, D_MODEL)) + tuple(jnp.stack(o) for o in outs)
```

```python
import functools

import jax
import jax.numpy as jnp
from jax import lax
from jax.experimental import pallas as pl
from jax.experimental.pallas import tpu as pltpu

D_MODEL = 1024
CHUNK = 64
PAST_CHUNKS = 8
BAND = PAST_CHUNKS * CHUNK
W_LRU = D_MODEL // 2
LRU_BLOCKS = 8
CONV_W = 4
RG_C = 8.0
HEAD_DIM = 64
W_ATT = D_MODEL // 2
N_HEADS = W_ATT // HEAD_DIM
MAX_REL = 256
PLE_DIM = 256
EPS = 1e-6
NEG = -1e30
LOG2E = 1.4426950408889634

SUBLANES = 8
LANES = 128
HEAD_GROUP = 256
HEADS_PER_GROUP = HEAD_GROUP // HEAD_DIM
N_GROUPS = W_ATT // HEAD_GROUP
QBLOCK = 256
KEY_TILES = BAND // QBLOCK + 1
PROMPT_BIAS_PERIOD = (KEY_TILES + 1) * QBLOCK
FRONT_BLOCK = 1024
BACK_ROWS = 1024
SAMPLE_STREAMS_PER_BLOCK = 8
VMEM_LIMIT = 56 * 1024 * 1024

F32 = jnp.float32
BF16 = jnp.bfloat16


def _dot(a, b):
    return jnp.dot(a, b, preferred_element_type=F32)


def _dot_nt(a, b):
    return lax.dot_general(a, b, (((1,), (1,)), ((), ())), preferred_element_type=F32)


def _rms_rows(x, g):
    ms = jnp.mean(x * x, axis=-1, keepdims=True)
    return x * lax.rsqrt(ms + EPS) * g


def _head_norm_rows(x, ones_bd, g_tiled):
    x2 = x * x
    hi = x2.astype(BF16)
    lo = (x2 - hi.astype(F32)).astype(BF16)
    ms = jnp.concatenate(
        [_dot(hi[:, g * HEAD_GROUP:(g + 1) * HEAD_GROUP], ones_bd)
         + _dot(lo[:, g * HEAD_GROUP:(g + 1) * HEAD_GROUP], ones_bd) for g in range(N_GROUPS)], axis=1)
    return x * lax.rsqrt(ms + EPS) * g_tiled


def _scan_rows(a_ref, u_ref, h_ref, row0, nrows, h0, unroll=False):
    ridx = lax.broadcasted_iota(jnp.int32, (SUBLANES, W_LRU), 0)

    def body(i, hprev):
        r = pl.multiple_of(row0 + i * SUBLANES, SUBLANES)
        a = a_ref[pl.ds(r, SUBLANES), :]
        u = u_ref[pl.ds(r, SUBLANES), :]
        for s in (1, 2, 4):
            a_s = jnp.where(ridx >= s, pltpu.roll(a, s, 0), 1.0)
            u_s = jnp.where(ridx >= s, pltpu.roll(u, s, 0), 0.0)
            u = a * u_s + u
            a = a * a_s
        h = a * hprev + u
        h_ref[pl.ds(r, SUBLANES), :] = h
        return h[SUBLANES - 1:SUBLANES, :]

    return lax.fori_loop(0, nrows // SUBLANES, body, h0, unroll=unroll)


def _lru_inputs(xc, wg_ref, bg_ref, lam_ref, a_ref, u_ref):
    xcb = xc.astype(BF16)
    half = wg_ref.shape[1]
    for j in range(wg_ref.shape[0]):
        sl = slice(j * half, (j + 1) * half)
        gates = _dot(xcb[:, sl], wg_ref[j]) + bg_ref[j]
        r = jax.nn.sigmoid(gates[:, :half])
        i = jax.nn.sigmoid(gates[:, half:])
        log_a = -RG_C * r * jax.nn.softplus(-lam_ref[:, sl])
        a = jnp.exp(log_a)
        a_ref[:, sl] = a
        u_ref[:, sl] = jnp.sqrt(jnp.tanh(-log_a) * (1.0 + a * a)) * (i * xc[:, sl])


def _conv_rows(cb_ref, base, nrows, cw_ref, cb_bias):
    out = cb_bias + cw_ref[CONV_W - 1:CONV_W, :] * cb_ref[pl.ds(base + SUBLANES, nrows), :]
    for k in range(CONV_W - 1):
        shift = CONV_W - 1 - k
        out = out + cw_ref[k:k + 1, :] * cb_ref[pl.ds(base + SUBLANES - shift, nrows), :]
    return out


def _front_prompt_kernel(x_ref, ng_ref, wnat_ref, wt_ref, cw_ref, cbias_ref, wg_ref, bg_ref, lam_ref,
                         qg_ref, kg_ref, ones_ref,
                         lru_ref, qt_ref, kbf_ref, vt_ref, gat_ref, pk_ref, pv_ref, pc_ref, ph_ref,
                         zn_scr, zt_scr, cb_scr, a_scr, u_scr, h_scr, hlast_scr, *, keep_subs):
    sub = QBLOCK
    n_sub = x_ref.shape[1] // sub

    @pl.when(pl.program_id(1) == 0)
    def _():
        cb_scr[0:SUBLANES, :] = jnp.zeros((SUBLANES, W_LRU), F32)
        hlast_scr[...] = jnp.zeros((1, W_LRU), F32)

    def project(i):
        xn = _rms_rows(x_ref[0, i * sub:(i + 1) * sub, :], ng_ref[...]).astype(BF16)
        zn_scr[i % 2] = _dot(xn, wnat_ref[...])
        zt_scr[i % 2] = _dot_nt(wt_ref[...], xn)

    def finish(i):
        rows = slice(i * sub, (i + 1) * sub)
        zn = zn_scr.at[i % 2]
        zt = zt_scr.at[i % 2]
        cb_scr[SUBLANES:SUBLANES + sub, :] = zn[:, :W_LRU]
        xc = _conv_rows(cb_scr, 0, sub, cw_ref, cbias_ref[...])
        pc_ref[0] = cb_scr[sub + SUBLANES - (CONV_W - 1):sub + SUBLANES, :]
        cb_scr[0:SUBLANES, :] = cb_scr[sub:sub + SUBLANES, :]
        _lru_inputs(xc, wg_ref, bg_ref, lam_ref, a_scr, u_scr)
        h_last = _scan_rows(a_scr, u_scr, h_scr, 0, sub, hlast_scr[...], unroll=True)
        hlast_scr[...] = h_last
        ph_ref[0] = h_last
        lru_ref[0, rows, :] = (h_scr[...] * jax.nn.silu(zn[:, W_LRU:2 * W_LRU])).astype(BF16)
        k = _head_norm_rows(zn[:, 2 * W_LRU:], ones_ref[...], kg_ref[...])
        kbf_ref[0, rows, :] = k.astype(BF16)
        q3 = zt[0:W_ATT, :].reshape(N_HEADS, HEAD_DIM, sub)
        ms = jnp.mean(q3 * q3, axis=1, keepdims=True)
        qn = q3 * lax.rsqrt(ms + EPS) * (qg_ref[...] * (HEAD_DIM ** -0.5 * LOG2E))
        qt_ref[0, :, rows] = qn.reshape(W_ATT, sub).astype(BF16)
        vt = zt[W_ATT:2 * W_ATT, :]
        vt_ref[0, :, rows] = vt.astype(BF16)
        gat_ref[0, :, rows] = zt[2 * W_ATT:, :]
        if i >= n_sub - keep_subs:
            first = (i - (n_sub - keep_subs)) * sub
            pk_ref[0, :, first:first + sub] = k.T
            pv_ref[0, :, first:first + sub] = vt

    project(0)
    for i in range(n_sub):
        if i + 1 < n_sub:
            project(i + 1)
        finish(i)


def _front_prompt(x, ng, wnat, wt, cw, cbias, wg, bg, lam, qg, kg, ones_bd):
    b, s, _ = x.shape
    tb = min(FRONT_BLOCK, s)
    keep = min(BAND, s)
    assert s % tb == 0 and tb % QBLOCK == 0 and keep % QBLOCK == 0 and keep <= tb
    const = lambda shape: pl.BlockSpec(shape, lambda i, j: (0,) * len(shape))
    rows_spec = lambda w: pl.BlockSpec((1, tb, w), lambda i, j: (i, j, 0))
    cols_spec = pl.BlockSpec((1, W_ATT, tb), lambda i, j: (i, 0, j))
    per_seq = lambda r, w: pl.BlockSpec((1, r, w), lambda i, j: (i, 0, 0))
    kern = functools.partial(_front_prompt_kernel, keep_subs=keep // QBLOCK)
    return pl.pallas_call(
        kern,
        grid=(b, s // tb),
        in_specs=[
            rows_spec(D_MODEL),
            const((1, D_MODEL)),
            const((D_MODEL, 3 * W_LRU)),
            const((3 * W_ATT, D_MODEL)),
            const((CONV_W, W_LRU)),
            const((1, W_LRU)),
            const((W_LRU // HEAD_GROUP, HEAD_GROUP, 2 * HEAD_GROUP)),
            const((W_LRU // HEAD_GROUP, 1, 2 * HEAD_GROUP)),
            const((1, W_LRU)),
            const((HEAD_DIM, 1)),
            const((1, W_ATT)),
            const((HEAD_GROUP, HEAD_GROUP)),
        ],
        out_specs=[
            rows_spec(W_LRU),
            cols_spec,
            rows_spec(W_ATT),
            cols_spec,
            cols_spec,
            per_seq(W_ATT, keep),
            per_seq(W_ATT, keep),
            per_seq(CONV_W - 1, W_LRU),
            per_seq(1, W_LRU),
        ],
        out_shape=[
            jax.ShapeDtypeStruct((b, s, W_LRU), BF16),
            jax.ShapeDtypeStruct((b, W_ATT, s), BF16),
            jax.ShapeDtypeStruct((b, s, W_ATT), BF16),
            jax.ShapeDtypeStruct((b, W_ATT, s), BF16),
            jax.ShapeDtypeStruct((b, W_ATT, s), F32),
            jax.ShapeDtypeStruct((b, W_ATT, keep), F32),
            jax.ShapeDtypeStruct((b, W_ATT, keep), F32),
            jax.ShapeDtypeStruct((b, CONV_W - 1, W_LRU), F32),
            jax.ShapeDtypeStruct((b, 1, W_LRU), F32),
        ],
        scratch_shapes=[
            pltpu.VMEM((2, QBLOCK, 3 * W_LRU), F32),
            pltpu.VMEM((2, 3 * W_ATT, QBLOCK), F32),
            pltpu.VMEM((QBLOCK + SUBLANES, W_LRU), F32),
            pltpu.VMEM((QBLOCK, W_LRU), F32),
            pltpu.VMEM((QBLOCK, W_LRU), F32),
            pltpu.VMEM((QBLOCK, W_LRU), F32),
            pltpu.VMEM((1, W_LRU), F32),
        ],
        compiler_params=pltpu.CompilerParams(
            dimension_semantics=("arbitrary", "arbitrary"), vmem_limit_bytes=VMEM_LIMIT),
        name="front_prompt",
    )(x, ng, wnat, wt, cw, cbias, wg, bg, lam, qg, kg, ones_bd)


def _toeplitz_rows(w_row, nrows, row0):
    x = jnp.broadcast_to(w_row, (nrows, w_row.shape[1]))
    return pltpu.roll(x, row0, 1, stride=1, stride_axis=0)


def _fill_prompt_bias(w_ref, bias_ref):
    q_chunk = (lax.broadcasted_iota(jnp.int32, (CHUNK, QBLOCK), 1) + BAND) // CHUNK
    for h in range(N_HEADS):
        def body(n, carry, h=h):
            r0 = pl.multiple_of(n * CHUNK, CHUNK)
            t = _toeplitz_rows(w_ref[h], CHUNK, r0)[:, :QBLOCK]
            dc = q_chunk - n
            bias_ref[h, pl.ds(r0, CHUNK), :] = jnp.where((dc >= 0) & (dc <= PAST_CHUNKS), t * LOG2E, NEG)
            return carry
        lax.fori_loop(0, KEY_TILES * QBLOCK // CHUNK, body, 0)


SCORE_LEAD = 4
ATTN_BLOCKS = 4
CHUNKS_PER_TILE = QBLOCK // CHUNK
FRAMES_PER_VREG = LANES // CHUNK


def _lane_cols(kc):
    cols = []
    for c in range(QBLOCK // LANES):
        q_lo = PAST_CHUNKS + c * FRAMES_PER_VREG
        q_hi = q_lo + FRAMES_PER_VREG - 1
        if q_lo - PAST_CHUNKS <= kc <= q_hi:
            cols.append(c)
    return cols


def _fold_rows(x):
    return x.reshape(x.shape[0] // SUBLANES, SUBLANES, x.shape[1])


def _attn_prompt_blocks(blocks, qt_ref, k_ref, vt_ref, gat_ref, out_ref, bias_ref, s_scr, att_scr):
    n_cols = QBLOCK // LANES
    rows = lax.broadcasted_iota(jnp.int32, (HEAD_GROUP, QBLOCK), 0)
    units = [(n, h) for n in range(len(blocks)) for h in range(N_HEADS)]

    def pieces(tiles):
        return [(i, cc, slice((i * CHUNKS_PER_TILE + cc) * CHUNK, (i * CHUNKS_PER_TILE + cc + 1) * CHUNK),
                 slice(c * LANES, (c + 1) * LANES), c)
                for i in tiles for cc in range(CHUNKS_PER_TILE) for c in _lane_cols(i * CHUNKS_PER_TILE + cc)]

    def key_rows(n, i):
        tiles, key0 = blocks[n]
        return pl.ds(key0 + (i - tiles[0]) * QBLOCK, QBLOCK)

    def scores(u):
        n, h = units[u]
        g, hl = divmod(h, HEADS_PER_GROUP)
        gsl = slice(g * HEAD_GROUP, (g + 1) * HEAD_GROUP)
        in_head = (rows >= hl * HEAD_DIM) & (rows < (hl + 1) * HEAD_DIM)
        qm = jnp.where(in_head, qt_ref[0, gsl, n * QBLOCK:(n + 1) * QBLOCK], jnp.zeros((), BF16))
        s = {i: _dot(k_ref[0, key_rows(n, i), gsl], qm) for i in blocks[n][0]}
        m_acc = [jnp.full((SUBLANES, LANES), NEG, F32) for _ in range(n_cols)]
        for i, cc, rsl, lsl, c in pieces(blocks[n][0]):
            sp = s[i][cc * CHUNK:(cc + 1) * CHUNK, lsl] + bias_ref[h, rsl, lsl]
            s_scr[u % (SCORE_LEAD + 1), rsl, lsl] = sp
            m_acc[c] = jnp.maximum(m_acc[c], jnp.max(_fold_rows(sp), axis=0))
        return [jnp.max(a, axis=0, keepdims=True) for a in m_acc]

    def weights(u, m):
        tiles_p = {}
        for i in blocks[units[u][0]][0]:
            chunks = []
            for cc in range(CHUNKS_PER_TILE):
                kc = i * CHUNKS_PER_TILE + cc
                rsl = slice(kc * CHUNK, (kc + 1) * CHUNK)
                cols = []
                for c in range(n_cols):
                    if c in _lane_cols(kc):
                        lsl = slice(c * LANES, (c + 1) * LANES)
                        cols.append(jnp.exp2(s_scr[u % (SCORE_LEAD + 1), rsl, lsl] - m[c]).astype(BF16))
                    else:
                        cols.append(jnp.zeros((CHUNK, LANES), BF16))
                chunks.append(jnp.concatenate(cols, axis=1))
            tiles_p[i] = jnp.concatenate(chunks, axis=0)
        return tiles_p

    ones_rows = jnp.ones((2 * SUBLANES, QBLOCK), BF16)

    def values(u, tiles_p):
        n, h = units[u]
        hsl = slice(h * HEAD_DIM, (h + 1) * HEAD_DIM)
        o = None
        for i in blocks[n][0]:
            oi = _dot(jnp.concatenate([vt_ref[0, hsl, key_rows(n, i)], ones_rows], axis=0), tiles_p[i])
            o = oi if o is None else o + oi
        att_scr[n, hsl, :] = o[0:HEAD_DIM, :] * (1.0 / o[HEAD_DIM:HEAD_DIM + 1, :])
        if h == N_HEADS - 1:
            fr = slice(n * QBLOCK, (n + 1) * QBLOCK)
            out_ref[0, fr, :] = (att_scr[n] * jax.nn.silu(gat_ref[0, :, fr])).T.astype(BF16)

    m = {u: scores(u) for u in range(min(SCORE_LEAD, len(units)))}
    w = {}
    for u in range(len(units)):
        if u + SCORE_LEAD < len(units):
            m[u + SCORE_LEAD] = scores(u + SCORE_LEAD)
        w[u] = weights(u, m.pop(u))
        if u >= 1:
            values(u - 1, w.pop(u - 1))
    values(len(units) - 1, w.pop(len(units) - 1))


def _attn_prompt_kernel(qt_ref, k_ref, vt_ref, w_ref, gat_ref, out_ref, bias_ref, s_scr, att_scr):
    step = pl.program_id(1)

    @pl.when((pl.program_id(0) == 0) & (step == 0))
    def _():
        _fill_prompt_bias(w_ref, bias_ref)

    assert ATTN_BLOCKS >= KEY_TILES - 1
    full = tuple(range(KEY_TILES))

    @pl.when(step == 0)
    def _():
        blocks = [(full[max(KEY_TILES - 1 - n, 0):], max(n - (KEY_TILES - 1), 0) * QBLOCK)
                  for n in range(ATTN_BLOCKS)]
        _attn_prompt_blocks(blocks, qt_ref, k_ref, vt_ref, gat_ref, out_ref, bias_ref, s_scr, att_scr)

    if k_ref.shape[1] > ATTN_BLOCKS * QBLOCK:
        @pl.when(step > 0)
        def _():
            first = step * ATTN_BLOCKS - (KEY_TILES - 1)
            blocks = [(full, pl.multiple_of((first + n) * QBLOCK, QBLOCK)) for n in range(ATTN_BLOCKS)]
            _attn_prompt_blocks(blocks, qt_ref, k_ref, vt_ref, gat_ref, out_ref, bias_ref, s_scr, att_scr)


def _attn_prompt(qt, kbf, vt, w_bias, gat):
    b, _, s = qt.shape
    fb = ATTN_BLOCKS * QBLOCK
    assert s % fb == 0
    return pl.pallas_call(
        _attn_prompt_kernel,
        grid=(b, s // fb),
        in_specs=[
            pl.BlockSpec((1, W_ATT, fb), lambda i, j: (i, 0, j)),
            pl.BlockSpec((1, s, W_ATT), lambda i, j: (i, 0, 0)),
            pl.BlockSpec((1, W_ATT, s), lambda i, j: (i, 0, 0)),
            pl.BlockSpec((N_HEADS, 1, PROMPT_BIAS_PERIOD), lambda i, j: (0, 0, 0)),
            pl.BlockSpec((1, W_ATT, fb), lambda i, j: (i, 0, j)),
        ],
        out_specs=pl.BlockSpec((1, fb, W_ATT), lambda i, j: (i, j, 0)),
        out_shape=jax.ShapeDtypeStruct((b, s, W_ATT), BF16),
        scratch_shapes=[
            pltpu.VMEM((N_HEADS, KEY_TILES * QBLOCK, QBLOCK), F32),
            pltpu.VMEM((SCORE_LEAD + 1, KEY_TILES * QBLOCK, QBLOCK), F32),
            pltpu.VMEM((ATTN_BLOCKS, W_ATT, QBLOCK), F32),
        ],
        compiler_params=pltpu.CompilerParams(
            dimension_semantics=("arbitrary", "arbitrary"), vmem_limit_bytes=VMEM_LIMIT),
        name="attn_prompt",
    )(qt, kbf, vt, w_bias, gat)


def _back_rows(x_ref, lru_ref, att_ref, p_ref, wo_ref, pg_ref, wpg_ref, wpe_ref, y_ref):
    mix = _dot(lru_ref[...], wo_ref[0:W_LRU, :]) + _dot(att_ref[...], wo_ref[W_LRU:, :])
    h = x_ref[...] + mix
    gate = jax.nn.sigmoid(_dot(_rms_rows(h, pg_ref[...]).astype(BF16), wpg_ref[...]))
    y_ref[...] = h + _dot(p_ref[...].astype(BF16), wpe_ref[...]) * gate


def _back_kernel(xp_ref, lrup_ref, attp_ref, pp_ref, xs_ref, lrus_ref, ps_ref,
                 q_ref, kn_ref, vn_ref, kc_ref, vc_ref, ga_ref, wb_ref,
                 wo_ref, pg_ref, wpg_ref, wpe_ref, yp_ref, ys_ref, bias_scr, atts_scr, *, prompt_steps):
    step = pl.program_id(0)
    weights = (wo_ref, pg_ref, wpg_ref, wpe_ref)
    streams, tt = q_ref.shape[0], q_ref.shape[1]

    @pl.when(step == 0)
    def _():
        _fill_sample_bias(wb_ref, bias_scr, tt)

    @pl.when(step < prompt_steps)
    def _():
        def write(s, lanes, rows):
            atts_scr[pl.ds(pl.multiple_of((step * streams + s) * tt, tt), tt), lanes] = rows

        _attn_sample_streams(q_ref, kn_ref, vn_ref, kc_ref, vc_ref, ga_ref, bias_scr, write)
        _back_rows(xp_ref, lrup_ref, attp_ref, pp_ref, *weights, yp_ref)

    @pl.when(step == prompt_steps)
    def _():
        _back_rows(xs_ref, lrus_ref, atts_scr, ps_ref, *weights, ys_ref)


def _back(prompt, sample, sample_attn, wo, pg, wpg, wpe, rows):
    n, ns = prompt[0].shape[0], sample[0].shape[0]
    q3, k3, v3, kc_t, vc_t, ga3, w_bias = sample_attn
    db, tt, _ = q3.shape
    assert n % rows == 0 and db * tt == ns
    steps = n // rows
    assert db % steps == 0
    per_step = db // steps
    const = lambda shape: pl.BlockSpec(shape, lambda i: (0,) * len(shape))
    walk = lambda w: pl.BlockSpec((rows, w), lambda i: (jnp.minimum(i, steps - 1), 0))
    whole = lambda w: pl.BlockSpec((ns, w), lambda i: (0, 0), pipeline_mode=pl.Buffered(1))
    seq = pl.BlockSpec((per_step, tt, W_ATT), lambda i: (jnp.minimum(i, steps - 1), 0, 0))
    cache = pl.BlockSpec((per_step, N_HEADS, HEAD_DIM, kc_t.shape[-1]),
                         lambda i: (jnp.minimum(i, steps - 1), 0, 0, 0))
    period = w_bias.shape[-1]
    return pl.pallas_call(
        functools.partial(_back_kernel, prompt_steps=steps),
        grid=(steps + 1,),
        in_specs=[walk(w) for w in (D_MODEL, W_LRU, W_ATT, PLE_DIM)]
        + [whole(w) for w in (D_MODEL, W_LRU, PLE_DIM)]
        + [seq, seq, seq, cache, cache, seq, const((N_HEADS, 1, period))]
        + [const((W_LRU + W_ATT, D_MODEL)), const((1, D_MODEL)), const((D_MODEL, D_MODEL)),
           const((PLE_DIM, D_MODEL))],
        out_specs=[walk(D_MODEL), pl.BlockSpec((ns, D_MODEL), lambda i: (0, 0))],
        out_shape=[jax.ShapeDtypeStruct((n, D_MODEL), F32), jax.ShapeDtypeStruct((ns, D_MODEL), F32)],
        scratch_shapes=[
            pltpu.VMEM((N_GROUPS, HEADS_PER_GROUP * tt, period), F32),
            pltpu.VMEM((ns, W_ATT), BF16),
        ],
        compiler_params=pltpu.CompilerParams(
            dimension_semantics=("arbitrary",), vmem_limit_bytes=VMEM_LIMIT),
        name="back",
    )(*prompt, *sample, q3, k3, v3, kc_t, vc_t, ga3, w_bias, wo, pg, wpg, wpe)


def _front_sample_kernel(x_ref, ng_ref, win_ref, cw_ref, cbias_ref, wg_ref, bg_ref, lam_ref,
                         qg_ref, kg_ref, ones_ref, sconv_ref, slru_ref,
                         lru_ref, q_ref, k_ref, v_ref, ga_ref, sc_ref, sh_ref,
                         cb_scr, xc_scr, a_scr, u_scr, h_scr, *, nb, tt):
    seg = tt + SUBLANES
    xn = _rms_rows(x_ref[...], ng_ref[...]).astype(BF16)
    z = _dot(xn, win_ref[...])
    xl = z[:, :W_LRU]
    for s in range(nb):
        cb_scr[s * seg:s * seg + SUBLANES, :] = sconv_ref[s]
        cb_scr[s * seg + SUBLANES:(s + 1) * seg, :] = xl[s * tt:(s + 1) * tt, :]
        xc_scr[s * tt:(s + 1) * tt, :] = _conv_rows(cb_scr, s * seg, tt, cw_ref, cbias_ref[...])
        sc_ref[s] = cb_scr[(s + 1) * seg - (CONV_W - 1):(s + 1) * seg, :]
    _lru_inputs(xc_scr[...], wg_ref, bg_ref, lam_ref, a_scr, u_scr)
    for s in range(nb):
        sh_ref[s] = _scan_rows(a_scr, u_scr, h_scr, s * tt, tt, slru_ref[s], unroll=True)
    lru_ref[...] = (h_scr[...] * jax.nn.silu(z[:, W_LRU:2 * W_LRU])).astype(BF16)

    o = 2 * W_LRU
    q = _head_norm_rows(z[:, o:o + W_ATT], ones_ref[...], qg_ref[...])
    q_ref[...] = (q * (HEAD_DIM ** -0.5)).astype(BF16)
    k_ref[...] = _head_norm_rows(z[:, o + W_ATT:o + 2 * W_ATT], ones_ref[...], kg_ref[...])
    v_ref[...] = z[:, o + 2 * W_ATT:o + 3 * W_ATT]
    ga_ref[...] = jax.nn.silu(z[:, o + 3 * W_ATT:])


def _front_sample(x2, ng, win, cw, cbias, wg, bg, lam, qg_t, kg_t, ones_bd, sconv_pad, slru, nb, tt):
    n = x2.shape[0]
    rows = nb * tt
    const = lambda shape: pl.BlockSpec(shape, lambda i: (0,) * len(shape))
    row_spec = lambda w: pl.BlockSpec((rows, w), lambda i: (i, 0))
    nseq = n // tt
    kern = functools.partial(_front_sample_kernel, nb=nb, tt=tt)
    return pl.pallas_call(
        kern,
        grid=(n // rows,),
        in_specs=[
            row_spec(D_MODEL),
            const((1, D_MODEL)),
            const((D_MODEL, 2 * W_LRU + 4 * W_ATT)),
            const((CONV_W, W_LRU)),
            const((1, W_LRU)),
            const((W_LRU // HEAD_GROUP, HEAD_GROUP, 2 * HEAD_GROUP)),
            const((W_LRU // HEAD_GROUP, 1, 2 * HEAD_GROUP)),
            const((1, W_LRU)),
            const((1, W_ATT)),
            const((1, W_ATT)),
            const((HEAD_GROUP, HEAD_GROUP)),
            pl.BlockSpec((nb, SUBLANES, W_LRU), lambda i: (i, 0, 0)),
            pl.BlockSpec((nb, 1, W_LRU), lambda i: (i, 0, 0)),
        ],
        out_specs=[
            row_spec(W_LRU), row_spec(W_ATT), row_spec(W_ATT), row_spec(W_ATT), row_spec(W_ATT),
            pl.BlockSpec((nb, CONV_W - 1, W_LRU), lambda i: (i, 0, 0)),
            pl.BlockSpec((nb, 1, W_LRU), lambda i: (i, 0, 0)),
        ],
        out_shape=[
            jax.ShapeDtypeStruct((n, W_LRU), BF16),
            jax.ShapeDtypeStruct((n, W_ATT), BF16),
            jax.ShapeDtypeStruct((n, W_ATT), F32),
            jax.ShapeDtypeStruct((n, W_ATT), F32),
            jax.ShapeDtypeStruct((n, W_ATT), F32),
            jax.ShapeDtypeStruct((nseq, CONV_W - 1, W_LRU), F32),
            jax.ShapeDtypeStruct((nseq, 1, W_LRU), F32),
        ],
        scratch_shapes=[
            pltpu.VMEM((nb * (tt + SUBLANES), W_LRU), F32),
            pltpu.VMEM((rows, W_LRU), F32),
            pltpu.VMEM((rows, W_LRU), F32),
            pltpu.VMEM((rows, W_LRU), F32),
            pltpu.VMEM((rows, W_LRU), F32),
        ],
        compiler_params=pltpu.CompilerParams(
            dimension_semantics=("arbitrary",), vmem_limit_bytes=VMEM_LIMIT),
        name="front_sample",
    )(x2, ng, win, cw, cbias, wg, bg, lam, qg_t, kg_t, ones_bd, sconv_pad, slru)


def _fill_sample_bias(w_ref, bias_ref, tt):
    for h in range(N_HEADS):
        g, hl = divmod(h, HEADS_PER_GROUP)
        bias_ref[g, hl * tt:(hl + 1) * tt, :] = _toeplitz_rows(w_ref[h], tt, 0)


def _attn_sample_streams(q_ref, kn_ref, vn_ref, kc_ref, vc_ref, ga_ref, bias_ref, write):
    tt = q_ref.shape[1]
    lc = kc_ref.shape[-1]
    lanes = lax.broadcasted_iota(jnp.int32, (tt, HEAD_GROUP), 1)
    masks = [(lanes >= hl * HEAD_DIM) & (lanes < (hl + 1) * HEAD_DIM) for hl in range(HEADS_PER_GROUP)]
    units = [(s, g) for s in range(q_ref.shape[0]) for g in range(N_GROUPS)]

    def cached(ref, s, g):
        heads = ref[s, g * HEADS_PER_GROUP:(g + 1) * HEADS_PER_GROUP]
        return heads.reshape(HEAD_GROUP, lc).astype(BF16)

    def scores(s, g):
        gsl = slice(g * HEAD_GROUP, (g + 1) * HEAD_GROUP)
        qg = q_ref[s, :, gsl]
        qs = jnp.concatenate([jnp.where(m, qg, jnp.zeros((), BF16)) for m in masks], axis=0)
        sc = _dot(qs, cached(kc_ref, s, g)) + bias_ref[g, :, 0:lc]
        sn = _dot_nt(qs, kn_ref[s, :, gsl].astype(BF16)) + bias_ref[g, :, lc:lc + tt]
        return sc, sn

    def finish(s, g, sc, sn):
        gsl = slice(g * HEAD_GROUP, (g + 1) * HEAD_GROUP)
        m = jnp.maximum(jnp.max(sc, axis=-1, keepdims=True), jnp.max(sn, axis=-1, keepdims=True))
        pc = jnp.exp(sc - m)
        pn = jnp.exp(sn - m)
        l = jnp.sum(pc, axis=-1, keepdims=True) + jnp.sum(pn, axis=-1, keepdims=True)
        o = _dot_nt(pc.astype(BF16), cached(vc_ref, s, g))
        o = (o + _dot(pn.astype(BF16), vn_ref[s, :, gsl].astype(BF16))) * (1.0 / l)
        att = jnp.zeros((tt, HEAD_GROUP), F32)
        for hl in range(HEADS_PER_GROUP):
            att = att + jnp.where(masks[hl], o[hl * tt:(hl + 1) * tt, :], 0.0)
        write(s, gsl, (att * ga_ref[s, :, gsl]).astype(BF16))

    nxt = scores(*units[0])
    for n, (s, g) in enumerate(units):
        cur = nxt
        if n + 1 < len(units):
            nxt = scores(*units[n + 1])
        finish(s, g, *cur)


def _block_diag(w):
    n, d, e = w.shape
    eye = jnp.eye(n, dtype=w.dtype)
    return (eye[:, None, :, None] * w[:, :, None, :]).reshape(n * d, n * e)


def _prompt_bias_period(table):
    assert BAND - MAX_REL == MAX_REL and PROMPT_BIAS_PERIOD == KEY_TILES * QBLOCK + QBLOCK
    last = table[2 * MAX_REL:]
    neg_d = jnp.concatenate([table, jnp.broadcast_to(last, (MAX_REL - 1, N_HEADS))])
    w = jnp.concatenate([jnp.broadcast_to(last, (QBLOCK, N_HEADS)), neg_d])
    return w.T.reshape(N_HEADS, 1, PROMPT_BIAS_PERIOD).astype(F32)


def _sample_bias_period(table, tt, l):
    assert l >= MAX_REL
    period = -(-(l + 2 * tt - 1) // LANES) * LANES
    last = table[2 * MAX_REL:]
    n_var = tt + MAX_REL - 1
    var = table[2 * MAX_REL - 1:2 * MAX_REL - 1 - n_var:-1]
    w = jnp.concatenate([jnp.broadcast_to(last, (l - MAX_REL + 1, N_HEADS)), var,
                         jnp.broadcast_to(last, (period - (l + tt), N_HEADS))])
    return w.T.reshape(N_HEADS, 1, period).astype(F32)


def kernel(x_prompt, x_sample, p_prompt, p_sample, cache_k, cache_v, state_conv, state_lru, norm_g, w_in, conv_w, conv_b, gate_a_w, gate_a_b, gate_x_w, gate_x_b, lru_lambda, q_norm_g, k_norm_g, rel_bias, w_out, ple_norm_g, w_ple_gate, w_ple_proj):
    depth = w_in.shape[0]
    b, s, _ = x_prompt.shape
    db, ds, _ = x_sample.shape
    lc = cache_k.shape[2]
    yp, ys = x_prompt, x_sample.reshape(db * ds, D_MODEL)
    ones_bd = _block_diag(jnp.full((HEADS_PER_GROUP, HEAD_DIM, HEAD_DIM), 1.0 / HEAD_DIM, F32)).astype(BF16)
    gate_halves = W_LRU // HEAD_GROUP
    blocks_per_half = LRU_BLOCKS // gate_halves
    outs = [[] for _ in range(8)]
    for l in range(depth):
        win = w_in[l].astype(BF16)
        o = 2 * W_LRU
        wnat = jnp.concatenate([win[:, :o], win[:, o + W_ATT:o + 2 * W_ATT]], axis=1)
        wt = jnp.concatenate([win[:, o:o + W_ATT], win[:, o + 2 * W_ATT:]], axis=1).T
        ng = norm_g[l].reshape(1, D_MODEL)
        cw = conv_w[l]
        cbias = conv_b[l].reshape(1, W_LRU)
        wg = jnp.stack([
            jnp.concatenate([_block_diag(w[j * blocks_per_half:(j + 1) * blocks_per_half])
                             for w in (gate_a_w[l], gate_x_w[l])], axis=1)
            for j in range(gate_halves)]).astype(BF16)
        bg = jnp.concatenate([gate_a_b[l].reshape(gate_halves, 1, HEAD_GROUP),
                              gate_x_b[l].reshape(gate_halves, 1, HEAD_GROUP)], axis=2)
        lam = lru_lambda[l].reshape(1, W_LRU)
        qg_col = q_norm_g[l].reshape(HEAD_DIM, 1)
        qg_t = jnp.tile(q_norm_g[l], N_HEADS).reshape(1, W_ATT)
        kg_t = jnp.tile(k_norm_g[l], N_HEADS).reshape(1, W_ATT)
        wo = w_out[l].astype(BF16)
        pg = ple_norm_g[l].reshape(1, D_MODEL)
        wpg = w_ple_gate[l].astype(BF16)
        wpe = w_ple_proj[l].astype(BF16)

        lru_g, qt, kbf, vt, gat, pk, pv, pc, ph = _front_prompt(
            yp, ng, wnat, wt, cw, cbias, wg, bg, lam, qg_col, kg_t, ones_bd)
        att_g = _attn_prompt(qt, kbf, vt, _prompt_bias_period(rel_bias[l]), gat)
        to_frames = lambda a: jnp.transpose(a.reshape(b, N_HEADS, HEAD_DIM, a.shape[-1]), (0, 3, 1, 2))
        outs[0].append(to_frames(pk))
        outs[1].append(to_frames(pv))
        outs[2].append(pc)
        outs[3].append(ph.reshape(b, W_LRU))

        sconv_pad = jnp.pad(state_conv[l], ((0, 0), (SUBLANES - (CONV_W - 1), 0), (0, 0)))
        lru_s, q_s, k_s, v_s, ga_s, sc, sh = _front_sample(
            ys, ng, win, cw, cbias, wg, bg, lam, qg_t, kg_t, ones_bd,
            sconv_pad, state_lru[l].reshape(db, 1, W_LRU), SAMPLE_STREAMS_PER_BLOCK, ds)
        yp, ys = _back(
            (yp.reshape(b * s, D_MODEL), lru_g.reshape(b * s, W_LRU), att_g.reshape(b * s, W_ATT),
             p_prompt[l].reshape(b * s, PLE_DIM)),
            (ys, lru_s, p_sample[l].reshape(db * ds, PLE_DIM)),
            (q_s.reshape(db, ds, W_ATT), k_s.reshape(db, ds, W_ATT), v_s.reshape(db, ds, W_ATT),
             jnp.transpose(cache_k[l], (0, 2, 3, 1)), jnp.transpose(cache_v[l], (0, 2, 3, 1)),
             ga_s.reshape(db, ds, W_ATT), _sample_bias_period(rel_bias[l], ds, lc)),
            wo, pg, wpg, wpe, BACK_ROWS)
        yp = yp.reshape(b, s, D_MODEL)
        outs[4].append(k_s.reshape(db, ds, N_HEADS, HEAD_DIM))
        outs[5].append(v_s.reshape(db, ds, N_HEADS, HEAD_DIM))
        outs[6].append(sc)
        outs[7].append(sh.reshape(db, W_LRU))
    return (yp, ys.reshape(db, ds, D_MODEL)) + tuple(jnp.stack(o) for o in outs)
```

```python
import functools

import jax
import jax.numpy as jnp
from jax import lax
from jax.experimental import pallas as pl
from jax.experimental.pallas import tpu as pltpu

D_MODEL = 1024
CHUNK = 64
PAST_CHUNKS = 8
BAND = PAST_CHUNKS * CHUNK
W_LRU = D_MODEL // 2
LRU_BLOCKS = 8
CONV_W = 4
RG_C = 8.0
HEAD_DIM = 64
W_ATT = D_MODEL // 2
N_HEADS = W_ATT // HEAD_DIM
MAX_REL = 256
PLE_DIM = 256
EPS = 1e-6
NEG = -1e30
LOG2E = 1.4426950408889634

SUBLANES = 8
LANES = 128
HEAD_GROUP = 256
HEADS_PER_GROUP = HEAD_GROUP // HEAD_DIM
N_GROUPS = W_ATT // HEAD_GROUP
QBLOCK = 256
KEY_TILES = BAND // QBLOCK + 1
PROMPT_BIAS_PERIOD = (KEY_TILES + 1) * QBLOCK
FRONT_BLOCK = 1024
BACK_ROWS = 1024
SAMPLE_STREAMS_PER_BLOCK = 8
VMEM_LIMIT = 56 * 1024 * 1024

F32 = jnp.float32
BF16 = jnp.bfloat16


def _dot(a, b):
    return jnp.dot(a, b, preferred_element_type=F32)


def _dot_nt(a, b):
    return lax.dot_general(a, b, (((1,), (1,)), ((), ())), preferred_element_type=F32)


def _rms_rows(x, g):
    ms = jnp.mean(x * x, axis=-1, keepdims=True)
    return x * lax.rsqrt(ms + EPS) * g


def _head_norm_rows(x, ones_bd, g_tiled):
    x2 = x * x
    hi = x2.astype(BF16)
    lo = (x2 - hi.astype(F32)).astype(BF16)
    ms = jnp.concatenate(
        [_dot(hi[:, g * HEAD_GROUP:(g + 1) * HEAD_GROUP], ones_bd)
         + _dot(lo[:, g * HEAD_GROUP:(g + 1) * HEAD_GROUP], ones_bd) for g in range(N_GROUPS)], axis=1)
    return x * lax.rsqrt(ms + EPS) * g_tiled


def _scan_rows(a_ref, u_ref, h_ref, row0, nrows, h0, unroll=False):
    ridx = lax.broadcasted_iota(jnp.int32, (SUBLANES, W_LRU), 0)

    def body(i, hprev):
        r = pl.multiple_of(row0 + i * SUBLANES, SUBLANES)
        a = a_ref[pl.ds(r, SUBLANES), :]
        u = u_ref[pl.ds(r, SUBLANES), :]
        for s in (1, 2, 4):
            a_s = jnp.where(ridx >= s, pltpu.roll(a, s, 0), 1.0)
            u_s = jnp.where(ridx >= s, pltpu.roll(u, s, 0), 0.0)
            u = a * u_s + u
            a = a * a_s
        h = a * hprev + u
        h_ref[pl.ds(r, SUBLANES), :] = h
        return h[SUBLANES - 1:SUBLANES, :]

    return lax.fori_loop(0, nrows // SUBLANES, body, h0, unroll=unroll)


def _lru_inputs(xc, wg_ref, bg_ref, lam_ref, a_ref, u_ref):
    xcb = xc.astype(BF16)
    half = wg_ref.shape[1]
    for j in range(wg_ref.shape[0]):
        sl = slice(j * half, (j + 1) * half)
        gates = _dot(xcb[:, sl], wg_ref[j]) + bg_ref[j]
        r = jax.nn.sigmoid(gates[:, :half])
        i = jax.nn.sigmoid(gates[:, half:])
        log_a = -RG_C * r * jax.nn.softplus(-lam_ref[:, sl])
        a = jnp.exp(log_a)
        a_ref[:, sl] = a
        u_ref[:, sl] = jnp.sqrt(jnp.tanh(-log_a) * (1.0 + a * a)) * (i * xc[:, sl])


def _conv_rows(cb_ref, base, nrows, cw_ref, cb_bias):
    out = cb_bias + cw_ref[CONV_W - 1:CONV_W, :] * cb_ref[pl.ds(base + SUBLANES, nrows), :]
    for k in range(CONV_W - 1):
        shift = CONV_W - 1 - k
        out = out + cw_ref[k:k + 1, :] * cb_ref[pl.ds(base + SUBLANES - shift, nrows), :]
    return out


def _front_prompt_kernel(x_ref, ng_ref, wnat_ref, wt_ref, cw_ref, cbias_ref, wg_ref, bg_ref, lam_ref,
                         qg_ref, kg_ref, ones_ref,
                         lru_ref, qt_ref, kbf_ref, vt_ref, gat_ref, pk_ref, pv_ref, pc_ref, ph_ref,
                         zn_scr, zt_scr, cb_scr, a_scr, u_scr, h_scr, hlast_scr, *, keep_subs):
    sub = QBLOCK
    n_sub = x_ref.shape[1] // sub

    @pl.when(pl.program_id(1) == 0)
    def _():
        cb_scr[0:SUBLANES, :] = jnp.zeros((SUBLANES, W_LRU), F32)
        hlast_scr[...] = jnp.zeros((1, W_LRU), F32)

    def project(i):
        xn = _rms_rows(x_ref[0, i * sub:(i + 1) * sub, :], ng_ref[...]).astype(BF16)
        zn_scr[i % 2] = _dot(xn, wnat_ref[...])
        zt_scr[i % 2] = _dot_nt(wt_ref[...], xn)

    def finish(i):
        rows = slice(i * sub, (i + 1) * sub)
        zn = zn_scr.at[i % 2]
        zt = zt_scr.at[i % 2]
        cb_scr[SUBLANES:SUBLANES + sub, :] = zn[:, :W_LRU]
        xc = _conv_rows(cb_scr, 0, sub, cw_ref, cbias_ref[...])
        pc_ref[0] = cb_scr[sub + SUBLANES - (CONV_W - 1):sub + SUBLANES, :]
        cb_scr[0:SUBLANES, :] = cb_scr[sub:sub + SUBLANES, :]
        _lru_inputs(xc, wg_ref, bg_ref, lam_ref, a_scr, u_scr)
        h_last = _scan_rows(a_scr, u_scr, h_scr, 0, sub, hlast_scr[...], unroll=True)
        hlast_scr[...] = h_last
        ph_ref[0] = h_last
        lru_ref[0, rows, :] = (h_scr[...] * jax.nn.silu(zn[:, W_LRU:2 * W_LRU])).astype(BF16)
        k = _head_norm_rows(zn[:, 2 * W_LRU:], ones_ref[...], kg_ref[...])
        kbf_ref[0, rows, :] = k.astype(BF16)
        q3 = zt[0:W_ATT, :].reshape(N_HEADS, HEAD_DIM, sub)
        ms = jnp.mean(q3 * q3, axis=1, keepdims=True)
        qn = q3 * lax.rsqrt(ms + EPS) * (qg_ref[...] * (HEAD_DIM ** -0.5 * LOG2E))
        qt_ref[0, :, rows] = qn.reshape(W_ATT, sub).astype(BF16)
        vt = zt[W_ATT:2 * W_ATT, :]
        vt_ref[0, :, rows] = vt.astype(BF16)
        gat_ref[0, :, rows] = jax.nn.silu(zt[2 * W_ATT:, :])
        if i >= n_sub - keep_subs:
            first = (i - (n_sub - keep_subs)) * sub
            pk_ref[0, :, first:first + sub] = k.T
            pv_ref[0, :, first:first + sub] = vt

    project(0)
    for i in range(n_sub):
        if i + 1 < n_sub:
            project(i + 1)
        finish(i)


def _front_prompt(x, ng, wnat, wt, cw, cbias, wg, bg, lam, qg, kg, ones_bd):
    b, s, _ = x.shape
    tb = min(FRONT_BLOCK, s)
    keep = min(BAND, s)
    assert s % tb == 0 and tb % QBLOCK == 0 and keep % QBLOCK == 0 and keep <= tb
    const = lambda shape: pl.BlockSpec(shape, lambda i, j: (0,) * len(shape))
    rows_spec = lambda w: pl.BlockSpec((1, tb, w), lambda i, j: (i, j, 0))
    cols_spec = pl.BlockSpec((1, W_ATT, tb), lambda i, j: (i, 0, j))
    per_seq = lambda r, w: pl.BlockSpec((1, r, w), lambda i, j: (i, 0, 0))
    kern = functools.partial(_front_prompt_kernel, keep_subs=keep // QBLOCK)
    return pl.pallas_call(
        kern,
        grid=(b, s // tb),
        in_specs=[
            rows_spec(D_MODEL),
            const((1, D_MODEL)),
            const((D_MODEL, 3 * W_LRU)),
            const((3 * W_ATT, D_MODEL)),
            const((CONV_W, W_LRU)),
            const((1, W_LRU)),
            const((W_LRU // HEAD_GROUP, HEAD_GROUP, 2 * HEAD_GROUP)),
            const((W_LRU // HEAD_GROUP, 1, 2 * HEAD_GROUP)),
            const((1, W_LRU)),
            const((HEAD_DIM, 1)),
            const((1, W_ATT)),
            const((HEAD_GROUP, HEAD_GROUP)),
        ],
        out_specs=[
            rows_spec(W_LRU),
            cols_spec,
            rows_spec(W_ATT),
            cols_spec,
            cols_spec,
            per_seq(W_ATT, keep),
            per_seq(W_ATT, keep),
            per_seq(CONV_W - 1, W_LRU),
            per_seq(1, W_LRU),
        ],
        out_shape=[
            jax.ShapeDtypeStruct((b, s, W_LRU), BF16),
            jax.ShapeDtypeStruct((b, W_ATT, s), BF16),
            jax.ShapeDtypeStruct((b, s, W_ATT), BF16),
            jax.ShapeDtypeStruct((b, W_ATT, s), BF16),
            jax.ShapeDtypeStruct((b, W_ATT, s), F32),
            jax.ShapeDtypeStruct((b, W_ATT, keep), F32),
            jax.ShapeDtypeStruct((b, W_ATT, keep), F32),
            jax.ShapeDtypeStruct((b, CONV_W - 1, W_LRU), F32),
            jax.ShapeDtypeStruct((b, 1, W_LRU), F32),
        ],
        scratch_shapes=[
            pltpu.VMEM((2, QBLOCK, 3 * W_LRU), F32),
            pltpu.VMEM((2, 3 * W_ATT, QBLOCK), F32),
            pltpu.VMEM((QBLOCK + SUBLANES, W_LRU), F32),
            pltpu.VMEM((QBLOCK, W_LRU), F32),
            pltpu.VMEM((QBLOCK, W_LRU), F32),
            pltpu.VMEM((QBLOCK, W_LRU), F32),
            pltpu.VMEM((1, W_LRU), F32),
        ],
        compiler_params=pltpu.CompilerParams(
            dimension_semantics=("parallel", "arbitrary"), vmem_limit_bytes=VMEM_LIMIT),
        name="front_prompt",
    )(x, ng, wnat, wt, cw, cbias, wg, bg, lam, qg, kg, ones_bd)


def _toeplitz_rows(w_row, nrows, row0):
    x = jnp.broadcast_to(w_row, (nrows, w_row.shape[1]))
    return pltpu.roll(x, row0, 1, stride=1, stride_axis=0)


def _fill_prompt_bias(w_ref, bias_ref):
    q_chunk = (lax.broadcasted_iota(jnp.int32, (CHUNK, QBLOCK), 1) + BAND) // CHUNK
    for h in range(N_HEADS):
        def body(n, carry, h=h):
            r0 = pl.multiple_of(n * CHUNK, CHUNK)
            t = _toeplitz_rows(w_ref[h], CHUNK, r0)[:, :QBLOCK]
            dc = q_chunk - n
            bias_ref[h, pl.ds(r0, CHUNK), :] = jnp.where((dc >= 0) & (dc <= PAST_CHUNKS), t * LOG2E, NEG)
            return carry
        lax.fori_loop(0, KEY_TILES * QBLOCK // CHUNK, body, 0)


SCORE_LEAD = 4
ATTN_BLOCKS = 4
CHUNKS_PER_TILE = QBLOCK // CHUNK
FRAMES_PER_VREG = LANES // CHUNK


def _lane_cols(kc):
    cols = []
    for c in range(QBLOCK // LANES):
        q_lo = PAST_CHUNKS + c * FRAMES_PER_VREG
        q_hi = q_lo + FRAMES_PER_VREG - 1
        if q_lo - PAST_CHUNKS <= kc <= q_hi:
            cols.append(c)
    return cols


def _fold_rows(x):
    return x.reshape(x.shape[0] // SUBLANES, SUBLANES, x.shape[1])


def _attn_prompt_blocks(blocks, qt_ref, k_ref, vt_ref, gat_ref, out_ref, bias_ref, s_scr, att_scr):
    n_cols = QBLOCK // LANES
    rows = lax.broadcasted_iota(jnp.int32, (HEAD_GROUP, QBLOCK), 0)
    units = [(n, h) for n in range(len(blocks)) for h in range(N_HEADS)]

    def pieces(tiles):
        return [(i, cc, slice((i * CHUNKS_PER_TILE + cc) * CHUNK, (i * CHUNKS_PER_TILE + cc + 1) * CHUNK),
                 slice(c * LANES, (c + 1) * LANES), c)
                for i in tiles for cc in range(CHUNKS_PER_TILE) for c in _lane_cols(i * CHUNKS_PER_TILE + cc)]

    def key_rows(n, i):
        tiles, key0 = blocks[n]
        return pl.ds(key0 + (i - tiles[0]) * QBLOCK, QBLOCK)

    def scores(u):
        n, h = units[u]
        g, hl = divmod(h, HEADS_PER_GROUP)
        gsl = slice(g * HEAD_GROUP, (g + 1) * HEAD_GROUP)
        in_head = (rows >= hl * HEAD_DIM) & (rows < (hl + 1) * HEAD_DIM)
        qm = jnp.where(in_head, qt_ref[0, gsl, n * QBLOCK:(n + 1) * QBLOCK], jnp.zeros((), BF16))
        s = {i: _dot(k_ref[0, key_rows(n, i), gsl], qm) for i in blocks[n][0]}
        m_acc = [jnp.full((SUBLANES, LANES), NEG, F32) for _ in range(n_cols)]
        for i, cc, rsl, lsl, c in pieces(blocks[n][0]):
            sp = s[i][cc * CHUNK:(cc + 1) * CHUNK, lsl] + bias_ref[h, rsl, lsl]
            s_scr[u % (SCORE_LEAD + 1), rsl, lsl] = sp
            m_acc[c] = jnp.maximum(m_acc[c], jnp.max(_fold_rows(sp), axis=0))
        return [jnp.max(a, axis=0, keepdims=True) for a in m_acc]

    def weights(u, m):
        tiles_p = {}
        for i in blocks[units[u][0]][0]:
            chunks = []
            for cc in range(CHUNKS_PER_TILE):
                kc = i * CHUNKS_PER_TILE + cc
                rsl = slice(kc * CHUNK, (kc + 1) * CHUNK)
                cols = []
                for c in range(n_cols):
                    if c in _lane_cols(kc):
                        lsl = slice(c * LANES, (c + 1) * LANES)
                        cols.append(jnp.exp2(s_scr[u % (SCORE_LEAD + 1), rsl, lsl] - m[c]).astype(BF16))
                    else:
                        cols.append(jnp.zeros((CHUNK, LANES), BF16))
                chunks.append(jnp.concatenate(cols, axis=1))
            tiles_p[i] = jnp.concatenate(chunks, axis=0)
        return tiles_p

    ones_rows = jnp.ones((2 * SUBLANES, QBLOCK), BF16)

    def values(u, tiles_p):
        n, h = units[u]
        hsl = slice(h * HEAD_DIM, (h + 1) * HEAD_DIM)
        o = None
        for i in blocks[n][0]:
            oi = _dot(jnp.concatenate([vt_ref[0, hsl, key_rows(n, i)], ones_rows], axis=0), tiles_p[i])
            o = oi if o is None else o + oi
        att_scr[n, hsl, :] = o[0:HEAD_DIM, :] * (1.0 / o[HEAD_DIM:HEAD_DIM + 1, :])
        if h == N_HEADS - 1:
            fr = slice(n * QBLOCK, (n + 1) * QBLOCK)
            out_ref[0, fr, :] = (att_scr[n] * gat_ref[0, :, fr]).T.astype(BF16)

    m = {u: scores(u) for u in range(min(SCORE_LEAD, len(units)))}
    w = {}
    for u in range(len(units)):
        if u + SCORE_LEAD < len(units):
            m[u + SCORE_LEAD] = scores(u + SCORE_LEAD)
        w[u] = weights(u, m.pop(u))
        if u >= 1:
            values(u - 1, w.pop(u - 1))
    values(len(units) - 1, w.pop(len(units) - 1))


def _attn_prompt_kernel(qt_ref, k_ref, vt_ref, w_ref, gat_ref, out_ref, bias_ref, s_scr, att_scr):
    step = pl.program_id(1)

    @pl.when((pl.program_id(0) == 0) & (step == 0))
    def _():
        _fill_prompt_bias(w_ref, bias_ref)

    assert ATTN_BLOCKS >= KEY_TILES - 1
    full = tuple(range(KEY_TILES))

    @pl.when(step == 0)
    def _():
        blocks = [(full[max(KEY_TILES - 1 - n, 0):], max(n - (KEY_TILES - 1), 0) * QBLOCK)
                  for n in range(ATTN_BLOCKS)]
        _attn_prompt_blocks(blocks, qt_ref, k_ref, vt_ref, gat_ref, out_ref, bias_ref, s_scr, att_scr)

    if k_ref.shape[1] > ATTN_BLOCKS * QBLOCK:
        @pl.when(step > 0)
        def _():
            first = step * ATTN_BLOCKS - (KEY_TILES - 1)
            blocks = [(full, pl.multiple_of((first + n) * QBLOCK, QBLOCK)) for n in range(ATTN_BLOCKS)]
            _attn_prompt_blocks(blocks, qt_ref, k_ref, vt_ref, gat_ref, out_ref, bias_ref, s_scr, att_scr)


def _attn_prompt(qt, kbf, vt, w_bias, gat):
    b, _, s = qt.shape
    fb = ATTN_BLOCKS * QBLOCK
    assert s % fb == 0
    return pl.pallas_call(
        _attn_prompt_kernel,
        grid=(b, s // fb),
        in_specs=[
            pl.BlockSpec((1, W_ATT, fb), lambda i, j: (i, 0, j)),
            pl.BlockSpec((1, s, W_ATT), lambda i, j: (i, 0, 0)),
            pl.BlockSpec((1, W_ATT, s), lambda i, j: (i, 0, 0)),
            pl.BlockSpec((N_HEADS, 1, PROMPT_BIAS_PERIOD), lambda i, j: (0, 0, 0)),
            pl.BlockSpec((1, W_ATT, fb), lambda i, j: (i, 0, j)),
        ],
        out_specs=pl.BlockSpec((1, fb, W_ATT), lambda i, j: (i, j, 0)),
        out_shape=jax.ShapeDtypeStruct((b, s, W_ATT), BF16),
        scratch_shapes=[
            pltpu.VMEM((N_HEADS, KEY_TILES * QBLOCK, QBLOCK), F32),
            pltpu.VMEM((SCORE_LEAD + 1, KEY_TILES * QBLOCK, QBLOCK), F32),
            pltpu.VMEM((ATTN_BLOCKS, W_ATT, QBLOCK), F32),
        ],
        compiler_params=pltpu.CompilerParams(
            dimension_semantics=("arbitrary", "arbitrary"), vmem_limit_bytes=VMEM_LIMIT),
        name="attn_prompt",
    )(qt, kbf, vt, w_bias, gat)


def _back_rows(x_ref, lru_ref, att_ref, p_ref, wo_ref, pg_ref, wpg_ref, wpe_ref, y_ref):
    mix = _dot(lru_ref[...], wo_ref[0:W_LRU, :]) + _dot(att_ref[...], wo_ref[W_LRU:, :])
    h = x_ref[...] + mix
    gate = jax.nn.sigmoid(_dot(_rms_rows(h, pg_ref[...]).astype(BF16), wpg_ref[...]))
    y_ref[...] = h + _dot(p_ref[...].astype(BF16), wpe_ref[...]) * gate


def _back_kernel(xp_ref, lrup_ref, attp_ref, pp_ref, xs_ref, lrus_ref, ps_ref,
                 q_ref, kn_ref, vn_ref, kc_ref, vc_ref, ga_ref, wb_ref,
                 wo_ref, pg_ref, wpg_ref, wpe_ref, yp_ref, ys_ref, bias_scr, atts_scr, *, prompt_steps):
    step = pl.program_id(0)
    weights = (wo_ref, pg_ref, wpg_ref, wpe_ref)
    streams, tt = q_ref.shape[0], q_ref.shape[1]

    @pl.when(step == 0)
    def _():
        _fill_sample_bias(wb_ref, bias_scr, tt)

    @pl.when(step < prompt_steps)
    def _():
        def write(s, lanes, rows):
            atts_scr[pl.ds(pl.multiple_of((step * streams + s) * tt, tt), tt), lanes] = rows

        _attn_sample_streams(q_ref, kn_ref, vn_ref, kc_ref, vc_ref, ga_ref, bias_scr, write)
        _back_rows(xp_ref, lrup_ref, attp_ref, pp_ref, *weights, yp_ref)

    @pl.when(step == prompt_steps)
    def _():
        _back_rows(xs_ref, lrus_ref, atts_scr, ps_ref, *weights, ys_ref)


def _back(prompt, sample, sample_attn, wo, pg, wpg, wpe, rows):
    n, ns = prompt[0].shape[0], sample[0].shape[0]
    q3, k3, v3, kc_t, vc_t, ga3, w_bias = sample_attn
    db, tt, _ = q3.shape
    assert n % rows == 0 and db * tt == ns
    steps = n // rows
    assert db % steps == 0
    per_step = db // steps
    const = lambda shape: pl.BlockSpec(shape, lambda i: (0,) * len(shape))
    walk = lambda w: pl.BlockSpec((rows, w), lambda i: (jnp.minimum(i, steps - 1), 0))
    whole = lambda w: pl.BlockSpec((ns, w), lambda i: (0, 0), pipeline_mode=pl.Buffered(1))
    seq = pl.BlockSpec((per_step, tt, W_ATT), lambda i: (jnp.minimum(i, steps - 1), 0, 0))
    cache = pl.BlockSpec((per_step, N_HEADS, HEAD_DIM, kc_t.shape[-1]),
                         lambda i: (jnp.minimum(i, steps - 1), 0, 0, 0))
    period = w_bias.shape[-1]
    return pl.pallas_call(
        functools.partial(_back_kernel, prompt_steps=steps),
        grid=(steps + 1,),
        in_specs=[walk(w) for w in (D_MODEL, W_LRU, W_ATT, PLE_DIM)]
        + [whole(w) for w in (D_MODEL, W_LRU, PLE_DIM)]
        + [seq, seq, seq, cache, cache, seq, const((N_HEADS, 1, period))]
        + [const((W_LRU + W_ATT, D_MODEL)), const((1, D_MODEL)), const((D_MODEL, D_MODEL)),
           const((PLE_DIM, D_MODEL))],
        out_specs=[walk(D_MODEL), pl.BlockSpec((ns, D_MODEL), lambda i: (0, 0))],
        out_shape=[jax.ShapeDtypeStruct((n, D_MODEL), F32), jax.ShapeDtypeStruct((ns, D_MODEL), F32)],
        scratch_shapes=[
            pltpu.VMEM((N_GROUPS, HEADS_PER_GROUP * tt, period), F32),
            pltpu.VMEM((ns, W_ATT), BF16),
        ],
        compiler_params=pltpu.CompilerParams(
            dimension_semantics=("arbitrary",), vmem_limit_bytes=VMEM_LIMIT),
        name="back",
    )(*prompt, *sample, q3, k3, v3, kc_t, vc_t, ga3, w_bias, wo, pg, wpg, wpe)


def _front_sample_kernel(x_ref, ng_ref, win_ref, cw_ref, cbias_ref, wg_ref, bg_ref, lam_ref,
                         qg_ref, kg_ref, ones_ref, sconv_ref, slru_ref,
                         lru_ref, q_ref, k_ref, v_ref, ga_ref, sc_ref, sh_ref,
                         cb_scr, xc_scr, a_scr, u_scr, h_scr, *, nb, tt):
    seg = tt + SUBLANES
    xn = _rms_rows(x_ref[...], ng_ref[...]).astype(BF16)
    z = _dot(xn, win_ref[...])
    xl = z[:, :W_LRU]
    for s in range(nb):
        cb_scr[s * seg:s * seg + SUBLANES, :] = sconv_ref[s]
        cb_scr[s * seg + SUBLANES:(s + 1) * seg, :] = xl[s * tt:(s + 1) * tt, :]
        xc_scr[s * tt:(s + 1) * tt, :] = _conv_rows(cb_scr, s * seg, tt, cw_ref, cbias_ref[...])
        sc_ref[s] = cb_scr[(s + 1) * seg - (CONV_W - 1):(s + 1) * seg, :]
    _lru_inputs(xc_scr[...], wg_ref, bg_ref, lam_ref, a_scr, u_scr)
    for s in range(nb):
        sh_ref[s] = _scan_rows(a_scr, u_scr, h_scr, s * tt, tt, slru_ref[s], unroll=True)
    lru_ref[...] = (h_scr[...] * jax.nn.silu(z[:, W_LRU:2 * W_LRU])).astype(BF16)

    o = 2 * W_LRU
    q = _head_norm_rows(z[:, o:o + W_ATT], ones_ref[...], qg_ref[...])
    q_ref[...] = (q * (HEAD_DIM ** -0.5)).astype(BF16)
    k_ref[...] = _head_norm_rows(z[:, o + W_ATT:o + 2 * W_ATT], ones_ref[...], kg_ref[...])
    v_ref[...] = z[:, o + 2 * W_ATT:o + 3 * W_ATT]
    ga_ref[...] = jax.nn.silu(z[:, o + 3 * W_ATT:])


def _front_sample(x2, ng, win, cw, cbias, wg, bg, lam, qg_t, kg_t, ones_bd, sconv_pad, slru, nb, tt):
    n = x2.shape[0]
    rows = nb * tt
    const = lambda shape: pl.BlockSpec(shape, lambda i: (0,) * len(shape))
    row_spec = lambda w: pl.BlockSpec((rows, w), lambda i: (i, 0))
    nseq = n // tt
    kern = functools.partial(_front_sample_kernel, nb=nb, tt=tt)
    return pl.pallas_call(
        kern,
        grid=(n // rows,),
        in_specs=[
            row_spec(D_MODEL),
            const((1, D_MODEL)),
            const((D_MODEL, 2 * W_LRU + 4 * W_ATT)),
            const((CONV_W, W_LRU)),
            const((1, W_LRU)),
            const((W_LRU // HEAD_GROUP, HEAD_GROUP, 2 * HEAD_GROUP)),
            const((W_LRU // HEAD_GROUP, 1, 2 * HEAD_GROUP)),
            const((1, W_LRU)),
            const((1, W_ATT)),
            const((1, W_ATT)),
            const((HEAD_GROUP, HEAD_GROUP)),
            pl.BlockSpec((nb, SUBLANES, W_LRU), lambda i: (i, 0, 0)),
            pl.BlockSpec((nb, 1, W_LRU), lambda i: (i, 0, 0)),
        ],
        out_specs=[
            row_spec(W_LRU), row_spec(W_ATT), row_spec(W_ATT), row_spec(W_ATT), row_spec(W_ATT),
            pl.BlockSpec((nb, CONV_W - 1, W_LRU), lambda i: (i, 0, 0)),
            pl.BlockSpec((nb, 1, W_LRU), lambda i: (i, 0, 0)),
        ],
        out_shape=[
            jax.ShapeDtypeStruct((n, W_LRU), BF16),
            jax.ShapeDtypeStruct((n, W_ATT), BF16),
            jax.ShapeDtypeStruct((n, W_ATT), F32),
            jax.ShapeDtypeStruct((n, W_ATT), F32),
            jax.ShapeDtypeStruct((n, W_ATT), F32),
            jax.ShapeDtypeStruct((nseq, CONV_W - 1, W_LRU), F32),
            jax.ShapeDtypeStruct((nseq, 1, W_LRU), F32),
        ],
        scratch_shapes=[
            pltpu.VMEM((nb * (tt + SUBLANES), W_LRU), F32),
            pltpu.VMEM((rows, W_LRU), F32),
            pltpu.VMEM((rows, W_LRU), F32),
            pltpu.VMEM((rows, W_LRU), F32),
            pltpu.VMEM((rows, W_LRU), F32),
        ],
        compiler_params=pltpu.CompilerParams(
            dimension_semantics=("arbitrary",), vmem_limit_bytes=VMEM_LIMIT),
        name="front_sample",
    )(x2, ng, win, cw, cbias, wg, bg, lam, qg_t, kg_t, ones_bd, sconv_pad, slru)


def _fill_sample_bias(w_ref, bias_ref, tt):
    for h in range(N_HEADS):
        g, hl = divmod(h, HEADS_PER_GROUP)
        bias_ref[g, hl * tt:(hl + 1) * tt, :] = _toeplitz_rows(w_ref[h], tt, 0)


def _attn_sample_streams(q_ref, kn_ref, vn_ref, kc_ref, vc_ref, ga_ref, bias_ref, write):
    tt = q_ref.shape[1]
    lc = kc_ref.shape[-1]
    lanes = lax.broadcasted_iota(jnp.int32, (tt, HEAD_GROUP), 1)
    masks = [(lanes >= hl * HEAD_DIM) & (lanes < (hl + 1) * HEAD_DIM) for hl in range(HEADS_PER_GROUP)]
    units = [(s, g) for s in range(q_ref.shape[0]) for g in range(N_GROUPS)]

    def cached(ref, s, g):
        heads = ref[s, g * HEADS_PER_GROUP:(g + 1) * HEADS_PER_GROUP]
        return heads.reshape(HEAD_GROUP, lc).astype(BF16)

    def scores(s, g):
        gsl = slice(g * HEAD_GROUP, (g + 1) * HEAD_GROUP)
        qg = q_ref[s, :, gsl]
        qs = jnp.concatenate([jnp.where(m, qg, jnp.zeros((), BF16)) for m in masks], axis=0)
        sc = _dot(qs, cached(kc_ref, s, g)) + bias_ref[g, :, 0:lc]
        sn = _dot_nt(qs, kn_ref[s, :, gsl].astype(BF16)) + bias_ref[g, :, lc:lc + tt]
        return sc, sn

    def finish(s, g, sc, sn):
        gsl = slice(g * HEAD_GROUP, (g + 1) * HEAD_GROUP)
        m = jnp.maximum(jnp.max(sc, axis=-1, keepdims=True), jnp.max(sn, axis=-1, keepdims=True))
        pc = jnp.exp(sc - m)
        pn = jnp.exp(sn - m)
        l = jnp.sum(pc, axis=-1, keepdims=True) + jnp.sum(pn, axis=-1, keepdims=True)
        o = _dot_nt(pc.astype(BF16), cached(vc_ref, s, g))
        o = (o + _dot(pn.astype(BF16), vn_ref[s, :, gsl].astype(BF16))) * (1.0 / l)
        att = jnp.zeros((tt, HEAD_GROUP), F32)
        for hl in range(HEADS_PER_GROUP):
            att = att + jnp.where(masks[hl], o[hl * tt:(hl + 1) * tt, :], 0.0)
        write(s, gsl, (att * ga_ref[s, :, gsl]).astype(BF16))

    nxt = scores(*units[0])
    for n, (s, g) in enumerate(units):
        cur = nxt
        if n + 1 < len(units):
            nxt = scores(*units[n + 1])
        finish(s, g, *cur)


def _block_diag(w):
    n, d, e = w.shape
    eye = jnp.eye(n, dtype=w.dtype)
    return (eye[:, None, :, None] * w[:, :, None, :]).reshape(n * d, n * e)


def _prompt_bias_period(table):
    assert BAND - MAX_REL == MAX_REL and PROMPT_BIAS_PERIOD == KEY_TILES * QBLOCK + QBLOCK
    last = table[2 * MAX_REL:]
    neg_d = jnp.concatenate([table, jnp.broadcast_to(last, (MAX_REL - 1, N_HEADS))])
    w = jnp.concatenate([jnp.broadcast_to(last, (QBLOCK, N_HEADS)), neg_d])
    return w.T.reshape(N_HEADS, 1, PROMPT_BIAS_PERIOD).astype(F32)


def _sample_bias_period(table, tt, l):
    assert l >= MAX_REL
    period = -(-(l + 2 * tt - 1) // LANES) * LANES
    last = table[2 * MAX_REL:]
    n_var = tt + MAX_REL - 1
    var = table[2 * MAX_REL - 1:2 * MAX_REL - 1 - n_var:-1]
    w = jnp.concatenate([jnp.broadcast_to(last, (l - MAX_REL + 1, N_HEADS)), var,
                         jnp.broadcast_to(last, (period - (l + tt), N_HEADS))])
    return w.T.reshape(N_HEADS, 1, period).astype(F32)


def kernel(x_prompt, x_sample, p_prompt, p_sample, cache_k, cache_v, state_conv, state_lru, norm_g, w_in, conv_w, conv_b, gate_a_w, gate_a_b, gate_x_w, gate_x_b, lru_lambda, q_norm_g, k_norm_g, rel_bias, w_out, ple_norm_g, w_ple_gate, w_ple_proj):
    depth = w_in.shape[0]
    b, s, _ = x_prompt.shape
    db, ds, _ = x_sample.shape
    lc = cache_k.shape[2]
    yp, ys = x_prompt, x_sample.reshape(db * ds, D_MODEL)
    ones_bd = _block_diag(jnp.full((HEADS_PER_GROUP, HEAD_DIM, HEAD_DIM), 1.0 / HEAD_DIM, F32)).astype(BF16)
    gate_halves = W_LRU // HEAD_GROUP
    blocks_per_half = LRU_BLOCKS // gate_halves
    outs = [[] for _ in range(8)]
    for l in range(depth):
        win = w_in[l].astype(BF16)
        o = 2 * W_LRU
        wnat = jnp.concatenate([win[:, :o], win[:, o + W_ATT:o + 2 * W_ATT]], axis=1)
        wt = jnp.concatenate([win[:, o:o + W_ATT], win[:, o + 2 * W_ATT:]], axis=1).T
        ng = norm_g[l].reshape(1, D_MODEL)
        cw = conv_w[l]
        cbias = conv_b[l].reshape(1, W_LRU)
        wg = jnp.stack([
            jnp.concatenate([_block_diag(w[j * blocks_per_half:(j + 1) * blocks_per_half])
                             for w in (gate_a_w[l], gate_x_w[l])], axis=1)
            for j in range(gate_halves)]).astype(BF16)
        bg = jnp.concatenate([gate_a_b[l].reshape(gate_halves, 1, HEAD_GROUP),
                              gate_x_b[l].reshape(gate_halves, 1, HEAD_GROUP)], axis=2)
        lam = lru_lambda[l].reshape(1, W_LRU)
        qg_col = q_norm_g[l].reshape(HEAD_DIM, 1)
        qg_t = jnp.tile(q_norm_g[l], N_HEADS).reshape(1, W_ATT)
        kg_t = jnp.tile(k_norm_g[l], N_HEADS).reshape(1, W_ATT)
        wo = w_out[l].astype(BF16)
        pg = ple_norm_g[l].reshape(1, D_MODEL)
        wpg = w_ple_gate[l].astype(BF16)
        wpe = w_ple_proj[l].astype(BF16)

        lru_g, qt, kbf, vt, gat, pk, pv, pc, ph = _front_prompt(
            yp, ng, wnat, wt, cw, cbias, wg, bg, lam, qg_col, kg_t, ones_bd)
        att_g = _attn_prompt(qt, kbf, vt, _prompt_bias_period(rel_bias[l]), gat)
        to_frames = lambda a: jnp.transpose(a.reshape(b, N_HEADS, HEAD_DIM, a.shape[-1]), (0, 3, 1, 2))
        outs[0].append(to_frames(pk))
        outs[1].append(to_frames(pv))
        outs[2].append(pc)
        outs[3].append(ph.reshape(b, W_LRU))

        sconv_pad = jnp.pad(state_conv[l], ((0, 0), (SUBLANES - (CONV_W - 1), 0), (0, 0)))
        lru_s, q_s, k_s, v_s, ga_s, sc, sh = _front_sample(
            ys, ng, win, cw, cbias, wg, bg, lam, qg_t, kg_t, ones_bd,
            sconv_pad, state_lru[l].reshape(db, 1, W_LRU), SAMPLE_STREAMS_PER_BLOCK, ds)
        yp, ys = _back(
            (yp.reshape(b * s, D_MODEL), lru_g.reshape(b * s, W_LRU), att_g.reshape(b * s, W_ATT),
             p_prompt[l].reshape(b * s, PLE_DIM)),
            (ys, lru_s, p_sample[l].reshape(db * ds, PLE_DIM)),
            (q_s.reshape(db, ds, W_ATT), k_s.reshape(db, ds, W_ATT), v_s.reshape(db, ds, W_ATT),
             jnp.transpose(cache_k[l], (0, 2, 3, 1)), jnp.transpose(cache_v[l], (0, 2, 3, 1)),
             ga_s.reshape(db, ds, W_ATT), _sample_bias_period(rel_bias[l], ds, lc)),
            wo, pg, wpg, wpe, BACK_ROWS)
        yp = yp.reshape(b, s, D_MODEL)
        outs[4].append(k_s.reshape(db, ds, N_HEADS, HEAD_DIM))
        outs[5].append(v_s.reshape(db, ds, N_HEADS, HEAD_DIM))
        outs[6].append(sc)
        outs[7].append(sh.reshape(db, W_LRU))
    return (yp, ys.reshape(db, ds, D_MODEL)) + tuple(jnp.stack(o) for o in outs)
```

```python
import functools

import jax
import jax.numpy as jnp
from jax import lax
from jax.experimental import pallas as pl
from jax.experimental.pallas import tpu as pltpu

D_MODEL = 1024
CHUNK = 64
PAST_CHUNKS = 8
BAND = PAST_CHUNKS * CHUNK
W_LRU = D_MODEL // 2
LRU_BLOCKS = 8
CONV_W = 4
RG_C = 8.0
HEAD_DIM = 64
W_ATT = D_MODEL // 2
N_HEADS = W_ATT // HEAD_DIM
MAX_REL = 256
PLE_DIM = 256
EPS = 1e-6
NEG = -1e30
LOG2E = 1.4426950408889634

SUBLANES = 8
LANES = 128
HEAD_GROUP = 256
HEADS_PER_GROUP = HEAD_GROUP // HEAD_DIM
N_GROUPS = W_ATT // HEAD_GROUP
QBLOCK = 256
KEY_TILES = BAND // QBLOCK + 1
PROMPT_BIAS_PERIOD = (KEY_TILES + 1) * QBLOCK
FRONT_BLOCK = 1024
BACK_ROWS = 1024
SAMPLE_STREAMS_PER_BLOCK = 8
VMEM_LIMIT = 56 * 1024 * 1024

F32 = jnp.float32
BF16 = jnp.bfloat16


def _dot(a, b):
    return jnp.dot(a, b, preferred_element_type=F32)


def _dot_nt(a, b):
    return lax.dot_general(a, b, (((1,), (1,)), ((), ())), preferred_element_type=F32)


def _rms_rows(x, g):
    ms = jnp.mean(x * x, axis=-1, keepdims=True)
    return x * lax.rsqrt(ms + EPS) * g


def _head_norm_rows(x, ones_bd, g_tiled):
    x2 = x * x
    hi = x2.astype(BF16)
    lo = (x2 - hi.astype(F32)).astype(BF16)
    ms = jnp.concatenate(
        [_dot(hi[:, g * HEAD_GROUP:(g + 1) * HEAD_GROUP], ones_bd)
         + _dot(lo[:, g * HEAD_GROUP:(g + 1) * HEAD_GROUP], ones_bd) for g in range(N_GROUPS)], axis=1)
    return x * lax.rsqrt(ms + EPS) * g_tiled


def _scan_rows(a_ref, u_ref, h_ref, row0, nrows, h0, unroll=False):
    ridx = lax.broadcasted_iota(jnp.int32, (SUBLANES, W_LRU), 0)

    def body(i, hprev):
        r = pl.multiple_of(row0 + i * SUBLANES, SUBLANES)
        a = a_ref[pl.ds(r, SUBLANES), :]
        u = u_ref[pl.ds(r, SUBLANES), :]
        for s in (1, 2, 4):
            a_s = jnp.where(ridx >= s, pltpu.roll(a, s, 0), 1.0)
            u_s = jnp.where(ridx >= s, pltpu.roll(u, s, 0), 0.0)
            u = a * u_s + u
            a = a * a_s
        h = a * hprev + u
        h_ref[pl.ds(r, SUBLANES), :] = h
        return h[SUBLANES - 1:SUBLANES, :]

    return lax.fori_loop(0, nrows // SUBLANES, body, h0, unroll=unroll)


def _lru_inputs(xc, wg_ref, bg_ref, lam_ref, a_ref, u_ref):
    xcb = xc.astype(BF16)
    half = wg_ref.shape[1]
    for j in range(wg_ref.shape[0]):
        sl = slice(j * half, (j + 1) * half)
        gates = _dot(xcb[:, sl], wg_ref[j]) + bg_ref[j]
        r = jax.nn.sigmoid(gates[:, :half])
        i = jax.nn.sigmoid(gates[:, half:])
        log_a = -RG_C * r * jax.nn.softplus(-lam_ref[:, sl])
        a = jnp.exp(log_a)
        a_ref[:, sl] = a
        u_ref[:, sl] = jnp.sqrt(jnp.tanh(-log_a) * (1.0 + a * a)) * (i * xc[:, sl])


def _conv_rows(cb_ref, base, nrows, cw_ref, cb_bias):
    out = cb_bias + cw_ref[CONV_W - 1:CONV_W, :] * cb_ref[pl.ds(base + SUBLANES, nrows), :]
    for k in range(CONV_W - 1):
        shift = CONV_W - 1 - k
        out = out + cw_ref[k:k + 1, :] * cb_ref[pl.ds(base + SUBLANES - shift, nrows), :]
    return out


def _front_prompt_kernel(x_ref, ng_ref, wnat_ref, wt_ref, cw_ref, cbias_ref, wg_ref, bg_ref, lam_ref,
                         qg_ref, kg_ref, ones_ref,
                         lru_ref, qt_ref, kbf_ref, vt_ref, gat_ref, pk_ref, pv_ref, pc_ref, ph_ref,
                         zn_scr, zt_scr, cb_scr, a_scr, u_scr, h_scr, hlast_scr, *, keep_subs):
    sub = QBLOCK
    n_sub = x_ref.shape[1] // sub

    @pl.when(pl.program_id(1) == 0)
    def _():
        cb_scr[0:SUBLANES, :] = jnp.zeros((SUBLANES, W_LRU), F32)
        hlast_scr[...] = jnp.zeros((1, W_LRU), F32)

    def project(i):
        xn = _rms_rows(x_ref[0, i * sub:(i + 1) * sub, :], ng_ref[...]).astype(BF16)
        zn_scr[i % 2] = _dot(xn, wnat_ref[...])
        zt_scr[i % 2] = _dot_nt(wt_ref[...], xn)

    def finish(i):
        rows = slice(i * sub, (i + 1) * sub)
        zn = zn_scr.at[i % 2]
        zt = zt_scr.at[i % 2]
        cb_scr[SUBLANES:SUBLANES + sub, :] = zn[:, :W_LRU]
        xc = _conv_rows(cb_scr, 0, sub, cw_ref, cbias_ref[...])
        pc_ref[0] = cb_scr[sub + SUBLANES - (CONV_W - 1):sub + SUBLANES, :]
        cb_scr[0:SUBLANES, :] = cb_scr[sub:sub + SUBLANES, :]
        _lru_inputs(xc, wg_ref, bg_ref, lam_ref, a_scr, u_scr)
        h_last = _scan_rows(a_scr, u_scr, h_scr, 0, sub, hlast_scr[...], unroll=True)
        hlast_scr[...] = h_last
        ph_ref[0] = h_last
        lru_ref[0, rows, :] = (h_scr[...] * jax.nn.silu(zn[:, W_LRU:2 * W_LRU])).astype(BF16)
        k = _head_norm_rows(zn[:, 2 * W_LRU:], ones_ref[...], kg_ref[...])
        kbf_ref[0, rows, :] = k.astype(BF16)
        q3 = zt[0:W_ATT, :].reshape(N_HEADS, HEAD_DIM, sub)
        ms = jnp.mean(q3 * q3, axis=1, keepdims=True)
        qn = q3 * lax.rsqrt(ms + EPS) * (qg_ref[...] * (HEAD_DIM ** -0.5 * LOG2E))
        qt_ref[0, :, rows] = qn.reshape(W_ATT, sub).astype(BF16)
        vt = zt[W_ATT:2 * W_ATT, :]
        vt_ref[0, :, rows] = vt.astype(BF16)
        gat_ref[0, :, rows] = jax.nn.silu(zt[2 * W_ATT:, :]).astype(BF16)
        if i >= n_sub - keep_subs:
            first = (i - (n_sub - keep_subs)) * sub
            pk_ref[0, :, first:first + sub] = k.T
            pv_ref[0, :, first:first + sub] = vt

    project(0)
    for i in range(n_sub):
        if i + 1 < n_sub:
            project(i + 1)
        finish(i)


def _front_prompt(x, ng, wnat, wt, cw, cbias, wg, bg, lam, qg, kg, ones_bd):
    b, s, _ = x.shape
    tb = min(FRONT_BLOCK, s)
    keep = min(BAND, s)
    assert s % tb == 0 and tb % QBLOCK == 0 and keep % QBLOCK == 0 and keep <= tb
    const = lambda shape: pl.BlockSpec(shape, lambda i, j: (0,) * len(shape))
    rows_spec = lambda w: pl.BlockSpec((1, tb, w), lambda i, j: (i, j, 0))
    cols_spec = pl.BlockSpec((1, W_ATT, tb), lambda i, j: (i, 0, j))
    per_seq = lambda r, w: pl.BlockSpec((1, r, w), lambda i, j: (i, 0, 0))
    kern = functools.partial(_front_prompt_kernel, keep_subs=keep // QBLOCK)
    return pl.pallas_call(
        kern,
        grid=(b, s // tb),
        in_specs=[
            rows_spec(D_MODEL),
            const((1, D_MODEL)),
            const((D_MODEL, 3 * W_LRU)),
            const((3 * W_ATT, D_MODEL)),
            const((CONV_W, W_LRU)),
            const((1, W_LRU)),
            const((W_LRU // HEAD_GROUP, HEAD_GROUP, 2 * HEAD_GROUP)),
            const((W_LRU // HEAD_GROUP, 1, 2 * HEAD_GROUP)),
            const((1, W_LRU)),
            const((HEAD_DIM, 1)),
            const((1, W_ATT)),
            const((HEAD_GROUP, HEAD_GROUP)),
        ],
        out_specs=[
            rows_spec(W_LRU),
            cols_spec,
            rows_spec(W_ATT),
            cols_spec,
            cols_spec,
            per_seq(W_ATT, keep),
            per_seq(W_ATT, keep),
            per_seq(CONV_W - 1, W_LRU),
            per_seq(1, W_LRU),
        ],
        out_shape=[
            jax.ShapeDtypeStruct((b, s, W_LRU), BF16),
            jax.ShapeDtypeStruct((b, W_ATT, s), BF16),
            jax.ShapeDtypeStruct((b, s, W_ATT), BF16),
            jax.ShapeDtypeStruct((b, W_ATT, s), BF16),
            jax.ShapeDtypeStruct((b, W_ATT, s), BF16),
            jax.ShapeDtypeStruct((b, W_ATT, keep), F32),
            jax.ShapeDtypeStruct((b, W_ATT, keep), F32),
            jax.ShapeDtypeStruct((b, CONV_W - 1, W_LRU), F32),
            jax.ShapeDtypeStruct((b, 1, W_LRU), F32),
        ],
        scratch_shapes=[
            pltpu.VMEM((2, QBLOCK, 3 * W_LRU), F32),
            pltpu.VMEM((2, 3 * W_ATT, QBLOCK), F32),
            pltpu.VMEM((QBLOCK + SUBLANES, W_LRU), F32),
            pltpu.VMEM((QBLOCK, W_LRU), F32),
            pltpu.VMEM((QBLOCK, W_LRU), F32),
            pltpu.VMEM((QBLOCK, W_LRU), F32),
            pltpu.VMEM((1, W_LRU), F32),
        ],
        compiler_params=pltpu.CompilerParams(
            dimension_semantics=("arbitrary", "arbitrary"), vmem_limit_bytes=VMEM_LIMIT),
        name="front_prompt",
    )(x, ng, wnat, wt, cw, cbias, wg, bg, lam, qg, kg, ones_bd)


def _toeplitz_rows(w_row, nrows, row0):
    x = jnp.broadcast_to(w_row, (nrows, w_row.shape[1]))
    return pltpu.roll(x, row0, 1, stride=1, stride_axis=0)


def _fill_prompt_bias(w_ref, bias_ref):
    q_chunk = (lax.broadcasted_iota(jnp.int32, (CHUNK, QBLOCK), 1) + BAND) // CHUNK
    for h in range(N_HEADS):
        def body(n, carry, h=h):
            r0 = pl.multiple_of(n * CHUNK, CHUNK)
            t = _toeplitz_rows(w_ref[h], CHUNK, r0)[:, :QBLOCK]
            dc = q_chunk - n
            bias_ref[h, pl.ds(r0, CHUNK), :] = jnp.where((dc >= 0) & (dc <= PAST_CHUNKS), t * LOG2E, NEG)
            return carry
        lax.fori_loop(0, KEY_TILES * QBLOCK // CHUNK, body, 0)


SCORE_LEAD = 4
ATTN_BLOCKS = 4
CHUNKS_PER_TILE = QBLOCK // CHUNK
FRAMES_PER_VREG = LANES // CHUNK


def _lane_cols(kc):
    cols = []
    for c in range(QBLOCK // LANES):
        q_lo = PAST_CHUNKS + c * FRAMES_PER_VREG
        q_hi = q_lo + FRAMES_PER_VREG - 1
        if q_lo - PAST_CHUNKS <= kc <= q_hi:
            cols.append(c)
    return cols


def _fold_rows(x):
    return x.reshape(x.shape[0] // SUBLANES, SUBLANES, x.shape[1])


def _attn_prompt_blocks(blocks, qt_ref, k_ref, vt_ref, gat_ref, out_ref, bias_ref, s_scr, att_scr):
    n_cols = QBLOCK // LANES
    rows = lax.broadcasted_iota(jnp.int32, (HEAD_GROUP, QBLOCK), 0)
    units = [(n, h) for n in range(len(blocks)) for h in range(N_HEADS)]

    def pieces(tiles):
        return [(i, cc, slice((i * CHUNKS_PER_TILE + cc) * CHUNK, (i * CHUNKS_PER_TILE + cc + 1) * CHUNK),
                 slice(c * LANES, (c + 1) * LANES), c)
                for i in tiles for cc in range(CHUNKS_PER_TILE) for c in _lane_cols(i * CHUNKS_PER_TILE + cc)]

    def key_rows(n, i):
        tiles, key0 = blocks[n]
        return pl.ds(key0 + (i - tiles[0]) * QBLOCK, QBLOCK)

    def scores(u):
        n, h = units[u]
        g, hl = divmod(h, HEADS_PER_GROUP)
        gsl = slice(g * HEAD_GROUP, (g + 1) * HEAD_GROUP)
        in_head = (rows >= hl * HEAD_DIM) & (rows < (hl + 1) * HEAD_DIM)
        qm = jnp.where(in_head, qt_ref[0, gsl, n * QBLOCK:(n + 1) * QBLOCK], jnp.zeros((), BF16))
        s = {i: _dot(k_ref[0, key_rows(n, i), gsl], qm) for i in blocks[n][0]}
        m_acc = [jnp.full((SUBLANES, LANES), NEG, F32) for _ in range(n_cols)]
        for i, cc, rsl, lsl, c in pieces(blocks[n][0]):
            sp = s[i][cc * CHUNK:(cc + 1) * CHUNK, lsl] + bias_ref[h, rsl, lsl]
            s_scr[u % (SCORE_LEAD + 1), rsl, lsl] = sp
            m_acc[c] = jnp.maximum(m_acc[c], jnp.max(_fold_rows(sp), axis=0))
        return [jnp.max(a, axis=0, keepdims=True) for a in m_acc]

    def weights(u, m):
        tiles_p = {}
        for i in blocks[units[u][0]][0]:
            chunks = []
            for cc in range(CHUNKS_PER_TILE):
                kc = i * CHUNKS_PER_TILE + cc
                rsl = slice(kc * CHUNK, (kc + 1) * CHUNK)
                cols = []
                for c in range(n_cols):
                    if c in _lane_cols(kc):
                        lsl = slice(c * LANES, (c + 1) * LANES)
                        cols.append(jnp.exp2(s_scr[u % (SCORE_LEAD + 1), rsl, lsl] - m[c]).astype(BF16))
                    else:
                        cols.append(jnp.zeros((CHUNK, LANES), BF16))
                chunks.append(jnp.concatenate(cols, axis=1))
            tiles_p[i] = jnp.concatenate(chunks, axis=0)
        return tiles_p

    ones_rows = jnp.ones((2 * SUBLANES, QBLOCK), BF16)

    def values(u, tiles_p):
        n, h = units[u]
        hsl = slice(h * HEAD_DIM, (h + 1) * HEAD_DIM)
        o = None
        for i in blocks[n][0]:
            oi = _dot(jnp.concatenate([vt_ref[0, hsl, key_rows(n, i)], ones_rows], axis=0), tiles_p[i])
            o = oi if o is None else o + oi
        att_scr[n, hsl, :] = o[0:HEAD_DIM, :] * (1.0 / o[HEAD_DIM:HEAD_DIM + 1, :])
        if h == N_HEADS - 1:
            fr = slice(n * QBLOCK, (n + 1) * QBLOCK)
            out_ref[0, fr, :] = (att_scr[n] * gat_ref[0, :, fr]).T.astype(BF16)

    m = {u: scores(u) for u in range(min(SCORE_LEAD, len(units)))}
    w = {}
    for u in range(len(units)):
        if u + SCORE_LEAD < len(units):
            m[u + SCORE_LEAD] = scores(u + SCORE_LEAD)
        w[u] = weights(u, m.pop(u))
        if u >= 1:
            values(u - 1, w.pop(u - 1))
    values(len(units) - 1, w.pop(len(units) - 1))


def _attn_prompt_kernel(qt_ref, k_ref, vt_ref, w_ref, gat_ref, out_ref, bias_ref, s_scr, att_scr):
    step = pl.program_id(1)

    @pl.when((pl.program_id(0) == 0) & (step == 0))
    def _():
        _fill_prompt_bias(w_ref, bias_ref)

    assert ATTN_BLOCKS >= KEY_TILES - 1
    full = tuple(range(KEY_TILES))

    @pl.when(step == 0)
    def _():
        blocks = [(full[max(KEY_TILES - 1 - n, 0):], max(n - (KEY_TILES - 1), 0) * QBLOCK)
                  for n in range(ATTN_BLOCKS)]
        _attn_prompt_blocks(blocks, qt_ref, k_ref, vt_ref, gat_ref, out_ref, bias_ref, s_scr, att_scr)

    if k_ref.shape[1] > ATTN_BLOCKS * QBLOCK:
        @pl.when(step > 0)
        def _():
            first = step * ATTN_BLOCKS - (KEY_TILES - 1)
            blocks = [(full, pl.multiple_of((first + n) * QBLOCK, QBLOCK)) for n in range(ATTN_BLOCKS)]
            _attn_prompt_blocks(blocks, qt_ref, k_ref, vt_ref, gat_ref, out_ref, bias_ref, s_scr, att_scr)


def _attn_prompt(qt, kbf, vt, w_bias, gat):
    b, _, s = qt.shape
    fb = ATTN_BLOCKS * QBLOCK
    assert s % fb == 0
    return pl.pallas_call(
        _attn_prompt_kernel,
        grid=(b, s // fb),
        in_specs=[
            pl.BlockSpec((1, W_ATT, fb), lambda i, j: (i, 0, j)),
            pl.BlockSpec((1, s, W_ATT), lambda i, j: (i, 0, 0)),
            pl.BlockSpec((1, W_ATT, s), lambda i, j: (i, 0, 0)),
            pl.BlockSpec((N_HEADS, 1, PROMPT_BIAS_PERIOD), lambda i, j: (0, 0, 0)),
            pl.BlockSpec((1, W_ATT, fb), lambda i, j: (i, 0, j)),
        ],
        out_specs=pl.BlockSpec((1, fb, W_ATT), lambda i, j: (i, j, 0)),
        out_shape=jax.ShapeDtypeStruct((b, s, W_ATT), BF16),
        scratch_shapes=[
            pltpu.VMEM((N_HEADS, KEY_TILES * QBLOCK, QBLOCK), F32),
            pltpu.VMEM((SCORE_LEAD + 1, KEY_TILES * QBLOCK, QBLOCK), F32),
            pltpu.VMEM((ATTN_BLOCKS, W_ATT, QBLOCK), F32),
        ],
        compiler_params=pltpu.CompilerParams(
            dimension_semantics=("arbitrary", "arbitrary"), vmem_limit_bytes=VMEM_LIMIT),
        name="attn_prompt",
    )(qt, kbf, vt, w_bias, gat)


def _back_rows(x_ref, lru_ref, att_ref, p_ref, wo_ref, pg_ref, wpg_ref, wpe_ref, y_ref):
    mix = _dot(lru_ref[...], wo_ref[0:W_LRU, :]) + _dot(att_ref[...], wo_ref[W_LRU:, :])
    h = x_ref[...] + mix
    gate = jax.nn.sigmoid(_dot(_rms_rows(h, pg_ref[...]).astype(BF16), wpg_ref[...]))
    y_ref[...] = h + _dot(p_ref[...].astype(BF16), wpe_ref[...]) * gate


def _back_kernel(xp_ref, lrup_ref, attp_ref, pp_ref, xs_ref, lrus_ref, ps_ref,
                 q_ref, kn_ref, vn_ref, kc_ref, vc_ref, ga_ref, wb_ref,
                 wo_ref, pg_ref, wpg_ref, wpe_ref, yp_ref, ys_ref, bias_scr, atts_scr, *, prompt_steps):
    step = pl.program_id(0)
    weights = (wo_ref, pg_ref, wpg_ref, wpe_ref)
    streams, tt = q_ref.shape[0], q_ref.shape[1]

    @pl.when(step == 0)
    def _():
        _fill_sample_bias(wb_ref, bias_scr, tt)

    @pl.when(step < prompt_steps)
    def _():
        def write(s, lanes, rows):
            atts_scr[pl.ds(pl.multiple_of((step * streams + s) * tt, tt), tt), lanes] = rows

        _attn_sample_streams(q_ref, kn_ref, vn_ref, kc_ref, vc_ref, ga_ref, bias_scr, write)
        _back_rows(xp_ref, lrup_ref, attp_ref, pp_ref, *weights, yp_ref)

    @pl.when(step == prompt_steps)
    def _():
        _back_rows(xs_ref, lrus_ref, atts_scr, ps_ref, *weights, ys_ref)


def _back(prompt, sample, sample_attn, wo, pg, wpg, wpe, rows):
    n, ns = prompt[0].shape[0], sample[0].shape[0]
    q3, k3, v3, kc_t, vc_t, ga3, w_bias = sample_attn
    db, tt, _ = q3.shape
    assert n % rows == 0 and db * tt == ns
    steps = n // rows
    assert db % steps == 0
    per_step = db // steps
    const = lambda shape: pl.BlockSpec(shape, lambda i: (0,) * len(shape))
    walk = lambda w: pl.BlockSpec((rows, w), lambda i: (jnp.minimum(i, steps - 1), 0))
    whole = lambda w: pl.BlockSpec((ns, w), lambda i: (0, 0), pipeline_mode=pl.Buffered(1))
    seq = pl.BlockSpec((per_step, tt, W_ATT), lambda i: (jnp.minimum(i, steps - 1), 0, 0))
    cache = pl.BlockSpec((per_step, N_HEADS, HEAD_DIM, kc_t.shape[-1]),
                         lambda i: (jnp.minimum(i, steps - 1), 0, 0, 0))
    period = w_bias.shape[-1]
    return pl.pallas_call(
        functools.partial(_back_kernel, prompt_steps=steps),
        grid=(steps + 1,),
        in_specs=[walk(w) for w in (D_MODEL, W_LRU, W_ATT, PLE_DIM)]
        + [whole(w) for w in (D_MODEL, W_LRU, PLE_DIM)]
        + [seq, seq, seq, cache, cache, seq, const((N_HEADS, 1, period))]
        + [const((W_LRU + W_ATT, D_MODEL)), const((1, D_MODEL)), const((D_MODEL, D_MODEL)),
           const((PLE_DIM, D_MODEL))],
        out_specs=[walk(D_MODEL), pl.BlockSpec((ns, D_MODEL), lambda i: (0, 0))],
        out_shape=[jax.ShapeDtypeStruct((n, D_MODEL), F32), jax.ShapeDtypeStruct((ns, D_MODEL), F32)],
        scratch_shapes=[
            pltpu.VMEM((N_GROUPS, HEADS_PER_GROUP * tt, period), F32),
            pltpu.VMEM((ns, W_ATT), BF16),
        ],
        compiler_params=pltpu.CompilerParams(
            dimension_semantics=("arbitrary",), vmem_limit_bytes=VMEM_LIMIT),
        name="back",
    )(*prompt, *sample, q3, k3, v3, kc_t, vc_t, ga3, w_bias, wo, pg, wpg, wpe)


def _front_sample_kernel(x_ref, ng_ref, win_ref, cw_ref, cbias_ref, wg_ref, bg_ref, lam_ref,
                         qg_ref, kg_ref, ones_ref, sconv_ref, slru_ref,
                         lru_ref, q_ref, k_ref, v_ref, ga_ref, sc_ref, sh_ref,
                         cb_scr, xc_scr, a_scr, u_scr, h_scr, *, nb, tt):
    seg = tt + SUBLANES
    xn = _rms_rows(x_ref[...], ng_ref[...]).astype(BF16)
    z = _dot(xn, win_ref[...])
    xl = z[:, :W_LRU]
    for s in range(nb):
        cb_scr[s * seg:s * seg + SUBLANES, :] = sconv_ref[s]
        cb_scr[s * seg + SUBLANES:(s + 1) * seg, :] = xl[s * tt:(s + 1) * tt, :]
        xc_scr[s * tt:(s + 1) * tt, :] = _conv_rows(cb_scr, s * seg, tt, cw_ref, cbias_ref[...])
        sc_ref[s] = cb_scr[(s + 1) * seg - (CONV_W - 1):(s + 1) * seg, :]
    _lru_inputs(xc_scr[...], wg_ref, bg_ref, lam_ref, a_scr, u_scr)
    for s in range(nb):
        sh_ref[s] = _scan_rows(a_scr, u_scr, h_scr, s * tt, tt, slru_ref[s], unroll=True)
    lru_ref[...] = (h_scr[...] * jax.nn.silu(z[:, W_LRU:2 * W_LRU])).astype(BF16)

    o = 2 * W_LRU
    q = _head_norm_rows(z[:, o:o + W_ATT], ones_ref[...], qg_ref[...])
    q_ref[...] = (q * (HEAD_DIM ** -0.5)).astype(BF16)
    k_ref[...] = _head_norm_rows(z[:, o + W_ATT:o + 2 * W_ATT], ones_ref[...], kg_ref[...])
    v_ref[...] = z[:, o + 2 * W_ATT:o + 3 * W_ATT]
    ga_ref[...] = jax.nn.silu(z[:, o + 3 * W_ATT:])


def _front_sample(x2, ng, win, cw, cbias, wg, bg, lam, qg_t, kg_t, ones_bd, sconv_pad, slru, nb, tt):
    n = x2.shape[0]
    rows = nb * tt
    const = lambda shape: pl.BlockSpec(shape, lambda i: (0,) * len(shape))
    row_spec = lambda w: pl.BlockSpec((rows, w), lambda i: (i, 0))
    nseq = n // tt
    kern = functools.partial(_front_sample_kernel, nb=nb, tt=tt)
    return pl.pallas_call(
        kern,
        grid=(n // rows,),
        in_specs=[
            row_spec(D_MODEL),
            const((1, D_MODEL)),
            const((D_MODEL, 2 * W_LRU + 4 * W_ATT)),
            const((CONV_W, W_LRU)),
            const((1, W_LRU)),
            const((W_LRU // HEAD_GROUP, HEAD_GROUP, 2 * HEAD_GROUP)),
            const((W_LRU // HEAD_GROUP, 1, 2 * HEAD_GROUP)),
            const((1, W_LRU)),
            const((1, W_ATT)),
            const((1, W_ATT)),
            const((HEAD_GROUP, HEAD_GROUP)),
            pl.BlockSpec((nb, SUBLANES, W_LRU), lambda i: (i, 0, 0)),
            pl.BlockSpec((nb, 1, W_LRU), lambda i: (i, 0, 0)),
        ],
        out_specs=[
            row_spec(W_LRU), row_spec(W_ATT), row_spec(W_ATT), row_spec(W_ATT), row_spec(W_ATT),
            pl.BlockSpec((nb, CONV_W - 1, W_LRU), lambda i: (i, 0, 0)),
            pl.BlockSpec((nb, 1, W_LRU), lambda i: (i, 0, 0)),
        ],
        out_shape=[
            jax.ShapeDtypeStruct((n, W_LRU), BF16),
            jax.ShapeDtypeStruct((n, W_ATT), BF16),
            jax.ShapeDtypeStruct((n, W_ATT), F32),
            jax.ShapeDtypeStruct((n, W_ATT), F32),
            jax.ShapeDtypeStruct((n, W_ATT), F32),
            jax.ShapeDtypeStruct((nseq, CONV_W - 1, W_LRU), F32),
            jax.ShapeDtypeStruct((nseq, 1, W_LRU), F32),
        ],
        scratch_shapes=[
            pltpu.VMEM((nb * (tt + SUBLANES), W_LRU), F32),
            pltpu.VMEM((rows, W_LRU), F32),
            pltpu.VMEM((rows, W_LRU), F32),
            pltpu.VMEM((rows, W_LRU), F32),
            pltpu.VMEM((rows, W_LRU), F32),
        ],
        compiler_params=pltpu.CompilerParams(
            dimension_semantics=("arbitrary",), vmem_limit_bytes=VMEM_LIMIT),
        name="front_sample",
    )(x2, ng, win, cw, cbias, wg, bg, lam, qg_t, kg_t, ones_bd, sconv_pad, slru)


def _fill_sample_bias(w_ref, bias_ref, tt):
    for h in range(N_HEADS):
        g, hl = divmod(h, HEADS_PER_GROUP)
        bias_ref[g, hl * tt:(hl + 1) * tt, :] = _toeplitz_rows(w_ref[h], tt, 0)


def _attn_sample_streams(q_ref, kn_ref, vn_ref, kc_ref, vc_ref, ga_ref, bias_ref, write):
    tt = q_ref.shape[1]
    lc = kc_ref.shape[-1]
    lanes = lax.broadcasted_iota(jnp.int32, (tt, HEAD_GROUP), 1)
    masks = [(lanes >= hl * HEAD_DIM) & (lanes < (hl + 1) * HEAD_DIM) for hl in range(HEADS_PER_GROUP)]
    units = [(s, g) for s in range(q_ref.shape[0]) for g in range(N_GROUPS)]

    def cached(ref, s, g):
        heads = ref[s, g * HEADS_PER_GROUP:(g + 1) * HEADS_PER_GROUP]
        return heads.reshape(HEAD_GROUP, lc).astype(BF16)

    def scores(s, g):
        gsl = slice(g * HEAD_GROUP, (g + 1) * HEAD_GROUP)
        qg = q_ref[s, :, gsl]
        qs = jnp.concatenate([jnp.where(m, qg, jnp.zeros((), BF16)) for m in masks], axis=0)
        sc = _dot(qs, cached(kc_ref, s, g)) + bias_ref[g, :, 0:lc]
        sn = _dot_nt(qs, kn_ref[s, :, gsl].astype(BF16)) + bias_ref[g, :, lc:lc + tt]
        return sc, sn

    def finish(s, g, sc, sn):
        gsl = slice(g * HEAD_GROUP, (g + 1) * HEAD_GROUP)
        m = jnp.maximum(jnp.max(sc, axis=-1, keepdims=True), jnp.max(sn, axis=-1, keepdims=True))
        pc = jnp.exp(sc - m)
        pn = jnp.exp(sn - m)
        l = jnp.sum(pc, axis=-1, keepdims=True) + jnp.sum(pn, axis=-1, keepdims=True)
        o = _dot_nt(pc.astype(BF16), cached(vc_ref, s, g))
        o = (o + _dot(pn.astype(BF16), vn_ref[s, :, gsl].astype(BF16))) * (1.0 / l)
        att = jnp.zeros((tt, HEAD_GROUP), F32)
        for hl in range(HEADS_PER_GROUP):
            att = att + jnp.where(masks[hl], o[hl * tt:(hl + 1) * tt, :], 0.0)
        write(s, gsl, (att * ga_ref[s, :, gsl]).astype(BF16))

    nxt = scores(*units[0])
    for n, (s, g) in enumerate(units):
        cur = nxt
        if n + 1 < len(units):
            nxt = scores(*units[n + 1])
        finish(s, g, *cur)


def _block_diag(w):
    n, d, e = w.shape
    eye = jnp.eye(n, dtype=w.dtype)
    return (eye[:, None, :, None] * w[:, :, None, :]).reshape(n * d, n * e)


def _prompt_bias_period(table):
    assert BAND - MAX_REL == MAX_REL and PROMPT_BIAS_PERIOD == KEY_TILES * QBLOCK + QBLOCK
    last = table[2 * MAX_REL:]
    neg_d = jnp.concatenate([table, jnp.broadcast_to(last, (MAX_REL - 1, N_HEADS))])
    w = jnp.concatenate([jnp.broadcast_to(last, (QBLOCK, N_HEADS)), neg_d])
    return w.T.reshape(N_HEADS, 1, PROMPT_BIAS_PERIOD).astype(F32)


def _sample_bias_period(table, tt, l):
    assert l >= MAX_REL
    period = -(-(l + 2 * tt - 1) // LANES) * LANES
    last = table[2 * MAX_REL:]
    n_var = tt + MAX_REL - 1
    var = table[2 * MAX_REL - 1:2 * MAX_REL - 1 - n_var:-1]
    w = jnp.concatenate([jnp.broadcast_to(last, (l - MAX_REL + 1, N_HEADS)), var,
                         jnp.broadcast_to(last, (period - (l + tt), N_HEADS))])
    return w.T.reshape(N_HEADS, 1, period).astype(F32)


def kernel(x_prompt, x_sample, p_prompt, p_sample, cache_k, cache_v, state_conv, state_lru, norm_g, w_in, conv_w, conv_b, gate_a_w, gate_a_b, gate_x_w, gate_x_b, lru_lambda, q_norm_g, k_norm_g, rel_bias, w_out, ple_norm_g, w_ple_gate, w_ple_proj):
    depth = w_in.shape[0]
    b, s, _ = x_prompt.shape
    db, ds, _ = x_sample.shape
    lc = cache_k.shape[2]
    yp, ys = x_prompt, x_sample.reshape(db * ds, D_MODEL)
    ones_bd = _block_diag(jnp.full((HEADS_PER_GROUP, HEAD_DIM, HEAD_DIM), 1.0 / HEAD_DIM, F32)).astype(BF16)
    gate_halves = W_LRU // HEAD_GROUP
    blocks_per_half = LRU_BLOCKS // gate_halves
    outs = [[] for _ in range(8)]
    for l in range(depth):
        win = w_in[l].astype(BF16)
        o = 2 * W_LRU
        wnat = jnp.concatenate([win[:, :o], win[:, o + W_ATT:o + 2 * W_ATT]], axis=1)
        wt = jnp.concatenate([win[:, o:o + W_ATT], win[:, o + 2 * W_ATT:]], axis=1).T
        ng = norm_g[l].reshape(1, D_MODEL)
        cw = conv_w[l]
        cbias = conv_b[l].reshape(1, W_LRU)
        wg = jnp.stack([
            jnp.concatenate([_block_diag(w[j * blocks_per_half:(j + 1) * blocks_per_half])
                             for w in (gate_a_w[l], gate_x_w[l])], axis=1)
            for j in range(gate_halves)]).astype(BF16)
        bg = jnp.concatenate([gate_a_b[l].reshape(gate_halves, 1, HEAD_GROUP),
                              gate_x_b[l].reshape(gate_halves, 1, HEAD_GROUP)], axis=2)
        lam = lru_lambda[l].reshape(1, W_LRU)
        qg_col = q_norm_g[l].reshape(HEAD_DIM, 1)
        qg_t = jnp.tile(q_norm_g[l], N_HEADS).reshape(1, W_ATT)
        kg_t = jnp.tile(k_norm_g[l], N_HEADS).reshape(1, W_ATT)
        wo = w_out[l].astype(BF16)
        pg = ple_norm_g[l].reshape(1, D_MODEL)
        wpg = w_ple_gate[l].astype(BF16)
        wpe = w_ple_proj[l].astype(BF16)

        lru_g, qt, kbf, vt, gat, pk, pv, pc, ph = _front_prompt(
            yp, ng, wnat, wt, cw, cbias, wg, bg, lam, qg_col, kg_t, ones_bd)
        att_g = _attn_prompt(qt, kbf, vt, _prompt_bias_period(rel_bias[l]), gat)
        to_frames = lambda a: jnp.transpose(a.reshape(b, N_HEADS, HEAD_DIM, a.shape[-1]), (0, 3, 1, 2))
        outs[0].append(to_frames(pk))
        outs[1].append(to_frames(pv))
        outs[2].append(pc)
        outs[3].append(ph.reshape(b, W_LRU))

        sconv_pad = jnp.pad(state_conv[l], ((0, 0), (SUBLANES - (CONV_W - 1), 0), (0, 0)))
        lru_s, q_s, k_s, v_s, ga_s, sc, sh = _front_sample(
            ys, ng, win, cw, cbias, wg, bg, lam, qg_t, kg_t, ones_bd,
            sconv_pad, state_lru[l].reshape(db, 1, W_LRU), SAMPLE_STREAMS_PER_BLOCK, ds)
        yp, ys = _back(
            (yp.reshape(b * s, D_MODEL), lru_g.reshape(b * s, W_LRU), att_g.reshape(b * s, W_ATT),
             p_prompt[l].reshape(b * s, PLE_DIM)),
            (ys, lru_s, p_sample[l].reshape(db * ds, PLE_DIM)),
            (q_s.reshape(db, ds, W_ATT), k_s.reshape(db, ds, W_ATT), v_s.reshape(db, ds, W_ATT),
             jnp.transpose(cache_k[l], (0, 2, 3, 1)), jnp.transpose(cache_v[l], (0, 2, 3, 1)),
             ga_s.reshape(db, ds, W_ATT), _sample_bias_period(rel_bias[l], ds, lc)),
            wo, pg, wpg, wpe, BACK_ROWS)
        yp = yp.reshape(b, s, D_MODEL)
        outs[4].append(k_s.reshape(db, ds, N_HEADS, HEAD_DIM))
        outs[5].append(v_s.reshape(db, ds, N_HEADS, HEAD_DIM))
        outs[6].append(sc)
        outs[7].append(sh.reshape(db, W_LRU))
    return (yp, ys.reshape(db, ds, D_MODEL)) + tuple(jnp.stack(o) for o in outs)
```

```python
import functools

import jax
import jax.numpy as jnp
import numpy as np
from jax import lax
from jax.experimental import pallas as pl
from jax.experimental.pallas import tpu as pltpu

D_MODEL = 1024
CHUNK = 64
PAST_CHUNKS = 8
BAND = PAST_CHUNKS * CHUNK
W_LRU = D_MODEL // 2
LRU_BLOCKS = 8
CONV_W = 4
RG_C = 8.0
HEAD_DIM = 64
W_ATT = D_MODEL // 2
N_HEADS = W_ATT // HEAD_DIM
MAX_REL = 256
PLE_DIM = 256
EPS = 1e-6
NEG = -1e30
LOG2E = 1.4426950408889634

SUBLANES = 8
LANES = 128
HEAD_GROUP = 256
HEADS_PER_GROUP = HEAD_GROUP // HEAD_DIM
N_GROUPS = W_ATT // HEAD_GROUP
QBLOCK = 256
KEY_TILES = BAND // QBLOCK + 1
PROMPT_BIAS_PERIOD = (KEY_TILES + 1) * QBLOCK
FRONT_BLOCK = 1024
BACK_ROWS = 1024
SAMPLE_STREAMS_PER_BLOCK = 8
VMEM_LIMIT = 56 * 1024 * 1024

F32 = jnp.float32
BF16 = jnp.bfloat16


def _dot(a, b):
    return jnp.dot(a, b, preferred_element_type=F32)


def _dot_nt(a, b):
    return lax.dot_general(a, b, (((1,), (1,)), ((), ())), preferred_element_type=F32)


def _rms_rows(x, g):
    ms = jnp.mean(x * x, axis=-1, keepdims=True)
    return x * lax.rsqrt(ms + EPS) * g


def _head_norm_rows(x, ones_bd, g_tiled):
    x2 = x * x
    hi = x2.astype(BF16)
    lo = (x2 - hi.astype(F32)).astype(BF16)
    ms = jnp.concatenate(
        [_dot(hi[:, g * HEAD_GROUP:(g + 1) * HEAD_GROUP], ones_bd)
         + _dot(lo[:, g * HEAD_GROUP:(g + 1) * HEAD_GROUP], ones_bd) for g in range(N_GROUPS)], axis=1)
    return x * lax.rsqrt(ms + EPS) * g_tiled


def _scan_rows(a_ref, u_ref, h_ref, row0, nrows, h0, unroll=False):
    ridx = lax.broadcasted_iota(jnp.int32, (SUBLANES, W_LRU), 0)

    def body(i, hprev):
        r = pl.multiple_of(row0 + i * SUBLANES, SUBLANES)
        a = a_ref[pl.ds(r, SUBLANES), :]
        u = u_ref[pl.ds(r, SUBLANES), :]
        for s in (1, 2, 4):
            a_s = jnp.where(ridx >= s, pltpu.roll(a, s, 0), 1.0)
            u_s = jnp.where(ridx >= s, pltpu.roll(u, s, 0), 0.0)
            u = a * u_s + u
            a = a * a_s
        h = a * hprev + u
        h_ref[pl.ds(r, SUBLANES), :] = h
        return h[SUBLANES - 1:SUBLANES, :]

    return lax.fori_loop(0, nrows // SUBLANES, body, h0, unroll=unroll)


def _lru_inputs(xc, wg_ref, bg_ref, lam_ref, a_ref, u_ref):
    xcb = xc.astype(BF16)
    half = wg_ref.shape[1]
    for j in range(wg_ref.shape[0]):
        sl = slice(j * half, (j + 1) * half)
        gates = _dot(xcb[:, sl], wg_ref[j]) + bg_ref[j]
        r = jax.nn.sigmoid(gates[:, :half])
        i = jax.nn.sigmoid(gates[:, half:])
        log_a = -RG_C * r * jax.nn.softplus(-lam_ref[:, sl])
        a = jnp.exp(log_a)
        a_ref[:, sl] = a
        u_ref[:, sl] = jnp.sqrt(jnp.tanh(-log_a) * (1.0 + a * a)) * (i * xc[:, sl])


def _conv_rows(cb_ref, base, nrows, cw_ref, cb_bias):
    out = cb_bias + cw_ref[CONV_W - 1:CONV_W, :] * cb_ref[pl.ds(base + SUBLANES, nrows), :]
    for k in range(CONV_W - 1):
        shift = CONV_W - 1 - k
        out = out + cw_ref[k:k + 1, :] * cb_ref[pl.ds(base + SUBLANES - shift, nrows), :]
    return out


def _front_prompt_kernel(x_ref, ng_ref, wnat_ref, wt_ref, cw_ref, cbias_ref, wg_ref, bg_ref, lam_ref,
                         qg_ref, kg_ref, ones_ref,
                         lru_ref, qt_ref, kbf_ref, vt_ref, gat_ref, pk_ref, pv_ref, pc_ref, ph_ref,
                         zn_scr, zt_scr, cb_scr, a_scr, u_scr, h_scr, hlast_scr, *, keep_subs):
    sub = QBLOCK
    n_sub = x_ref.shape[1] // sub

    @pl.when(pl.program_id(1) == 0)
    def _():
        cb_scr[0:SUBLANES, :] = jnp.zeros((SUBLANES, W_LRU), F32)
        hlast_scr[...] = jnp.zeros((1, W_LRU), F32)

    def project(i):
        xn = _rms_rows(x_ref[0, i * sub:(i + 1) * sub, :], ng_ref[...]).astype(BF16)
        zn_scr[i % 2] = _dot(xn, wnat_ref[...])
        zt_scr[i % 2] = _dot_nt(wt_ref[...], xn)

    def finish(i):
        rows = slice(i * sub, (i + 1) * sub)
        zn = zn_scr.at[i % 2]
        zt = zt_scr.at[i % 2]
        cb_scr[SUBLANES:SUBLANES + sub, :] = zn[:, :W_LRU]
        xc = _conv_rows(cb_scr, 0, sub, cw_ref, cbias_ref[...])
        pc_ref[0] = cb_scr[sub + SUBLANES - (CONV_W - 1):sub + SUBLANES, :]
        cb_scr[0:SUBLANES, :] = cb_scr[sub:sub + SUBLANES, :]
        _lru_inputs(xc, wg_ref, bg_ref, lam_ref, a_scr, u_scr)
        h_last = _scan_rows(a_scr, u_scr, h_scr, 0, sub, hlast_scr[...], unroll=True)
        hlast_scr[...] = h_last
        ph_ref[0] = h_last
        lru_ref[0, rows, :] = (h_scr[...] * jax.nn.silu(zn[:, W_LRU:2 * W_LRU])).astype(BF16)
        k = _head_norm_rows(zn[:, 2 * W_LRU:], ones_ref[...], kg_ref[...])
        kbf_ref[0, rows, :] = k.astype(BF16)
        q3 = zt[0:W_ATT, :].reshape(N_HEADS, HEAD_DIM, sub)
        ms = jnp.mean(q3 * q3, axis=1, keepdims=True)
        qn = q3 * lax.rsqrt(ms + EPS) * (qg_ref[...] * (HEAD_DIM ** -0.5 * LOG2E))
        qt_ref[0, :, rows] = qn.reshape(W_ATT, sub).astype(BF16)
        vt = zt[W_ATT:2 * W_ATT, :]
        vt_ref[0, :, rows] = vt.astype(BF16)
        gat_ref[0, :, rows] = jax.nn.silu(zt[2 * W_ATT:, :])
        if i >= n_sub - keep_subs:
            first = (i - (n_sub - keep_subs)) * sub
            pk_ref[0, :, first:first + sub] = k.T
            pv_ref[0, :, first:first + sub] = vt

    project(0)
    for i in range(n_sub):
        if i + 1 < n_sub:
            project(i + 1)
        finish(i)


def _front_prompt(x, ng, wnat, wt, cw, cbias, wg, bg, lam, qg, kg, ones_bd):
    b, s, _ = x.shape
    tb = min(FRONT_BLOCK, s)
    keep = min(BAND, s)
    assert s % tb == 0 and tb % QBLOCK == 0 and keep % QBLOCK == 0 and keep <= tb
    const = lambda shape: pl.BlockSpec(shape, lambda i, j: (0,) * len(shape))
    rows_spec = lambda w: pl.BlockSpec((1, tb, w), lambda i, j: (i, j, 0))
    cols_spec = pl.BlockSpec((1, W_ATT, tb), lambda i, j: (i, 0, j))
    per_seq = lambda r, w: pl.BlockSpec((1, r, w), lambda i, j: (i, 0, 0))
    kern = functools.partial(_front_prompt_kernel, keep_subs=keep // QBLOCK)
    return pl.pallas_call(
        kern,
        grid=(b, s // tb),
        in_specs=[
            rows_spec(D_MODEL),
            const((1, D_MODEL)),
            const((D_MODEL, 3 * W_LRU)),
            const((3 * W_ATT, D_MODEL)),
            const((CONV_W, W_LRU)),
            const((1, W_LRU)),
            const((W_LRU // HEAD_GROUP, HEAD_GROUP, 2 * HEAD_GROUP)),
            const((W_LRU // HEAD_GROUP, 1, 2 * HEAD_GROUP)),
            const((1, W_LRU)),
            const((HEAD_DIM, 1)),
            const((1, W_ATT)),
            const((HEAD_GROUP, HEAD_GROUP)),
        ],
        out_specs=[
            rows_spec(W_LRU),
            cols_spec,
            rows_spec(W_ATT),
            cols_spec,
            cols_spec,
            per_seq(W_ATT, keep),
            per_seq(W_ATT, keep),
            per_seq(CONV_W - 1, W_LRU),
            per_seq(1, W_LRU),
        ],
        out_shape=[
            jax.ShapeDtypeStruct((b, s, W_LRU), BF16),
            jax.ShapeDtypeStruct((b, W_ATT, s), BF16),
            jax.ShapeDtypeStruct((b, s, W_ATT), BF16),
            jax.ShapeDtypeStruct((b, W_ATT, s), BF16),
            jax.ShapeDtypeStruct((b, W_ATT, s), F32),
            jax.ShapeDtypeStruct((b, W_ATT, keep), F32),
            jax.ShapeDtypeStruct((b, W_ATT, keep), F32),
            jax.ShapeDtypeStruct((b, CONV_W - 1, W_LRU), F32),
            jax.ShapeDtypeStruct((b, 1, W_LRU), F32),
        ],
        scratch_shapes=[
            pltpu.VMEM((2, QBLOCK, 3 * W_LRU), F32),
            pltpu.VMEM((2, 3 * W_ATT, QBLOCK), F32),
            pltpu.VMEM((QBLOCK + SUBLANES, W_LRU), F32),
            pltpu.VMEM((QBLOCK, W_LRU), F32),
            pltpu.VMEM((QBLOCK, W_LRU), F32),
            pltpu.VMEM((QBLOCK, W_LRU), F32),
            pltpu.VMEM((1, W_LRU), F32),
        ],
        compiler_params=pltpu.CompilerParams(
            dimension_semantics=("arbitrary", "arbitrary"), vmem_limit_bytes=VMEM_LIMIT),
        name="front_prompt",
    )(x, ng, wnat, wt, cw, cbias, wg, bg, lam, qg, kg, ones_bd)


def _toeplitz_rows(w_row, nrows, row0):
    x = jnp.broadcast_to(w_row, (nrows, w_row.shape[1]))
    return pltpu.roll(x, row0, 1, stride=1, stride_axis=0)


def _fill_prompt_bias(w_ref, bias_ref):
    q_chunk = (lax.broadcasted_iota(jnp.int32, (CHUNK, QBLOCK), 1) + BAND) // CHUNK
    for h in range(N_HEADS):
        def body(n, carry, h=h):
            r0 = pl.multiple_of(n * CHUNK, CHUNK)
            t = _toeplitz_rows(w_ref[h], CHUNK, r0)[:, :QBLOCK]
            dc = q_chunk - n
            bias_ref[h, pl.ds(r0, CHUNK), :] = jnp.where((dc >= 0) & (dc <= PAST_CHUNKS), t * LOG2E, NEG)
            return carry
        lax.fori_loop(0, KEY_TILES * QBLOCK // CHUNK, body, 0)


SCORE_LEAD = 4
ATTN_BLOCKS = 4
CHUNKS_PER_TILE = QBLOCK // CHUNK
FRAMES_PER_VREG = LANES // CHUNK


def _lane_cols(kc):
    cols = []
    for c in range(QBLOCK // LANES):
        q_lo = PAST_CHUNKS + c * FRAMES_PER_VREG
        q_hi = q_lo + FRAMES_PER_VREG - 1
        if q_lo - PAST_CHUNKS <= kc <= q_hi:
            cols.append(c)
    return cols


def _fold_rows(x):
    return x.reshape(x.shape[0] // SUBLANES, SUBLANES, x.shape[1])


def _attn_prompt_blocks(blocks, qt_ref, k_ref, vt_ref, gat_ref, out_ref, bias_ref, s_scr, att_scr):
    n_cols = QBLOCK // LANES
    rows = lax.broadcasted_iota(jnp.int32, (HEAD_GROUP, QBLOCK), 0)
    units = [(n, h) for n in range(len(blocks)) for h in range(N_HEADS)]

    def pieces(tiles):
        return [(i, cc, slice((i * CHUNKS_PER_TILE + cc) * CHUNK, (i * CHUNKS_PER_TILE + cc + 1) * CHUNK),
                 slice(c * LANES, (c + 1) * LANES), c)
                for i in tiles for cc in range(CHUNKS_PER_TILE) for c in _lane_cols(i * CHUNKS_PER_TILE + cc)]

    def key_rows(n, i):
        tiles, key0 = blocks[n]
        return pl.ds(key0 + (i - tiles[0]) * QBLOCK, QBLOCK)

    def scores(u):
        n, h = units[u]
        g, hl = divmod(h, HEADS_PER_GROUP)
        gsl = slice(g * HEAD_GROUP, (g + 1) * HEAD_GROUP)
        in_head = (rows >= hl * HEAD_DIM) & (rows < (hl + 1) * HEAD_DIM)
        qm = jnp.where(in_head, qt_ref[0, gsl, n * QBLOCK:(n + 1) * QBLOCK], jnp.zeros((), BF16))
        s = {i: _dot(k_ref[0, key_rows(n, i), gsl], qm) for i in blocks[n][0]}
        m_acc = [jnp.full((SUBLANES, LANES), NEG, F32) for _ in range(n_cols)]
        for i, cc, rsl, lsl, c in pieces(blocks[n][0]):
            sp = s[i][cc * CHUNK:(cc + 1) * CHUNK, lsl] + bias_ref[h, rsl, lsl]
            s_scr[u % (SCORE_LEAD + 1), rsl, lsl] = sp
            m_acc[c] = jnp.maximum(m_acc[c], jnp.max(_fold_rows(sp), axis=0))
        return [jnp.max(a, axis=0, keepdims=True) for a in m_acc]

    def weights(u, m):
        tiles_p = {}
        for i in blocks[units[u][0]][0]:
            chunks = []
            for cc in range(CHUNKS_PER_TILE):
                kc = i * CHUNKS_PER_TILE + cc
                rsl = slice(kc * CHUNK, (kc + 1) * CHUNK)
                cols = []
                for c in range(n_cols):
                    if c in _lane_cols(kc):
                        lsl = slice(c * LANES, (c + 1) * LANES)
                        cols.append(jnp.exp2(s_scr[u % (SCORE_LEAD + 1), rsl, lsl] - m[c]).astype(BF16))
                    else:
                        cols.append(jnp.zeros((CHUNK, LANES), BF16))
                chunks.append(jnp.concatenate(cols, axis=1))
            tiles_p[i] = jnp.concatenate(chunks, axis=0)
        return tiles_p

    ones_rows = jnp.ones((2 * SUBLANES, QBLOCK), BF16)

    def values(u, tiles_p):
        n, h = units[u]
        hsl = slice(h * HEAD_DIM, (h + 1) * HEAD_DIM)
        o = None
        for i in blocks[n][0]:
            oi = _dot(jnp.concatenate([vt_ref[0, hsl, key_rows(n, i)], ones_rows], axis=0), tiles_p[i])
            o = oi if o is None else o + oi
        att_scr[n, hsl, :] = o[0:HEAD_DIM, :] * (1.0 / o[HEAD_DIM:HEAD_DIM + 1, :])
        if h == N_HEADS - 1:
            fr = slice(n * QBLOCK, (n + 1) * QBLOCK)
            out_ref[0, fr, :] = (att_scr[n] * gat_ref[0, :, fr]).T.astype(BF16)

    m = {u: scores(u) for u in range(min(SCORE_LEAD, len(units)))}
    w = {}
    for u in range(len(units)):
        if u + SCORE_LEAD < len(units):
            m[u + SCORE_LEAD] = scores(u + SCORE_LEAD)
        w[u] = weights(u, m.pop(u))
        if u >= 1:
            values(u - 1, w.pop(u - 1))
    values(len(units) - 1, w.pop(len(units) - 1))


def _attn_prompt_kernel(qt_ref, k_ref, vt_ref, w_ref, gat_ref, out_ref, bias_ref, s_scr, att_scr):
    step = pl.program_id(1)

    @pl.when((pl.program_id(0) == 0) & (step == 0))
    def _():
        _fill_prompt_bias(w_ref, bias_ref)

    assert ATTN_BLOCKS >= KEY_TILES - 1
    full = tuple(range(KEY_TILES))

    @pl.when(step == 0)
    def _():
        blocks = [(full[max(KEY_TILES - 1 - n, 0):], max(n - (KEY_TILES - 1), 0) * QBLOCK)
                  for n in range(ATTN_BLOCKS)]
        _attn_prompt_blocks(blocks, qt_ref, k_ref, vt_ref, gat_ref, out_ref, bias_ref, s_scr, att_scr)

    if k_ref.shape[1] > ATTN_BLOCKS * QBLOCK:
        @pl.when(step > 0)
        def _():
            first = step * ATTN_BLOCKS - (KEY_TILES - 1)
            blocks = [(full, pl.multiple_of((first + n) * QBLOCK, QBLOCK)) for n in range(ATTN_BLOCKS)]
            _attn_prompt_blocks(blocks, qt_ref, k_ref, vt_ref, gat_ref, out_ref, bias_ref, s_scr, att_scr)


def _attn_prompt(qt, kbf, vt, w_bias, gat):
    b, _, s = qt.shape
    fb = ATTN_BLOCKS * QBLOCK
    assert s % fb == 0
    return pl.pallas_call(
        _attn_prompt_kernel,
        grid=(b, s // fb),
        in_specs=[
            pl.BlockSpec((1, W_ATT, fb), lambda i, j: (i, 0, j)),
            pl.BlockSpec((1, s, W_ATT), lambda i, j: (i, 0, 0)),
            pl.BlockSpec((1, W_ATT, s), lambda i, j: (i, 0, 0)),
            pl.BlockSpec((N_HEADS, 1, PROMPT_BIAS_PERIOD), lambda i, j: (0, 0, 0)),
            pl.BlockSpec((1, W_ATT, fb), lambda i, j: (i, 0, j)),
        ],
        out_specs=pl.BlockSpec((1, fb, W_ATT), lambda i, j: (i, j, 0)),
        out_shape=jax.ShapeDtypeStruct((b, s, W_ATT), BF16),
        scratch_shapes=[
            pltpu.VMEM((N_HEADS, KEY_TILES * QBLOCK, QBLOCK), F32),
            pltpu.VMEM((SCORE_LEAD + 1, KEY_TILES * QBLOCK, QBLOCK), F32),
            pltpu.VMEM((ATTN_BLOCKS, W_ATT, QBLOCK), F32),
        ],
        compiler_params=pltpu.CompilerParams(
            dimension_semantics=("arbitrary", "arbitrary"), vmem_limit_bytes=VMEM_LIMIT),
        name="attn_prompt",
    )(qt, kbf, vt, w_bias, gat)


def _back_rows(x_ref, lru_ref, att_ref, p_ref, wo_ref, pg_ref, wpg_ref, wpe_ref, y_ref):
    mix = _dot(lru_ref[...], wo_ref[0:W_LRU, :]) + _dot(att_ref[...], wo_ref[W_LRU:, :])
    h = x_ref[...] + mix
    gate = jax.nn.sigmoid(_dot(_rms_rows(h, pg_ref[...]).astype(BF16), wpg_ref[...]))
    y_ref[...] = h + _dot(p_ref[...].astype(BF16), wpe_ref[...]) * gate


def _back_kernel(xp_ref, lrup_ref, attp_ref, pp_ref, xs_ref, lrus_ref, ps_ref,
                 q_ref, kn_ref, vn_ref, kc_ref, vc_ref, ga_ref, wb_ref,
                 wo_ref, pg_ref, wpg_ref, wpe_ref, yp_ref, ys_ref, bias_scr, atts_scr, *, prompt_steps):
    step = pl.program_id(0)
    weights = (wo_ref, pg_ref, wpg_ref, wpe_ref)
    streams, tt = q_ref.shape[0], q_ref.shape[1]

    @pl.when(step == 0)
    def _():
        _fill_sample_bias(wb_ref, bias_scr, tt)

    @pl.when(step < prompt_steps)
    def _():
        def write(s, lanes, rows):
            atts_scr[pl.ds(pl.multiple_of((step * streams + s) * tt, tt), tt), lanes] = rows

        _attn_sample_streams(q_ref, kn_ref, vn_ref, kc_ref, vc_ref, ga_ref, bias_scr, write)
        _back_rows(xp_ref, lrup_ref, attp_ref, pp_ref, *weights, yp_ref)

    @pl.when(step == prompt_steps)
    def _():
        _back_rows(xs_ref, lrus_ref, atts_scr, ps_ref, *weights, ys_ref)


def _back(prompt, sample, sample_attn, wo, pg, wpg, wpe, rows):
    n, ns = prompt[0].shape[0], sample[0].shape[0]
    q3, k3, v3, kc_t, vc_t, ga3, w_bias = sample_attn
    db, tt, _ = q3.shape
    assert n % rows == 0 and db * tt == ns
    steps = n // rows
    assert db % steps == 0
    per_step = db // steps
    const = lambda shape: pl.BlockSpec(shape, lambda i: (0,) * len(shape))
    walk = lambda w: pl.BlockSpec((rows, w), lambda i: (jnp.minimum(i, steps - 1), 0))
    whole = lambda w: pl.BlockSpec((ns, w), lambda i: (0, 0), pipeline_mode=pl.Buffered(1))
    seq = pl.BlockSpec((per_step, tt, W_ATT), lambda i: (jnp.minimum(i, steps - 1), 0, 0))
    cache = pl.BlockSpec((per_step, N_HEADS, HEAD_DIM, kc_t.shape[-1]),
                         lambda i: (jnp.minimum(i, steps - 1), 0, 0, 0))
    period = w_bias.shape[-1]
    return pl.pallas_call(
        functools.partial(_back_kernel, prompt_steps=steps),
        grid=(steps + 1,),
        in_specs=[walk(w) for w in (D_MODEL, W_LRU, W_ATT, PLE_DIM)]
        + [whole(w) for w in (D_MODEL, W_LRU, PLE_DIM)]
        + [seq, seq, seq, cache, cache, seq, const((N_HEADS, 1, period))]
        + [const((W_LRU + W_ATT, D_MODEL)), const((1, D_MODEL)), const((D_MODEL, D_MODEL)),
           const((PLE_DIM, D_MODEL))],
        out_specs=[walk(D_MODEL), pl.BlockSpec((ns, D_MODEL), lambda i: (0, 0))],
        out_shape=[jax.ShapeDtypeStruct((n, D_MODEL), F32), jax.ShapeDtypeStruct((ns, D_MODEL), F32)],
        scratch_shapes=[
            pltpu.VMEM((N_GROUPS, HEADS_PER_GROUP * tt, period), F32),
            pltpu.VMEM((ns, W_ATT), BF16),
        ],
        compiler_params=pltpu.CompilerParams(
            dimension_semantics=("arbitrary",), vmem_limit_bytes=VMEM_LIMIT),
        name="back",
    )(*prompt, *sample, q3, k3, v3, kc_t, vc_t, ga3, w_bias, wo, pg, wpg, wpe)


def _front_sample_kernel(x_ref, ng_ref, win_ref, cw_ref, cbias_ref, wg_ref, bg_ref, lam_ref,
                         qg_ref, kg_ref, ones_ref, sconv_ref, slru_ref,
                         lru_ref, q_ref, k_ref, v_ref, ga_ref, sc_ref, sh_ref,
                         cb_scr, xc_scr, a_scr, u_scr, h_scr, *, nb, tt):
    seg = tt + SUBLANES
    xn = _rms_rows(x_ref[...], ng_ref[...]).astype(BF16)
    z = _dot(xn, win_ref[...])
    xl = z[:, :W_LRU]
    for s in range(nb):
        cb_scr[s * seg:s * seg + SUBLANES, :] = sconv_ref[s]
        cb_scr[s * seg + SUBLANES:(s + 1) * seg, :] = xl[s * tt:(s + 1) * tt, :]
        xc_scr[s * tt:(s + 1) * tt, :] = _conv_rows(cb_scr, s * seg, tt, cw_ref, cbias_ref[...])
        sc_ref[s] = cb_scr[(s + 1) * seg - (CONV_W - 1):(s + 1) * seg, :]
    _lru_inputs(xc_scr[...], wg_ref, bg_ref, lam_ref, a_scr, u_scr)
    for s in range(nb):
        sh_ref[s] = _scan_rows(a_scr, u_scr, h_scr, s * tt, tt, slru_ref[s], unroll=True)
    lru_ref[...] = (h_scr[...] * jax.nn.silu(z[:, W_LRU:2 * W_LRU])).astype(BF16)

    o = 2 * W_LRU
    q = _head_norm_rows(z[:, o:o + W_ATT], ones_ref[...], qg_ref[...])
    q_ref[...] = (q * (HEAD_DIM ** -0.5)).astype(BF16)
    k_ref[...] = _head_norm_rows(z[:, o + W_ATT:o + 2 * W_ATT], ones_ref[...], kg_ref[...])
    v_ref[...] = z[:, o + 2 * W_ATT:o + 3 * W_ATT]
    ga_ref[...] = jax.nn.silu(z[:, o + 3 * W_ATT:])


def _front_sample(x2, ng, win, cw, cbias, wg, bg, lam, qg_t, kg_t, ones_bd, sconv_pad, slru, nb, tt):
    n = x2.shape[0]
    rows = nb * tt
    const = lambda shape: pl.BlockSpec(shape, lambda i: (0,) * len(shape))
    row_spec = lambda w: pl.BlockSpec((rows, w), lambda i: (i, 0))
    nseq = n // tt
    kern = functools.partial(_front_sample_kernel, nb=nb, tt=tt)
    return pl.pallas_call(
        kern,
        grid=(n // rows,),
        in_specs=[
            row_spec(D_MODEL),
            const((1, D_MODEL)),
            const((D_MODEL, 2 * W_LRU + 4 * W_ATT)),
            const((CONV_W, W_LRU)),
            const((1, W_LRU)),
            const((W_LRU // HEAD_GROUP, HEAD_GROUP, 2 * HEAD_GROUP)),
            const((W_LRU // HEAD_GROUP, 1, 2 * HEAD_GROUP)),
            const((1, W_LRU)),
            const((1, W_ATT)),
            const((1, W_ATT)),
            const((HEAD_GROUP, HEAD_GROUP)),
            pl.BlockSpec((nb, SUBLANES, W_LRU), lambda i: (i, 0, 0)),
            pl.BlockSpec((nb, 1, W_LRU), lambda i: (i, 0, 0)),
        ],
        out_specs=[
            row_spec(W_LRU), row_spec(W_ATT), row_spec(W_ATT), row_spec(W_ATT), row_spec(W_ATT),
            pl.BlockSpec((nb, CONV_W - 1, W_LRU), lambda i: (i, 0, 0)),
            pl.BlockSpec((nb, 1, W_LRU), lambda i: (i, 0, 0)),
        ],
        out_shape=[
            jax.ShapeDtypeStruct((n, W_LRU), BF16),
            jax.ShapeDtypeStruct((n, W_ATT), BF16),
            jax.ShapeDtypeStruct((n, W_ATT), F32),
            jax.ShapeDtypeStruct((n, W_ATT), F32),
            jax.ShapeDtypeStruct((n, W_ATT), F32),
            jax.ShapeDtypeStruct((nseq, CONV_W - 1, W_LRU), F32),
            jax.ShapeDtypeStruct((nseq, 1, W_LRU), F32),
        ],
        scratch_shapes=[
            pltpu.VMEM((nb * (tt + SUBLANES), W_LRU), F32),
            pltpu.VMEM((rows, W_LRU), F32),
            pltpu.VMEM((rows, W_LRU), F32),
            pltpu.VMEM((rows, W_LRU), F32),
            pltpu.VMEM((rows, W_LRU), F32),
        ],
        compiler_params=pltpu.CompilerParams(
            dimension_semantics=("arbitrary",), vmem_limit_bytes=VMEM_LIMIT),
        name="front_sample",
    )(x2, ng, win, cw, cbias, wg, bg, lam, qg_t, kg_t, ones_bd, sconv_pad, slru)


def _fill_sample_bias(w_ref, bias_ref, tt):
    for h in range(N_HEADS):
        g, hl = divmod(h, HEADS_PER_GROUP)
        bias_ref[g, hl * tt:(hl + 1) * tt, :] = _toeplitz_rows(w_ref[h], tt, 0)


def _attn_sample_streams(q_ref, kn_ref, vn_ref, kc_ref, vc_ref, ga_ref, bias_ref, write):
    tt = q_ref.shape[1]
    lc = kc_ref.shape[-1]
    lanes = lax.broadcasted_iota(jnp.int32, (tt, HEAD_GROUP), 1)
    masks = [(lanes >= hl * HEAD_DIM) & (lanes < (hl + 1) * HEAD_DIM) for hl in range(HEADS_PER_GROUP)]
    units = [(s, g) for s in range(q_ref.shape[0]) for g in range(N_GROUPS)]

    def cached(ref, s, g):
        heads = ref[s, g * HEADS_PER_GROUP:(g + 1) * HEADS_PER_GROUP]
        return heads.reshape(HEAD_GROUP, lc).astype(BF16)

    def scores(s, g):
        gsl = slice(g * HEAD_GROUP, (g + 1) * HEAD_GROUP)
        qg = q_ref[s, :, gsl]
        qs = jnp.concatenate([jnp.where(m, qg, jnp.zeros((), BF16)) for m in masks], axis=0)
        sc = _dot(qs, cached(kc_ref, s, g)) + bias_ref[g, :, 0:lc]
        sn = _dot_nt(qs, kn_ref[s, :, gsl].astype(BF16)) + bias_ref[g, :, lc:lc + tt]
        return sc, sn

    def finish(s, g, sc, sn):
        gsl = slice(g * HEAD_GROUP, (g + 1) * HEAD_GROUP)
        m = jnp.maximum(jnp.max(sc, axis=-1, keepdims=True), jnp.max(sn, axis=-1, keepdims=True))
        pc = jnp.exp(sc - m)
        pn = jnp.exp(sn - m)
        l = jnp.sum(pc, axis=-1, keepdims=True) + jnp.sum(pn, axis=-1, keepdims=True)
        o = _dot_nt(pc.astype(BF16), cached(vc_ref, s, g))
        o = (o + _dot(pn.astype(BF16), vn_ref[s, :, gsl].astype(BF16))) * (1.0 / l)
        att = jnp.zeros((tt, HEAD_GROUP), F32)
        for hl in range(HEADS_PER_GROUP):
            att = att + jnp.where(masks[hl], o[hl * tt:(hl + 1) * tt, :], 0.0)
        write(s, gsl, (att * ga_ref[s, :, gsl]).astype(BF16))

    nxt = scores(*units[0])
    for n, (s, g) in enumerate(units):
        cur = nxt
        if n + 1 < len(units):
            nxt = scores(*units[n + 1])
        finish(s, g, *cur)


def _block_diag(w):
    n, d, e = w.shape
    eye = jnp.eye(n, dtype=w.dtype)
    return (eye[:, None, :, None] * w[:, :, None, :]).reshape(n * d, n * e)


def _prompt_bias_period(table):
    assert BAND - MAX_REL == MAX_REL and PROMPT_BIAS_PERIOD == KEY_TILES * QBLOCK + QBLOCK
    last = table[2 * MAX_REL:]
    neg_d = jnp.concatenate([table, jnp.broadcast_to(last, (MAX_REL - 1, N_HEADS))])
    w = jnp.concatenate([jnp.broadcast_to(last, (QBLOCK, N_HEADS)), neg_d])
    return w.T.reshape(N_HEADS, 1, PROMPT_BIAS_PERIOD).astype(F32)


def _sample_bias_period(table, tt, l):
    assert l >= MAX_REL
    period = -(-(l + 2 * tt - 1) // LANES) * LANES
    last = table[2 * MAX_REL:]
    n_var = tt + MAX_REL - 1
    var = table[2 * MAX_REL - 1:2 * MAX_REL - 1 - n_var:-1]
    w = jnp.concatenate([jnp.broadcast_to(last, (l - MAX_REL + 1, N_HEADS)), var,
                         jnp.broadcast_to(last, (period - (l + tt), N_HEADS))])
    return w.T.reshape(N_HEADS, 1, period).astype(F32)


def kernel(x_prompt, x_sample, p_prompt, p_sample, cache_k, cache_v, state_conv, state_lru, norm_g, w_in, conv_w, conv_b, gate_a_w, gate_a_b, gate_x_w, gate_x_b, lru_lambda, q_norm_g, k_norm_g, rel_bias, w_out, ple_norm_g, w_ple_gate, w_ple_proj):
    depth = w_in.shape[0]
    b, s, _ = x_prompt.shape
    db, ds, _ = x_sample.shape
    lc = cache_k.shape[2]
    yp, ys = x_prompt, x_sample.reshape(db * ds, D_MODEL)
    ones_bd = jnp.asarray(np.kron(np.eye(HEADS_PER_GROUP), np.full((HEAD_DIM, HEAD_DIM), 1.0 / HEAD_DIM)), BF16)
    gate_halves = W_LRU // HEAD_GROUP
    blocks_per_half = LRU_BLOCKS // gate_halves
    outs = [[] for _ in range(8)]
    for l in range(depth):
        win = w_in[l].astype(BF16)
        o = 2 * W_LRU
        wnat = jnp.concatenate([win[:, :o], win[:, o + W_ATT:o + 2 * W_ATT]], axis=1)
        wt = jnp.concatenate([win[:, o:o + W_ATT], win[:, o + 2 * W_ATT:]], axis=1).T
        ng = norm_g[l].reshape(1, D_MODEL)
        cw = conv_w[l]
        cbias = conv_b[l].reshape(1, W_LRU)
        wg = jnp.stack([
            jnp.concatenate([_block_diag(w[j * blocks_per_half:(j + 1) * blocks_per_half])
                             for w in (gate_a_w[l], gate_x_w[l])], axis=1)
            for j in range(gate_halves)]).astype(BF16)
        bg = jnp.concatenate([gate_a_b[l].reshape(gate_halves, 1, HEAD_GROUP),
                              gate_x_b[l].reshape(gate_halves, 1, HEAD_GROUP)], axis=2)
        lam = lru_lambda[l].reshape(1, W_LRU)
        qg_col = q_norm_g[l].reshape(HEAD_DIM, 1)
        qg_t = jnp.tile(q_norm_g[l], N_HEADS).reshape(1, W_ATT)
        kg_t = jnp.tile(k_norm_g[l], N_HEADS).reshape(1, W_ATT)
        wo = w_out[l].astype(BF16)
        pg = ple_norm_g[l].reshape(1, D_MODEL)
        wpg = w_ple_gate[l].astype(BF16)
        wpe = w_ple_proj[l].astype(BF16)

        lru_g, qt, kbf, vt, gat, pk, pv, pc, ph = _front_prompt(
            yp, ng, wnat, wt, cw, cbias, wg, bg, lam, qg_col, kg_t, ones_bd)
        att_g = _attn_prompt(qt, kbf, vt, _prompt_bias_period(rel_bias[l]), gat)
        to_frames = lambda a: jnp.transpose(a.reshape(b, N_HEADS, HEAD_DIM, a.shape[-1]), (0, 3, 1, 2))
        outs[0].append(to_frames(pk))
        outs[1].append(to_frames(pv))
        outs[2].append(pc)
        outs[3].append(ph.reshape(b, W_LRU))

        sconv_pad = jnp.pad(state_conv[l], ((0, 0), (SUBLANES - (CONV_W - 1), 0), (0, 0)))
        lru_s, q_s, k_s, v_s, ga_s, sc, sh = _front_sample(
            ys, ng, win, cw, cbias, wg, bg, lam, qg_t, kg_t, ones_bd,
            sconv_pad, state_lru[l].reshape(db, 1, W_LRU), SAMPLE_STREAMS_PER_BLOCK, ds)
        yp, ys = _back(
            (yp.reshape(b * s, D_MODEL), lru_g.reshape(b * s, W_LRU), att_g.reshape(b * s, W_ATT),
             p_prompt[l].reshape(b * s, PLE_DIM)),
            (ys, lru_s, p_sample[l].reshape(db * ds, PLE_DIM)),
            (q_s.reshape(db, ds, W_ATT), k_s.reshape(db, ds, W_ATT), v_s.reshape(db, ds, W_ATT),
             jnp.transpose(cache_k[l], (0, 2, 3, 1)), jnp.transpose(cache_v[l], (0, 2, 3, 1)),
             ga_s.reshape(db, ds, W_ATT), _sample_bias_period(rel_bias[l], ds, lc)),
            wo, pg, wpg, wpe, BACK_ROWS)
        yp = yp.reshape(b, s, D_MODEL)
        outs[4].append(k_s.reshape(db, ds, N_HEADS, HEAD_DIM))
        outs[5].append(v_s.reshape(db, ds, N_HEADS, HEAD_DIM))
        outs[6].append(sc)
        outs[7].append(sh.reshape(db, W_LRU))
    return (yp, ys.reshape(db, ds, D_MODEL)) + tuple(jnp.stack(o) for o in outs)
```
